```python
import math
import jax
import jax.numpy as jnp
from jax import lax
import numpy as np

D_MODEL = 1024
BATCH = 4
SEQ = 4096
DEPTH = 4
DEC_BATCH = 128
DEC_SEQ = 8
PAST_LEN = 8192
PAGE_SIZE = 128

N_MIXERS = 3
N_A_LAYERS = (DEPTH + N_MIXERS - 1) // N_MIXERS
N_B_LAYERS = (DEPTH + N_MIXERS - 2) // N_MIXERS
N_C_LAYERS = DEPTH // N_MIXERS
D_FF = 2816
D_PLE = 256
EPS = 1e-6
NEG_INF = -1e30

A_HEADS = 16
A_KV_HEADS = 4
A_HEAD_DIM = 64
WINDOW = 128
A_BLOCK = 128
N_BUCKETS = 32
BUCKET_MAX_DIST = 128

B_HEADS = 16
Q_LORA = 768
KV_LORA = 256
QK_NOPE = 64
QK_ROPE = 32
V_HEAD = 64
ROPE_THETA = 10000.0
B_QBLOCK = 128
MLA_SCALE = (QK_NOPE + QK_ROPE) ** -0.5

D_INNER = 2 * D_MODEL
C_HEAD_DIM = 64
C_HEADS = D_INNER // C_HEAD_DIM
C_GROUPS = 4
D_STATE = 128
D_CONV = 4
CONV_DIM = D_INNER + 2 * C_GROUPS * D_STATE
SSD_CHUNK = 128

kernel_name = 'hybrid_swa_mla_ssd_macaron_step'


def rmsnorm(x, g):
    xf = x.astype(jnp.float32)
    r = lax.rsqrt(jnp.mean(xf * xf, axis=-1, keepdims=True) + EPS)
    return (xf * r).astype(x.dtype) * g


def swiglu(h, w_gu, w_down):
    g, u = jnp.split(h @ w_gu, 2, axis=-1)
    return (jax.nn.silu(g) * u) @ w_down


def rope(x, pos):
    half = x.shape[-1] // 2
    inv = jnp.power(ROPE_THETA, -jnp.arange(half, dtype=jnp.float32) / half)
    ang = pos.astype(jnp.float32)[:, None] * inv
    shape = (pos.shape[0],) + (1,) * (x.ndim - 3) + (half,)
    cos = jnp.cos(ang).reshape(shape).astype(x.dtype)
    sin = jnp.sin(ang).reshape(shape).astype(x.dtype)
    x1, x2 = x[..., :half], x[..., half:]
    return jnp.concatenate([x1 * cos - x2 * sin, x2 * cos + x1 * sin], axis=-1)


def t5_bucket(dist):
    max_exact = N_BUCKETS // 2
    d = jnp.maximum(dist, 0)
    large = max_exact + (jnp.log(jnp.maximum(d, 1).astype(jnp.float32) / max_exact)
                         / math.log(BUCKET_MAX_DIST / max_exact) * (N_BUCKETS - max_exact)).astype(jnp.int32)
    large = jnp.minimum(large, N_BUCKETS - 1)
    return jnp.where(d < max_exact, d, large)


def rel_bias(table, dist):
    return jnp.moveaxis(table[t5_bucket(dist)], -1, 0).astype(jnp.float32)


def swa_core(q, k, v, sinks, bias, valid):
    b, n, t = q.shape[:3]
    grp = A_HEADS // A_KV_HEADS
    qg = q.reshape(b, n, t, A_KV_HEADS, grp, A_HEAD_DIM)
    logits = jnp.einsum('bntkgd,bnskd->bnkgts', qg, k).astype(jnp.float32) * (A_HEAD_DIM ** -0.5)
    logits = logits + bias.reshape(A_KV_HEADS, grp, *bias.shape[1:])
    logits = jnp.where(valid[None, :, None, None], logits, NEG_INF)
    sink = jnp.broadcast_to(sinks.astype(jnp.float32).reshape(1, 1, A_KV_HEADS, grp, 1, 1),
                            logits.shape[:-1] + (1,))
    p = jax.nn.softmax(jnp.concatenate([logits, sink], axis=-1), axis=-1)[..., :-1]
    o = jnp.einsum('bnkgts,bnskd->bntkgd', p.astype(v.dtype), v)
    return o.reshape(b, n, t, A_HEADS * A_HEAD_DIM)


def swa_qkv(h, w_qkv):
    b, T, _ = h.shape
    q, k, v = jnp.split(h @ w_qkv, [A_HEADS * A_HEAD_DIM, (A_HEADS + A_KV_HEADS) * A_HEAD_DIM], axis=-1)
    return (q.reshape(b, T, A_HEADS, A_HEAD_DIM), k.reshape(b, T, A_KV_HEADS, A_HEAD_DIM),
            v.reshape(b, T, A_KV_HEADS, A_HEAD_DIM))


def swa_prompt(h, w_qkv, w_o, sinks, table):
    b, T, _ = h.shape
    w_buf = min(WINDOW, PAST_LEN)
    q, k, v = swa_qkv(h, w_qkv)
    nb = T // A_BLOCK
    qb = q.reshape(b, nb, A_BLOCK, A_HEADS, A_HEAD_DIM)
    pad = jnp.zeros((b, A_BLOCK, A_KV_HEADS, A_HEAD_DIM), k.dtype)
    kp = jnp.concatenate([pad, k], axis=1).reshape(b, nb + 1, A_BLOCK, A_KV_HEADS, A_HEAD_DIM)
    vp = jnp.concatenate([pad, v], axis=1).reshape(b, nb + 1, A_BLOCK, A_KV_HEADS, A_HEAD_DIM)
    kb = jnp.concatenate([kp[:, :-1], kp[:, 1:]], axis=2)
    vb = jnp.concatenate([vp[:, :-1], vp[:, 1:]], axis=2)
    i = jnp.arange(A_BLOCK)[:, None]
    j = jnp.arange(2 * A_BLOCK)[None, :]
    dist = A_BLOCK + i - j
    k_pos = (jnp.arange(nb)[:, None, None] - 1) * A_BLOCK + j
    valid = (dist >= 0) & (dist < WINDOW) & (k_pos >= 0)
    o = swa_core(qb, kb, vb, sinks, rel_bias(table, dist), valid).reshape(b, T, A_HEADS * A_HEAD_DIM)
    return o @ w_o, k[:, T - w_buf:], v[:, T - w_buf:]


def swa_sample(h, w_qkv, w_o, sinks, table, k_buf, v_buf):
    b, L, _ = h.shape
    w_buf = k_buf.shape[1]
    q, k, v = swa_qkv(h, w_qkv)
    kc = jnp.concatenate([k_buf, k], axis=1)
    vc = jnp.concatenate([v_buf, v], axis=1)
    q_pos = PAST_LEN + jnp.arange(L)
    k_pos = jnp.concatenate([PAST_LEN - w_buf + jnp.arange(w_buf), q_pos])
    dist = q_pos[:, None] - k_pos[None, :]
    valid = ((dist >= 0) & (dist < WINDOW))[None]
    o = swa_core(q[:, None], kc[:, None], vc[:, None], sinks, rel_bias(table, dist), valid)
    return o.reshape(b, L, A_HEADS * A_HEAD_DIM) @ w_o, kc[:, L:], vc[:, L:]


def mla_project(h, pos, w_in, q_norm, w_uq, kv_norm, w_ukv):
    b, T, _ = h.shape
    cq, ckv, kpe = jnp.split(h @ w_in, [Q_LORA, Q_LORA + KV_LORA], axis=-1)
    q = (rmsnorm(cq, q_norm) @ w_uq).reshape(b, T, B_HEADS, QK_NOPE + QK_ROPE)
    q_nope, q_pe = q[..., :QK_NOPE], rope(q[..., QK_NOPE:], pos)
    w = w_ukv.reshape(KV_LORA, B_HEADS, QK_NOPE + V_HEAD)
    w_uk, w_uv = w[..., :QK_NOPE], w[..., QK_NOPE:]
    q_lat = jnp.einsum('bthn,chn->bthc', q_nope, w_uk)
    q_cat = jnp.concatenate([q_lat, q_pe], axis=-1)
    kv_row = jnp.concatenate([rmsnorm(ckv, kv_norm), rope(kpe, pos)], axis=-1)
    return q_cat, kv_row, w_uv


def mla_logits(q_cat, kv):
    return jnp.einsum('bthc,bsc->bhts', q_cat, kv).astype(jnp.float32) * MLA_SCALE


def mla_out(o_lat, w_uv, w_o):
    b, T = o_lat.shape[:2]
    return jnp.einsum('bthc,chv->bthv', o_lat, w_uv).reshape(b, T, B_HEADS * V_HEAD) @ w_o


def mla_prompt(h, pos, w_in, q_norm, w_uq, kv_norm, w_ukv, w_o):
    b, T, _ = h.shape
    q_cat, kv, w_uv = mla_project(h, pos, w_in, q_norm, w_uq, kv_norm, w_ukv)
    nb = T // B_QBLOCK
    qb = jnp.moveaxis(q_cat.reshape(b, nb, B_QBLOCK, B_HEADS, KV_LORA + QK_ROPE), 1, 0)
    qpos = pos.reshape(nb, B_QBLOCK)
    lat = kv[..., :KV_LORA]

    def block(args):
        q_blk, qp = args
        lg = mla_logits(q_blk, kv)
        lg = jnp.where(pos[None, None, None, :] <= qp[None, None, :, None], lg, NEG_INF)
        p = jax.nn.softmax(lg, axis=-1).astype(kv.dtype)
        return jnp.einsum('bhts,bsc->bthc', p, lat)

    o_lat = lax.map(block, (qb, qpos))
    o_lat = jnp.moveaxis(o_lat, 0, 1).reshape(b, T, B_HEADS, KV_LORA)
    return mla_out(o_lat, w_uv, w_o), kv


def mla_sample(h, pos, w_in, q_norm, w_uq, kv_norm, w_ukv, w_o, cache, page_table):
    b, L, _ = h.shape
    q_cat, kv, w_uv = mla_project(h, pos, w_in, q_norm, w_uq, kv_norm, w_ukv)
    n_pages = page_table.shape[1]
    past = cache[page_table].reshape(b, n_pages * PAGE_SIZE, KV_LORA + QK_ROPE)
    n_past = past.shape[1]
    lg_past = mla_logits(q_cat, past)
    causal = jnp.tril(jnp.ones((L, L), dtype=bool))
    lg_new = jnp.where(causal[None, None], mla_logits(q_cat, kv), NEG_INF)
    p = jax.nn.softmax(jnp.concatenate([lg_past, lg_new], axis=-1), axis=-1).astype(kv.dtype)
    o_lat = (jnp.einsum('bhts,bsc->bthc', p[..., :n_past], past[..., :KV_LORA])
             + jnp.einsum('bhts,bsc->bthc', p[..., n_past:], kv[..., :KV_LORA]))
    return mla_out(o_lat, w_uv, w_o), kv


def causal_dwconv(u, prev, w, bias):
    up = jnp.concatenate([prev, u], axis=1)
    out = lax.conv_general_dilated(up, w[:, None, :], window_strides=(1,), padding='VALID',
                                   dimension_numbers=('NWC', 'WIO', 'NWC'),
                                   feature_group_count=u.shape[-1])
    return out + bias, up[:, -(D_CONV - 1):]


def ssd(x, dt, a, bm, cm, h0, chunk):
    b, T, H, P = x.shape
    G, N = bm.shape[2], bm.shape[3]
    hpg = H // G
    nc = T // chunk
    f32 = jnp.float32
    xc = x.astype(f32).reshape(b, nc, chunk, G, hpg, P)
    dtc = dt.reshape(b, nc, chunk, G, hpg)
    bc = bm.astype(f32).reshape(b, nc, chunk, G, N)
    cc = cm.astype(f32).reshape(b, nc, chunk, G, N)
    acs = jnp.cumsum(dtc * a.reshape(G, hpg), axis=2)
    causal = jnp.tril(jnp.ones((chunk, chunk), dtype=bool))
    seg = acs[:, :, :, None] - acs[:, :, None, :]
    decay = jnp.exp(jnp.where(causal[:, :, None, None], seg, -jnp.inf))
    cb = jnp.einsum('bclgn,bcsgn->bclsg', cc, bc)
    w = cb[..., None] * decay * dtc[:, :, None]
    y_diag = jnp.einsum('bclsgh,bcsghp->bclghp', w, xc)
    to_end = jnp.exp(acs[:, :, -1:] - acs) * dtc
    states = jnp.einsum('bclgn,bclghp->bcghpn', bc, xc * to_end[..., None])
    chunk_decay = jnp.exp(acs[:, :, -1])

    def step(h, inp):
        s, d = inp
        return d[..., None, None] * h + s, h

    h_fin, h_prev = lax.scan(step, h0.astype(f32).reshape(b, G, hpg, P, N),
                             (jnp.moveaxis(states, 1, 0), jnp.moveaxis(chunk_decay, 1, 0)))
    h_prev = jnp.moveaxis(h_prev, 0, 1)
    y_off = jnp.einsum('bclgn,bcghpn->bclghp', cc, h_prev) * jnp.exp(acs)[..., None]
    return (y_diag + y_off).reshape(b, T, H, P), h_fin.reshape(b, H, P, N)


def mamba_mixer(h, w_in, conv_w, conv_b, dt_bias, a_log, d_skip, norm_g, w_o, conv_prev, h0):
    b, T, _ = h.shape
    z, xbc, dt = jnp.split(h @ w_in, [D_INNER, D_INNER + CONV_DIM], axis=-1)
    xbc, conv_new = causal_dwconv(xbc, conv_prev, conv_w, conv_b)
    xbc = jax.nn.silu(xbc)
    xs, bm, cm = jnp.split(xbc, [D_INNER, D_INNER + C_GROUPS * D_STATE], axis=-1)
    xs = xs.reshape(b, T, C_HEADS, C_HEAD_DIM)
    bm = bm.reshape(b, T, C_GROUPS, D_STATE)
    cm = cm.reshape(b, T, C_GROUPS, D_STATE)
    dt = jax.nn.softplus(dt.astype(jnp.float32) + dt_bias.astype(jnp.float32))
    a = -jnp.exp(a_log.astype(jnp.float32))
    y, h_fin = ssd(xs, dt, a, bm, cm, h0, min(SSD_CHUNK, T))
    y = y.astype(h.dtype) + d_skip[:, None] * xs
    y = y.reshape(b, T, D_INNER) * jax.nn.silu(z)
    yg = y.reshape(b, T, C_GROUPS, D_INNER // C_GROUPS).astype(jnp.float32)
    yg = yg * lax.rsqrt(jnp.mean(yg * yg, axis=-1, keepdims=True) + EPS)
    y = yg.reshape(b, T, D_INNER).astype(h.dtype) * norm_g
    return y @ w_o, conv_new, h_fin.astype(h.dtype)


def setup_inputs(seed: int = 0) -> dict:
    key = jax.random.key(seed)
    ks = iter(jax.random.split(key, 64))
    f32 = jnp.float32

    def nrm(shape, scale=1.0):
        return jax.random.normal(next(ks), shape, f32) * scale

    def gain(shape):
        return 1.0 + nrm(shape, 0.05)

    n_pages = PAST_LEN // PAGE_SIZE
    n_pool = (5 * DEC_BATCH * n_pages) // 4
    w_buf = min(WINDOW, PAST_LEN)
    page_table = jax.random.permutation(next(ks), n_pool)[:DEC_BATCH * n_pages].reshape(DEC_BATCH, n_pages).astype(jnp.int32)
    dt_init = jnp.exp(jax.random.uniform(next(ks), (N_C_LAYERS, C_HEADS), f32, math.log(1e-3), math.log(1e-1)))
    c_dt_bias = dt_init + jnp.log(-jnp.expm1(-dt_init))
    c_a_log = jnp.log(jax.random.uniform(next(ks), (N_C_LAYERS, C_HEADS), f32, 1.0, 16.0))
    a_qkv = (A_HEADS + 2 * A_KV_HEADS) * A_HEAD_DIM
    return {
        'x_prompt': nrm((BATCH, SEQ, D_MODEL)),
        'x_sample': nrm((DEC_BATCH, DEC_SEQ, D_MODEL)),
        'state_swa_k': nrm((N_A_LAYERS, DEC_BATCH, w_buf, A_KV_HEADS, A_HEAD_DIM)),
        'state_swa_v': nrm((N_A_LAYERS, DEC_BATCH, w_buf, A_KV_HEADS, A_HEAD_DIM)),
        'cache_mla': nrm((N_B_LAYERS, n_pool, PAGE_SIZE, KV_LORA + QK_ROPE)),
        'state_ssm': nrm((N_C_LAYERS, DEC_BATCH, C_HEADS, C_HEAD_DIM, D_STATE), 0.1),
        'state_conv': nrm((N_C_LAYERS, DEC_BATCH, D_CONV - 1, CONV_DIM)),
        'page_table': page_table,
        'p_prompt': nrm((DEPTH, BATCH, SEQ, D_PLE)),
        'p_sample': nrm((DEPTH, DEC_BATCH, DEC_SEQ, D_PLE)),
        'norm_ffn1': gain((DEPTH, D_MODEL)),
        'ffn1_w_gu': nrm((DEPTH, D_MODEL, 2 * D_FF), D_MODEL ** -0.5),
        'ffn1_w_down': nrm((DEPTH, D_FF, D_MODEL), D_FF ** -0.5),
        'norm_mix': gain((DEPTH, D_MODEL)),
        'norm_ffn2': gain((DEPTH, D_MODEL)),
        'ffn2_w_gu': nrm((DEPTH, D_MODEL, 2 * D_FF), D_MODEL ** -0.5),
        'ffn2_w_down': nrm((DEPTH, D_FF, D_MODEL), D_FF ** -0.5),
        'norm_ple': gain((DEPTH, D_MODEL)),
        'ple_w_proj': nrm((DEPTH, D_PLE, D_MODEL), D_PLE ** -0.5),
        'ple_w_gate': nrm((DEPTH, D_MODEL, D_MODEL), D_MODEL ** -0.5),
        'rel_bias_table': nrm((N_BUCKETS, A_HEADS), 0.5),
        'a_w_qkv': nrm((N_A_LAYERS, D_MODEL, a_qkv), D_MODEL ** -0.5),
        'a_w_o': nrm((N_A_LAYERS, A_HEADS * A_HEAD_DIM, D_MODEL), (A_HEADS * A_HEAD_DIM) ** -0.5),
        'a_sinks': nrm((N_A_LAYERS, A_HEADS), 0.5),
        'b_w_in': nrm((N_B_LAYERS, D_MODEL, Q_LORA + KV_LORA + QK_ROPE), D_MODEL ** -0.5),
        'b_q_norm': gain((N_B_LAYERS, Q_LORA)),
        'b_w_uq': nrm((N_B_LAYERS, Q_LORA, B_HEADS * (QK_NOPE + QK_ROPE)), Q_LORA ** -0.5),
        'b_kv_norm': gain((N_B_LAYERS, KV_LORA)),
        'b_w_ukv': nrm((N_B_LAYERS, KV_LORA, B_HEADS * (QK_NOPE + V_HEAD)), KV_LORA ** -0.5),
        'b_w_o': nrm((N_B_LAYERS, B_HEADS * V_HEAD, D_MODEL), (B_HEADS * V_HEAD) ** -0.5),
        'c_w_in': nrm((N_C_LAYERS, D_MODEL, 2 * D_INNER + 2 * C_GROUPS * D_STATE + C_HEADS), D_MODEL ** -0.5),
        'c_conv_w': nrm((N_C_LAYERS, D_CONV, CONV_DIM), D_CONV ** -0.5),
        'c_conv_b': nrm((N_C_LAYERS, CONV_DIM), 0.01),
        'c_dt_bias': c_dt_bias,
        'c_a_log': c_a_log,
        'c_d': gain((N_C_LAYERS, C_HEADS)),
        'c_norm': gain((N_C_LAYERS, D_INNER)),
        'c_w_o': nrm((N_C_LAYERS, D_INNER, D_MODEL), D_INNER ** -0.5),
        'final_norm': gain((D_MODEL,)),
    }


def reference(x_prompt, x_sample, state_swa_k, state_swa_v, cache_mla, state_ssm, state_conv, page_table,
              p_prompt, p_sample, norm_ffn1, ffn1_w_gu, ffn1_w_down, norm_mix, norm_ffn2, ffn2_w_gu,
              ffn2_w_down, norm_ple, ple_w_proj, ple_w_gate, rel_bias_table, a_w_qkv, a_w_o, a_sinks,
              b_w_in, b_q_norm, b_w_uq, b_kv_norm, b_w_ukv, b_w_o, c_w_in, c_conv_w, c_conv_b, c_dt_bias,
              c_a_log, c_d, c_norm, c_w_o, final_norm):
    b_p, t_p, _ = x_prompt.shape
    t_s = x_sample.shape[1]
    pos_p = jnp.arange(t_p)
    pos_s = PAST_LEN + jnp.arange(t_s)
    zero_conv = jnp.zeros((b_p, D_CONV - 1, CONV_DIM), x_prompt.dtype)
    zero_ssm = jnp.zeros((b_p, C_HEADS, C_HEAD_DIM, D_STATE), jnp.float32)
    swa_k_p, swa_v_p, mla_p, ssm_p, conv_p = [], [], [], [], []
    swa_k_s, swa_v_s, mla_s, ssm_s, conv_s = [], [], [], [], []
    xp, xs = x_prompt, x_sample
    for i in range(DEPTH):
        kind, j = i % N_MIXERS, i // N_MIXERS
        xp = xp + 0.5 * swiglu(rmsnorm(xp, norm_ffn1[i]), ffn1_w_gu[i], ffn1_w_down[i])
        xs = xs + 0.5 * swiglu(rmsnorm(xs, norm_ffn1[i]), ffn1_w_gu[i], ffn1_w_down[i])
        hp = rmsnorm(xp, norm_mix[i])
        hs = rmsnorm(xs, norm_mix[i])
        if kind == 0:
            yp, kp_, vp_ = swa_prompt(hp, a_w_qkv[j], a_w_o[j], a_sinks[j], rel_bias_table)
            ys, ks_, vs_ = swa_sample(hs, a_w_qkv[j], a_w_o[j], a_sinks[j], rel_bias_table,
                                      state_swa_k[j], state_swa_v[j])
            swa_k_p.append(kp_)
            swa_v_p.append(vp_)
            swa_k_s.append(ks_)
            swa_v_s.append(vs_)
        elif kind == 1:
            mla_w = (b_w_in[j], b_q_norm[j], b_w_uq[j], b_kv_norm[j], b_w_ukv[j], b_w_o[j])
            yp, rows_p = mla_prompt(hp, pos_p, *mla_w)
            ys, rows_s = mla_sample(hs, pos_s, *mla_w, cache_mla[j], page_table)
            mla_p.append(rows_p)
            mla_s.append(rows_s)
        else:
            ssd_w = (c_w_in[j], c_conv_w[j], c_conv_b[j], c_dt_bias[j], c_a_log[j], c_d[j], c_norm[j], c_w_o[j])
            yp, cp_, sp_ = mamba_mixer(hp, *ssd_w, zero_conv, zero_ssm)
            ys, cs_, ss_ = mamba_mixer(hs, *ssd_w, state_conv[j], state_ssm[j])
            conv_p.append(cp_)
            ssm_p.append(sp_)
            conv_s.append(cs_)
            ssm_s.append(ss_)
        xp = xp + yp
        xs = xs + ys
        xp = xp + 0.5 * swiglu(rmsnorm(xp, norm_ffn2[i]), ffn2_w_gu[i], ffn2_w_down[i])
        xs = xs + 0.5 * swiglu(rmsnorm(xs, norm_ffn2[i]), ffn2_w_gu[i], ffn2_w_down[i])
        xp = xp + (p_prompt[i] @ ple_w_proj[i]) * jax.nn.sigmoid(rmsnorm(xp, norm_ple[i]) @ ple_w_gate[i])
        xs = xs + (p_sample[i] @ ple_w_proj[i]) * jax.nn.sigmoid(rmsnorm(xs, norm_ple[i]) @ ple_w_gate[i])
    y_prompt = rmsnorm(xp, final_norm)
    y_sample = rmsnorm(xs, final_norm)
    return (y_prompt, y_sample,
            jnp.stack(swa_k_p), jnp.stack(swa_v_p), jnp.stack(mla_p), jnp.stack(ssm_p), jnp.stack(conv_p),
            jnp.stack(swa_k_s), jnp.stack(swa_v_s), jnp.stack(mla_s), jnp.stack(ssm_s), jnp.stack(conv_s))
```

```python
import functools
import math

import jax
import jax.numpy as jnp
from jax import lax
from jax.experimental import pallas as pl
from jax.experimental.pallas import tpu as pltpu

F32 = jnp.float32
BF16 = jnp.bfloat16

D_MODEL = 1024
D_FF = 2816
D_PLE = 256
EPS = 1e-6
NEG_INF = -1e30
PAST_LEN = 8192
PAGE_SIZE = 128
N_MIXERS = 3

A_HEADS = 16
A_KV_HEADS = 4
A_HEAD_DIM = 64
A_GROUP = A_HEADS // A_KV_HEADS
A_Q = A_HEADS * A_HEAD_DIM
A_KV = A_KV_HEADS * A_HEAD_DIM
WINDOW = 128
A_BLOCK = 128
N_BUCKETS = 32
BUCKET_MAX_DIST = 128

B_HEADS = 16
Q_LORA = 768
KV_LORA = 256
QK_NOPE = 64
QK_ROPE = 32
V_HEAD = 64
KV_ROW = KV_LORA + QK_ROPE
ROPE_THETA = 10000.0
MLA_SCALE = (QK_NOPE + QK_ROPE) ** -0.5
B_QBLOCK = 128
B_KVBLOCK = 512
B_KVCHUNK = 256
B_PAGES_PER_STEP = 16

D_INNER = 2 * D_MODEL
C_HEAD_DIM = 64
C_HEADS = D_INNER // C_HEAD_DIM
C_GROUPS = 4
C_HPG = C_HEADS // C_GROUPS
D_STATE = 128
D_CONV = 4
C_BC = C_GROUPS * D_STATE
CONV_DIM = D_INNER + 2 * C_BC
SSD_CHUNK = 128

LANES = 128
SUBLANES = 8
VMEM_LIMIT = 56 * 1024 * 1024


def _cparams(*sem):
    return pltpu.CompilerParams(dimension_semantics=sem, vmem_limit_bytes=VMEM_LIMIT)


def _resident(shape, index_map):
    return pl.BlockSpec(shape, index_map, pipeline_mode=pl.Buffered(1))


def _dot(a, b):
    return jnp.dot(a, b, preferred_element_type=F32)


def _dot_nt(a, b):
    return lax.dot_general(a, b, (((1,), (1,)), ((), ())), preferred_element_type=F32)


def _dot_tn(a, b):
    return lax.dot_general(a, b, (((0,), (0,)), ((), ())), preferred_element_type=F32)


def _dot_f32(a, b):
    return jnp.dot(a, b, preferred_element_type=F32, precision=lax.Precision.HIGHEST)


def _rms(x, g):
    r = lax.rsqrt(jnp.mean(x * x, axis=-1, keepdims=True) + EPS)
    return (x * r) * g


def _silu(x):
    return x * jax.nn.sigmoid(x)


def _softplus(x):
    return jnp.maximum(x, 0.0) + jnp.log1p(jnp.exp(-jnp.abs(x)))


def _lanes(a, n):
    if n <= LANES:
        return a[:, :n]
    return jnp.concatenate([a] * (n // LANES), axis=1)


FFN_TM = 512
FFN_TF = 256


def _ffn_body(x_ref, g_ref, wgu_ref, wd_ref, o_ref):
    x = x_ref[...]
    h = _rms(x, g_ref[...]).astype(BF16)
    acc = jnp.zeros(x.shape, F32)
    for j in range(D_FF // FFN_TF):
        lo = j * FFN_TF
        gate = _dot(h, wgu_ref[:, lo:lo + FFN_TF])
        up = _dot(h, wgu_ref[:, D_FF + lo:D_FF + lo + FFN_TF])
        act = (_silu(gate) * up).astype(BF16)
        acc = acc + _dot(act, wd_ref[lo:lo + FFN_TF, :])
    o_ref[...] = x + 0.5 * acc


def _ffn(x, norm, w_gu, w_down, layer):
    rows = x.shape[0]
    return pl.pallas_call(
        _ffn_body,
        grid=(rows // FFN_TM,),
        in_specs=[
            pl.BlockSpec((FFN_TM, D_MODEL), lambda i: (i, 0)),
            _resident((None, 1, D_MODEL), lambda i: (layer, 0, 0)),
            _resident((None, D_MODEL, 2 * D_FF), lambda i: (layer, 0, 0)),
            _resident((None, D_FF, D_MODEL), lambda i: (layer, 0, 0)),
        ],
        out_specs=pl.BlockSpec((FFN_TM, D_MODEL), lambda i: (i, 0)),
        out_shape=jax.ShapeDtypeStruct(x.shape, F32),
        compiler_params=_cparams("parallel"),
        name="ffn",
    )(x, norm, w_gu, w_down)


ROW_TM = 256


def _ple_body(x_ref, p_ref, g_ref, wp_ref, wg_ref, gf_ref, o_ref, *, final):
    x = x_ref[...]
    gate = _dot(_rms(x, g_ref[...]).astype(BF16), wg_ref[...])
    proj = _dot(p_ref[...].astype(BF16), wp_ref[...])
    y = x + proj * jax.nn.sigmoid(gate)
    if final:
        y = _rms(y, gf_ref[...])
    o_ref[...] = y


def _ple(x, p_all, norm, w_proj, w_gate, final_norm, layer, final):
    rows = x.shape[0]
    return pl.pallas_call(
        functools.partial(_ple_body, final=final),
        grid=(rows // ROW_TM,),
        in_specs=[
            pl.BlockSpec((ROW_TM, D_MODEL), lambda i: (i, 0)),
            pl.BlockSpec((None, ROW_TM, D_PLE), lambda i: (layer, i, 0)),
            _resident((None, 1, D_MODEL), lambda i: (layer, 0, 0)),
            _resident((None, D_PLE, D_MODEL), lambda i: (layer, 0, 0)),
            _resident((None, D_MODEL, D_MODEL), lambda i: (layer, 0, 0)),
            _resident((1, D_MODEL), lambda i: (0, 0)),
        ],
        out_specs=pl.BlockSpec((ROW_TM, D_MODEL), lambda i: (i, 0)),
        out_shape=jax.ShapeDtypeStruct(x.shape, F32),
        compiler_params=_cparams("parallel"),
        name="ple",
    )(x, p_all, norm, w_proj, w_gate, final_norm)


def _norm_proj_body(x_ref, g_ref, w_ref, o_ref):
    o_ref[...] = _dot(_rms(x_ref[...], g_ref[...]).astype(BF16), w_ref[...])


def _norm_proj(x, norm, w, layer, wl, name):
    rows = x.shape[0]
    k, n = w.shape[1:]
    return pl.pallas_call(
        _norm_proj_body,
        grid=(rows // ROW_TM,),
        in_specs=[
            pl.BlockSpec((ROW_TM, k), lambda i: (i, 0)),
            _resident((None, 1, k), lambda i: (layer, 0, 0)),
            _resident((None, k, n), lambda i: (wl, 0, 0)),
        ],
        out_specs=pl.BlockSpec((ROW_TM, n), lambda i: (i, 0)),
        out_shape=jax.ShapeDtypeStruct((rows, n), F32),
        compiler_params=_cparams("parallel"),
        name=name,
    )(x, norm, w)


def _res_proj_body(x_ref, a_ref, w_ref, o_ref):
    o_ref[...] = x_ref[...] + _dot(a_ref[...].astype(BF16), w_ref[...])


def _res_proj(x, a, w, wl, row0, name):
    rows = a.shape[0]
    k, n = w.shape[1:]
    blk0 = row0 // ROW_TM
    return pl.pallas_call(
        _res_proj_body,
        grid=(rows // ROW_TM,),
        in_specs=[
            pl.BlockSpec((ROW_TM, n), lambda i: (i + blk0, 0)),
            pl.BlockSpec((ROW_TM, k), lambda i: (i, 0)),
            _resident((None, k, n), lambda i: (wl, 0, 0)),
        ],
        out_specs=pl.BlockSpec((ROW_TM, n), lambda i: (i + blk0, 0)),
        out_shape=jax.ShapeDtypeStruct(x.shape, F32),
        input_output_aliases={0: 0},
        compiler_params=_cparams("parallel"),
        name=name,
    )(x, a, w)


def _t5_bucket(dist):
    max_exact = N_BUCKETS // 2
    d = jnp.maximum(dist, 0)
    large = max_exact + (jnp.log(jnp.maximum(d, 1).astype(F32) / max_exact)
                         / math.log(BUCKET_MAX_DIST / max_exact) * (N_BUCKETS - max_exact)).astype(jnp.int32)
    large = jnp.minimum(large, N_BUCKETS - 1)
    return jnp.where(d < max_exact, d, large)


def _bias_body(tab_ref, bm_ref, o_ref):
    h = pl.program_id(0)
    bm = bm_ref[...]
    acc = jnp.zeros(bm.shape, F32)
    for b in range(N_BUCKETS):
        acc = jnp.where(bm == b, tab_ref[b, h], acc)
    o_ref[...] = acc


def _rel_bias(table, bucket_map):
    r, c = bucket_map.shape
    return pl.pallas_call(
        _bias_body,
        grid=(A_HEADS,),
        in_specs=[pl.BlockSpec(memory_space=pltpu.SMEM), pl.BlockSpec((r, c), lambda h: (0, 0))],
        out_specs=pl.BlockSpec((None, r, c), lambda h: (h, 0, 0)),
        out_shape=jax.ShapeDtypeStruct((A_HEADS, r, c), F32),
        compiler_params=_cparams("parallel"),
        name="rel_bias",
    )(table, bucket_map)


def _swa_heads(q_ref, k, v, bias_ref, sink_ref, valid, o_scr):
    scale = A_HEAD_DIM ** -0.5
    for g in range(A_KV_HEADS):
        kg = k[:, g * A_HEAD_DIM:(g + 1) * A_HEAD_DIM].astype(BF16)
        vg = v[:, g * A_HEAD_DIM:(g + 1) * A_HEAD_DIM].astype(BF16)
        for hh in range(A_GROUP):
            h = g * A_GROUP + hh
            hs = slice(h * A_HEAD_DIM, (h + 1) * A_HEAD_DIM)
            s = _dot_nt(q_ref[:, hs].astype(BF16), kg) * scale + bias_ref[h]
            s = jnp.where(valid, s, NEG_INF)
            sink = sink_ref[h]
            m = jnp.maximum(jnp.max(s, axis=-1, keepdims=True), sink)
            e = jnp.exp(s - m)
            den = jnp.sum(e, axis=-1, keepdims=True) + jnp.exp(sink - m)
            o_scr[:, hs] = _dot((e / den).astype(BF16), vg)


def _swa_prompt_body(sink_ref, q_ref, kp_ref, ko_ref, vp_ref, vo_ref, bias_ref, x_ref, wo_ref, o_ref, o_scr):
    n = pl.program_id(1)
    row = lax.broadcasted_iota(jnp.int32, (A_BLOCK, A_BLOCK), 0)
    col = lax.broadcasted_iota(jnp.int32, (A_BLOCK, A_BLOCK), 1)
    valid = jnp.concatenate([col > row + jnp.where(n > 0, 0, A_BLOCK), col <= row], axis=1)
    k = jnp.concatenate([kp_ref[...], ko_ref[...]], axis=0)
    v = jnp.concatenate([vp_ref[...], vo_ref[...]], axis=0)
    _swa_heads(q_ref, k, v, bias_ref, sink_ref, valid, o_scr)
    o_ref[...] = x_ref[...] + _dot(o_scr[...].astype(BF16), wo_ref[...])


def _swa_prompt(x, qkv, bias, sinks, w_o, wl, nb_p, t_p):
    nblk = t_p // A_BLOCK
    kcol = A_Q // A_KV
    vcol = kcol + 1

    def own(b, n):
        return b * nblk + n

    def prev(b, n):
        return b * nblk + jnp.maximum(n - 1, 0)

    return pl.pallas_call(
        _swa_prompt_body,
        grid=(nb_p, nblk),
        in_specs=[
            pl.BlockSpec(memory_space=pltpu.SMEM),
            pl.BlockSpec((A_BLOCK, A_Q), lambda b, n: (own(b, n), 0)),
            pl.BlockSpec((A_BLOCK, A_KV), lambda b, n: (prev(b, n), kcol)),
            pl.BlockSpec((A_BLOCK, A_KV), lambda b, n: (own(b, n), kcol)),
            pl.BlockSpec((A_BLOCK, A_KV), lambda b, n: (prev(b, n), vcol)),
            pl.BlockSpec((A_BLOCK, A_KV), lambda b, n: (own(b, n), vcol)),
            _resident((A_HEADS, A_BLOCK, 2 * A_BLOCK), lambda b, n: (0, 0, 0)),
            pl.BlockSpec((A_BLOCK, D_MODEL), lambda b, n: (own(b, n), 0)),
            _resident((None, A_Q, D_MODEL), lambda b, n: (wl, 0, 0)),
        ],
        out_specs=pl.BlockSpec((A_BLOCK, D_MODEL), lambda b, n: (own(b, n), 0)),
        out_shape=jax.ShapeDtypeStruct(x.shape, F32),
        scratch_shapes=[pltpu.VMEM((A_BLOCK, A_Q), F32)],
        input_output_aliases={7: 0},
        compiler_params=_cparams("parallel", "arbitrary"),
        name="swa_prompt",
    )(sinks, qkv, qkv, qkv, qkv, qkv, bias, x, w_o)


def _swa_sample_body(sink_ref, qkv_ref, kb_ref, vb_ref, bias_ref, o_ref, o_scr, *, l_s, w_buf):
    row = lax.broadcasted_iota(jnp.int32, (l_s, w_buf + l_s), 0)
    col = lax.broadcasted_iota(jnp.int32, (l_s, w_buf + l_s), 1)
    valid = jnp.logical_or(jnp.logical_and(col < w_buf, w_buf + row - col < WINDOW),
                           jnp.logical_and(col >= w_buf, col - w_buf <= row))
    k = jnp.concatenate([kb_ref[...], qkv_ref[:, A_Q:A_Q + A_KV]], axis=0)
    v = jnp.concatenate([vb_ref[...], qkv_ref[:, A_Q + A_KV:]], axis=0)
    _swa_heads(qkv_ref, k, v, bias_ref, sink_ref, valid, o_scr)
    o_ref[...] = o_scr[...]


def _swa_sample(qkv, k_buf, v_buf, bias, sinks, sl, row0, nb_s, l_s):
    w_buf = k_buf.shape[2]
    blk0 = row0 // l_s
    return pl.pallas_call(
        functools.partial(_swa_sample_body, l_s=l_s, w_buf=w_buf),
        grid=(nb_s,),
        in_specs=[
            pl.BlockSpec(memory_space=pltpu.SMEM),
            pl.BlockSpec((l_s, A_Q + 2 * A_KV), lambda b: (b + blk0, 0)),
            pl.BlockSpec((None, None, w_buf, A_KV), lambda b: (sl, b, 0, 0)),
            pl.BlockSpec((None, None, w_buf, A_KV), lambda b: (sl, b, 0, 0)),
            _resident((A_HEADS, l_s, w_buf + l_s), lambda b: (0, 0, 0)),
        ],
        out_specs=pl.BlockSpec((l_s, A_Q), lambda b: (b, 0)),
        out_shape=jax.ShapeDtypeStruct((nb_s * l_s, A_Q), F32),
        scratch_shapes=[pltpu.VMEM((l_s, A_Q), F32)],
        compiler_params=_cparams("parallel"),
        name="swa_sample",
    )(sinks, qkv, k_buf, v_buf, bias)


B_IN_COLS = Q_LORA + KV_LORA + 2 * LANES
B_KPE = Q_LORA + KV_LORA
B_KPE_ROT = B_KPE + LANES


def _mla_proj_body(x_ref, g_ref, win_ref, qn_ref, kvn_ref, wnope_ref, wpe_ref, wper_ref, wuk_ref,
                   cos_ref, sin_ref, kv_ref, q_ref, *, sample):
    tm = x_ref.shape[0]
    h = _rms(x_ref[...], g_ref[...]).astype(BF16)
    c = _dot(h, win_ref[...])
    cos = cos_ref[...]
    sin = sin_ref[...]
    kv_ref[:, :KV_LORA] = _rms(c[:, Q_LORA:B_KPE], kvn_ref[...])
    kv_ref[:, KV_LORA:] = (c[:, B_KPE:B_KPE + QK_ROPE] * cos[:, :QK_ROPE]
                           + c[:, B_KPE_ROT:B_KPE_ROT + QK_ROPE] * sin[:, :QK_ROPE])
    qn = _rms(c[:, :Q_LORA], qn_ref[...]).astype(BF16)
    qnope = _dot(qn, wnope_ref[...])
    n_rep = B_HEADS * QK_ROPE // LANES
    qpe = (_dot(qn, wpe_ref[...]) * jnp.concatenate([cos] * n_rep, axis=1)
           + _dot(qn, wper_ref[...]) * jnp.concatenate([sin] * n_rep, axis=1))
    for hd in range(B_HEADS):
        qlat = _dot(qnope[:, hd * QK_NOPE:(hd + 1) * QK_NOPE].astype(BF16), wuk_ref[hd])
        qrope = qpe[:, hd * QK_ROPE:(hd + 1) * QK_ROPE]
        if sample:
            l_s = q_ref.shape[1] // B_HEADS
            rs = slice(hd * l_s, (hd + 1) * l_s)
            q_ref[:, rs, :KV_LORA] = qlat.reshape(tm // l_s, l_s, KV_LORA)
            q_ref[:, rs, KV_LORA:] = qrope.reshape(tm // l_s, l_s, QK_ROPE)
        else:
            for s in range(tm // B_QBLOCK):
                ts = slice(s * B_QBLOCK, (s + 1) * B_QBLOCK)
                q_ref[s, hd, :, :KV_LORA] = qlat[ts].astype(BF16)
                q_ref[s, hd, :, KV_LORA:] = qrope[ts].astype(BF16)


def _mla_proj(x, norm, layer, w, cos, sin, row0, rows, l_s):
    sample = l_s is not None
    tm = ROW_TM
    blk0 = row0 // tm
    if sample:
        q_shape = jax.ShapeDtypeStruct((rows // l_s, B_HEADS * l_s, KV_ROW), F32)
        q_spec = pl.BlockSpec((tm // l_s, B_HEADS * l_s, KV_ROW), lambda i: (i, 0, 0))
    else:
        q_shape = jax.ShapeDtypeStruct((rows // B_QBLOCK, B_HEADS, B_QBLOCK, KV_ROW), BF16)
        q_spec = pl.BlockSpec((tm // B_QBLOCK, B_HEADS, B_QBLOCK, KV_ROW), lambda i: (i, 0, 0, 0))
    const2 = lambda i: (0, 0)
    return pl.pallas_call(
        functools.partial(_mla_proj_body, sample=sample),
        grid=(rows // tm,),
        in_specs=[
            pl.BlockSpec((tm, D_MODEL), lambda i: (i + blk0, 0)),
            _resident((None, 1, D_MODEL), lambda i: (layer, 0, 0)),
            _resident((D_MODEL, B_IN_COLS), const2),
            _resident((1, Q_LORA), const2),
            _resident((1, KV_LORA), const2),
            _resident((Q_LORA, B_HEADS * QK_NOPE), const2),
            _resident((Q_LORA, B_HEADS * QK_ROPE), const2),
            _resident((Q_LORA, B_HEADS * QK_ROPE), const2),
            _resident((B_HEADS, QK_NOPE, KV_LORA), lambda i: (0, 0, 0)),
            pl.BlockSpec((tm, LANES), lambda i: (i + blk0, 0)),
            pl.BlockSpec((tm, LANES), lambda i: (i + blk0, 0)),
        ],
        out_specs=[pl.BlockSpec((tm, KV_ROW), lambda i: (i, 0)), q_spec],
        out_shape=[jax.ShapeDtypeStruct((rows, KV_ROW), F32), q_shape],
        compiler_params=_cparams("parallel"),
        name="mla_proj_sample" if sample else "mla_proj_prompt",
    )(x, norm, w["w_in"], w["q_norm"], w["kv_norm"], w["w_nope"], w["w_pe"], w["w_pe_rot"], w["w_uk"], cos, sin)


def _softmax_update(q, kvb, m_scr, l_scr, acc_scr, valid=None):
    nk = kvb.shape[0]
    s = _dot_nt(q, kvb) * MLA_SCALE
    if valid is not None:
        s = jnp.where(valid, s, NEG_INF)
    m_prev = m_scr[...]
    m_new = jnp.maximum(m_prev, jnp.max(s, axis=-1, keepdims=True))
    alpha = jnp.exp(m_prev - m_new)
    p = jnp.exp(s - _lanes(m_new, nk))
    l_scr[...] = alpha * l_scr[...] + jnp.sum(p, axis=-1, keepdims=True)
    acc_scr[...] = acc_scr[...] * _lanes(alpha, KV_LORA) + _dot(p.astype(BF16), kvb[:, :KV_LORA])
    m_scr[...] = m_new


def _softmax_init(m_scr, l_scr, acc_scr):
    m_scr[...] = jnp.full(m_scr.shape, NEG_INF, F32)
    l_scr[...] = jnp.zeros(l_scr.shape, F32)
    acc_scr[...] = jnp.zeros(acc_scr.shape, F32)


def _mla_prompt_body(qi_ref, ki_ref, q_ref, kv_ref, x_ref, wuv_ref, wo_ref, o_ref, m_scr, l_scr, acc_scr, u_scr):
    p = pl.program_id(1)
    qi = qi_ref[p]
    ki = ki_ref[p]
    r = B_HEADS * B_QBLOCK
    q_per_kv = B_KVBLOCK // B_QBLOCK
    is_diag = ki == qi // q_per_kv

    @pl.when(ki == 0)
    def _():
        _softmax_init(m_scr, l_scr, acc_scr)

    q = q_ref[...].reshape(r, KV_ROW)
    tok = lax.broadcasted_iota(jnp.int32, (r, B_KVCHUNK), 0) & (B_QBLOCK - 1)
    key = lax.broadcasted_iota(jnp.int32, (r, B_KVCHUNK), 1)
    key_minus_tok = key - tok
    for c in range(B_KVBLOCK // B_KVCHUNK):
        def chunk(c=c):
            kvb = kv_ref[c * B_KVCHUNK:(c + 1) * B_KVCHUNK, :].astype(BF16)
            thr = qi * B_QBLOCK - (ki * B_KVBLOCK + c * B_KVCHUNK)
            _softmax_update(q, kvb, m_scr, l_scr, acc_scr, valid=key_minus_tok <= thr)

        needed = jnp.logical_or(jnp.logical_not(is_diag), (qi % q_per_kv) * B_QBLOCK >= c * B_KVCHUNK)
        pl.when(needed)(chunk)

    @pl.when(is_diag)
    def _():
        o = acc_scr[...] * _lanes(1.0 / l_scr[...], KV_LORA)
        for hd in range(B_HEADS):
            oh = o[hd * B_QBLOCK:(hd + 1) * B_QBLOCK].astype(BF16)
            u_scr[:, hd * V_HEAD:(hd + 1) * V_HEAD] = _dot(oh, wuv_ref[hd])
        o_ref[...] = x_ref[...] + _dot(u_scr[...].astype(BF16), wo_ref[...])


def _mla_prompt(x, q_cat, kv_rows, w, nb_p, t_p):
    nq = t_p // B_QBLOCK
    nkv = t_p // B_KVBLOCK
    q_per_kv = B_KVBLOCK // B_QBLOCK
    pairs = [(qi, ki) for qi in range(nq) for ki in range(qi // q_per_kv + 1)]
    qi_list = jnp.asarray([p[0] for p in pairs], jnp.int32)
    ki_list = jnp.asarray([p[1] for p in pairs], jnp.int32)
    r = B_HEADS * B_QBLOCK
    return pl.pallas_call(
        _mla_prompt_body,
        grid_spec=pltpu.PrefetchScalarGridSpec(
            num_scalar_prefetch=2,
            grid=(nb_p, len(pairs)),
            in_specs=[
                pl.BlockSpec((None, B_HEADS, B_QBLOCK, KV_ROW), lambda b, p, qi, ki: (b * nq + qi[p], 0, 0, 0)),
                pl.BlockSpec((B_KVBLOCK, KV_ROW), lambda b, p, qi, ki: (b * nkv + ki[p], 0)),
                pl.BlockSpec((B_QBLOCK, D_MODEL), lambda b, p, qi, ki: (b * nq + qi[p], 0)),
                _resident((B_HEADS, KV_LORA, V_HEAD), lambda b, p, qi, ki: (0, 0, 0)),
                _resident((B_HEADS * V_HEAD, D_MODEL), lambda b, p, qi, ki: (0, 0)),
            ],
            out_specs=pl.BlockSpec((B_QBLOCK, D_MODEL), lambda b, p, qi, ki: (b * nq + qi[p], 0)),
            scratch_shapes=[
                pltpu.VMEM((r, LANES), F32),
                pltpu.VMEM((r, LANES), F32),
                pltpu.VMEM((r, KV_LORA), F32),
                pltpu.VMEM((B_QBLOCK, B_HEADS * V_HEAD), F32),
            ],
        ),
        out_shape=jax.ShapeDtypeStruct(x.shape, F32),
        input_output_aliases={4: 0},
        compiler_params=_cparams("parallel", "arbitrary"),
        name="mla_prompt",
    )(qi_list, ki_list, q_cat, kv_rows, x, w["w_uv"], w["w_o"])


def _mla_sample_body(pt_ref, q_ref, *refs, l_s):
    page_refs = refs[:B_PAGES_PER_STEP]
    kvn_ref, o_ref, m_scr, l_scr, acc_scr = refs[B_PAGES_PER_STEP:]
    pc = pl.program_id(1)

    @pl.when(pc == 0)
    def _():
        _softmax_init(m_scr, l_scr, acc_scr)

    q = q_ref[...].astype(BF16)
    pages_per_chunk = B_KVCHUNK // PAGE_SIZE
    for c in range(B_PAGES_PER_STEP // pages_per_chunk):
        kvb = jnp.concatenate([page_refs[c * pages_per_chunk + i][...] for i in range(pages_per_chunk)], axis=0)
        _softmax_update(q, kvb.astype(BF16), m_scr, l_scr, acc_scr)

    @pl.when(pc == pl.num_programs(1) - 1)
    def _():
        r = q_ref.shape[0]
        tok = lax.broadcasted_iota(jnp.int32, (r, l_s), 0) % l_s
        key = lax.broadcasted_iota(jnp.int32, (r, l_s), 1)
        _softmax_update(q, kvn_ref[...].astype(BF16), m_scr, l_scr, acc_scr, valid=key <= tok)
        o_ref[...] = acc_scr[...] * _lanes(1.0 / l_scr[...], KV_LORA)


def _mla_sample(q_cat, kv_rows, cache, page_table, cl, row0, nb_s, l_s):
    n_pages = page_table.shape[1]
    steps = n_pages // B_PAGES_PER_STEP
    r = B_HEADS * l_s
    blk0 = row0 // l_s

    def page_spec(k):
        return pl.BlockSpec((None, None, PAGE_SIZE, KV_ROW),
                            lambda b, pc, pt: (cl, pt[b * n_pages + pc * B_PAGES_PER_STEP + k], 0, 0))

    return pl.pallas_call(
        functools.partial(_mla_sample_body, l_s=l_s),
        grid_spec=pltpu.PrefetchScalarGridSpec(
            num_scalar_prefetch=1,
            grid=(nb_s, steps),
            in_specs=[pl.BlockSpec((None, r, KV_ROW), lambda b, pc, pt: (b, 0, 0))]
            + [page_spec(k) for k in range(B_PAGES_PER_STEP)]
            + [pl.BlockSpec((l_s, KV_ROW), lambda b, pc, pt: (b + blk0, 0))],
            out_specs=pl.BlockSpec((None, r, KV_LORA), lambda b, pc, pt: (b, 0, 0)),
            scratch_shapes=[
                pltpu.VMEM((r, LANES), F32),
                pltpu.VMEM((r, LANES), F32),
                pltpu.VMEM((r, KV_LORA), F32),
            ],
        ),
        out_shape=jax.ShapeDtypeStruct((nb_s, r, KV_LORA), F32),
        compiler_params=_cparams("parallel", "arbitrary"),
        name="mla_sample",
    )(page_table.reshape(-1), q_cat, *([cache] * B_PAGES_PER_STEP), kv_rows)


def _mla_out_sample_body(o_ref, x_ref, wuv_ref, wo_ref, out_ref, u_scr, *, l_s):
    nb = o_ref.shape[0]
    for hd in range(B_HEADS):
        oh = o_ref[:, hd * l_s:(hd + 1) * l_s, :].reshape(nb * l_s, KV_LORA).astype(BF16)
        u_scr[:, hd * V_HEAD:(hd + 1) * V_HEAD] = _dot(oh, wuv_ref[hd])
    out_ref[...] = x_ref[...] + _dot(u_scr[...].astype(BF16), wo_ref[...])


def _mla_out_sample(x, o_lat, w, row0, l_s):
    nb_s = o_lat.shape[0]
    nb = ROW_TM // l_s
    blk0 = row0 // ROW_TM
    return pl.pallas_call(
        functools.partial(_mla_out_sample_body, l_s=l_s),
        grid=(nb_s // nb,),
        in_specs=[
            pl.BlockSpec((nb, B_HEADS * l_s, KV_LORA), lambda i: (i, 0, 0)),
            pl.BlockSpec((ROW_TM, D_MODEL), lambda i: (i + blk0, 0)),
            _resident((B_HEADS, KV_LORA, V_HEAD), lambda i: (0, 0, 0)),
            _resident((B_HEADS * V_HEAD, D_MODEL), lambda i: (0, 0)),
        ],
        out_specs=pl.BlockSpec((ROW_TM, D_MODEL), lambda i: (i + blk0, 0)),
        out_shape=jax.ShapeDtypeStruct(x.shape, F32),
        scratch_shapes=[pltpu.VMEM((ROW_TM, B_HEADS * V_HEAD), F32)],
        input_output_aliases={1: 0},
        compiler_params=_cparams("parallel"),
        name="mla_out_sample",
    )(o_lat, x, w["w_uv"], w["w_o"])


C_IN_COLS = D_INNER + CONV_DIM + LANES


def _mamba_in_body(x_ref, g_ref, w_ref, z_ref, xbc_ref, dt_ref):
    h = _rms(x_ref[...], g_ref[...]).astype(BF16)
    z_ref[...] = _dot(h, w_ref[:, :D_INNER])
    xbc_ref[...] = _dot(h, w_ref[:, D_INNER:D_INNER + CONV_DIM])
    dt_ref[...] = _dot(h, w_ref[:, D_INNER + CONV_DIM:])


def _mamba_in(x, norm, layer, w_in):
    rows = x.shape[0]
    return pl.pallas_call(
        _mamba_in_body,
        grid=(rows // ROW_TM,),
        in_specs=[
            pl.BlockSpec((ROW_TM, D_MODEL), lambda i: (i, 0)),
            _resident((None, 1, D_MODEL), lambda i: (layer, 0, 0)),
            _resident((D_MODEL, C_IN_COLS), lambda i: (0, 0)),
        ],
        out_specs=[
            pl.BlockSpec((ROW_TM, D_INNER), lambda i: (i, 0)),
            pl.BlockSpec((ROW_TM, CONV_DIM), lambda i: (i, 0)),
            pl.BlockSpec((ROW_TM, LANES), lambda i: (i, 0)),
        ],
        out_shape=[
            jax.ShapeDtypeStruct((rows, D_INNER), F32),
            jax.ShapeDtypeStruct((rows, CONV_DIM), F32),
            jax.ShapeDtypeStruct((rows, LANES), F32),
        ],
        compiler_params=_cparams("parallel"),
        name="mamba_in",
    )(x, norm, w_in)


def _causal_conv(ext_scr, u, cw_ref, cb_ref):
    t = u.shape[0]
    ext_scr[SUBLANES:SUBLANES + t, :] = u
    acc = cb_ref[...] + cw_ref[D_CONV - 1:D_CONV, :] * u
    for k in range(D_CONV - 1):
        lo = SUBLANES - (D_CONV - 1) + k
        acc = acc + cw_ref[k:k + 1, :] * ext_scr[lo:lo + t, :]
    return acc


def _ssd_prompt_body(xbc_ref, dt_ref, cw_ref, cb_ref, dtb_ref, alog_ref, dskip_ref, y_ref, st_ref,
                     ext_scr, ht_scr):
    c = pl.program_id(1)
    t = SSD_CHUNK

    @pl.when(c == 0)
    def _():
        ext_scr[0:SUBLANES, :] = jnp.zeros((SUBLANES, CONV_DIM), F32)
        ht_scr[...] = jnp.zeros(ht_scr.shape, F32)

    u = xbc_ref[...]
    act = _silu(_causal_conv(ext_scr, u, cw_ref, cb_ref))
    ext_scr[0:SUBLANES, :] = u[t - SUBLANES:, :]

    dt = _softplus(dt_ref[...] + dtb_ref[...])
    a = -jnp.exp(alog_ref[...])
    row = lax.broadcasted_iota(jnp.int32, (t, t), 0)
    col = lax.broadcasted_iota(jnp.int32, (t, t), 1)
    causal = col <= row
    acs = _dot_f32(causal.astype(F32), dt * a)
    acs_t = acs.T
    dt_t = dt.T
    last = acs[t - 1:t, :]
    gw = C_HPG * C_HEAD_DIM
    for g in range(C_GROUPS):
        bg = act[:, D_INNER + g * D_STATE:D_INNER + (g + 1) * D_STATE]
        cg = act[:, D_INNER + C_BC + g * D_STATE:D_INNER + C_BC + (g + 1) * D_STATE].astype(BF16)
        cb = _dot_nt(cg, bg.astype(BF16))
        bg_t = bg.T.astype(BF16)
        y_off = _dot(cg, ht_scr[:, g * gw:(g + 1) * gw].astype(BF16))
        for j in range(C_HPG):
            h = g * C_HPG + j
            hs = slice(h * C_HEAD_DIM, (h + 1) * C_HEAD_DIM)
            a_col = jnp.broadcast_to(acs[:, h:h + 1], (t, t))
            a_row = jnp.broadcast_to(acs_t[h:h + 1, :], (t, t))
            dt_row = jnp.broadcast_to(dt_t[h:h + 1, :], (t, t))
            dt_col = jnp.broadcast_to(dt[:, h:h + 1], (t, C_HEAD_DIM))
            a_last = jnp.broadcast_to(last[:, h:h + 1], (t, C_HEAD_DIM))
            a_colp = a_col[:, :C_HEAD_DIM]
            m = cb * jnp.exp(jnp.where(causal, a_col - a_row, -jnp.inf)) * dt_row
            xh = act[:, hs]
            y_ref[:, hs] = (_dot(m.astype(BF16), xh.astype(BF16))
                            + y_off[:, j * C_HEAD_DIM:(j + 1) * C_HEAD_DIM] * jnp.exp(a_colp)
                            + dskip_ref[:, hs] * xh)
            xw = (xh * (jnp.exp(a_last - a_colp) * dt_col)).astype(BF16)
            ht_scr[:, hs] = jnp.exp(a_last) * ht_scr[:, hs] + _dot(bg_t, xw)

    @pl.when(c == pl.num_programs(1) - 1)
    def _():
        st_ref[...] = ht_scr[...].T


def _ssd_prompt(xbc, dt, w, nb_p, t_p):
    nc = t_p // SSD_CHUNK
    const2 = lambda b, c: (0, 0)
    return pl.pallas_call(
        _ssd_prompt_body,
        grid=(nb_p, nc),
        in_specs=[
            pl.BlockSpec((SSD_CHUNK, CONV_DIM), lambda b, c: (b * nc + c, 0)),
            pl.BlockSpec((SSD_CHUNK, LANES), lambda b, c: (b * nc + c, 0)),
            _resident((D_CONV, CONV_DIM), const2),
            _resident((1, CONV_DIM), const2),
            _resident((1, LANES), const2),
            _resident((1, LANES), const2),
            _resident((1, D_INNER), const2),
        ],
        out_specs=[
            pl.BlockSpec((SSD_CHUNK, D_INNER), lambda b, c: (b * nc + c, 0)),
            pl.BlockSpec((None, D_INNER, D_STATE), lambda b, c: (b, 0, 0)),
        ],
        out_shape=[
            jax.ShapeDtypeStruct((nb_p * t_p, D_INNER), F32),
            jax.ShapeDtypeStruct((nb_p, D_INNER, D_STATE), F32),
        ],
        scratch_shapes=[
            pltpu.VMEM((SUBLANES + SSD_CHUNK, CONV_DIM), F32),
            pltpu.VMEM((D_STATE, D_INNER), F32),
        ],
        compiler_params=_cparams("parallel", "arbitrary"),
        name="ssd_prompt",
    )(xbc, dt, w["conv_w"], w["conv_b"], w["dt_bias"], w["a_log"], w["d_skip"])


def _ssd_sample_body(xbc_ref, dtx_ref, cprev_ref, h0_ref, cw_ref, cb_ref, dtb_ref, alog_ref, dskip_ref,
                     y_ref, st_ref, ext_scr, *, l_s):
    u = xbc_ref[...]
    ext_scr[SUBLANES - (D_CONV - 1):SUBLANES, :] = cprev_ref[...]
    act = _silu(_causal_conv(ext_scr, u, cw_ref, cb_ref))
    xs = act[:, :D_INNER]

    dt = _softplus(dtx_ref[...] + dtb_ref[...])
    a = -jnp.exp(alog_ref[...])
    row = lax.broadcasted_iota(jnp.int32, (l_s, l_s), 0)
    col = lax.broadcasted_iota(jnp.int32, (l_s, l_s), 1)
    acs = _dot_f32((col <= row).astype(F32), dt * a)
    last = acs[l_s - 1:l_s, :]
    xdt = xs * dt
    xw = xdt * jnp.exp(last - acs)
    decay_col = jnp.broadcast_to(jnp.exp(last), (LANES, D_INNER)).T
    gw = C_HPG * C_HEAD_DIM
    lrow = lax.broadcasted_iota(jnp.int32, (l_s, gw), 0)
    for g in range(C_GROUPS):
        gs = slice(g * gw, (g + 1) * gw)
        bg = act[:, D_INNER + g * D_STATE:D_INNER + (g + 1) * D_STATE]
        cg = act[:, D_INNER + C_BC + g * D_STATE:D_INNER + C_BC + (g + 1) * D_STATE]
        cb = _dot_nt(cg.astype(BF16), bg.astype(BF16))
        h0 = h0_ref[gs, :]
        y = _dot_nt(cg.astype(BF16), h0.astype(BF16)) * jnp.exp(acs[:, gs])
        for s in range(l_s):
            seg = acs[:, gs] - acs[s:s + 1, gs]
            w_s = jnp.exp(jnp.where(lrow >= s, seg, -jnp.inf)) * jnp.broadcast_to(cb[:, s:s + 1], (l_s, gw))
            y = y + w_s * xdt[s:s + 1, gs]
        y_ref[:, gs] = y + dskip_ref[:, gs] * xs[:, gs]
        st_ref[gs, :] = decay_col[gs, :] * h0 + _dot_tn(xw[:, gs].astype(BF16), bg.astype(BF16))


def _ssd_sample(xbc, dtx, conv_prev, h0, sl, w, row0, nb_s, l_s):
    blk0 = row0 // l_s
    const2 = lambda b: (0, 0)
    return pl.pallas_call(
        functools.partial(_ssd_sample_body, l_s=l_s),
        grid=(nb_s,),
        in_specs=[
            pl.BlockSpec((l_s, CONV_DIM), lambda b: (b + blk0, 0)),
            pl.BlockSpec((l_s, D_INNER), lambda b: (b, 0)),
            pl.BlockSpec((None, None, D_CONV - 1, CONV_DIM), lambda b: (sl, b, 0, 0)),
            pl.BlockSpec((None, None, D_INNER, D_STATE), lambda b: (sl, b, 0, 0)),
            _resident((D_CONV, CONV_DIM), const2),
            _resident((1, CONV_DIM), const2),
            _resident((1, D_INNER), const2),
            _resident((1, D_INNER), const2),
            _resident((1, D_INNER), const2),
        ],
        out_specs=[
            pl.BlockSpec((l_s, D_INNER), lambda b: (b, 0)),
            pl.BlockSpec((None, D_INNER, D_STATE), lambda b: (b, 0, 0)),
        ],
        out_shape=[
            jax.ShapeDtypeStruct((nb_s * l_s, D_INNER), F32),
            jax.ShapeDtypeStruct((nb_s, D_INNER, D_STATE), F32),
        ],
        scratch_shapes=[pltpu.VMEM((SUBLANES + l_s, CONV_DIM), F32)],
        compiler_params=_cparams("parallel"),
        name="ssd_sample",
    )(xbc, dtx, conv_prev, h0, w["conv_w"], w["conv_b"], w["dt_bias_x"], w["a_log_x"], w["d_skip"])


def _mamba_out_body(y_ref, z_ref, x_ref, ng_ref, wo_ref, o_ref):
    yz = y_ref[...] * _silu(z_ref[...])
    gw = D_INNER // C_GROUPS
    parts = []
    for g in range(C_GROUPS):
        part = yz[:, g * gw:(g + 1) * gw]
        parts.append(part * lax.rsqrt(jnp.mean(part * part, axis=-1, keepdims=True) + EPS))
    yn = jnp.concatenate(parts, axis=1) * ng_ref[...]
    o_ref[...] = x_ref[...] + _dot(yn.astype(BF16), wo_ref[...])


def _mamba_out(x, y, z, norm_g, w_o, row0):
    rows = y.shape[0]
    blk0 = row0 // ROW_TM
    return pl.pallas_call(
        _mamba_out_body,
        grid=(rows // ROW_TM,),
        in_specs=[
            pl.BlockSpec((ROW_TM, D_INNER), lambda i: (i, 0)),
            pl.BlockSpec((ROW_TM, D_INNER), lambda i: (i + blk0, 0)),
            pl.BlockSpec((ROW_TM, D_MODEL), lambda i: (i + blk0, 0)),
            _resident((1, D_INNER), lambda i: (0, 0)),
            _resident((D_INNER, D_MODEL), lambda i: (0, 0)),
        ],
        out_specs=pl.BlockSpec((ROW_TM, D_MODEL), lambda i: (i + blk0, 0)),
        out_shape=jax.ShapeDtypeStruct(x.shape, F32),
        input_output_aliases={2: 0},
        compiler_params=_cparams("parallel"),
        name="mamba_out",
    )(y, z, x, norm_g, w_o)


def _rope_tables(pos):
    half = QK_ROPE // 2
    inv = jnp.power(ROPE_THETA, -jnp.arange(half, dtype=F32) / half)
    ang = pos.astype(F32)[:, None] * inv
    reps = LANES // half
    return jnp.tile(jnp.cos(ang), (1, reps)), jnp.tile(jnp.sin(ang), (1, reps))


def _rot_cols(w):
    half = w.shape[-1] // 2
    return jnp.concatenate([-w[..., half:], w[..., :half]], axis=-1)


def _mla_weights(w_in, q_norm, w_uq, kv_norm, w_ukv, w_o):
    kpe = w_in[:, B_KPE:]
    pad = jnp.zeros((D_MODEL, LANES - QK_ROPE), w_in.dtype)
    w_in_aug = jnp.concatenate([w_in[:, :B_KPE], kpe, pad, _rot_cols(kpe), pad], axis=1)
    uq = w_uq.reshape(Q_LORA, B_HEADS, QK_NOPE + QK_ROPE)
    pe = uq[:, :, QK_NOPE:]
    ukv = w_ukv.reshape(KV_LORA, B_HEADS, QK_NOPE + V_HEAD)
    return {
        "w_in": w_in_aug.astype(BF16),
        "q_norm": q_norm.reshape(1, Q_LORA),
        "kv_norm": kv_norm.reshape(1, KV_LORA),
        "w_nope": uq[:, :, :QK_NOPE].reshape(Q_LORA, B_HEADS * QK_NOPE).astype(BF16),
        "w_pe": pe.reshape(Q_LORA, B_HEADS * QK_ROPE).astype(BF16),
        "w_pe_rot": _rot_cols(pe).reshape(Q_LORA, B_HEADS * QK_ROPE).astype(BF16),
        "w_uk": jnp.transpose(ukv[:, :, :QK_NOPE], (1, 2, 0)).astype(BF16),
        "w_uv": jnp.transpose(ukv[:, :, QK_NOPE:], (1, 0, 2)).astype(BF16),
        "w_o": w_o.astype(BF16),
    }


def _pad_lanes(v):
    return jnp.pad(v, (0, LANES - v.shape[0])).reshape(1, LANES)


def _expand_heads(v):
    return jnp.repeat(v, C_HEAD_DIM).reshape(1, D_INNER)


def _mamba_weights(w_in, conv_w, conv_b, dt_bias, a_log, d_skip, norm_g, w_o):
    w_dt = w_in[:, D_INNER + CONV_DIM:]
    pad = jnp.zeros((D_MODEL, LANES - C_HEADS), w_in.dtype)
    return {
        "w_in": jnp.concatenate([w_in, pad], axis=1).astype(BF16),
        "w_dt_x": jnp.repeat(w_dt, C_HEAD_DIM, axis=1).astype(BF16)[None],
        "conv_w": conv_w,
        "conv_b": conv_b.reshape(1, CONV_DIM),
        "dt_bias": _pad_lanes(dt_bias.astype(F32)),
        "a_log": _pad_lanes(a_log.astype(F32)),
        "dt_bias_x": _expand_heads(dt_bias.astype(F32)),
        "a_log_x": _expand_heads(a_log.astype(F32)),
        "d_skip": _expand_heads(d_skip),
        "norm_g": norm_g.reshape(1, D_INNER),
        "w_o": w_o.astype(BF16),
    }


def kernel(x_prompt, x_sample, state_swa_k, state_swa_v, cache_mla, state_ssm, state_conv, page_table, p_prompt, p_sample, norm_ffn1, ffn1_w_gu, ffn1_w_down, norm_mix, norm_ffn2, ffn2_w_gu, ffn2_w_down, norm_ple, ple_w_proj, ple_w_gate, rel_bias_table, a_w_qkv, a_w_o, a_sinks, b_w_in, b_q_norm, b_w_uq, b_kv_norm, b_w_ukv, b_w_o, c_w_in, c_conv_w, c_conv_b, c_dt_bias, c_a_log, c_d, c_norm, c_w_o, final_norm):
    nb_p, t_p, _ = x_prompt.shape
    nb_s, l_s, _ = x_sample.shape
    depth = norm_ffn1.shape[0]
    rows_p = nb_p * t_p
    rows_s = nb_s * l_s
    w_buf = state_swa_k.shape[2]
    assert rows_p % FFN_TM == 0 and (rows_p + rows_s) % FFN_TM == 0 and rows_s % ROW_TM == 0
    assert t_p % B_KVBLOCK == 0 and t_p % SSD_CHUNK == 0 and t_p % A_BLOCK == 0
    assert l_s == SUBLANES and w_buf == WINDOW and page_table.shape[1] % B_PAGES_PER_STEP == 0

    x = jnp.concatenate([x_prompt.reshape(rows_p, D_MODEL), x_sample.reshape(rows_s, D_MODEL)], axis=0)
    p_all = jnp.concatenate([p_prompt.reshape(depth, rows_p, D_PLE), p_sample.reshape(depth, rows_s, D_PLE)], axis=1)

    def norm3(g):
        return g.reshape(g.shape[0], 1, g.shape[1])

    norm_ffn1, norm_mix, norm_ffn2, norm_ple = map(norm3, (norm_ffn1, norm_mix, norm_ffn2, norm_ple))
    ffn1_w_gu, ffn1_w_down, ffn2_w_gu, ffn2_w_down, ple_w_proj, ple_w_gate, a_w_qkv, a_w_o = (
        w.astype(BF16) for w in (ffn1_w_gu, ffn1_w_down, ffn2_w_gu, ffn2_w_down, ple_w_proj, ple_w_gate, a_w_qkv, a_w_o))
    final_norm = final_norm.reshape(1, D_MODEL)

    i = jnp.arange(A_BLOCK)[:, None]
    j = jnp.arange(2 * A_BLOCK)[None, :]
    bias_p = _rel_bias(rel_bias_table, _t5_bucket(A_BLOCK + i - j))
    q_pos = PAST_LEN + jnp.arange(l_s)
    k_pos = jnp.concatenate([PAST_LEN - w_buf + jnp.arange(w_buf), q_pos])
    bias_s = _rel_bias(rel_bias_table, _t5_bucket(q_pos[:, None] - k_pos[None, :]))
    swa_k_buf = state_swa_k.reshape(state_swa_k.shape[0], nb_s, w_buf, A_KV)
    swa_v_buf = state_swa_v.reshape(state_swa_v.shape[0], nb_s, w_buf, A_KV)

    pos = jnp.concatenate([jnp.tile(jnp.arange(t_p), nb_p), jnp.tile(PAST_LEN + jnp.arange(l_s), nb_s)])
    cos, sin = _rope_tables(pos)

    ssm_flat = state_ssm.reshape(state_ssm.shape[0], nb_s, D_INNER, D_STATE)

    swa_k_p, swa_v_p, mla_p, ssm_p, conv_p = [], [], [], [], []
    swa_k_s, swa_v_s, mla_s, ssm_s, conv_s = [], [], [], [], []
    for layer in range(depth):
        kind, sl = layer % N_MIXERS, layer // N_MIXERS
        x = _ffn(x, norm_ffn1, ffn1_w_gu, ffn1_w_down, layer)
        if kind == 0:
            qkv = _norm_proj(x, norm_mix, a_w_qkv, layer, sl, "swa_qkv")
            x = _swa_prompt(x, qkv, bias_p, a_sinks[sl], a_w_o, sl, nb_p, t_p)
            o_s = _swa_sample(qkv, swa_k_buf, swa_v_buf, bias_s, a_sinks[sl], sl, rows_p, nb_s, l_s)
            x = _res_proj(x, o_s, a_w_o, sl, rows_p, "swa_out_sample")
            kv_p = qkv[:rows_p, A_Q:].reshape(nb_p, t_p, 2, A_KV_HEADS, A_HEAD_DIM)[:, t_p - w_buf:]
            kv_s = qkv[rows_p:, A_Q:].reshape(nb_s, l_s, 2, A_KV_HEADS, A_HEAD_DIM)
            swa_k_p.append(kv_p[:, :, 0])
            swa_v_p.append(kv_p[:, :, 1])
            swa_k_s.append(jnp.concatenate([state_swa_k[sl], kv_s[:, :, 0]], axis=1)[:, l_s:])
            swa_v_s.append(jnp.concatenate([state_swa_v[sl], kv_s[:, :, 1]], axis=1)[:, l_s:])
        elif kind == 1:
            w = _mla_weights(b_w_in[sl], b_q_norm[sl], b_w_uq[sl], b_kv_norm[sl], b_w_ukv[sl], b_w_o[sl])
            kv_rows_p, q_p = _mla_proj(x, norm_mix, layer, w, cos, sin, 0, rows_p, None)
            kv_rows_s, q_s = _mla_proj(x, norm_mix, layer, w, cos, sin, rows_p, rows_s, l_s)
            x = _mla_prompt(x, q_p, kv_rows_p, w, nb_p, t_p)
            o_lat = _mla_sample(q_s, kv_rows_s, cache_mla, page_table, sl, 0, nb_s, l_s)
            x = _mla_out_sample(x, o_lat, w, rows_p, l_s)
            mla_p.append(kv_rows_p.reshape(nb_p, t_p, KV_ROW))
            mla_s.append(kv_rows_s.reshape(nb_s, l_s, KV_ROW))
        else:
            w = _mamba_weights(c_w_in[sl], c_conv_w[sl], c_conv_b[sl], c_dt_bias[sl], c_a_log[sl], c_d[sl],
                               c_norm[sl], c_w_o[sl])
            z, xbc, dt = _mamba_in(x, norm_mix, layer, w["w_in"])
            dtx = _norm_proj(x[rows_p:], norm_mix, w["w_dt_x"], layer, 0, "mamba_dt_sample")
            y_p, st_p = _ssd_prompt(xbc, dt, w, nb_p, t_p)
            y_s, st_s = _ssd_sample(xbc, dtx, state_conv, ssm_flat, sl, w, rows_p, nb_s, l_s)
            x = _mamba_out(x, y_p, z, w["norm_g"], w["w_o"], 0)
            x = _mamba_out(x, y_s, z, w["norm_g"], w["w_o"], rows_p)
            ssm_p.append(st_p.reshape(nb_p, C_HEADS, C_HEAD_DIM, D_STATE))
            ssm_s.append(st_s.reshape(nb_s, C_HEADS, C_HEAD_DIM, D_STATE))
            conv_p.append(xbc[:rows_p].reshape(nb_p, t_p, CONV_DIM)[:, t_p - (D_CONV - 1):])
            conv_s.append(jnp.concatenate([state_conv[sl], xbc[rows_p:].reshape(nb_s, l_s, CONV_DIM)],
                                          axis=1)[:, -(D_CONV - 1):])
        x = _ffn(x, norm_ffn2, ffn2_w_gu, ffn2_w_down, layer)
        x = _ple(x, p_all, norm_ple, ple_w_proj, ple_w_gate, final_norm, layer, layer == depth - 1)

    return (x[:rows_p].reshape(nb_p, t_p, D_MODEL), x[rows_p:].reshape(nb_s, l_s, D_MODEL),
            jnp.stack(swa_k_p), jnp.stack(swa_v_p), jnp.stack(mla_p), jnp.stack(ssm_p), jnp.stack(conv_p),
            jnp.stack(swa_k_s), jnp.stack(swa_v_s), jnp.stack(mla_s), jnp.stack(ssm_s), jnp.stack(conv_s))
```

```python
import functools
import math

import jax
import jax.numpy as jnp
from jax import lax
from jax.experimental import pallas as pl
from jax.experimental.pallas import tpu as pltpu

F32 = jnp.float32
BF16 = jnp.bfloat16

D_MODEL = 1024
D_FF = 2816
D_PLE = 256
EPS = 1e-6
NEG_INF = -1e30
PAST_LEN = 8192
PAGE_SIZE = 128
N_MIXERS = 3

A_HEADS = 16
A_KV_HEADS = 4
A_HEAD_DIM = 64
A_GROUP = A_HEADS // A_KV_HEADS
A_Q = A_HEADS * A_HEAD_DIM
A_KV = A_KV_HEADS * A_HEAD_DIM
WINDOW = 128
A_BLOCK = 128
N_BUCKETS = 32
BUCKET_MAX_DIST = 128

B_HEADS = 16
Q_LORA = 768
KV_LORA = 256
QK_NOPE = 64
QK_ROPE = 32
V_HEAD = 64
KV_ROW = KV_LORA + QK_ROPE
ROPE_THETA = 10000.0
MLA_SCALE = (QK_NOPE + QK_ROPE) ** -0.5
B_QBLOCK = 128
B_KVBLOCK = 512
B_KVCHUNK = 256
B_PAGES_PER_STEP = 32

D_INNER = 2 * D_MODEL
C_HEAD_DIM = 64
C_HEADS = D_INNER // C_HEAD_DIM
C_GROUPS = 4
C_HPG = C_HEADS // C_GROUPS
D_STATE = 128
D_CONV = 4
C_BC = C_GROUPS * D_STATE
CONV_DIM = D_INNER + 2 * C_BC
SSD_CHUNK = 128

LANES = 128
SUBLANES = 8
VMEM_LIMIT = 56 * 1024 * 1024


def _cparams(*sem):
    return pltpu.CompilerParams(dimension_semantics=sem, vmem_limit_bytes=VMEM_LIMIT)


def _resident(shape, index_map):
    return pl.BlockSpec(shape, index_map, pipeline_mode=pl.Buffered(1))


def _dot(a, b):
    return jnp.dot(a, b, preferred_element_type=F32)


def _dot_nt(a, b):
    return lax.dot_general(a, b, (((1,), (1,)), ((), ())), preferred_element_type=F32)


def _dot_tn(a, b):
    return lax.dot_general(a, b, (((0,), (0,)), ((), ())), preferred_element_type=F32)


def _dot_f32(a, b):
    return jnp.dot(a, b, preferred_element_type=F32, precision=lax.Precision.HIGHEST)


def _rms(x, g):
    r = lax.rsqrt(jnp.mean(x * x, axis=-1, keepdims=True) + EPS)
    return (x * r) * g


def _silu(x):
    return x * jax.nn.sigmoid(x)


def _softplus(x):
    return jnp.maximum(x, 0.0) + jnp.log1p(jnp.exp(-jnp.abs(x)))


def _lanes(a, n):
    if n <= LANES:
        return a[:, :n]
    return jnp.concatenate([a] * (n // LANES), axis=1)


FFN_TM = 512
FFN_TF = 256


def _ffn_body(x_ref, g_ref, wgu_ref, wd_ref, o_ref):
    x = x_ref[...]
    h = _rms(x, g_ref[...]).astype(BF16)
    acc = jnp.zeros(x.shape, F32)
    for j in range(D_FF // FFN_TF):
        lo = j * FFN_TF
        gate = _dot(h, wgu_ref[:, lo:lo + FFN_TF])
        up = _dot(h, wgu_ref[:, D_FF + lo:D_FF + lo + FFN_TF])
        act = (_silu(gate) * up).astype(BF16)
        acc = acc + _dot(act, wd_ref[lo:lo + FFN_TF, :])
    o_ref[...] = x + 0.5 * acc


def _ffn(x, norm, w_gu, w_down, layer):
    rows = x.shape[0]
    return pl.pallas_call(
        _ffn_body,
        grid=(rows // FFN_TM,),
        in_specs=[
            pl.BlockSpec((FFN_TM, D_MODEL), lambda i: (i, 0)),
            _resident((None, 1, D_MODEL), lambda i: (layer, 0, 0)),
            _resident((None, D_MODEL, 2 * D_FF), lambda i: (layer, 0, 0)),
            _resident((None, D_FF, D_MODEL), lambda i: (layer, 0, 0)),
        ],
        out_specs=pl.BlockSpec((FFN_TM, D_MODEL), lambda i: (i, 0)),
        out_shape=jax.ShapeDtypeStruct(x.shape, F32),
        compiler_params=_cparams("parallel"),
        name="ffn",
    )(x, norm, w_gu, w_down)


ROW_TM = 256


def _ple_body(x_ref, p_ref, g_ref, wp_ref, wg_ref, gf_ref, o_ref, *, final):
    x = x_ref[...]
    gate = _dot(_rms(x, g_ref[...]).astype(BF16), wg_ref[...])
    proj = _dot(p_ref[...].astype(BF16), wp_ref[...])
    y = x + proj * jax.nn.sigmoid(gate)
    if final:
        y = _rms(y, gf_ref[...])
    o_ref[...] = y


def _ple(x, p_all, norm, w_proj, w_gate, final_norm, layer, final, row0=0, rows=None):
    rows = x.shape[0] if rows is None else rows
    blk0 = row0 // ROW_TM
    return pl.pallas_call(
        functools.partial(_ple_body, final=final),
        grid=(rows // ROW_TM,),
        in_specs=[
            pl.BlockSpec((ROW_TM, D_MODEL), lambda i: (i + blk0, 0)),
            pl.BlockSpec((None, ROW_TM, D_PLE), lambda i: (layer, i + blk0, 0)),
            _resident((None, 1, D_MODEL), lambda i: (layer, 0, 0)),
            _resident((None, D_PLE, D_MODEL), lambda i: (layer, 0, 0)),
            _resident((None, D_MODEL, D_MODEL), lambda i: (layer, 0, 0)),
            _resident((1, D_MODEL), lambda i: (0, 0)),
        ],
        out_specs=pl.BlockSpec((ROW_TM, D_MODEL), lambda i: (i, 0)),
        out_shape=jax.ShapeDtypeStruct((rows, D_MODEL), F32),
        compiler_params=_cparams("parallel"),
        name="ple",
    )(x, p_all, norm, w_proj, w_gate, final_norm)


def _norm_proj_body(x_ref, g_ref, w_ref, o_ref):
    o_ref[...] = _dot(_rms(x_ref[...], g_ref[...]).astype(BF16), w_ref[...])


def _norm_proj(x, norm, w, layer, wl, name):
    rows = x.shape[0]
    k, n = w.shape[1:]
    return pl.pallas_call(
        _norm_proj_body,
        grid=(rows // ROW_TM,),
        in_specs=[
            pl.BlockSpec((ROW_TM, k), lambda i: (i, 0)),
            _resident((None, 1, k), lambda i: (layer, 0, 0)),
            _resident((None, k, n), lambda i: (wl, 0, 0)),
        ],
        out_specs=pl.BlockSpec((ROW_TM, n), lambda i: (i, 0)),
        out_shape=jax.ShapeDtypeStruct((rows, n), F32),
        compiler_params=_cparams("parallel"),
        name=name,
    )(x, norm, w)


def _res_proj_body(x_ref, a_ref, w_ref, o_ref):
    o_ref[...] = x_ref[...] + _dot(a_ref[...].astype(BF16), w_ref[...])


def _res_proj(x, a, w, wl, row0, name):
    rows = a.shape[0]
    k, n = w.shape[1:]
    blk0 = row0 // ROW_TM
    return pl.pallas_call(
        _res_proj_body,
        grid=(rows // ROW_TM,),
        in_specs=[
            pl.BlockSpec((ROW_TM, n), lambda i: (i + blk0, 0)),
            pl.BlockSpec((ROW_TM, k), lambda i: (i, 0)),
            _resident((None, k, n), lambda i: (wl, 0, 0)),
        ],
        out_specs=pl.BlockSpec((ROW_TM, n), lambda i: (i + blk0, 0)),
        out_shape=jax.ShapeDtypeStruct(x.shape, F32),
        input_output_aliases={0: 0},
        compiler_params=_cparams("parallel"),
        name=name,
    )(x, a, w)


def _t5_bucket(dist):
    max_exact = N_BUCKETS // 2
    d = jnp.maximum(dist, 0)
    large = max_exact + (jnp.log(jnp.maximum(d, 1).astype(F32) / max_exact)
                         / math.log(BUCKET_MAX_DIST / max_exact) * (N_BUCKETS - max_exact)).astype(jnp.int32)
    large = jnp.minimum(large, N_BUCKETS - 1)
    return jnp.where(d < max_exact, d, large)


def _bias_body(tab_ref, bm_ref, o_ref):
    h = pl.program_id(0)
    bm = bm_ref[...]
    acc = jnp.zeros(bm.shape, F32)
    for b in range(N_BUCKETS):
        acc = jnp.where(bm == b, tab_ref[b, h], acc)
    o_ref[...] = acc


def _rel_bias(table, bucket_map):
    r, c = bucket_map.shape
    return pl.pallas_call(
        _bias_body,
        grid=(A_HEADS,),
        in_specs=[pl.BlockSpec(memory_space=pltpu.SMEM), pl.BlockSpec((r, c), lambda h: (0, 0))],
        out_specs=pl.BlockSpec((None, r, c), lambda h: (h, 0, 0)),
        out_shape=jax.ShapeDtypeStruct((A_HEADS, r, c), F32),
        compiler_params=_cparams("parallel"),
        name="rel_bias",
    )(table, bucket_map)


def _swa_heads(q_ref, k, v, bias_ref, sink_ref, valid, o_scr):
    scale = A_HEAD_DIM ** -0.5
    for g in range(A_KV_HEADS):
        kg = k[:, g * A_HEAD_DIM:(g + 1) * A_HEAD_DIM].astype(BF16)
        vg = v[:, g * A_HEAD_DIM:(g + 1) * A_HEAD_DIM].astype(BF16)
        for hh in range(A_GROUP):
            h = g * A_GROUP + hh
            hs = slice(h * A_HEAD_DIM, (h + 1) * A_HEAD_DIM)
            s = _dot_nt(q_ref[:, hs].astype(BF16), kg) * scale + bias_ref[h]
            s = jnp.where(valid, s, NEG_INF)
            sink = sink_ref[h]
            m = jnp.maximum(jnp.max(s, axis=-1, keepdims=True), sink)
            e = jnp.exp(s - m)
            den = jnp.sum(e, axis=-1, keepdims=True) + jnp.exp(sink - m)
            o_scr[:, hs] = _dot((e / den).astype(BF16), vg)


def _swa_prompt_body(sink_ref, q_ref, kp_ref, ko_ref, vp_ref, vo_ref, bias_ref, x_ref, wo_ref, o_ref, o_scr):
    n = pl.program_id(1)
    row = lax.broadcasted_iota(jnp.int32, (A_BLOCK, A_BLOCK), 0)
    col = lax.broadcasted_iota(jnp.int32, (A_BLOCK, A_BLOCK), 1)
    valid = jnp.concatenate([col > row + jnp.where(n > 0, 0, A_BLOCK), col <= row], axis=1)
    k = jnp.concatenate([kp_ref[...], ko_ref[...]], axis=0)
    v = jnp.concatenate([vp_ref[...], vo_ref[...]], axis=0)
    _swa_heads(q_ref, k, v, bias_ref, sink_ref, valid, o_scr)
    o_ref[...] = x_ref[...] + _dot(o_scr[...].astype(BF16), wo_ref[...])


def _swa_prompt(x, qkv, bias, sinks, w_o, wl, nb_p, t_p):
    nblk = t_p // A_BLOCK
    kcol = A_Q // A_KV
    vcol = kcol + 1

    def own(b, n):
        return b * nblk + n

    def prev(b, n):
        return b * nblk + jnp.maximum(n - 1, 0)

    return pl.pallas_call(
        _swa_prompt_body,
        grid=(nb_p, nblk),
        in_specs=[
            pl.BlockSpec(memory_space=pltpu.SMEM),
            pl.BlockSpec((A_BLOCK, A_Q), lambda b, n: (own(b, n), 0)),
            pl.BlockSpec((A_BLOCK, A_KV), lambda b, n: (prev(b, n), kcol)),
            pl.BlockSpec((A_BLOCK, A_KV), lambda b, n: (own(b, n), kcol)),
            pl.BlockSpec((A_BLOCK, A_KV), lambda b, n: (prev(b, n), vcol)),
            pl.BlockSpec((A_BLOCK, A_KV), lambda b, n: (own(b, n), vcol)),
            _resident((A_HEADS, A_BLOCK, 2 * A_BLOCK), lambda b, n: (0, 0, 0)),
            pl.BlockSpec((A_BLOCK, D_MODEL), lambda b, n: (own(b, n), 0)),
            _resident((None, A_Q, D_MODEL), lambda b, n: (wl, 0, 0)),
        ],
        out_specs=pl.BlockSpec((A_BLOCK, D_MODEL), lambda b, n: (own(b, n), 0)),
        out_shape=jax.ShapeDtypeStruct(x.shape, F32),
        scratch_shapes=[pltpu.VMEM((A_BLOCK, A_Q), F32)],
        input_output_aliases={7: 0},
        compiler_params=_cparams("parallel", "arbitrary"),
        name="swa_prompt",
    )(sinks, qkv, qkv, qkv, qkv, qkv, bias, x, w_o)


SWA_S_NB = 8


def _swa_sample_body(qkv_ref, kt_ref, vt_ref, bb_ref, bn_ref, sink_ref, o_ref, *, l_s, w_buf):
    r = A_GROUP * l_s
    scale = A_HEAD_DIM ** -0.5
    tok_b = lax.broadcasted_iota(jnp.int32, (r, w_buf), 0) % l_s
    key_b = lax.broadcasted_iota(jnp.int32, (r, w_buf), 1)
    valid_b = w_buf + tok_b - key_b < WINDOW
    tok_n = lax.broadcasted_iota(jnp.int32, (r, l_s), 0) % l_s
    key_n = lax.broadcasted_iota(jnp.int32, (r, l_s), 1)
    valid_n = key_n <= tok_n

    def one(b, carry):
        rows = pl.ds(pl.multiple_of(b * l_s, l_s), l_s)
        for g in range(A_KV_HEADS):
            heads = range(g * A_GROUP, (g + 1) * A_GROUP)
            qg = jnp.concatenate([qkv_ref[rows, h * A_HEAD_DIM:(h + 1) * A_HEAD_DIM] for h in heads], axis=0)
            qg = qg.astype(BF16)
            kn = qkv_ref[rows, A_Q + g * A_HEAD_DIM:A_Q + (g + 1) * A_HEAD_DIM].astype(BF16)
            vn = qkv_ref[rows, A_Q + A_KV + g * A_HEAD_DIM:A_Q + A_KV + (g + 1) * A_HEAD_DIM].astype(BF16)
            sb = jnp.where(valid_b, _dot(qg, kt_ref[b, g].astype(BF16)) * scale + bb_ref[g], NEG_INF)
            sn = jnp.where(valid_n, _dot_nt(qg, kn) * scale + bn_ref[g], NEG_INF)
            sink = sink_ref[g][:, :1]
            m = jnp.maximum(jnp.maximum(jnp.max(sb, axis=-1, keepdims=True), jnp.max(sn, axis=-1, keepdims=True)), sink)
            eb = jnp.exp(sb - m)
            en = jnp.exp(sn - m)
            den = jnp.sum(eb, axis=-1, keepdims=True) + jnp.sum(en, axis=-1, keepdims=True) + jnp.exp(sink - m)
            og = _dot_nt((eb / den).astype(BF16), vt_ref[b, g].astype(BF16)) + _dot((en / den).astype(BF16), vn)
            for i, h in enumerate(heads):
                o_ref[rows, h * A_HEAD_DIM:(h + 1) * A_HEAD_DIM] = og[i * l_s:(i + 1) * l_s]
        return carry

    lax.fori_loop(0, SWA_S_NB, one, 0, unroll=2)


def _swa_sample(qkv, kt_buf, vt_buf, bias, sinks, sl, row0, nb_s, l_s):
    w_buf = kt_buf.shape[-1]
    r = A_GROUP * l_s
    tm = SWA_S_NB * l_s
    blk0 = row0 // tm
    bias_g = bias.reshape(A_KV_HEADS, r, w_buf + l_s)
    sink_g = jnp.broadcast_to(sinks.reshape(A_KV_HEADS, A_GROUP, 1, 1),
                              (A_KV_HEADS, A_GROUP, l_s, LANES)).reshape(A_KV_HEADS, r, LANES)
    buf_spec = pl.BlockSpec((None, SWA_S_NB, A_KV_HEADS, A_HEAD_DIM, w_buf), lambda i: (sl, i, 0, 0, 0))
    const3 = lambda i: (0, 0, 0)
    return pl.pallas_call(
        functools.partial(_swa_sample_body, l_s=l_s, w_buf=w_buf),
        grid=(nb_s // SWA_S_NB,),
        in_specs=[
            pl.BlockSpec((tm, A_Q + 2 * A_KV), lambda i: (i + blk0, 0)),
            buf_spec,
            buf_spec,
            _resident((A_KV_HEADS, r, w_buf), const3),
            _resident((A_KV_HEADS, r, l_s), const3),
            _resident((A_KV_HEADS, r, LANES), const3),
        ],
        out_specs=pl.BlockSpec((tm, A_Q), lambda i: (i, 0)),
        out_shape=jax.ShapeDtypeStruct((nb_s * l_s, A_Q), F32),
        compiler_params=_cparams("parallel"),
        name="swa_sample",
    )(qkv, kt_buf, vt_buf, bias_g[:, :, :w_buf], bias_g[:, :, w_buf:], sink_g)


B_IN_COLS = Q_LORA + KV_LORA + 2 * LANES
B_KPE = Q_LORA + KV_LORA
B_KPE_ROT = B_KPE + LANES


def _mla_proj_body(x_ref, g_ref, win_ref, qn_ref, kvn_ref, wnope_ref, wpe_ref, wper_ref, wuk_ref,
                   cos_ref, sin_ref, kv_ref, q_ref, *, sample):
    tm = x_ref.shape[0]
    h = _rms(x_ref[...], g_ref[...]).astype(BF16)
    c = _dot(h, win_ref[...])
    cos = cos_ref[...]
    sin = sin_ref[...]
    kv_ref[:, :KV_LORA] = _rms(c[:, Q_LORA:B_KPE], kvn_ref[...])
    kv_ref[:, KV_LORA:] = (c[:, B_KPE:B_KPE + QK_ROPE] * cos[:, :QK_ROPE]
                           + c[:, B_KPE_ROT:B_KPE_ROT + QK_ROPE] * sin[:, :QK_ROPE])
    qn = _rms(c[:, :Q_LORA], qn_ref[...]).astype(BF16)
    qnope = _dot(qn, wnope_ref[...])
    n_rep = B_HEADS * QK_ROPE // LANES
    qpe = (_dot(qn, wpe_ref[...]) * jnp.concatenate([cos] * n_rep, axis=1)
           + _dot(qn, wper_ref[...]) * jnp.concatenate([sin] * n_rep, axis=1))
    for hd in range(B_HEADS):
        qlat = _dot(qnope[:, hd * QK_NOPE:(hd + 1) * QK_NOPE].astype(BF16), wuk_ref[hd])
        qrope = qpe[:, hd * QK_ROPE:(hd + 1) * QK_ROPE]
        if sample:
            l_s = q_ref.shape[1] // B_HEADS
            rs = slice(hd * l_s, (hd + 1) * l_s)
            q_ref[:, rs, :KV_LORA] = qlat.reshape(tm // l_s, l_s, KV_LORA)
            q_ref[:, rs, KV_LORA:] = qrope.reshape(tm // l_s, l_s, QK_ROPE)
        else:
            for s in range(tm // B_QBLOCK):
                ts = slice(s * B_QBLOCK, (s + 1) * B_QBLOCK)
                q_ref[s, hd, :, :KV_LORA] = qlat[ts].astype(BF16)
                q_ref[s, hd, :, KV_LORA:] = qrope[ts].astype(BF16)


def _mla_proj(x, norm, layer, w, cos, sin, row0, rows, l_s):
    sample = l_s is not None
    tm = ROW_TM
    blk0 = row0 // tm
    if sample:
        q_shape = jax.ShapeDtypeStruct((rows // l_s, B_HEADS * l_s, KV_ROW), F32)
        q_spec = pl.BlockSpec((tm // l_s, B_HEADS * l_s, KV_ROW), lambda i: (i, 0, 0))
    else:
        q_shape = jax.ShapeDtypeStruct((rows // B_QBLOCK, B_HEADS, B_QBLOCK, KV_ROW), BF16)
        q_spec = pl.BlockSpec((tm // B_QBLOCK, B_HEADS, B_QBLOCK, KV_ROW), lambda i: (i, 0, 0, 0))
    const2 = lambda i: (0, 0)
    return pl.pallas_call(
        functools.partial(_mla_proj_body, sample=sample),
        grid=(rows // tm,),
        in_specs=[
            pl.BlockSpec((tm, D_MODEL), lambda i: (i + blk0, 0)),
            _resident((None, 1, D_MODEL), lambda i: (layer, 0, 0)),
            _resident((D_MODEL, B_IN_COLS), const2),
            _resident((1, Q_LORA), const2),
            _resident((1, KV_LORA), const2),
            _resident((Q_LORA, B_HEADS * QK_NOPE), const2),
            _resident((Q_LORA, B_HEADS * QK_ROPE), const2),
            _resident((Q_LORA, B_HEADS * QK_ROPE), const2),
            _resident((B_HEADS, QK_NOPE, KV_LORA), lambda i: (0, 0, 0)),
            pl.BlockSpec((tm, LANES), lambda i: (i + blk0, 0)),
            pl.BlockSpec((tm, LANES), lambda i: (i + blk0, 0)),
        ],
        out_specs=[pl.BlockSpec((tm, KV_ROW), lambda i: (i, 0)), q_spec],
        out_shape=[jax.ShapeDtypeStruct((rows, KV_ROW), F32), q_shape],
        compiler_params=_cparams("parallel"),
        name="mla_proj_sample" if sample else "mla_proj_prompt",
    )(x, norm, w["w_in"], w["q_norm"], w["kv_norm"], w["w_nope"], w["w_pe"], w["w_pe_rot"], w["w_uk"], cos, sin)


def _softmax_update(q, kvb, m_scr, l_scr, acc_scr, valid=None):
    nk = kvb.shape[0]
    s = _dot_nt(q, kvb) * MLA_SCALE
    if valid is not None:
        s = jnp.where(valid, s, NEG_INF)
    m_prev = m_scr[...]
    m_new = jnp.maximum(m_prev, jnp.max(s, axis=-1, keepdims=True))
    alpha = jnp.exp(m_prev - m_new)
    p = jnp.exp(s - _lanes(m_new, nk))
    l_scr[...] = alpha * l_scr[...] + jnp.sum(p, axis=-1, keepdims=True)
    acc_scr[...] = acc_scr[...] * _lanes(alpha, KV_LORA) + _dot(p.astype(BF16), kvb[:, :KV_LORA])
    m_scr[...] = m_new


def _softmax_init(m_scr, l_scr, acc_scr):
    m_scr[...] = jnp.full(m_scr.shape, NEG_INF, F32)
    l_scr[...] = jnp.zeros(l_scr.shape, F32)
    acc_scr[...] = jnp.zeros(acc_scr.shape, F32)


def _mla_prompt_body(qi_ref, ki_ref, q_ref, kv_ref, x_ref, wuv_ref, wo_ref, o_ref, m_scr, l_scr, acc_scr, u_scr):
    p = pl.program_id(1)
    qi = qi_ref[p]
    ki = ki_ref[p]
    r = B_HEADS * B_QBLOCK
    q_per_kv = B_KVBLOCK // B_QBLOCK
    is_diag = ki == qi // q_per_kv

    @pl.when(ki == 0)
    def _():
        _softmax_init(m_scr, l_scr, acc_scr)

    q = q_ref[...].reshape(r, KV_ROW)
    tok = lax.broadcasted_iota(jnp.int32, (r, B_KVCHUNK), 0) & (B_QBLOCK - 1)
    key = lax.broadcasted_iota(jnp.int32, (r, B_KVCHUNK), 1)
    key_minus_tok = key - tok
    for c in range(B_KVBLOCK // B_KVCHUNK):
        def chunk(c=c):
            kvb = kv_ref[c * B_KVCHUNK:(c + 1) * B_KVCHUNK, :].astype(BF16)
            thr = qi * B_QBLOCK - (ki * B_KVBLOCK + c * B_KVCHUNK)
            _softmax_update(q, kvb, m_scr, l_scr, acc_scr, valid=key_minus_tok <= thr)

        needed = jnp.logical_or(jnp.logical_not(is_diag), (qi % q_per_kv) * B_QBLOCK >= c * B_KVCHUNK)
        pl.when(needed)(chunk)

    @pl.when(is_diag)
    def _():
        o = acc_scr[...] * _lanes(1.0 / l_scr[...], KV_LORA)
        for hd in range(B_HEADS):
            oh = o[hd * B_QBLOCK:(hd + 1) * B_QBLOCK].astype(BF16)
            u_scr[:, hd * V_HEAD:(hd + 1) * V_HEAD] = _dot(oh, wuv_ref[hd])
        o_ref[...] = x_ref[...] + _dot(u_scr[...].astype(BF16), wo_ref[...])


def _mla_prompt(x, q_cat, kv_rows, w, nb_p, t_p):
    nq = t_p // B_QBLOCK
    nkv = t_p // B_KVBLOCK
    q_per_kv = B_KVBLOCK // B_QBLOCK
    pairs = [(qi, ki) for qi in range(nq) for ki in range(qi // q_per_kv + 1)]
    qi_list = jnp.asarray([p[0] for p in pairs], jnp.int32)
    ki_list = jnp.asarray([p[1] for p in pairs], jnp.int32)
    r = B_HEADS * B_QBLOCK
    return pl.pallas_call(
        _mla_prompt_body,
        grid_spec=pltpu.PrefetchScalarGridSpec(
            num_scalar_prefetch=2,
            grid=(nb_p, len(pairs)),
            in_specs=[
                pl.BlockSpec((None, B_HEADS, B_QBLOCK, KV_ROW), lambda b, p, qi, ki: (b * nq + qi[p], 0, 0, 0)),
                pl.BlockSpec((B_KVBLOCK, KV_ROW), lambda b, p, qi, ki: (b * nkv + ki[p], 0)),
                pl.BlockSpec((B_QBLOCK, D_MODEL), lambda b, p, qi, ki: (b * nq + qi[p], 0)),
                _resident((B_HEADS, KV_LORA, V_HEAD), lambda b, p, qi, ki: (0, 0, 0)),
                _resident((B_HEADS * V_HEAD, D_MODEL), lambda b, p, qi, ki: (0, 0)),
            ],
            out_specs=pl.BlockSpec((B_QBLOCK, D_MODEL), lambda b, p, qi, ki: (b * nq + qi[p], 0)),
            scratch_shapes=[
                pltpu.VMEM((r, LANES), F32),
                pltpu.VMEM((r, LANES), F32),
                pltpu.VMEM((r, KV_LORA), F32),
                pltpu.VMEM((B_QBLOCK, B_HEADS * V_HEAD), F32),
            ],
        ),
        out_shape=jax.ShapeDtypeStruct(x.shape, F32),
        input_output_aliases={4: 0},
        compiler_params=_cparams("parallel", "arbitrary"),
        name="mla_prompt",
    )(qi_list, ki_list, q_cat, kv_rows, x, w["w_uv"], w["w_o"])


def _mla_sample_body(pt_ref, q_ref, *refs, l_s):
    page_refs = refs[:B_PAGES_PER_STEP]
    kvn_ref, o_ref, m_scr, l_scr, acc_scr = refs[B_PAGES_PER_STEP:]
    pc = pl.program_id(1)

    @pl.when(pc == 0)
    def _():
        _softmax_init(m_scr, l_scr, acc_scr)

    q = q_ref[...].astype(BF16)
    kt = jnp.concatenate([pr[...].astype(BF16) for pr in page_refs], axis=1)
    nk = kt.shape[1]
    s = _dot(q, kt) * MLA_SCALE
    m_prev = m_scr[...]
    m_new = jnp.maximum(m_prev, jnp.max(s, axis=-1, keepdims=True))
    alpha = jnp.exp(m_prev - m_new)
    p = jnp.exp(s - _lanes(m_new, nk))
    l_scr[...] = alpha * l_scr[...] + jnp.sum(p, axis=-1, keepdims=True)
    acc_scr[...] = acc_scr[...] * _lanes(alpha, KV_LORA) + _dot_nt(p.astype(BF16), kt[:KV_LORA, :])
    m_scr[...] = m_new

    @pl.when(pc == pl.num_programs(1) - 1)
    def _():
        r = q_ref.shape[0]
        tok = lax.broadcasted_iota(jnp.int32, (r, l_s), 0) % l_s
        key = lax.broadcasted_iota(jnp.int32, (r, l_s), 1)
        _softmax_update(q, kvn_ref[...].astype(BF16), m_scr, l_scr, acc_scr, valid=key <= tok)
        o_ref[...] = acc_scr[...] * _lanes(1.0 / l_scr[...], KV_LORA)


def _mla_sample(q_cat, kv_rows, cache_t, page_table, cl, row0, nb_s, l_s):
    n_pages = page_table.shape[1]
    steps = n_pages // B_PAGES_PER_STEP
    r = B_HEADS * l_s
    blk0 = row0 // l_s

    def page_spec(k):
        return pl.BlockSpec((None, None, KV_ROW, PAGE_SIZE),
                            lambda b, pc, pt: (cl, pt[b * n_pages + pc * B_PAGES_PER_STEP + k], 0, 0))

    return pl.pallas_call(
        functools.partial(_mla_sample_body, l_s=l_s),
        grid_spec=pltpu.PrefetchScalarGridSpec(
            num_scalar_prefetch=1,
            grid=(nb_s, steps),
            in_specs=[pl.BlockSpec((None, r, KV_ROW), lambda b, pc, pt: (b, 0, 0))]
            + [page_spec(k) for k in range(B_PAGES_PER_STEP)]
            + [pl.BlockSpec((l_s, KV_ROW), lambda b, pc, pt: (b + blk0, 0))],
            out_specs=pl.BlockSpec((None, r, KV_LORA), lambda b, pc, pt: (b, 0, 0)),
            scratch_shapes=[
                pltpu.VMEM((r, LANES), F32),
                pltpu.VMEM((r, LANES), F32),
                pltpu.VMEM((r, KV_LORA), F32),
            ],
        ),
        out_shape=jax.ShapeDtypeStruct((nb_s, r, KV_LORA), F32),
        compiler_params=_cparams("parallel", "arbitrary"),
        name="mla_sample",
    )(page_table.reshape(-1), q_cat, *([cache_t] * B_PAGES_PER_STEP), kv_rows)


def _mla_out_sample_body(o_ref, x_ref, wuv_ref, wo_ref, out_ref, u_scr, *, l_s):
    nb = o_ref.shape[0]
    for hd in range(B_HEADS):
        oh = o_ref[:, hd * l_s:(hd + 1) * l_s, :].reshape(nb * l_s, KV_LORA).astype(BF16)
        u_scr[:, hd * V_HEAD:(hd + 1) * V_HEAD] = _dot(oh, wuv_ref[hd])
    out_ref[...] = x_ref[...] + _dot(u_scr[...].astype(BF16), wo_ref[...])


def _mla_out_sample(x, o_lat, w, row0, l_s):
    nb_s = o_lat.shape[0]
    nb = ROW_TM // l_s
    blk0 = row0 // ROW_TM
    return pl.pallas_call(
        functools.partial(_mla_out_sample_body, l_s=l_s),
        grid=(nb_s // nb,),
        in_specs=[
            pl.BlockSpec((nb, B_HEADS * l_s, KV_LORA), lambda i: (i, 0, 0)),
            pl.BlockSpec((ROW_TM, D_MODEL), lambda i: (i + blk0, 0)),
            _resident((B_HEADS, KV_LORA, V_HEAD), lambda i: (0, 0, 0)),
            _resident((B_HEADS * V_HEAD, D_MODEL), lambda i: (0, 0)),
        ],
        out_specs=pl.BlockSpec((ROW_TM, D_MODEL), lambda i: (i + blk0, 0)),
        out_shape=jax.ShapeDtypeStruct(x.shape, F32),
        scratch_shapes=[pltpu.VMEM((ROW_TM, B_HEADS * V_HEAD), F32)],
        input_output_aliases={1: 0},
        compiler_params=_cparams("parallel"),
        name="mla_out_sample",
    )(o_lat, x, w["w_uv"], w["w_o"])


C_IN_COLS = D_INNER + CONV_DIM + LANES


def _mamba_in_body(x_ref, g_ref, w_ref, z_ref, xbc_ref, dt_ref):
    h = _rms(x_ref[...], g_ref[...]).astype(BF16)
    z_ref[...] = _dot(h, w_ref[:, :D_INNER])
    xbc_ref[...] = _dot(h, w_ref[:, D_INNER:D_INNER + CONV_DIM])
    dt_ref[...] = _dot(h, w_ref[:, D_INNER + CONV_DIM:])


def _mamba_in(x, norm, layer, w_in):
    rows = x.shape[0]
    return pl.pallas_call(
        _mamba_in_body,
        grid=(rows // ROW_TM,),
        in_specs=[
            pl.BlockSpec((ROW_TM, D_MODEL), lambda i: (i, 0)),
            _resident((None, 1, D_MODEL), lambda i: (layer, 0, 0)),
            _resident((D_MODEL, C_IN_COLS), lambda i: (0, 0)),
        ],
        out_specs=[
            pl.BlockSpec((ROW_TM, D_INNER), lambda i: (i, 0)),
            pl.BlockSpec((ROW_TM, CONV_DIM), lambda i: (i, 0)),
            pl.BlockSpec((ROW_TM, LANES), lambda i: (i, 0)),
        ],
        out_shape=[
            jax.ShapeDtypeStruct((rows, D_INNER), F32),
            jax.ShapeDtypeStruct((rows, CONV_DIM), F32),
            jax.ShapeDtypeStruct((rows, LANES), F32),
        ],
        compiler_params=_cparams("parallel"),
        name="mamba_in",
    )(x, norm, w_in)


def _causal_conv(ext_scr, u, cw_ref, cb_ref):
    t = u.shape[0]
    ext_scr[SUBLANES:SUBLANES + t, :] = u
    acc = cb_ref[...] + cw_ref[D_CONV - 1:D_CONV, :] * u
    for k in range(D_CONV - 1):
        lo = SUBLANES - (D_CONV - 1) + k
        acc = acc + cw_ref[k:k + 1, :] * ext_scr[lo:lo + t, :]
    return acc


def _ssd_prompt_body(xbc_ref, dt_ref, cw_ref, cb_ref, dtb_ref, alog_ref, dskip_ref, y_ref, st_ref,
                     ext_scr, ht_scr):
    c = pl.program_id(1)
    t = SSD_CHUNK

    @pl.when(c == 0)
    def _():
        ext_scr[0:SUBLANES, :] = jnp.zeros((SUBLANES, CONV_DIM), F32)
        ht_scr[...] = jnp.zeros(ht_scr.shape, F32)

    u = xbc_ref[...]
    act = _silu(_causal_conv(ext_scr, u, cw_ref, cb_ref))
    ext_scr[0:SUBLANES, :] = u[t - SUBLANES:, :]

    dt = _softplus(dt_ref[...] + dtb_ref[...])
    a = -jnp.exp(alog_ref[...])
    row = lax.broadcasted_iota(jnp.int32, (t, t), 0)
    col = lax.broadcasted_iota(jnp.int32, (t, t), 1)
    causal = col <= row
    acs = _dot_f32(causal.astype(F32), dt * a)
    acs_t = acs.T
    dt_t = dt.T
    last = acs[t - 1:t, :]
    gw = C_HPG * C_HEAD_DIM
    for g in range(C_GROUPS):
        bg = act[:, D_INNER + g * D_STATE:D_INNER + (g + 1) * D_STATE]
        cg = act[:, D_INNER + C_BC + g * D_STATE:D_INNER + C_BC + (g + 1) * D_STATE].astype(BF16)
        cb = _dot_nt(cg, bg.astype(BF16))
        bg_t = bg.T.astype(BF16)
        y_off = _dot(cg, ht_scr[:, g * gw:(g + 1) * gw].astype(BF16))
        for j in range(C_HPG):
            h = g * C_HPG + j
            hs = slice(h * C_HEAD_DIM, (h + 1) * C_HEAD_DIM)
            a_col = jnp.broadcast_to(acs[:, h:h + 1], (t, t))
            a_row = jnp.broadcast_to(acs_t[h:h + 1, :], (t, t))
            dt_row = jnp.broadcast_to(dt_t[h:h + 1, :], (t, t))
            dt_col = jnp.broadcast_to(dt[:, h:h + 1], (t, C_HEAD_DIM))
            a_last = jnp.broadcast_to(last[:, h:h + 1], (t, C_HEAD_DIM))
            a_colp = a_col[:, :C_HEAD_DIM]
            m = cb * jnp.exp(jnp.where(causal, a_col - a_row, -jnp.inf)) * dt_row
            xh = act[:, hs]
            y_ref[:, hs] = (_dot(m.astype(BF16), xh.astype(BF16))
                            + y_off[:, j * C_HEAD_DIM:(j + 1) * C_HEAD_DIM] * jnp.exp(a_colp)
                            + dskip_ref[:, hs] * xh)
            xw = (xh * (jnp.exp(a_last - a_colp) * dt_col)).astype(BF16)
            ht_scr[:, hs] = jnp.exp(a_last) * ht_scr[:, hs] + _dot(bg_t, xw)

    @pl.when(c == pl.num_programs(1) - 1)
    def _():
        st_ref[...] = ht_scr[...].T


def _ssd_prompt(xbc, dt, w, nb_p, t_p):
    nc = t_p // SSD_CHUNK
    const2 = lambda b, c: (0, 0)
    return pl.pallas_call(
        _ssd_prompt_body,
        grid=(nb_p, nc),
        in_specs=[
            pl.BlockSpec((SSD_CHUNK, CONV_DIM), lambda b, c: (b * nc + c, 0)),
            pl.BlockSpec((SSD_CHUNK, LANES), lambda b, c: (b * nc + c, 0)),
            _resident((D_CONV, CONV_DIM), const2),
            _resident((1, CONV_DIM), const2),
            _resident((1, LANES), const2),
            _resident((1, LANES), const2),
            _resident((1, D_INNER), const2),
        ],
        out_specs=[
            pl.BlockSpec((SSD_CHUNK, D_INNER), lambda b, c: (b * nc + c, 0)),
            pl.BlockSpec((None, D_INNER, D_STATE), lambda b, c: (b, 0, 0)),
        ],
        out_shape=[
            jax.ShapeDtypeStruct((nb_p * t_p, D_INNER), F32),
            jax.ShapeDtypeStruct((nb_p, D_INNER, D_STATE), F32),
        ],
        scratch_shapes=[
            pltpu.VMEM((SUBLANES + SSD_CHUNK, CONV_DIM), F32),
            pltpu.VMEM((D_STATE, D_INNER), F32),
        ],
        compiler_params=_cparams("parallel", "arbitrary"),
        name="ssd_prompt",
    )(xbc, dt, w["conv_w"], w["conv_b"], w["dt_bias"], w["a_log"], w["d_skip"])


def _ssd_sample_body(xbc_ref, dtx_ref, cprev_ref, h0_ref, cw_ref, cb_ref, dtb_ref, alog_ref, dskip_ref,
                     y_ref, st_ref, ext_scr, *, l_s):
    u = xbc_ref[...]
    ext_scr[SUBLANES - (D_CONV - 1):SUBLANES, :] = cprev_ref[...]
    act = _silu(_causal_conv(ext_scr, u, cw_ref, cb_ref))
    xs = act[:, :D_INNER]

    dt = _softplus(dtx_ref[...] + dtb_ref[...])
    a = -jnp.exp(alog_ref[...])
    row = lax.broadcasted_iota(jnp.int32, (l_s, l_s), 0)
    col = lax.broadcasted_iota(jnp.int32, (l_s, l_s), 1)
    acs = _dot_f32((col <= row).astype(F32), dt * a)
    last = acs[l_s - 1:l_s, :]
    xdt = xs * dt
    xw = xdt * jnp.exp(last - acs)
    decay_col = jnp.broadcast_to(jnp.exp(last), (LANES, D_INNER)).T
    gw = C_HPG * C_HEAD_DIM
    lrow = lax.broadcasted_iota(jnp.int32, (l_s, gw), 0)
    for g in range(C_GROUPS):
        gs = slice(g * gw, (g + 1) * gw)
        bg = act[:, D_INNER + g * D_STATE:D_INNER + (g + 1) * D_STATE]
        cg = act[:, D_INNER + C_BC + g * D_STATE:D_INNER + C_BC + (g + 1) * D_STATE]
        cb = _dot_nt(cg.astype(BF16), bg.astype(BF16))
        h0 = h0_ref[gs, :]
        y = _dot_nt(cg.astype(BF16), h0.astype(BF16)) * jnp.exp(acs[:, gs])
        for s in range(l_s):
            seg = acs[:, gs] - acs[s:s + 1, gs]
            w_s = jnp.exp(jnp.where(lrow >= s, seg, -jnp.inf)) * jnp.broadcast_to(cb[:, s:s + 1], (l_s, gw))
            y = y + w_s * xdt[s:s + 1, gs]
        y_ref[:, gs] = y + dskip_ref[:, gs] * xs[:, gs]
        st_ref[gs, :] = decay_col[gs, :] * h0 + _dot_tn(xw[:, gs].astype(BF16), bg.astype(BF16))


def _ssd_sample(xbc, dtx, conv_prev, h0, sl, w, row0, nb_s, l_s):
    blk0 = row0 // l_s
    const2 = lambda b: (0, 0)
    return pl.pallas_call(
        functools.partial(_ssd_sample_body, l_s=l_s),
        grid=(nb_s,),
        in_specs=[
            pl.BlockSpec((l_s, CONV_DIM), lambda b: (b + blk0, 0)),
            pl.BlockSpec((l_s, D_INNER), lambda b: (b, 0)),
            pl.BlockSpec((None, None, D_CONV - 1, CONV_DIM), lambda b: (sl, b, 0, 0)),
            pl.BlockSpec((None, None, D_INNER, D_STATE), lambda b: (sl, b, 0, 0)),
            _resident((D_CONV, CONV_DIM), const2),
            _resident((1, CONV_DIM), const2),
            _resident((1, D_INNER), const2),
            _resident((1, D_INNER), const2),
            _resident((1, D_INNER), const2),
        ],
        out_specs=[
            pl.BlockSpec((l_s, D_INNER), lambda b: (b, 0)),
            pl.BlockSpec((None, D_INNER, D_STATE), lambda b: (b, 0, 0)),
        ],
        out_shape=[
            jax.ShapeDtypeStruct((nb_s * l_s, D_INNER), F32),
            jax.ShapeDtypeStruct((nb_s, D_INNER, D_STATE), F32),
        ],
        scratch_shapes=[pltpu.VMEM((SUBLANES + l_s, CONV_DIM), F32)],
        compiler_params=_cparams("parallel"),
        name="ssd_sample",
    )(xbc, dtx, conv_prev, h0, w["conv_w"], w["conv_b"], w["dt_bias_x"], w["a_log_x"], w["d_skip"])


def _mamba_out_body(y_ref, z_ref, x_ref, ng_ref, wo_ref, o_ref):
    yz = y_ref[...] * _silu(z_ref[...])
    gw = D_INNER // C_GROUPS
    parts = []
    for g in range(C_GROUPS):
        part = yz[:, g * gw:(g + 1) * gw]
        parts.append(part * lax.rsqrt(jnp.mean(part * part, axis=-1, keepdims=True) + EPS))
    yn = jnp.concatenate(parts, axis=1) * ng_ref[...]
    o_ref[...] = x_ref[...] + _dot(yn.astype(BF16), wo_ref[...])


def _mamba_out(x, y, z, norm_g, w_o, row0):
    rows = y.shape[0]
    blk0 = row0 // ROW_TM
    return pl.pallas_call(
        _mamba_out_body,
        grid=(rows // ROW_TM,),
        in_specs=[
            pl.BlockSpec((ROW_TM, D_INNER), lambda i: (i, 0)),
            pl.BlockSpec((ROW_TM, D_INNER), lambda i: (i + blk0, 0)),
            pl.BlockSpec((ROW_TM, D_MODEL), lambda i: (i + blk0, 0)),
            _resident((1, D_INNER), lambda i: (0, 0)),
            _resident((D_INNER, D_MODEL), lambda i: (0, 0)),
        ],
        out_specs=pl.BlockSpec((ROW_TM, D_MODEL), lambda i: (i + blk0, 0)),
        out_shape=jax.ShapeDtypeStruct(x.shape, F32),
        input_output_aliases={2: 0},
        compiler_params=_cparams("parallel"),
        name="mamba_out",
    )(y, z, x, norm_g, w_o)


def _rope_tables(pos):
    half = QK_ROPE // 2
    inv = jnp.power(ROPE_THETA, -jnp.arange(half, dtype=F32) / half)
    ang = pos.astype(F32)[:, None] * inv
    reps = LANES // half
    return jnp.tile(jnp.cos(ang), (1, reps)), jnp.tile(jnp.sin(ang), (1, reps))


def _rot_cols(w):
    half = w.shape[-1] // 2
    return jnp.concatenate([-w[..., half:], w[..., :half]], axis=-1)


def _mla_weights(w_in, q_norm, w_uq, kv_norm, w_ukv, w_o):
    kpe = w_in[:, B_KPE:]
    pad = jnp.zeros((D_MODEL, LANES - QK_ROPE), w_in.dtype)
    w_in_aug = jnp.concatenate([w_in[:, :B_KPE], kpe, pad, _rot_cols(kpe), pad], axis=1)
    uq = w_uq.reshape(Q_LORA, B_HEADS, QK_NOPE + QK_ROPE)
    pe = uq[:, :, QK_NOPE:]
    ukv = w_ukv.reshape(KV_LORA, B_HEADS, QK_NOPE + V_HEAD)
    return {
        "w_in": w_in_aug.astype(BF16),
        "q_norm": q_norm.reshape(1, Q_LORA),
        "kv_norm": kv_norm.reshape(1, KV_LORA),
        "w_nope": uq[:, :, :QK_NOPE].reshape(Q_LORA, B_HEADS * QK_NOPE).astype(BF16),
        "w_pe": pe.reshape(Q_LORA, B_HEADS * QK_ROPE).astype(BF16),
        "w_pe_rot": _rot_cols(pe).reshape(Q_LORA, B_HEADS * QK_ROPE).astype(BF16),
        "w_uk": jnp.transpose(ukv[:, :, :QK_NOPE], (1, 2, 0)).astype(BF16),
        "w_uv": jnp.transpose(ukv[:, :, QK_NOPE:], (1, 0, 2)).astype(BF16),
        "w_o": w_o.astype(BF16),
    }


def _pad_lanes(v):
    return jnp.pad(v, (0, LANES - v.shape[0])).reshape(1, LANES)


def _expand_heads(v):
    return jnp.repeat(v, C_HEAD_DIM).reshape(1, D_INNER)


def _mamba_weights(w_in, conv_w, conv_b, dt_bias, a_log, d_skip, norm_g, w_o):
    w_dt = w_in[:, D_INNER + CONV_DIM:]
    pad = jnp.zeros((D_MODEL, LANES - C_HEADS), w_in.dtype)
    return {
        "w_in": jnp.concatenate([w_in, pad], axis=1).astype(BF16),
        "w_dt_x": jnp.repeat(w_dt, C_HEAD_DIM, axis=1).astype(BF16)[None],
        "conv_w": conv_w,
        "conv_b": conv_b.reshape(1, CONV_DIM),
        "dt_bias": _pad_lanes(dt_bias.astype(F32)),
        "a_log": _pad_lanes(a_log.astype(F32)),
        "dt_bias_x": _expand_heads(dt_bias.astype(F32)),
        "a_log_x": _expand_heads(a_log.astype(F32)),
        "d_skip": _expand_heads(d_skip),
        "norm_g": norm_g.reshape(1, D_INNER),
        "w_o": w_o.astype(BF16),
    }


def kernel(x_prompt, x_sample, state_swa_k, state_swa_v, cache_mla, state_ssm, state_conv, page_table, p_prompt, p_sample, norm_ffn1, ffn1_w_gu, ffn1_w_down, norm_mix, norm_ffn2, ffn2_w_gu, ffn2_w_down, norm_ple, ple_w_proj, ple_w_gate, rel_bias_table, a_w_qkv, a_w_o, a_sinks, b_w_in, b_q_norm, b_w_uq, b_kv_norm, b_w_ukv, b_w_o, c_w_in, c_conv_w, c_conv_b, c_dt_bias, c_a_log, c_d, c_norm, c_w_o, final_norm):
    nb_p, t_p, _ = x_prompt.shape
    nb_s, l_s, _ = x_sample.shape
    depth = norm_ffn1.shape[0]
    rows_p = nb_p * t_p
    rows_s = nb_s * l_s
    w_buf = state_swa_k.shape[2]
    assert rows_p % FFN_TM == 0 and (rows_p + rows_s) % FFN_TM == 0 and rows_s % ROW_TM == 0
    assert t_p % B_KVBLOCK == 0 and t_p % SSD_CHUNK == 0 and t_p % A_BLOCK == 0
    assert l_s == SUBLANES and w_buf == WINDOW and page_table.shape[1] % B_PAGES_PER_STEP == 0

    x = jnp.concatenate([x_prompt.reshape(rows_p, D_MODEL), x_sample.reshape(rows_s, D_MODEL)], axis=0)
    p_all = jnp.concatenate([p_prompt.reshape(depth, rows_p, D_PLE), p_sample.reshape(depth, rows_s, D_PLE)], axis=1)

    def norm3(g):
        return g.reshape(g.shape[0], 1, g.shape[1])

    norm_ffn1, norm_mix, norm_ffn2, norm_ple = map(norm3, (norm_ffn1, norm_mix, norm_ffn2, norm_ple))
    ffn1_w_gu, ffn1_w_down, ffn2_w_gu, ffn2_w_down, ple_w_proj, ple_w_gate, a_w_qkv, a_w_o = (
        w.astype(BF16) for w in (ffn1_w_gu, ffn1_w_down, ffn2_w_gu, ffn2_w_down, ple_w_proj, ple_w_gate, a_w_qkv, a_w_o))
    final_norm = final_norm.reshape(1, D_MODEL)

    i = jnp.arange(A_BLOCK)[:, None]
    j = jnp.arange(2 * A_BLOCK)[None, :]
    bias_p = _rel_bias(rel_bias_table, _t5_bucket(A_BLOCK + i - j))
    q_pos = PAST_LEN + jnp.arange(l_s)
    k_pos = jnp.concatenate([PAST_LEN - w_buf + jnp.arange(w_buf), q_pos])
    bias_s = _rel_bias(rel_bias_table, _t5_bucket(q_pos[:, None] - k_pos[None, :]))
    swa_k_buf = jnp.transpose(state_swa_k, (0, 1, 3, 4, 2))
    swa_v_buf = jnp.transpose(state_swa_v, (0, 1, 3, 4, 2))

    pos = jnp.concatenate([jnp.tile(jnp.arange(t_p), nb_p), jnp.tile(PAST_LEN + jnp.arange(l_s), nb_s)])
    cos, sin = _rope_tables(pos)

    ssm_flat = state_ssm.reshape(state_ssm.shape[0], nb_s, D_INNER, D_STATE)
    cache_t = jnp.swapaxes(cache_mla, 2, 3)

    def last_rows(a, n):
        return jnp.stack([a[(b + 1) * t_p - n:(b + 1) * t_p] for b in range(nb_p)])

    swa_k_p, swa_v_p, mla_p, ssm_p, conv_p = [], [], [], [], []
    swa_k_s, swa_v_s, mla_s, ssm_s, conv_s = [], [], [], [], []
    for layer in range(depth):
        kind, sl = layer % N_MIXERS, layer // N_MIXERS
        x = _ffn(x, norm_ffn1, ffn1_w_gu, ffn1_w_down, layer)
        if kind == 0:
            qkv = _norm_proj(x, norm_mix, a_w_qkv, layer, sl, "swa_qkv")
            x = _swa_prompt(x, qkv, bias_p, a_sinks[sl], a_w_o, sl, nb_p, t_p)
            o_s = _swa_sample(qkv, swa_k_buf, swa_v_buf, bias_s, a_sinks[sl], sl, rows_p, nb_s, l_s)
            x = _res_proj(x, o_s, a_w_o, sl, rows_p, "swa_out_sample")
            kv_p = last_rows(qkv, w_buf)[:, :, A_Q:].reshape(nb_p, w_buf, 2, A_KV_HEADS, A_HEAD_DIM)
            kv_s = qkv[rows_p:, A_Q:].reshape(nb_s, l_s, 2, A_KV_HEADS, A_HEAD_DIM)
            swa_k_p.append(kv_p[:, :, 0])
            swa_v_p.append(kv_p[:, :, 1])
            swa_k_s.append(jnp.concatenate([state_swa_k[sl], kv_s[:, :, 0]], axis=1)[:, l_s:])
            swa_v_s.append(jnp.concatenate([state_swa_v[sl], kv_s[:, :, 1]], axis=1)[:, l_s:])
        elif kind == 1:
            w = _mla_weights(b_w_in[sl], b_q_norm[sl], b_w_uq[sl], b_kv_norm[sl], b_w_ukv[sl], b_w_o[sl])
            kv_rows_p, q_p = _mla_proj(x, norm_mix, layer, w, cos, sin, 0, rows_p, None)
            kv_rows_s, q_s = _mla_proj(x, norm_mix, layer, w, cos, sin, rows_p, rows_s, l_s)
            x = _mla_prompt(x, q_p, kv_rows_p, w, nb_p, t_p)
            o_lat = _mla_sample(q_s, kv_rows_s, cache_t, page_table, sl, 0, nb_s, l_s)
            x = _mla_out_sample(x, o_lat, w, rows_p, l_s)
            mla_p.append(kv_rows_p.reshape(nb_p, t_p, KV_ROW))
            mla_s.append(kv_rows_s.reshape(nb_s, l_s, KV_ROW))
        else:
            w = _mamba_weights(c_w_in[sl], c_conv_w[sl], c_conv_b[sl], c_dt_bias[sl], c_a_log[sl], c_d[sl],
                               c_norm[sl], c_w_o[sl])
            z, xbc, dt = _mamba_in(x, norm_mix, layer, w["w_in"])
            dtx = _norm_proj(x[rows_p:], norm_mix, w["w_dt_x"], layer, 0, "mamba_dt_sample")
            y_p, st_p = _ssd_prompt(xbc, dt, w, nb_p, t_p)
            y_s, st_s = _ssd_sample(xbc, dtx, state_conv, ssm_flat, sl, w, rows_p, nb_s, l_s)
            x = _mamba_out(x, y_p, z, w["norm_g"], w["w_o"], 0)
            x = _mamba_out(x, y_s, z, w["norm_g"], w["w_o"], rows_p)
            ssm_p.append(st_p.reshape(nb_p, C_HEADS, C_HEAD_DIM, D_STATE))
            ssm_s.append(st_s.reshape(nb_s, C_HEADS, C_HEAD_DIM, D_STATE))
            conv_p.append(last_rows(xbc, D_CONV - 1))
            conv_s.append(jnp.concatenate([state_conv[sl], xbc[rows_p:].reshape(nb_s, l_s, CONV_DIM)],
                                          axis=1)[:, -(D_CONV - 1):])
        x = _ffn(x, norm_ffn2, ffn2_w_gu, ffn2_w_down, layer)
        if layer < depth - 1:
            x = _ple(x, p_all, norm_ple, ple_w_proj, ple_w_gate, final_norm, layer, False)
    ple_last = functools.partial(_ple, x, p_all, norm_ple, ple_w_proj, ple_w_gate, final_norm, depth - 1, True)
    y_p = ple_last(0, rows_p)
    y_s = ple_last(rows_p, rows_s)

    return (y_p.reshape(nb_p, t_p, D_MODEL), y_s.reshape(nb_s, l_s, D_MODEL),
            jnp.stack(swa_k_p), jnp.stack(swa_v_p), jnp.stack(mla_p), jnp.stack(ssm_p), jnp.stack(conv_p),
            jnp.stack(swa_k_s), jnp.stack(swa_v_s), jnp.stack(mla_s), jnp.stack(ssm_s), jnp.stack(conv_s))
```

```python
import functools
import math

import jax
import jax.numpy as jnp
from jax import lax
from jax.experimental import pallas as pl
from jax.experimental.pallas import tpu as pltpu

F32 = jnp.float32
BF16 = jnp.bfloat16

D_MODEL = 1024
D_FF = 2816
D_PLE = 256
EPS = 1e-6
NEG_INF = -1e30
PAST_LEN = 8192
PAGE_SIZE = 128
N_MIXERS = 3

A_HEADS = 16
A_KV_HEADS = 4
A_HEAD_DIM = 64
A_GROUP = A_HEADS // A_KV_HEADS
A_Q = A_HEADS * A_HEAD_DIM
A_KV = A_KV_HEADS * A_HEAD_DIM
WINDOW = 128
A_BLOCK = 128
N_BUCKETS = 32
BUCKET_MAX_DIST = 128

B_HEADS = 16
Q_LORA = 768
KV_LORA = 256
QK_NOPE = 64
QK_ROPE = 32
V_HEAD = 64
KV_ROW = KV_LORA + QK_ROPE
ROPE_THETA = 10000.0
MLA_SCALE = (QK_NOPE + QK_ROPE) ** -0.5
B_QBLOCK = 128
B_KVBLOCK = 512
B_PAGES_PER_STEP = 32

D_INNER = 2 * D_MODEL
C_HEAD_DIM = 64
C_HEADS = D_INNER // C_HEAD_DIM
C_GROUPS = 4
C_HPG = C_HEADS // C_GROUPS
D_STATE = 128
D_CONV = 4
C_BC = C_GROUPS * D_STATE
CONV_DIM = D_INNER + 2 * C_BC
SSD_CHUNK = 128

LANES = 128
SUBLANES = 8
VMEM_LIMIT = 56 * 1024 * 1024


def _cparams(*sem):
    return pltpu.CompilerParams(dimension_semantics=sem, vmem_limit_bytes=VMEM_LIMIT)


def _resident(shape, index_map):
    return pl.BlockSpec(shape, index_map, pipeline_mode=pl.Buffered(1))


def _dot(a, b):
    return jnp.dot(a, b, preferred_element_type=F32)


def _dot_nt(a, b):
    return lax.dot_general(a, b, (((1,), (1,)), ((), ())), preferred_element_type=F32)


def _dot_tn(a, b):
    return lax.dot_general(a, b, (((0,), (0,)), ((), ())), preferred_element_type=F32)


def _dot_f32(a, b):
    return jnp.dot(a, b, preferred_element_type=F32, precision=lax.Precision.HIGHEST)


def _rms(x, g):
    r = lax.rsqrt(jnp.mean(x * x, axis=-1, keepdims=True) + EPS)
    return (x * r) * g


def _silu(x):
    return x * jax.nn.sigmoid(x)


def _softplus(x):
    return jnp.maximum(x, 0.0) + jnp.log1p(jnp.exp(-jnp.abs(x)))


def _lanes(a, n):
    if n <= LANES:
        return a[:, :n]
    return jnp.concatenate([a] * (n // LANES), axis=1)


FFN_TM = 512
FFN_TF = 256


def _ple_update(x, p_ref, g_ref, wp_ref, wg_ref):
    gate = _dot(_rms(x, g_ref[...]).astype(BF16), wg_ref[...])
    proj = _dot(p_ref[...].astype(BF16), wp_ref[...])
    return x + proj * jax.nn.sigmoid(gate)


def _ffn_body(x_ref, g_ref, wgu_ref, wd_ref, *refs, ple):
    x = x_ref[...]
    if ple:
        x = _ple_update(x, *refs[:4])
    o_ref = refs[-1]
    h = _rms(x, g_ref[...]).astype(BF16)
    acc = jnp.zeros(x.shape, F32)
    for j in range(D_FF // FFN_TF):
        lo = j * FFN_TF
        gate = _dot(h, wgu_ref[:, lo:lo + FFN_TF])
        up = _dot(h, wgu_ref[:, D_FF + lo:D_FF + lo + FFN_TF])
        act = (_silu(gate) * up).astype(BF16)
        acc = acc + _dot(act, wd_ref[lo:lo + FFN_TF, :])
    o_ref[...] = x + 0.5 * acc


def _ffn(x, norm, w_gu, w_down, layer, ple=None):
    rows = x.shape[0]
    in_specs = [
        pl.BlockSpec((FFN_TM, D_MODEL), lambda i: (i, 0)),
        _resident((None, 1, D_MODEL), lambda i: (layer, 0, 0)),
        _resident((None, D_MODEL, 2 * D_FF), lambda i: (layer, 0, 0)),
        _resident((None, D_FF, D_MODEL), lambda i: (layer, 0, 0)),
    ]
    args = [x, norm, w_gu, w_down]
    if ple is not None:
        pl_layer = ple[4]
        in_specs += [
            pl.BlockSpec((None, FFN_TM, D_PLE), lambda i: (pl_layer, i, 0)),
            _resident((None, 1, D_MODEL), lambda i: (pl_layer, 0, 0)),
            _resident((None, D_PLE, D_MODEL), lambda i: (pl_layer, 0, 0)),
            _resident((None, D_MODEL, D_MODEL), lambda i: (pl_layer, 0, 0)),
        ]
        args += list(ple[:4])
    return pl.pallas_call(
        functools.partial(_ffn_body, ple=ple is not None),
        grid=(rows // FFN_TM,),
        in_specs=in_specs,
        out_specs=pl.BlockSpec((FFN_TM, D_MODEL), lambda i: (i, 0)),
        out_shape=jax.ShapeDtypeStruct(x.shape, F32),
        compiler_params=_cparams("parallel"),
        name="ffn_ple" if ple is not None else "ffn",
    )(*args)


ROW_TM = 256


def _ple_body(x_ref, p_ref, g_ref, wp_ref, wg_ref, gf_ref, o_ref, *, final):
    y = _ple_update(x_ref[...], p_ref, g_ref, wp_ref, wg_ref)
    if final:
        y = _rms(y, gf_ref[...])
    o_ref[...] = y


def _ple(x, p_all, norm, w_proj, w_gate, final_norm, layer, final, row0=0, rows=None):
    rows = x.shape[0] if rows is None else rows
    blk0 = row0 // ROW_TM
    return pl.pallas_call(
        functools.partial(_ple_body, final=final),
        grid=(rows // ROW_TM,),
        in_specs=[
            pl.BlockSpec((ROW_TM, D_MODEL), lambda i: (i + blk0, 0)),
            pl.BlockSpec((None, ROW_TM, D_PLE), lambda i: (layer, i + blk0, 0)),
            _resident((None, 1, D_MODEL), lambda i: (layer, 0, 0)),
            _resident((None, D_PLE, D_MODEL), lambda i: (layer, 0, 0)),
            _resident((None, D_MODEL, D_MODEL), lambda i: (layer, 0, 0)),
            _resident((1, D_MODEL), lambda i: (0, 0)),
        ],
        out_specs=pl.BlockSpec((ROW_TM, D_MODEL), lambda i: (i, 0)),
        out_shape=jax.ShapeDtypeStruct((rows, D_MODEL), F32),
        compiler_params=_cparams("parallel"),
        name="ple",
    )(x, p_all, norm, w_proj, w_gate, final_norm)


def _norm_proj_body(x_ref, g_ref, w_ref, o_ref):
    o_ref[...] = _dot(_rms(x_ref[...], g_ref[...]).astype(BF16), w_ref[...])


def _norm_proj(x, norm, w, layer, wl, name):
    rows = x.shape[0]
    k, n = w.shape[1:]
    return pl.pallas_call(
        _norm_proj_body,
        grid=(rows // ROW_TM,),
        in_specs=[
            pl.BlockSpec((ROW_TM, k), lambda i: (i, 0)),
            _resident((None, 1, k), lambda i: (layer, 0, 0)),
            _resident((None, k, n), lambda i: (wl, 0, 0)),
        ],
        out_specs=pl.BlockSpec((ROW_TM, n), lambda i: (i, 0)),
        out_shape=jax.ShapeDtypeStruct((rows, n), F32),
        compiler_params=_cparams("parallel"),
        name=name,
    )(x, norm, w)


def _res_proj_body(x_ref, a_ref, w_ref, o_ref):
    o_ref[...] = x_ref[...] + _dot(a_ref[...].astype(BF16), w_ref[...])


def _res_proj(x, a, w, wl, row0, name):
    rows = a.shape[0]
    k, n = w.shape[1:]
    blk0 = row0 // ROW_TM
    return pl.pallas_call(
        _res_proj_body,
        grid=(rows // ROW_TM,),
        in_specs=[
            pl.BlockSpec((ROW_TM, n), lambda i: (i + blk0, 0)),
            pl.BlockSpec((ROW_TM, k), lambda i: (i, 0)),
            _resident((None, k, n), lambda i: (wl, 0, 0)),
        ],
        out_specs=pl.BlockSpec((ROW_TM, n), lambda i: (i + blk0, 0)),
        out_shape=jax.ShapeDtypeStruct(x.shape, F32),
        input_output_aliases={0: 0},
        compiler_params=_cparams("parallel"),
        name=name,
    )(x, a, w)


def _t5_bucket(dist):
    max_exact = N_BUCKETS // 2
    d = jnp.maximum(dist, 0)
    large = max_exact + (jnp.log(jnp.maximum(d, 1).astype(F32) / max_exact)
                         / math.log(BUCKET_MAX_DIST / max_exact) * (N_BUCKETS - max_exact)).astype(jnp.int32)
    large = jnp.minimum(large, N_BUCKETS - 1)
    return jnp.where(d < max_exact, d, large)


def _bias_body(tab_ref, bm_ref, o_ref):
    h = pl.program_id(0)
    bm = bm_ref[...]
    acc = jnp.zeros(bm.shape, F32)
    for b in range(N_BUCKETS):
        acc = jnp.where(bm == b, tab_ref[b, h], acc)
    o_ref[...] = acc


def _rel_bias(table, bucket_map):
    r, c = bucket_map.shape
    return pl.pallas_call(
        _bias_body,
        grid=(A_HEADS,),
        in_specs=[pl.BlockSpec(memory_space=pltpu.SMEM), pl.BlockSpec((r, c), lambda h: (0, 0))],
        out_specs=pl.BlockSpec((None, r, c), lambda h: (h, 0, 0)),
        out_shape=jax.ShapeDtypeStruct((A_HEADS, r, c), F32),
        compiler_params=_cparams("parallel"),
        name="rel_bias",
    )(table, bucket_map)


def _swa_heads(q_ref, k, v, bias_ref, sink_ref, valid, o_scr):
    scale = A_HEAD_DIM ** -0.5
    for g in range(A_KV_HEADS):
        kg = k[:, g * A_HEAD_DIM:(g + 1) * A_HEAD_DIM].astype(BF16)
        vg = v[:, g * A_HEAD_DIM:(g + 1) * A_HEAD_DIM].astype(BF16)
        for hh in range(A_GROUP):
            h = g * A_GROUP + hh
            hs = slice(h * A_HEAD_DIM, (h + 1) * A_HEAD_DIM)
            s = _dot_nt(q_ref[:, hs].astype(BF16), kg) * scale + bias_ref[h]
            s = jnp.where(valid, s, NEG_INF)
            sink = sink_ref[h]
            m = jnp.maximum(jnp.max(s, axis=-1, keepdims=True), sink)
            e = jnp.exp(s - m)
            den = jnp.sum(e, axis=-1, keepdims=True) + jnp.exp(sink - m)
            o_scr[:, hs] = _dot((e / den).astype(BF16), vg)


def _swa_prompt_body(sink_ref, q_ref, kp_ref, ko_ref, vp_ref, vo_ref, bias_ref, x_ref, wo_ref, o_ref, o_scr):
    n = pl.program_id(1)
    row = lax.broadcasted_iota(jnp.int32, (A_BLOCK, A_BLOCK), 0)
    col = lax.broadcasted_iota(jnp.int32, (A_BLOCK, A_BLOCK), 1)
    valid = jnp.concatenate([col > row + jnp.where(n > 0, 0, A_BLOCK), col <= row], axis=1)
    k = jnp.concatenate([kp_ref[...], ko_ref[...]], axis=0)
    v = jnp.concatenate([vp_ref[...], vo_ref[...]], axis=0)
    _swa_heads(q_ref, k, v, bias_ref, sink_ref, valid, o_scr)
    o_ref[...] = x_ref[...] + _dot(o_scr[...].astype(BF16), wo_ref[...])


def _swa_prompt(x, qkv, bias, sinks, w_o, wl, nb_p, t_p):
    nblk = t_p // A_BLOCK
    kcol = A_Q // A_KV
    vcol = kcol + 1

    def own(b, n):
        return b * nblk + n

    def prev(b, n):
        return b * nblk + jnp.maximum(n - 1, 0)

    return pl.pallas_call(
        _swa_prompt_body,
        grid=(nb_p, nblk),
        in_specs=[
            pl.BlockSpec(memory_space=pltpu.SMEM),
            pl.BlockSpec((A_BLOCK, A_Q), lambda b, n: (own(b, n), 0)),
            pl.BlockSpec((A_BLOCK, A_KV), lambda b, n: (prev(b, n), kcol)),
            pl.BlockSpec((A_BLOCK, A_KV), lambda b, n: (own(b, n), kcol)),
            pl.BlockSpec((A_BLOCK, A_KV), lambda b, n: (prev(b, n), vcol)),
            pl.BlockSpec((A_BLOCK, A_KV), lambda b, n: (own(b, n), vcol)),
            _resident((A_HEADS, A_BLOCK, 2 * A_BLOCK), lambda b, n: (0, 0, 0)),
            pl.BlockSpec((A_BLOCK, D_MODEL), lambda b, n: (own(b, n), 0)),
            _resident((None, A_Q, D_MODEL), lambda b, n: (wl, 0, 0)),
        ],
        out_specs=pl.BlockSpec((A_BLOCK, D_MODEL), lambda b, n: (own(b, n), 0)),
        out_shape=jax.ShapeDtypeStruct(x.shape, F32),
        scratch_shapes=[pltpu.VMEM((A_BLOCK, A_Q), F32)],
        input_output_aliases={7: 0},
        compiler_params=_cparams("parallel", "arbitrary"),
        name="swa_prompt",
    )(sinks, qkv, qkv, qkv, qkv, qkv, bias, x, w_o)


SWA_S_NB = 8


def _swa_sample_body(qkv_ref, kt_ref, vt_ref, bb_ref, bn_ref, sink_ref, o_ref, *, l_s, w_buf):
    r = A_GROUP * l_s
    scale = A_HEAD_DIM ** -0.5
    tok_b = lax.broadcasted_iota(jnp.int32, (r, w_buf), 0) % l_s
    key_b = lax.broadcasted_iota(jnp.int32, (r, w_buf), 1)
    valid_b = w_buf + tok_b - key_b < WINDOW
    tok_n = lax.broadcasted_iota(jnp.int32, (r, l_s), 0) % l_s
    key_n = lax.broadcasted_iota(jnp.int32, (r, l_s), 1)
    valid_n = key_n <= tok_n

    def one(b, carry):
        rows = pl.ds(pl.multiple_of(b * l_s, l_s), l_s)
        for g in range(A_KV_HEADS):
            heads = range(g * A_GROUP, (g + 1) * A_GROUP)
            qg = jnp.concatenate([qkv_ref[rows, h * A_HEAD_DIM:(h + 1) * A_HEAD_DIM] for h in heads], axis=0)
            qg = qg.astype(BF16)
            kn = qkv_ref[rows, A_Q + g * A_HEAD_DIM:A_Q + (g + 1) * A_HEAD_DIM].astype(BF16)
            vn = qkv_ref[rows, A_Q + A_KV + g * A_HEAD_DIM:A_Q + A_KV + (g + 1) * A_HEAD_DIM].astype(BF16)
            sb = jnp.where(valid_b, _dot(qg, kt_ref[b, g].astype(BF16)) * scale + bb_ref[g], NEG_INF)
            sn = jnp.where(valid_n, _dot_nt(qg, kn) * scale + bn_ref[g], NEG_INF)
            sink = sink_ref[g][:, :1]
            m = jnp.maximum(jnp.maximum(jnp.max(sb, axis=-1, keepdims=True), jnp.max(sn, axis=-1, keepdims=True)), sink)
            eb = jnp.exp(sb - m)
            en = jnp.exp(sn - m)
            den = jnp.sum(eb, axis=-1, keepdims=True) + jnp.sum(en, axis=-1, keepdims=True) + jnp.exp(sink - m)
            og = _dot_nt((eb / den).astype(BF16), vt_ref[b, g].astype(BF16)) + _dot((en / den).astype(BF16), vn)
            for i, h in enumerate(heads):
                o_ref[rows, h * A_HEAD_DIM:(h + 1) * A_HEAD_DIM] = og[i * l_s:(i + 1) * l_s]
        return carry

    lax.fori_loop(0, SWA_S_NB, one, 0, unroll=2)


def _swa_sample(qkv, kt_buf, vt_buf, bias, sinks, sl, row0, nb_s, l_s):
    w_buf = kt_buf.shape[-1]
    r = A_GROUP * l_s
    tm = SWA_S_NB * l_s
    blk0 = row0 // tm
    bias_g = bias.reshape(A_KV_HEADS, r, w_buf + l_s)
    sink_g = jnp.broadcast_to(sinks.reshape(A_KV_HEADS, A_GROUP, 1, 1),
                              (A_KV_HEADS, A_GROUP, l_s, LANES)).reshape(A_KV_HEADS, r, LANES)
    buf_spec = pl.BlockSpec((None, SWA_S_NB, A_KV_HEADS, A_HEAD_DIM, w_buf), lambda i: (sl, i, 0, 0, 0))
    const3 = lambda i: (0, 0, 0)
    return pl.pallas_call(
        functools.partial(_swa_sample_body, l_s=l_s, w_buf=w_buf),
        grid=(nb_s // SWA_S_NB,),
        in_specs=[
            pl.BlockSpec((tm, A_Q + 2 * A_KV), lambda i: (i + blk0, 0)),
            buf_spec,
            buf_spec,
            _resident((A_KV_HEADS, r, w_buf), const3),
            _resident((A_KV_HEADS, r, l_s), const3),
            _resident((A_KV_HEADS, r, LANES), const3),
        ],
        out_specs=pl.BlockSpec((tm, A_Q), lambda i: (i, 0)),
        out_shape=jax.ShapeDtypeStruct((nb_s * l_s, A_Q), F32),
        compiler_params=_cparams("parallel"),
        name="swa_sample",
    )(qkv, kt_buf, vt_buf, bias_g[:, :, :w_buf], bias_g[:, :, w_buf:], sink_g)


B_IN_COLS = Q_LORA + KV_LORA + 2 * LANES
B_KPE = Q_LORA + KV_LORA
B_KPE_ROT = B_KPE + LANES


def _mla_proj_body(x_ref, g_ref, win_ref, qn_ref, kvn_ref, wnope_ref, wpe_ref, wper_ref, wuk_ref,
                   cos_ref, sin_ref, kv_ref, q_ref, *, sample):
    tm = x_ref.shape[0]
    h = _rms(x_ref[...], g_ref[...]).astype(BF16)
    c = _dot(h, win_ref[...])
    cos = cos_ref[...]
    sin = sin_ref[...]
    kv_ref[:, :KV_LORA] = _rms(c[:, Q_LORA:B_KPE], kvn_ref[...])
    kv_ref[:, KV_LORA:] = (c[:, B_KPE:B_KPE + QK_ROPE] * cos[:, :QK_ROPE]
                           + c[:, B_KPE_ROT:B_KPE_ROT + QK_ROPE] * sin[:, :QK_ROPE])
    qn = _rms(c[:, :Q_LORA], qn_ref[...]).astype(BF16)
    qnope = _dot(qn, wnope_ref[...])
    n_rep = B_HEADS * QK_ROPE // LANES
    qpe = (_dot(qn, wpe_ref[...]) * jnp.concatenate([cos] * n_rep, axis=1)
           + _dot(qn, wper_ref[...]) * jnp.concatenate([sin] * n_rep, axis=1))
    for hd in range(B_HEADS):
        qlat = _dot(qnope[:, hd * QK_NOPE:(hd + 1) * QK_NOPE].astype(BF16), wuk_ref[hd])
        qrope = qpe[:, hd * QK_ROPE:(hd + 1) * QK_ROPE]
        if sample:
            l_s = q_ref.shape[1] // B_HEADS
            rs = slice(hd * l_s, (hd + 1) * l_s)
            q_ref[:, rs, :KV_LORA] = qlat.reshape(tm // l_s, l_s, KV_LORA)
            q_ref[:, rs, KV_LORA:] = qrope.reshape(tm // l_s, l_s, QK_ROPE)
        else:
            for s in range(tm // B_QBLOCK):
                ts = slice(s * B_QBLOCK, (s + 1) * B_QBLOCK)
                q_ref[s, hd, :, :KV_LORA] = qlat[ts].astype(BF16)
                q_ref[s, hd, :, KV_LORA:] = qrope[ts].astype(BF16)


def _mla_proj(x, norm, layer, w, cos, sin, row0, rows, l_s):
    sample = l_s is not None
    tm = ROW_TM
    blk0 = row0 // tm
    if sample:
        q_shape = jax.ShapeDtypeStruct((rows // l_s, B_HEADS * l_s, KV_ROW), F32)
        q_spec = pl.BlockSpec((tm // l_s, B_HEADS * l_s, KV_ROW), lambda i: (i, 0, 0))
    else:
        q_shape = jax.ShapeDtypeStruct((rows // B_QBLOCK, B_HEADS, B_QBLOCK, KV_ROW), BF16)
        q_spec = pl.BlockSpec((tm // B_QBLOCK, B_HEADS, B_QBLOCK, KV_ROW), lambda i: (i, 0, 0, 0))
    const2 = lambda i: (0, 0)
    return pl.pallas_call(
        functools.partial(_mla_proj_body, sample=sample),
        grid=(rows // tm,),
        in_specs=[
            pl.BlockSpec((tm, D_MODEL), lambda i: (i + blk0, 0)),
            _resident((None, 1, D_MODEL), lambda i: (layer, 0, 0)),
            _resident((D_MODEL, B_IN_COLS), const2),
            _resident((1, Q_LORA), const2),
            _resident((1, KV_LORA), const2),
            _resident((Q_LORA, B_HEADS * QK_NOPE), const2),
            _resident((Q_LORA, B_HEADS * QK_ROPE), const2),
            _resident((Q_LORA, B_HEADS * QK_ROPE), const2),
            _resident((B_HEADS, QK_NOPE, KV_LORA), lambda i: (0, 0, 0)),
            pl.BlockSpec((tm, LANES), lambda i: (i + blk0, 0)),
            pl.BlockSpec((tm, LANES), lambda i: (i + blk0, 0)),
        ],
        out_specs=[pl.BlockSpec((tm, KV_ROW), lambda i: (i, 0)), q_spec],
        out_shape=[jax.ShapeDtypeStruct((rows, KV_ROW), F32), q_shape],
        compiler_params=_cparams("parallel"),
        name="mla_proj_sample" if sample else "mla_proj_prompt",
    )(x, norm, w["w_in"], w["q_norm"], w["kv_norm"], w["w_nope"], w["w_pe"], w["w_pe_rot"], w["w_uk"], cos, sin)


def _softmax_accumulate(s, pv, m_scr, l_scr, acc_scr):
    c = MLA_SCALE * math.log2(math.e)
    m_prev = m_scr[...]
    m_new = jnp.maximum(m_prev, jnp.max(s, axis=-1, keepdims=True))
    alpha = jnp.exp2((m_prev - m_new) * c)
    p = jnp.exp2((s - _lanes(m_new, s.shape[1])) * c)
    l_scr[...] = alpha * l_scr[...] + jnp.sum(p, axis=-1, keepdims=True)
    acc_scr[...] = acc_scr[...] * _lanes(alpha, KV_LORA) + pv(p.astype(BF16))
    m_scr[...] = m_new


def _softmax_update(q, kvb, m_scr, l_scr, acc_scr, valid=None):
    s = _dot_nt(q, kvb)
    if valid is not None:
        s = jnp.where(valid, s, NEG_INF)
    _softmax_accumulate(s, lambda p: _dot(p, kvb[:, :KV_LORA]), m_scr, l_scr, acc_scr)


def _softmax_init(m_scr, l_scr, acc_scr):
    m_scr[...] = jnp.full(m_scr.shape, NEG_INF, F32)
    l_scr[...] = jnp.zeros(l_scr.shape, F32)
    acc_scr[...] = jnp.zeros(acc_scr.shape, F32)


def _mla_prompt_body(qi_ref, ki_ref, q_ref, kv_ref, x_ref, wuv_ref, wo_ref, o_ref, m_scr, l_scr, acc_scr, u_scr):
    p = pl.program_id(1)
    qi = qi_ref[p]
    ki = ki_ref[p]
    r = B_HEADS * B_QBLOCK
    q_per_kv = B_KVBLOCK // B_QBLOCK
    is_diag = ki == qi // q_per_kv

    @pl.when(ki == 0)
    def _():
        _softmax_init(m_scr, l_scr, acc_scr)

    def update(masked):
        q = q_ref[...].reshape(r, KV_ROW)
        kvb = kv_ref[...].astype(BF16)
        valid = None
        if masked:
            tok = lax.broadcasted_iota(jnp.int32, (r, B_KVBLOCK), 0) & (B_QBLOCK - 1)
            key = lax.broadcasted_iota(jnp.int32, (r, B_KVBLOCK), 1)
            valid = key - tok <= qi * B_QBLOCK - ki * B_KVBLOCK
        _softmax_update(q, kvb, m_scr, l_scr, acc_scr, valid=valid)

    @pl.when(jnp.logical_not(is_diag))
    def _():
        update(False)

    @pl.when(is_diag)
    def _():
        update(True)
        o = acc_scr[...] * _lanes(1.0 / l_scr[...], KV_LORA)
        for hd in range(B_HEADS):
            oh = o[hd * B_QBLOCK:(hd + 1) * B_QBLOCK].astype(BF16)
            u_scr[:, hd * V_HEAD:(hd + 1) * V_HEAD] = _dot(oh, wuv_ref[hd])
        o_ref[...] = x_ref[...] + _dot(u_scr[...].astype(BF16), wo_ref[...])


def _mla_prompt(x, q_cat, kv_rows, w, nb_p, t_p):
    nq = t_p // B_QBLOCK
    nkv = t_p // B_KVBLOCK
    q_per_kv = B_KVBLOCK // B_QBLOCK
    pairs = [(qi, ki) for qi in range(nq) for ki in range(qi // q_per_kv + 1)]
    qi_list = jnp.asarray([p[0] for p in pairs], jnp.int32)
    ki_list = jnp.asarray([p[1] for p in pairs], jnp.int32)
    r = B_HEADS * B_QBLOCK
    return pl.pallas_call(
        _mla_prompt_body,
        grid_spec=pltpu.PrefetchScalarGridSpec(
            num_scalar_prefetch=2,
            grid=(nb_p, len(pairs)),
            in_specs=[
                pl.BlockSpec((None, B_HEADS, B_QBLOCK, KV_ROW), lambda b, p, qi, ki: (b * nq + qi[p], 0, 0, 0)),
                pl.BlockSpec((B_KVBLOCK, KV_ROW), lambda b, p, qi, ki: (b * nkv + ki[p], 0)),
                pl.BlockSpec((B_QBLOCK, D_MODEL), lambda b, p, qi, ki: (b * nq + qi[p], 0)),
                _resident((B_HEADS, KV_LORA, V_HEAD), lambda b, p, qi, ki: (0, 0, 0)),
                _resident((B_HEADS * V_HEAD, D_MODEL), lambda b, p, qi, ki: (0, 0)),
            ],
            out_specs=pl.BlockSpec((B_QBLOCK, D_MODEL), lambda b, p, qi, ki: (b * nq + qi[p], 0)),
            scratch_shapes=[
                pltpu.VMEM((r, LANES), F32),
                pltpu.VMEM((r, LANES), F32),
                pltpu.VMEM((r, KV_LORA), F32),
                pltpu.VMEM((B_QBLOCK, B_HEADS * V_HEAD), F32),
            ],
        ),
        out_shape=jax.ShapeDtypeStruct(x.shape, F32),
        input_output_aliases={4: 0},
        compiler_params=_cparams("parallel", "arbitrary"),
        name="mla_prompt",
    )(qi_list, ki_list, q_cat, kv_rows, x, w["w_uv"], w["w_o"])


def _mla_sample_body(pt_ref, q_ref, *refs, l_s):
    page_refs = refs[:B_PAGES_PER_STEP]
    kvn_ref, o_ref, m_scr, l_scr, acc_scr = refs[B_PAGES_PER_STEP:]
    pc = pl.program_id(1)

    @pl.when(pc == 0)
    def _():
        _softmax_init(m_scr, l_scr, acc_scr)

    q = q_ref[...].astype(BF16)
    kt = jnp.concatenate([pr[...].astype(BF16) for pr in page_refs], axis=1)
    _softmax_accumulate(_dot(q, kt), lambda p: _dot_nt(p, kt[:KV_LORA, :]), m_scr, l_scr, acc_scr)

    @pl.when(pc == pl.num_programs(1) - 1)
    def _():
        r = q_ref.shape[0]
        tok = lax.broadcasted_iota(jnp.int32, (r, l_s), 0) % l_s
        key = lax.broadcasted_iota(jnp.int32, (r, l_s), 1)
        _softmax_update(q, kvn_ref[...].astype(BF16), m_scr, l_scr, acc_scr, valid=key <= tok)
        o_ref[...] = acc_scr[...] * _lanes(1.0 / l_scr[...], KV_LORA)


def _mla_sample(q_cat, kv_rows, cache_t, page_table, cl, row0, nb_s, l_s):
    n_pages = page_table.shape[1]
    steps = n_pages // B_PAGES_PER_STEP
    r = B_HEADS * l_s
    blk0 = row0 // l_s

    def page_spec(k):
        return pl.BlockSpec((None, None, KV_ROW, PAGE_SIZE),
                            lambda b, pc, pt: (cl, pt[b * n_pages + pc * B_PAGES_PER_STEP + k], 0, 0))

    return pl.pallas_call(
        functools.partial(_mla_sample_body, l_s=l_s),
        grid_spec=pltpu.PrefetchScalarGridSpec(
            num_scalar_prefetch=1,
            grid=(nb_s, steps),
            in_specs=[pl.BlockSpec((None, r, KV_ROW), lambda b, pc, pt: (b, 0, 0))]
            + [page_spec(k) for k in range(B_PAGES_PER_STEP)]
            + [pl.BlockSpec((l_s, KV_ROW), lambda b, pc, pt: (b + blk0, 0))],
            out_specs=pl.BlockSpec((None, r, KV_LORA), lambda b, pc, pt: (b, 0, 0)),
            scratch_shapes=[
                pltpu.VMEM((r, LANES), F32),
                pltpu.VMEM((r, LANES), F32),
                pltpu.VMEM((r, KV_LORA), F32),
            ],
        ),
        out_shape=jax.ShapeDtypeStruct((nb_s, r, KV_LORA), F32),
        compiler_params=_cparams("parallel", "arbitrary"),
        name="mla_sample",
    )(page_table.reshape(-1), q_cat, *([cache_t] * B_PAGES_PER_STEP), kv_rows)


def _mla_out_sample_body(o_ref, x_ref, wuv_ref, wo_ref, out_ref, u_scr, *, l_s):
    nb = o_ref.shape[0]
    for hd in range(B_HEADS):
        oh = o_ref[:, hd * l_s:(hd + 1) * l_s, :].reshape(nb * l_s, KV_LORA).astype(BF16)
        u_scr[:, hd * V_HEAD:(hd + 1) * V_HEAD] = _dot(oh, wuv_ref[hd])
    out_ref[...] = x_ref[...] + _dot(u_scr[...].astype(BF16), wo_ref[...])


def _mla_out_sample(x, o_lat, w, row0, l_s):
    nb_s = o_lat.shape[0]
    nb = ROW_TM // l_s
    blk0 = row0 // ROW_TM
    return pl.pallas_call(
        functools.partial(_mla_out_sample_body, l_s=l_s),
        grid=(nb_s // nb,),
        in_specs=[
            pl.BlockSpec((nb, B_HEADS * l_s, KV_LORA), lambda i: (i, 0, 0)),
            pl.BlockSpec((ROW_TM, D_MODEL), lambda i: (i + blk0, 0)),
            _resident((B_HEADS, KV_LORA, V_HEAD), lambda i: (0, 0, 0)),
            _resident((B_HEADS * V_HEAD, D_MODEL), lambda i: (0, 0)),
        ],
        out_specs=pl.BlockSpec((ROW_TM, D_MODEL), lambda i: (i + blk0, 0)),
        out_shape=jax.ShapeDtypeStruct(x.shape, F32),
        scratch_shapes=[pltpu.VMEM((ROW_TM, B_HEADS * V_HEAD), F32)],
        input_output_aliases={1: 0},
        compiler_params=_cparams("parallel"),
        name="mla_out_sample",
    )(o_lat, x, w["w_uv"], w["w_o"])


C_IN_COLS = D_INNER + CONV_DIM + LANES


def _mamba_in_body(x_ref, g_ref, w_ref, z_ref, xbc_ref, dt_ref):
    h = _rms(x_ref[...], g_ref[...]).astype(BF16)
    z_ref[...] = _dot(h, w_ref[:, :D_INNER])
    xbc_ref[...] = _dot(h, w_ref[:, D_INNER:D_INNER + CONV_DIM])
    dt_ref[...] = _dot(h, w_ref[:, D_INNER + CONV_DIM:])


def _mamba_in(x, norm, layer, w_in):
    rows = x.shape[0]
    return pl.pallas_call(
        _mamba_in_body,
        grid=(rows // ROW_TM,),
        in_specs=[
            pl.BlockSpec((ROW_TM, D_MODEL), lambda i: (i, 0)),
            _resident((None, 1, D_MODEL), lambda i: (layer, 0, 0)),
            _resident((D_MODEL, C_IN_COLS), lambda i: (0, 0)),
        ],
        out_specs=[
            pl.BlockSpec((ROW_TM, D_INNER), lambda i: (i, 0)),
            pl.BlockSpec((ROW_TM, CONV_DIM), lambda i: (i, 0)),
            pl.BlockSpec((ROW_TM, LANES), lambda i: (i, 0)),
        ],
        out_shape=[
            jax.ShapeDtypeStruct((rows, D_INNER), F32),
            jax.ShapeDtypeStruct((rows, CONV_DIM), F32),
            jax.ShapeDtypeStruct((rows, LANES), F32),
        ],
        compiler_params=_cparams("parallel"),
        name="mamba_in",
    )(x, norm, w_in)


def _causal_conv(ext_scr, u, cw_ref, cb_ref):
    t = u.shape[0]
    ext_scr[SUBLANES:SUBLANES + t, :] = u
    acc = cb_ref[...] + cw_ref[D_CONV - 1:D_CONV, :] * u
    for k in range(D_CONV - 1):
        lo = SUBLANES - (D_CONV - 1) + k
        acc = acc + cw_ref[k:k + 1, :] * ext_scr[lo:lo + t, :]
    return acc


def _ssd_prompt_body(xbc_ref, dt_ref, cw_ref, cb_ref, dtb_ref, alog_ref, dskip_ref, y_ref, st_ref,
                     ext_scr, ht_scr):
    c = pl.program_id(1)
    t = SSD_CHUNK

    @pl.when(c == 0)
    def _():
        ext_scr[0:SUBLANES, :] = jnp.zeros((SUBLANES, CONV_DIM), F32)
        ht_scr[...] = jnp.zeros(ht_scr.shape, F32)

    u = xbc_ref[...]
    act = _silu(_causal_conv(ext_scr, u, cw_ref, cb_ref))
    ext_scr[0:SUBLANES, :] = u[t - SUBLANES:, :]

    dt = _softplus(dt_ref[...] + dtb_ref[...])
    a = -jnp.exp(alog_ref[...])
    row = lax.broadcasted_iota(jnp.int32, (t, t), 0)
    col = lax.broadcasted_iota(jnp.int32, (t, t), 1)
    causal = col <= row
    acs = _dot_f32(causal.astype(F32), dt * a)
    acs_t = acs.T
    dt_t = dt.T
    to_end = jnp.exp(acs[t - 1:t, :] - acs) * dt
    gw = C_HPG * C_HEAD_DIM
    heads_per_vreg = LANES // C_HEAD_DIM
    assert heads_per_vreg == 2 and t == LANES
    first = lax.broadcasted_iota(jnp.int32, (t, LANES), 1) < C_HEAD_DIM
    for g in range(C_GROUPS):
        bg = act[:, D_INNER + g * D_STATE:D_INNER + (g + 1) * D_STATE]
        cg = act[:, D_INNER + C_BC + g * D_STATE:D_INNER + C_BC + (g + 1) * D_STATE].astype(BF16)
        cb = _dot_nt(cg, bg.astype(BF16))
        bg_t = bg.T.astype(BF16)
        y_off = _dot(cg, ht_scr[:, g * gw:(g + 1) * gw].astype(BF16))
        for jp in range(C_HPG // heads_per_vreg):
            h0 = g * C_HPG + heads_per_vreg * jp
            ps = slice(h0 * C_HEAD_DIM, h0 * C_HEAD_DIM + LANES)
            ms, a_cols, te_cols = [], [], []
            for h in (h0, h0 + 1):
                a_col = jnp.broadcast_to(acs[:, h:h + 1], (t, t))
                a_row = jnp.broadcast_to(acs_t[h:h + 1, :], (t, t))
                dt_row = jnp.broadcast_to(dt_t[h:h + 1, :], (t, t))
                ms.append(cb * jnp.exp(jnp.where(causal, a_col - a_row, -jnp.inf)) * dt_row)
                a_cols.append(a_col)
                te_cols.append(jnp.broadcast_to(to_end[:, h:h + 1], (t, LANES)))
            x2 = act[:, ps]
            lhs = jnp.concatenate(ms, axis=1).astype(BF16)
            rhs = jnp.concatenate([jnp.where(first, x2, 0.0), jnp.where(first, 0.0, x2)], axis=0).astype(BF16)
            a_col2 = jnp.where(first, a_cols[0], a_cols[1])
            y_ref[:, ps] = (_dot(lhs, rhs)
                            + y_off[:, jp * LANES:(jp + 1) * LANES] * jnp.exp(a_col2)
                            + dskip_ref[:, ps] * x2)
            xw = (x2 * jnp.where(first, te_cols[0], te_cols[1])).astype(BF16)
            ht_scr[:, ps] = jnp.exp(a_col2[t - 1:t, :]) * ht_scr[:, ps] + _dot(bg_t, xw)

    @pl.when(c == pl.num_programs(1) - 1)
    def _():
        st_ref[...] = ht_scr[...].T


def _ssd_prompt(xbc, dt, w, nb_p, t_p):
    nc = t_p // SSD_CHUNK
    const2 = lambda b, c: (0, 0)
    return pl.pallas_call(
        _ssd_prompt_body,
        grid=(nb_p, nc),
        in_specs=[
            pl.BlockSpec((SSD_CHUNK, CONV_DIM), lambda b, c: (b * nc + c, 0)),
            pl.BlockSpec((SSD_CHUNK, LANES), lambda b, c: (b * nc + c, 0)),
            _resident((D_CONV, CONV_DIM), const2),
            _resident((1, CONV_DIM), const2),
            _resident((1, LANES), const2),
            _resident((1, LANES), const2),
            _resident((1, D_INNER), const2),
        ],
        out_specs=[
            pl.BlockSpec((SSD_CHUNK, D_INNER), lambda b, c: (b * nc + c, 0)),
            pl.BlockSpec((None, D_INNER, D_STATE), lambda b, c: (b, 0, 0)),
        ],
        out_shape=[
            jax.ShapeDtypeStruct((nb_p * t_p, D_INNER), F32),
            jax.ShapeDtypeStruct((nb_p, D_INNER, D_STATE), F32),
        ],
        scratch_shapes=[
            pltpu.VMEM((SUBLANES + SSD_CHUNK, CONV_DIM), F32),
            pltpu.VMEM((D_STATE, D_INNER), F32),
        ],
        compiler_params=_cparams("parallel", "arbitrary"),
        name="ssd_prompt",
    )(xbc, dt, w["conv_w"], w["conv_b"], w["dt_bias"], w["a_log"], w["d_skip"])


def _ssd_sample_body(xbc_ref, dtx_ref, cprev_ref, h0_ref, cw_ref, cb_ref, dtb_ref, alog_ref, dskip_ref,
                     y_ref, st_ref, ext_scr, *, l_s):
    u = xbc_ref[...]
    ext_scr[SUBLANES - (D_CONV - 1):SUBLANES, :] = cprev_ref[...]
    act = _silu(_causal_conv(ext_scr, u, cw_ref, cb_ref))
    xs = act[:, :D_INNER]

    dt = _softplus(dtx_ref[...] + dtb_ref[...])
    a = -jnp.exp(alog_ref[...])
    row = lax.broadcasted_iota(jnp.int32, (l_s, l_s), 0)
    col = lax.broadcasted_iota(jnp.int32, (l_s, l_s), 1)
    acs = _dot_f32((col <= row).astype(F32), dt * a)
    last = acs[l_s - 1:l_s, :]
    xdt = xs * dt
    xw = xdt * jnp.exp(last - acs)
    decay_col = jnp.broadcast_to(jnp.exp(last), (LANES, D_INNER)).T
    gw = C_HPG * C_HEAD_DIM
    lrow = lax.broadcasted_iota(jnp.int32, (l_s, gw), 0)
    for g in range(C_GROUPS):
        gs = slice(g * gw, (g + 1) * gw)
        bg = act[:, D_INNER + g * D_STATE:D_INNER + (g + 1) * D_STATE]
        cg = act[:, D_INNER + C_BC + g * D_STATE:D_INNER + C_BC + (g + 1) * D_STATE]
        cb = _dot_nt(cg.astype(BF16), bg.astype(BF16))
        h0 = h0_ref[gs, :]
        y = _dot_nt(cg.astype(BF16), h0.astype(BF16)) * jnp.exp(acs[:, gs])
        for s in range(l_s):
            seg = acs[:, gs] - acs[s:s + 1, gs]
            w_s = jnp.exp(jnp.where(lrow >= s, seg, -jnp.inf)) * jnp.broadcast_to(cb[:, s:s + 1], (l_s, gw))
            y = y + w_s * xdt[s:s + 1, gs]
        y_ref[:, gs] = y + dskip_ref[:, gs] * xs[:, gs]
        st_ref[gs, :] = decay_col[gs, :] * h0 + _dot_tn(xw[:, gs].astype(BF16), bg.astype(BF16))


def _ssd_sample(xbc, dtx, conv_prev, h0, sl, w, row0, nb_s, l_s):
    blk0 = row0 // l_s
    const2 = lambda b: (0, 0)
    return pl.pallas_call(
        functools.partial(_ssd_sample_body, l_s=l_s),
        grid=(nb_s,),
        in_specs=[
            pl.BlockSpec((l_s, CONV_DIM), lambda b: (b + blk0, 0)),
            pl.BlockSpec((l_s, D_INNER), lambda b: (b, 0)),
            pl.BlockSpec((None, None, D_CONV - 1, CONV_DIM), lambda b: (sl, b, 0, 0)),
            pl.BlockSpec((None, None, D_INNER, D_STATE), lambda b: (sl, b, 0, 0)),
            _resident((D_CONV, CONV_DIM), const2),
            _resident((1, CONV_DIM), const2),
            _resident((1, D_INNER), const2),
            _resident((1, D_INNER), const2),
            _resident((1, D_INNER), const2),
        ],
        out_specs=[
            pl.BlockSpec((l_s, D_INNER), lambda b: (b, 0)),
            pl.BlockSpec((None, D_INNER, D_STATE), lambda b: (b, 0, 0)),
        ],
        out_shape=[
            jax.ShapeDtypeStruct((nb_s * l_s, D_INNER), F32),
            jax.ShapeDtypeStruct((nb_s, D_INNER, D_STATE), F32),
        ],
        scratch_shapes=[pltpu.VMEM((SUBLANES + l_s, CONV_DIM), F32)],
        compiler_params=_cparams("parallel"),
        name="ssd_sample",
    )(xbc, dtx, conv_prev, h0, w["conv_w"], w["conv_b"], w["dt_bias_x"], w["a_log_x"], w["d_skip"])


def _mamba_out_body(y_ref, z_ref, x_ref, ng_ref, wo_ref, o_ref):
    yz = y_ref[...] * _silu(z_ref[...])
    gw = D_INNER // C_GROUPS
    parts = []
    for g in range(C_GROUPS):
        part = yz[:, g * gw:(g + 1) * gw]
        parts.append(part * lax.rsqrt(jnp.mean(part * part, axis=-1, keepdims=True) + EPS))
    yn = jnp.concatenate(parts, axis=1) * ng_ref[...]
    o_ref[...] = x_ref[...] + _dot(yn.astype(BF16), wo_ref[...])


def _mamba_out(x, y, z, norm_g, w_o, row0):
    rows = y.shape[0]
    blk0 = row0 // ROW_TM
    return pl.pallas_call(
        _mamba_out_body,
        grid=(rows // ROW_TM,),
        in_specs=[
            pl.BlockSpec((ROW_TM, D_INNER), lambda i: (i, 0)),
            pl.BlockSpec((ROW_TM, D_INNER), lambda i: (i + blk0, 0)),
            pl.BlockSpec((ROW_TM, D_MODEL), lambda i: (i + blk0, 0)),
            _resident((1, D_INNER), lambda i: (0, 0)),
            _resident((D_INNER, D_MODEL), lambda i: (0, 0)),
        ],
        out_specs=pl.BlockSpec((ROW_TM, D_MODEL), lambda i: (i + blk0, 0)),
        out_shape=jax.ShapeDtypeStruct(x.shape, F32),
        input_output_aliases={2: 0},
        compiler_params=_cparams("parallel"),
        name="mamba_out",
    )(y, z, x, norm_g, w_o)


def _rope_tables(pos):
    half = QK_ROPE // 2
    inv = jnp.power(ROPE_THETA, -jnp.arange(half, dtype=F32) / half)
    ang = pos.astype(F32)[:, None] * inv
    reps = LANES // half
    return jnp.tile(jnp.cos(ang), (1, reps)), jnp.tile(jnp.sin(ang), (1, reps))


def _rot_cols(w):
    half = w.shape[-1] // 2
    return jnp.concatenate([-w[..., half:], w[..., :half]], axis=-1)


def _mla_weights(w_in, q_norm, w_uq, kv_norm, w_ukv, w_o):
    kpe = w_in[:, B_KPE:]
    pad = jnp.zeros((D_MODEL, LANES - QK_ROPE), w_in.dtype)
    w_in_aug = jnp.concatenate([w_in[:, :B_KPE], kpe, pad, _rot_cols(kpe), pad], axis=1)
    uq = w_uq.reshape(Q_LORA, B_HEADS, QK_NOPE + QK_ROPE)
    pe = uq[:, :, QK_NOPE:]
    ukv = w_ukv.reshape(KV_LORA, B_HEADS, QK_NOPE + V_HEAD)
    return {
        "w_in": w_in_aug.astype(BF16),
        "q_norm": q_norm.reshape(1, Q_LORA),
        "kv_norm": kv_norm.reshape(1, KV_LORA),
        "w_nope": uq[:, :, :QK_NOPE].reshape(Q_LORA, B_HEADS * QK_NOPE).astype(BF16),
        "w_pe": pe.reshape(Q_LORA, B_HEADS * QK_ROPE).astype(BF16),
        "w_pe_rot": _rot_cols(pe).reshape(Q_LORA, B_HEADS * QK_ROPE).astype(BF16),
        "w_uk": jnp.transpose(ukv[:, :, :QK_NOPE], (1, 2, 0)).astype(BF16),
        "w_uv": jnp.transpose(ukv[:, :, QK_NOPE:], (1, 0, 2)).astype(BF16),
        "w_o": w_o.astype(BF16),
    }


def _pad_lanes(v):
    return jnp.pad(v, (0, LANES - v.shape[0])).reshape(1, LANES)


def _expand_heads(v):
    return jnp.repeat(v, C_HEAD_DIM).reshape(1, D_INNER)


def _mamba_weights(w_in, conv_w, conv_b, dt_bias, a_log, d_skip, norm_g, w_o):
    w_dt = w_in[:, D_INNER + CONV_DIM:]
    pad = jnp.zeros((D_MODEL, LANES - C_HEADS), w_in.dtype)
    return {
        "w_in": jnp.concatenate([w_in, pad], axis=1).astype(BF16),
        "w_dt_x": jnp.repeat(w_dt, C_HEAD_DIM, axis=1).astype(BF16)[None],
        "conv_w": conv_w,
        "conv_b": conv_b.reshape(1, CONV_DIM),
        "dt_bias": _pad_lanes(dt_bias.astype(F32)),
        "a_log": _pad_lanes(a_log.astype(F32)),
        "dt_bias_x": _expand_heads(dt_bias.astype(F32)),
        "a_log_x": _expand_heads(a_log.astype(F32)),
        "d_skip": _expand_heads(d_skip),
        "norm_g": norm_g.reshape(1, D_INNER),
        "w_o": w_o.astype(BF16),
    }


def kernel(x_prompt, x_sample, state_swa_k, state_swa_v, cache_mla, state_ssm, state_conv, page_table, p_prompt, p_sample, norm_ffn1, ffn1_w_gu, ffn1_w_down, norm_mix, norm_ffn2, ffn2_w_gu, ffn2_w_down, norm_ple, ple_w_proj, ple_w_gate, rel_bias_table, a_w_qkv, a_w_o, a_sinks, b_w_in, b_q_norm, b_w_uq, b_kv_norm, b_w_ukv, b_w_o, c_w_in, c_conv_w, c_conv_b, c_dt_bias, c_a_log, c_d, c_norm, c_w_o, final_norm):
    nb_p, t_p, _ = x_prompt.shape
    nb_s, l_s, _ = x_sample.shape
    depth = norm_ffn1.shape[0]
    rows_p = nb_p * t_p
    rows_s = nb_s * l_s
    w_buf = state_swa_k.shape[2]
    assert rows_p % FFN_TM == 0 and (rows_p + rows_s) % FFN_TM == 0 and rows_s % ROW_TM == 0
    assert t_p % B_KVBLOCK == 0 and t_p % SSD_CHUNK == 0 and t_p % A_BLOCK == 0
    assert l_s == SUBLANES and w_buf == WINDOW and page_table.shape[1] % B_PAGES_PER_STEP == 0

    x = jnp.concatenate([x_prompt.reshape(rows_p, D_MODEL), x_sample.reshape(rows_s, D_MODEL)], axis=0)
    p_all = jnp.concatenate([p_prompt.reshape(depth, rows_p, D_PLE), p_sample.reshape(depth, rows_s, D_PLE)], axis=1)

    def norm3(g):
        return g.reshape(g.shape[0], 1, g.shape[1])

    norm_ffn1, norm_mix, norm_ffn2, norm_ple = map(norm3, (norm_ffn1, norm_mix, norm_ffn2, norm_ple))
    ffn1_w_gu, ffn1_w_down, ffn2_w_gu, ffn2_w_down, ple_w_proj, ple_w_gate, a_w_qkv, a_w_o = (
        w.astype(BF16) for w in (ffn1_w_gu, ffn1_w_down, ffn2_w_gu, ffn2_w_down, ple_w_proj, ple_w_gate, a_w_qkv, a_w_o))
    final_norm = final_norm.reshape(1, D_MODEL)

    i = jnp.arange(A_BLOCK)[:, None]
    j = jnp.arange(2 * A_BLOCK)[None, :]
    bias_p = _rel_bias(rel_bias_table, _t5_bucket(A_BLOCK + i - j))
    q_pos = PAST_LEN + jnp.arange(l_s)
    k_pos = jnp.concatenate([PAST_LEN - w_buf + jnp.arange(w_buf), q_pos])
    bias_s = _rel_bias(rel_bias_table, _t5_bucket(q_pos[:, None] - k_pos[None, :]))
    swa_k_buf = jnp.transpose(state_swa_k, (0, 1, 3, 4, 2))
    swa_v_buf = jnp.transpose(state_swa_v, (0, 1, 3, 4, 2))

    pos = jnp.concatenate([jnp.tile(jnp.arange(t_p), nb_p), jnp.tile(PAST_LEN + jnp.arange(l_s), nb_s)])
    cos, sin = _rope_tables(pos)

    ssm_flat = state_ssm.reshape(state_ssm.shape[0], nb_s, D_INNER, D_STATE)
    cache_t = jnp.swapaxes(cache_mla, 2, 3)

    def last_rows(a, n):
        return jnp.stack([a[(b + 1) * t_p - n:(b + 1) * t_p] for b in range(nb_p)])

    swa_k_p, swa_v_p, mla_p, ssm_p, conv_p = [], [], [], [], []
    swa_k_s, swa_v_s, mla_s, ssm_s, conv_s = [], [], [], [], []
    for layer in range(depth):
        kind, sl = layer % N_MIXERS, layer // N_MIXERS
        prev_ple = None if layer == 0 else (p_all, norm_ple, ple_w_proj, ple_w_gate, layer - 1)
        x = _ffn(x, norm_ffn1, ffn1_w_gu, ffn1_w_down, layer, ple=prev_ple)
        if kind == 0:
            qkv = _norm_proj(x, norm_mix, a_w_qkv, layer, sl, "swa_qkv")
            x = _swa_prompt(x, qkv, bias_p, a_sinks[sl], a_w_o, sl, nb_p, t_p)
            o_s = _swa_sample(qkv, swa_k_buf, swa_v_buf, bias_s, a_sinks[sl], sl, rows_p, nb_s, l_s)
            x = _res_proj(x, o_s, a_w_o, sl, rows_p, "swa_out_sample")
            kv_p = last_rows(qkv, w_buf)[:, :, A_Q:].reshape(nb_p, w_buf, 2, A_KV_HEADS, A_HEAD_DIM)
            kv_s = qkv[rows_p:, A_Q:].reshape(nb_s, l_s, 2, A_KV_HEADS, A_HEAD_DIM)
            swa_k_p.append(kv_p[:, :, 0])
            swa_v_p.append(kv_p[:, :, 1])
            swa_k_s.append(jnp.concatenate([state_swa_k[sl], kv_s[:, :, 0]], axis=1)[:, l_s:])
            swa_v_s.append(jnp.concatenate([state_swa_v[sl], kv_s[:, :, 1]], axis=1)[:, l_s:])
        elif kind == 1:
            w = _mla_weights(b_w_in[sl], b_q_norm[sl], b_w_uq[sl], b_kv_norm[sl], b_w_ukv[sl], b_w_o[sl])
            kv_rows_p, q_p = _mla_proj(x, norm_mix, layer, w, cos, sin, 0, rows_p, None)
            kv_rows_s, q_s = _mla_proj(x, norm_mix, layer, w, cos, sin, rows_p, rows_s, l_s)
            x = _mla_prompt(x, q_p, kv_rows_p, w, nb_p, t_p)
            o_lat = _mla_sample(q_s, kv_rows_s, cache_t, page_table, sl, 0, nb_s, l_s)
            x = _mla_out_sample(x, o_lat, w, rows_p, l_s)
            mla_p.append(kv_rows_p.reshape(nb_p, t_p, KV_ROW))
            mla_s.append(kv_rows_s.reshape(nb_s, l_s, KV_ROW))
        else:
            w = _mamba_weights(c_w_in[sl], c_conv_w[sl], c_conv_b[sl], c_dt_bias[sl], c_a_log[sl], c_d[sl],
                               c_norm[sl], c_w_o[sl])
            z, xbc, dt = _mamba_in(x, norm_mix, layer, w["w_in"])
            dtx = _norm_proj(x[rows_p:], norm_mix, w["w_dt_x"], layer, 0, "mamba_dt_sample")
            y_p, st_p = _ssd_prompt(xbc, dt, w, nb_p, t_p)
            y_s, st_s = _ssd_sample(xbc, dtx, state_conv, ssm_flat, sl, w, rows_p, nb_s, l_s)
            x = _mamba_out(x, y_p, z, w["norm_g"], w["w_o"], 0)
            x = _mamba_out(x, y_s, z, w["norm_g"], w["w_o"], rows_p)
            ssm_p.append(st_p.reshape(nb_p, C_HEADS, C_HEAD_DIM, D_STATE))
            ssm_s.append(st_s.reshape(nb_s, C_HEADS, C_HEAD_DIM, D_STATE))
            conv_p.append(last_rows(xbc, D_CONV - 1))
            conv_s.append(jnp.concatenate([state_conv[sl], xbc[rows_p:].reshape(nb_s, l_s, CONV_DIM)],
                                          axis=1)[:, -(D_CONV - 1):])
        x = _ffn(x, norm_ffn2, ffn2_w_gu, ffn2_w_down, layer)
    ple_last = functools.partial(_ple, x, p_all, norm_ple, ple_w_proj, ple_w_gate, final_norm, depth - 1, True)
    y_p = ple_last(0, rows_p)
    y_s = ple_last(rows_p, rows_s)

    return (y_p.reshape(nb_p, t_p, D_MODEL), y_s.reshape(nb_s, l_s, D_MODEL),
            jnp.stack(swa_k_p), jnp.stack(swa_v_p), jnp.stack(mla_p), jnp.stack(ssm_p), jnp.stack(conv_p),
            jnp.stack(swa_k_s), jnp.stack(swa_v_s), jnp.stack(mla_s), jnp.stack(ssm_s), jnp.stack(conv_s))
```

```python
import functools
import math

import jax
import jax.numpy as jnp
from jax import lax
from jax.experimental import pallas as pl
from jax.experimental.pallas import tpu as pltpu

F32 = jnp.float32
BF16 = jnp.bfloat16

D_MODEL = 1024
D_FF = 2816
D_PLE = 256
EPS = 1e-6
NEG_INF = -1e30
PAST_LEN = 8192
PAGE_SIZE = 128
N_MIXERS = 3

A_HEADS = 16
A_KV_HEADS = 4
A_HEAD_DIM = 64
A_GROUP = A_HEADS // A_KV_HEADS
A_Q = A_HEADS * A_HEAD_DIM
A_KV = A_KV_HEADS * A_HEAD_DIM
WINDOW = 128
A_BLOCK = 128
N_BUCKETS = 32
BUCKET_MAX_DIST = 128

B_HEADS = 16
Q_LORA = 768
KV_LORA = 256
QK_NOPE = 64
QK_ROPE = 32
V_HEAD = 64
KV_ROW = KV_LORA + QK_ROPE
ROPE_THETA = 10000.0
MLA_SCALE = (QK_NOPE + QK_ROPE) ** -0.5
B_QBLOCK = 128
B_KVBLOCK = 512
B_PAGES_PER_STEP = 32

D_INNER = 2 * D_MODEL
C_HEAD_DIM = 64
C_HEADS = D_INNER // C_HEAD_DIM
C_GROUPS = 4
C_HPG = C_HEADS // C_GROUPS
D_STATE = 128
D_CONV = 4
C_BC = C_GROUPS * D_STATE
CONV_DIM = D_INNER + 2 * C_BC
SSD_CHUNK = 128

LANES = 128
SUBLANES = 8
VMEM_LIMIT = 56 * 1024 * 1024


def _cparams(*sem):
    return pltpu.CompilerParams(dimension_semantics=sem, vmem_limit_bytes=VMEM_LIMIT)


def _resident(shape, index_map):
    return pl.BlockSpec(shape, index_map, pipeline_mode=pl.Buffered(1))


def _dot(a, b):
    return jnp.dot(a, b, preferred_element_type=F32)


def _dot_nt(a, b):
    return lax.dot_general(a, b, (((1,), (1,)), ((), ())), preferred_element_type=F32)


def _dot_tn(a, b):
    return lax.dot_general(a, b, (((0,), (0,)), ((), ())), preferred_element_type=F32)


def _dot_f32(a, b):
    return jnp.dot(a, b, preferred_element_type=F32, precision=lax.Precision.HIGHEST)


def _rms(x, g):
    r = lax.rsqrt(jnp.mean(x * x, axis=-1, keepdims=True) + EPS)
    return (x * r) * g


def _silu(x):
    return x * jax.nn.sigmoid(x)


def _softplus(x):
    return jnp.maximum(x, 0.0) + jnp.log1p(jnp.exp(-jnp.abs(x)))


def _lanes(a, n):
    if n <= LANES:
        return a[:, :n]
    return jnp.concatenate([a] * (n // LANES), axis=1)


FFN_TM = 512
FFN_TF = 256


def _ple_update(x, p_ref, g_ref, wp_ref, wg_ref):
    gate = _dot(_rms(x, g_ref[...]).astype(BF16), wg_ref[...])
    proj = _dot(p_ref[...].astype(BF16), wp_ref[...])
    return x + proj * jax.nn.sigmoid(gate)


def _ffn_body(x_ref, g_ref, wgu_ref, wd_ref, *refs, ple):
    x = x_ref[...]
    if ple:
        x = _ple_update(x, *refs[:4])
    o_ref = refs[-1]
    h = _rms(x, g_ref[...]).astype(BF16)
    acc = jnp.zeros(x.shape, F32)
    for j in range(D_FF // FFN_TF):
        lo = j * FFN_TF
        gate = _dot(h, wgu_ref[:, lo:lo + FFN_TF])
        up = _dot(h, wgu_ref[:, D_FF + lo:D_FF + lo + FFN_TF])
        act = (_silu(gate) * up).astype(BF16)
        acc = acc + _dot(act, wd_ref[lo:lo + FFN_TF, :])
    o_ref[...] = x + 0.5 * acc


def _ffn(x, norm, w_gu, w_down, layer, ple=None):
    rows = x.shape[0]
    in_specs = [
        pl.BlockSpec((FFN_TM, D_MODEL), lambda i: (i, 0)),
        _resident((None, 1, D_MODEL), lambda i: (layer, 0, 0)),
        _resident((None, D_MODEL, 2 * D_FF), lambda i: (layer, 0, 0)),
        _resident((None, D_FF, D_MODEL), lambda i: (layer, 0, 0)),
    ]
    args = [x, norm, w_gu, w_down]
    if ple is not None:
        pl_layer = ple[4]
        in_specs += [
            pl.BlockSpec((None, FFN_TM, D_PLE), lambda i: (pl_layer, i, 0)),
            _resident((None, 1, D_MODEL), lambda i: (pl_layer, 0, 0)),
            _resident((None, D_PLE, D_MODEL), lambda i: (pl_layer, 0, 0)),
            _resident((None, D_MODEL, D_MODEL), lambda i: (pl_layer, 0, 0)),
        ]
        args += list(ple[:4])
    return pl.pallas_call(
        functools.partial(_ffn_body, ple=ple is not None),
        grid=(rows // FFN_TM,),
        in_specs=in_specs,
        out_specs=pl.BlockSpec((FFN_TM, D_MODEL), lambda i: (i, 0)),
        out_shape=jax.ShapeDtypeStruct(x.shape, F32),
        compiler_params=_cparams("parallel"),
        name="ffn_ple" if ple is not None else "ffn",
    )(*args)


ROW_TM = 256


def _ple_body(x_ref, p_ref, g_ref, wp_ref, wg_ref, gf_ref, o_ref, *, final):
    y = _ple_update(x_ref[...], p_ref, g_ref, wp_ref, wg_ref)
    if final:
        y = _rms(y, gf_ref[...])
    o_ref[...] = y


def _ple(x, p_all, norm, w_proj, w_gate, final_norm, layer, final, row0=0, rows=None):
    rows = x.shape[0] if rows is None else rows
    blk0 = row0 // ROW_TM
    return pl.pallas_call(
        functools.partial(_ple_body, final=final),
        grid=(rows // ROW_TM,),
        in_specs=[
            pl.BlockSpec((ROW_TM, D_MODEL), lambda i: (i + blk0, 0)),
            pl.BlockSpec((None, ROW_TM, D_PLE), lambda i: (layer, i + blk0, 0)),
            _resident((None, 1, D_MODEL), lambda i: (layer, 0, 0)),
            _resident((None, D_PLE, D_MODEL), lambda i: (layer, 0, 0)),
            _resident((None, D_MODEL, D_MODEL), lambda i: (layer, 0, 0)),
            _resident((1, D_MODEL), lambda i: (0, 0)),
        ],
        out_specs=pl.BlockSpec((ROW_TM, D_MODEL), lambda i: (i, 0)),
        out_shape=jax.ShapeDtypeStruct((rows, D_MODEL), F32),
        compiler_params=_cparams("parallel"),
        name="ple",
    )(x, p_all, norm, w_proj, w_gate, final_norm)


def _norm_proj_body(x_ref, g_ref, w_ref, o_ref):
    o_ref[...] = _dot(_rms(x_ref[...], g_ref[...]).astype(BF16), w_ref[...])


def _norm_proj(x, norm, w, layer, wl, name):
    rows = x.shape[0]
    k, n = w.shape[1:]
    return pl.pallas_call(
        _norm_proj_body,
        grid=(rows // ROW_TM,),
        in_specs=[
            pl.BlockSpec((ROW_TM, k), lambda i: (i, 0)),
            _resident((None, 1, k), lambda i: (layer, 0, 0)),
            _resident((None, k, n), lambda i: (wl, 0, 0)),
        ],
        out_specs=pl.BlockSpec((ROW_TM, n), lambda i: (i, 0)),
        out_shape=jax.ShapeDtypeStruct((rows, n), F32),
        compiler_params=_cparams("parallel"),
        name=name,
    )(x, norm, w)


def _res_proj_body(x_ref, a_ref, w_ref, o_ref):
    o_ref[...] = x_ref[...] + _dot(a_ref[...].astype(BF16), w_ref[...])


def _res_proj(x, a, w, wl, row0, name):
    rows = a.shape[0]
    k, n = w.shape[1:]
    blk0 = row0 // ROW_TM
    return pl.pallas_call(
        _res_proj_body,
        grid=(rows // ROW_TM,),
        in_specs=[
            pl.BlockSpec((ROW_TM, n), lambda i: (i + blk0, 0)),
            pl.BlockSpec((ROW_TM, k), lambda i: (i, 0)),
            _resident((None, k, n), lambda i: (wl, 0, 0)),
        ],
        out_specs=pl.BlockSpec((ROW_TM, n), lambda i: (i + blk0, 0)),
        out_shape=jax.ShapeDtypeStruct(x.shape, F32),
        input_output_aliases={0: 0},
        compiler_params=_cparams("parallel"),
        name=name,
    )(x, a, w)


def _t5_bucket(dist):
    max_exact = N_BUCKETS // 2
    d = jnp.maximum(dist, 0)
    large = max_exact + (jnp.log(jnp.maximum(d, 1).astype(F32) / max_exact)
                         / math.log(BUCKET_MAX_DIST / max_exact) * (N_BUCKETS - max_exact)).astype(jnp.int32)
    large = jnp.minimum(large, N_BUCKETS - 1)
    return jnp.where(d < max_exact, d, large)


def _bias_body(tab_ref, bm_ref, o_ref):
    h = pl.program_id(0)
    bm = bm_ref[...]
    acc = jnp.zeros(bm.shape, F32)
    for b in range(N_BUCKETS):
        acc = jnp.where(bm == b, tab_ref[b, h], acc)
    o_ref[...] = acc


def _rel_bias(table, bucket_map):
    r, c = bucket_map.shape
    return pl.pallas_call(
        _bias_body,
        grid=(A_HEADS,),
        in_specs=[pl.BlockSpec(memory_space=pltpu.SMEM), pl.BlockSpec((r, c), lambda h: (0, 0))],
        out_specs=pl.BlockSpec((None, r, c), lambda h: (h, 0, 0)),
        out_shape=jax.ShapeDtypeStruct((A_HEADS, r, c), F32),
        compiler_params=_cparams("parallel"),
        name="rel_bias",
    )(table, bucket_map)


def _swa_heads(q_ref, k, v, bias_ref, sink_ref, valid, o_scr):
    scale = A_HEAD_DIM ** -0.5
    for g in range(A_KV_HEADS):
        kg = k[:, g * A_HEAD_DIM:(g + 1) * A_HEAD_DIM].astype(BF16)
        vg = v[:, g * A_HEAD_DIM:(g + 1) * A_HEAD_DIM].astype(BF16)
        for hh in range(A_GROUP):
            h = g * A_GROUP + hh
            hs = slice(h * A_HEAD_DIM, (h + 1) * A_HEAD_DIM)
            s = _dot_nt(q_ref[:, hs].astype(BF16), kg) * scale + bias_ref[h]
            s = jnp.where(valid, s, NEG_INF)
            sink = sink_ref[h]
            m = jnp.maximum(jnp.max(s, axis=-1, keepdims=True), sink)
            e = jnp.exp(s - m)
            den = jnp.sum(e, axis=-1, keepdims=True) + jnp.exp(sink - m)
            o_scr[:, hs] = _dot((e / den).astype(BF16), vg)


def _swa_prompt_body(sink_ref, q_ref, kp_ref, ko_ref, vp_ref, vo_ref, bias_ref, x_ref, wo_ref, o_ref, o_scr):
    n = pl.program_id(1)
    row = lax.broadcasted_iota(jnp.int32, (A_BLOCK, A_BLOCK), 0)
    col = lax.broadcasted_iota(jnp.int32, (A_BLOCK, A_BLOCK), 1)
    valid = jnp.concatenate([col > row + jnp.where(n > 0, 0, A_BLOCK), col <= row], axis=1)
    k = jnp.concatenate([kp_ref[...], ko_ref[...]], axis=0)
    v = jnp.concatenate([vp_ref[...], vo_ref[...]], axis=0)
    _swa_heads(q_ref, k, v, bias_ref, sink_ref, valid, o_scr)
    o_ref[...] = x_ref[...] + _dot(o_scr[...].astype(BF16), wo_ref[...])


def _swa_prompt(x, qkv, bias, sinks, w_o, wl, nb_p, t_p):
    nblk = t_p // A_BLOCK
    kcol = A_Q // A_KV
    vcol = kcol + 1

    def own(b, n):
        return b * nblk + n

    def prev(b, n):
        return b * nblk + jnp.maximum(n - 1, 0)

    return pl.pallas_call(
        _swa_prompt_body,
        grid=(nb_p, nblk),
        in_specs=[
            pl.BlockSpec(memory_space=pltpu.SMEM),
            pl.BlockSpec((A_BLOCK, A_Q), lambda b, n: (own(b, n), 0)),
            pl.BlockSpec((A_BLOCK, A_KV), lambda b, n: (prev(b, n), kcol)),
            pl.BlockSpec((A_BLOCK, A_KV), lambda b, n: (own(b, n), kcol)),
            pl.BlockSpec((A_BLOCK, A_KV), lambda b, n: (prev(b, n), vcol)),
            pl.BlockSpec((A_BLOCK, A_KV), lambda b, n: (own(b, n), vcol)),
            _resident((A_HEADS, A_BLOCK, 2 * A_BLOCK), lambda b, n: (0, 0, 0)),
            pl.BlockSpec((A_BLOCK, D_MODEL), lambda b, n: (own(b, n), 0)),
            _resident((None, A_Q, D_MODEL), lambda b, n: (wl, 0, 0)),
        ],
        out_specs=pl.BlockSpec((A_BLOCK, D_MODEL), lambda b, n: (own(b, n), 0)),
        out_shape=jax.ShapeDtypeStruct(x.shape, F32),
        scratch_shapes=[pltpu.VMEM((A_BLOCK, A_Q), F32)],
        input_output_aliases={7: 0},
        compiler_params=_cparams("parallel", "arbitrary"),
        name="swa_prompt",
    )(sinks, qkv, qkv, qkv, qkv, qkv, bias, x, w_o)


SWA_S_NB = 8


def _swa_sample_body(qkv_ref, kt_ref, vt_ref, bb_ref, bn_ref, sink_ref, o_ref, *, l_s, w_buf):
    r = A_GROUP * l_s
    scale = A_HEAD_DIM ** -0.5
    tok_b = lax.broadcasted_iota(jnp.int32, (r, w_buf), 0) % l_s
    key_b = lax.broadcasted_iota(jnp.int32, (r, w_buf), 1)
    valid_b = w_buf + tok_b - key_b < WINDOW
    tok_n = lax.broadcasted_iota(jnp.int32, (r, l_s), 0) % l_s
    key_n = lax.broadcasted_iota(jnp.int32, (r, l_s), 1)
    valid_n = key_n <= tok_n

    def one(b, carry):
        rows = pl.ds(pl.multiple_of(b * l_s, l_s), l_s)
        for g in range(A_KV_HEADS):
            heads = range(g * A_GROUP, (g + 1) * A_GROUP)
            qg = jnp.concatenate([qkv_ref[rows, h * A_HEAD_DIM:(h + 1) * A_HEAD_DIM] for h in heads], axis=0)
            qg = qg.astype(BF16)
            kn = qkv_ref[rows, A_Q + g * A_HEAD_DIM:A_Q + (g + 1) * A_HEAD_DIM].astype(BF16)
            vn = qkv_ref[rows, A_Q + A_KV + g * A_HEAD_DIM:A_Q + A_KV + (g + 1) * A_HEAD_DIM].astype(BF16)
            sb = jnp.where(valid_b, _dot(qg, kt_ref[b, g].astype(BF16)) * scale + bb_ref[g], NEG_INF)
            sn = jnp.where(valid_n, _dot_nt(qg, kn) * scale + bn_ref[g], NEG_INF)
            sink = sink_ref[g][:, :1]
            m = jnp.maximum(jnp.maximum(jnp.max(sb, axis=-1, keepdims=True), jnp.max(sn, axis=-1, keepdims=True)), sink)
            eb = jnp.exp(sb - m)
            en = jnp.exp(sn - m)
            den = jnp.sum(eb, axis=-1, keepdims=True) + jnp.sum(en, axis=-1, keepdims=True) + jnp.exp(sink - m)
            og = _dot_nt((eb / den).astype(BF16), vt_ref[b, g].astype(BF16)) + _dot((en / den).astype(BF16), vn)
            for i, h in enumerate(heads):
                o_ref[rows, h * A_HEAD_DIM:(h + 1) * A_HEAD_DIM] = og[i * l_s:(i + 1) * l_s]
        return carry

    lax.fori_loop(0, SWA_S_NB, one, 0, unroll=2)


def _swa_sample(qkv, kt_buf, vt_buf, bias, sinks, sl, row0, nb_s, l_s):
    w_buf = kt_buf.shape[-1]
    r = A_GROUP * l_s
    tm = SWA_S_NB * l_s
    blk0 = row0 // tm
    bias_g = bias.reshape(A_KV_HEADS, r, w_buf + l_s)
    sink_g = jnp.broadcast_to(sinks.reshape(A_KV_HEADS, A_GROUP, 1, 1),
                              (A_KV_HEADS, A_GROUP, l_s, LANES)).reshape(A_KV_HEADS, r, LANES)
    buf_spec = pl.BlockSpec((None, SWA_S_NB, A_KV_HEADS, A_HEAD_DIM, w_buf), lambda i: (sl, i, 0, 0, 0))
    const3 = lambda i: (0, 0, 0)
    return pl.pallas_call(
        functools.partial(_swa_sample_body, l_s=l_s, w_buf=w_buf),
        grid=(nb_s // SWA_S_NB,),
        in_specs=[
            pl.BlockSpec((tm, A_Q + 2 * A_KV), lambda i: (i + blk0, 0)),
            buf_spec,
            buf_spec,
            _resident((A_KV_HEADS, r, w_buf), const3),
            _resident((A_KV_HEADS, r, l_s), const3),
            _resident((A_KV_HEADS, r, LANES), const3),
        ],
        out_specs=pl.BlockSpec((tm, A_Q), lambda i: (i, 0)),
        out_shape=jax.ShapeDtypeStruct((nb_s * l_s, A_Q), F32),
        compiler_params=_cparams("parallel"),
        name="swa_sample",
    )(qkv, kt_buf, vt_buf, bias_g[:, :, :w_buf], bias_g[:, :, w_buf:], sink_g)


B_IN_COLS = Q_LORA + KV_LORA + 2 * LANES
B_KPE = Q_LORA + KV_LORA
B_KPE_ROT = B_KPE + LANES


def _mla_proj_body(x_ref, g_ref, win_ref, qn_ref, kvn_ref, wnope_ref, wpe_ref, wper_ref, wuk_ref,
                   cos_ref, sin_ref, kv_ref, q_ref, *maybe_kvb_ref, sample):
    tm = x_ref.shape[0]
    h = _rms(x_ref[...], g_ref[...]).astype(BF16)
    c = _dot(h, win_ref[...])
    cos = cos_ref[...]
    sin = sin_ref[...]
    kv_lat = _rms(c[:, Q_LORA:B_KPE], kvn_ref[...])
    kv_rope = (c[:, B_KPE:B_KPE + QK_ROPE] * cos[:, :QK_ROPE]
               + c[:, B_KPE_ROT:B_KPE_ROT + QK_ROPE] * sin[:, :QK_ROPE])
    kv_ref[:, :KV_LORA] = kv_lat
    kv_ref[:, KV_LORA:] = kv_rope
    for kvb_ref in maybe_kvb_ref:
        kvb_ref[:, :KV_LORA] = kv_lat.astype(BF16)
        kvb_ref[:, KV_LORA:] = kv_rope.astype(BF16)
    qn = _rms(c[:, :Q_LORA], qn_ref[...]).astype(BF16)
    qnope = _dot(qn, wnope_ref[...])
    n_rep = B_HEADS * QK_ROPE // LANES
    qpe = (_dot(qn, wpe_ref[...]) * jnp.concatenate([cos] * n_rep, axis=1)
           + _dot(qn, wper_ref[...]) * jnp.concatenate([sin] * n_rep, axis=1))
    for hd in range(B_HEADS):
        qlat = _dot(qnope[:, hd * QK_NOPE:(hd + 1) * QK_NOPE].astype(BF16), wuk_ref[hd])
        qrope = qpe[:, hd * QK_ROPE:(hd + 1) * QK_ROPE]
        if sample:
            l_s = q_ref.shape[1] // B_HEADS
            rs = slice(hd * l_s, (hd + 1) * l_s)
            q_ref[:, rs, :KV_LORA] = qlat.reshape(tm // l_s, l_s, KV_LORA)
            q_ref[:, rs, KV_LORA:] = qrope.reshape(tm // l_s, l_s, QK_ROPE)
        else:
            for s in range(tm // B_QBLOCK):
                ts = slice(s * B_QBLOCK, (s + 1) * B_QBLOCK)
                q_ref[s, hd, :, :KV_LORA] = qlat[ts].astype(BF16)
                q_ref[s, hd, :, KV_LORA:] = qrope[ts].astype(BF16)


def _mla_proj(x, norm, layer, w, cos, sin, row0, rows, l_s):
    sample = l_s is not None
    tm = ROW_TM
    blk0 = row0 // tm
    if sample:
        q_shape = jax.ShapeDtypeStruct((rows // l_s, B_HEADS * l_s, KV_ROW), F32)
        q_spec = pl.BlockSpec((tm // l_s, B_HEADS * l_s, KV_ROW), lambda i: (i, 0, 0))
    else:
        q_shape = jax.ShapeDtypeStruct((rows // B_QBLOCK, B_HEADS, B_QBLOCK, KV_ROW), BF16)
        q_spec = pl.BlockSpec((tm // B_QBLOCK, B_HEADS, B_QBLOCK, KV_ROW), lambda i: (i, 0, 0, 0))
    const2 = lambda i: (0, 0)
    kv_spec = pl.BlockSpec((tm, KV_ROW), lambda i: (i, 0))
    out_specs = [kv_spec, q_spec]
    out_shape = [jax.ShapeDtypeStruct((rows, KV_ROW), F32), q_shape]
    if not sample:
        out_specs.append(kv_spec)
        out_shape.append(jax.ShapeDtypeStruct((rows, KV_ROW), BF16))
    return pl.pallas_call(
        functools.partial(_mla_proj_body, sample=sample),
        grid=(rows // tm,),
        in_specs=[
            pl.BlockSpec((tm, D_MODEL), lambda i: (i + blk0, 0)),
            _resident((None, 1, D_MODEL), lambda i: (layer, 0, 0)),
            _resident((D_MODEL, B_IN_COLS), const2),
            _resident((1, Q_LORA), const2),
            _resident((1, KV_LORA), const2),
            _resident((Q_LORA, B_HEADS * QK_NOPE), const2),
            _resident((Q_LORA, B_HEADS * QK_ROPE), const2),
            _resident((Q_LORA, B_HEADS * QK_ROPE), const2),
            _resident((B_HEADS, QK_NOPE, KV_LORA), lambda i: (0, 0, 0)),
            pl.BlockSpec((tm, LANES), lambda i: (i + blk0, 0)),
            pl.BlockSpec((tm, LANES), lambda i: (i + blk0, 0)),
        ],
        out_specs=out_specs,
        out_shape=out_shape,
        compiler_params=_cparams("parallel"),
        name="mla_proj_sample" if sample else "mla_proj_prompt",
    )(x, norm, w["w_in"], w["q_norm"], w["kv_norm"], w["w_nope"], w["w_pe"], w["w_pe_rot"], w["w_uk"], cos, sin)


def _softmax_stats(s, m_scr, l_scr):
    c = MLA_SCALE * math.log2(math.e)
    m_prev = m_scr[...]
    m_new = jnp.maximum(m_prev, jnp.max(s, axis=-1, keepdims=True))
    alpha = jnp.exp2((m_prev - m_new) * c)
    p = jnp.exp2((s - _lanes(m_new, s.shape[1])) * c)
    l_scr[...] = alpha * l_scr[...] + jnp.sum(p, axis=-1, keepdims=True)
    m_scr[...] = m_new
    return alpha, p.astype(BF16)


def _softmax_accumulate(s, pv, m_scr, l_scr, acc_scr):
    alpha, p = _softmax_stats(s, m_scr, l_scr)
    acc_scr[...] = acc_scr[...] * _lanes(alpha, KV_LORA) + pv(p)


def _softmax_update(q, kvb, m_scr, l_scr, acc_scr, valid=None):
    s = _dot_nt(q, kvb)
    if valid is not None:
        s = jnp.where(valid, s, NEG_INF)
    _softmax_accumulate(s, lambda p: _dot(p, kvb[:, :KV_LORA]), m_scr, l_scr, acc_scr)


def _softmax_init(m_scr, l_scr, acc_scr):
    m_scr[...] = jnp.full(m_scr.shape, NEG_INF, F32)
    l_scr[...] = jnp.zeros(l_scr.shape, F32)
    acc_scr[...] = jnp.zeros(acc_scr.shape, F32)


def _mla_prompt_body(q_ref, kv_ref, x_ref, wuv_ref, wo_ref, o_ref, s0_scr, s1_scr, m_scr, l_scr, acc_scr, u_scr):
    qi = pl.program_id(1)
    r = B_HEADS * B_QBLOCK
    n_off = qi // (B_KVBLOCK // B_QBLOCK)
    q = q_ref[...].reshape(r, KV_ROW)

    def kv_block(k):
        return kv_ref[pl.ds(pl.multiple_of(k * B_KVBLOCK, B_KVBLOCK), B_KVBLOCK), :]

    def update(s, k):
        kvb = kv_block(k)
        _softmax_accumulate(s, lambda p: _dot(p, kvb[:, :KV_LORA]), m_scr, l_scr, acc_scr)

    def step(s_cur, s_next, k):
        s_next[...] = _dot_nt(q, kv_block(k + 1))
        update(s_cur[...], k)

    def last(s_cur):
        tok = lax.broadcasted_iota(jnp.int32, (r, B_KVBLOCK), 0) & (B_QBLOCK - 1)
        key = lax.broadcasted_iota(jnp.int32, (r, B_KVBLOCK), 1)
        valid = key - tok <= qi * B_QBLOCK - n_off * B_KVBLOCK
        update(jnp.where(valid, s_cur[...], NEG_INF), n_off)
        o = acc_scr[...] * _lanes(1.0 / l_scr[...], KV_LORA)
        for hd in range(B_HEADS):
            oh = o[hd * B_QBLOCK:(hd + 1) * B_QBLOCK].astype(BF16)
            u_scr[:, hd * V_HEAD:(hd + 1) * V_HEAD] = _dot(oh, wuv_ref[hd])
        o_ref[...] = x_ref[...] + _dot(u_scr[...].astype(BF16), wo_ref[...])

    _softmax_init(m_scr, l_scr, acc_scr)
    s0_scr[...] = _dot_nt(q, kv_block(0))

    def two_steps(j, carry):
        step(s0_scr, s1_scr, 2 * j)
        step(s1_scr, s0_scr, 2 * j + 1)
        return carry

    lax.fori_loop(0, n_off // 2, two_steps, 0)
    odd = n_off % 2 == 1

    @pl.when(odd)
    def _():
        step(s0_scr, s1_scr, n_off - 1)
        last(s1_scr)

    @pl.when(jnp.logical_not(odd))
    def _():
        last(s0_scr)


def _mla_prompt(x, q_cat, kv_bf16, w, nb_p, t_p):
    nq = t_p // B_QBLOCK
    r = B_HEADS * B_QBLOCK
    return pl.pallas_call(
        _mla_prompt_body,
        grid=(nb_p, nq),
        in_specs=[
            pl.BlockSpec((None, B_HEADS, B_QBLOCK, KV_ROW), lambda b, qi: (b * nq + qi, 0, 0, 0)),
            pl.BlockSpec((t_p, KV_ROW), lambda b, qi: (b, 0)),
            pl.BlockSpec((B_QBLOCK, D_MODEL), lambda b, qi: (b * nq + qi, 0)),
            _resident((B_HEADS, KV_LORA, V_HEAD), lambda b, qi: (0, 0, 0)),
            _resident((B_HEADS * V_HEAD, D_MODEL), lambda b, qi: (0, 0)),
        ],
        out_specs=pl.BlockSpec((B_QBLOCK, D_MODEL), lambda b, qi: (b * nq + qi, 0)),
        out_shape=jax.ShapeDtypeStruct(x.shape, F32),
        scratch_shapes=[
            pltpu.VMEM((r, B_KVBLOCK), F32),
            pltpu.VMEM((r, B_KVBLOCK), F32),
            pltpu.VMEM((r, LANES), F32),
            pltpu.VMEM((r, LANES), F32),
            pltpu.VMEM((r, KV_LORA), F32),
            pltpu.VMEM((B_QBLOCK, B_HEADS * V_HEAD), F32),
        ],
        input_output_aliases={2: 0},
        compiler_params=_cparams("parallel", "arbitrary"),
        name="mla_prompt",
    )(q_cat, kv_bf16, x, w["w_uv"], w["w_o"])


def _mla_sample_body(pt_ref, q_ref, *refs, l_s):
    page_refs = refs[:B_PAGES_PER_STEP]
    kvn_ref, o_ref, m_scr, l_scr, acc_scr, acct_scr = refs[B_PAGES_PER_STEP:]
    pc = pl.program_id(1)
    r = q_ref.shape[0]
    assert r == LANES

    @pl.when(pc == 0)
    def _():
        _softmax_init(m_scr, l_scr, acct_scr)

    q = q_ref[...].astype(BF16)
    kt = jnp.concatenate([pr[...].astype(BF16) for pr in page_refs], axis=1)
    alpha, p = _softmax_stats(_dot(q, kt), m_scr, l_scr)
    alpha_t = jnp.concatenate([alpha.T] * (KV_LORA // LANES), axis=0)
    acct_scr[...] = acct_scr[...] * alpha_t + _dot_nt(kt[:KV_LORA, :], p)

    @pl.when(pc == pl.num_programs(1) - 1)
    def _():
        acc_scr[...] = acct_scr[...].T
        tok = lax.broadcasted_iota(jnp.int32, (r, l_s), 0) % l_s
        key = lax.broadcasted_iota(jnp.int32, (r, l_s), 1)
        _softmax_update(q, kvn_ref[...].astype(BF16), m_scr, l_scr, acc_scr, valid=key <= tok)
        o_ref[...] = acc_scr[...] * _lanes(1.0 / l_scr[...], KV_LORA)


def _mla_sample(q_cat, kv_rows, cache_t, page_table, cl, row0, nb_s, l_s):
    n_pages = page_table.shape[1]
    steps = n_pages // B_PAGES_PER_STEP
    r = B_HEADS * l_s
    blk0 = row0 // l_s

    def page_spec(k):
        return pl.BlockSpec((None, None, KV_ROW, PAGE_SIZE),
                            lambda b, pc, pt: (cl, pt[b * n_pages + pc * B_PAGES_PER_STEP + k], 0, 0))

    return pl.pallas_call(
        functools.partial(_mla_sample_body, l_s=l_s),
        grid_spec=pltpu.PrefetchScalarGridSpec(
            num_scalar_prefetch=1,
            grid=(nb_s, steps),
            in_specs=[pl.BlockSpec((None, r, KV_ROW), lambda b, pc, pt: (b, 0, 0))]
            + [page_spec(k) for k in range(B_PAGES_PER_STEP)]
            + [pl.BlockSpec((l_s, KV_ROW), lambda b, pc, pt: (b + blk0, 0))],
            out_specs=pl.BlockSpec((None, r, KV_LORA), lambda b, pc, pt: (b, 0, 0)),
            scratch_shapes=[
                pltpu.VMEM((r, LANES), F32),
                pltpu.VMEM((r, LANES), F32),
                pltpu.VMEM((r, KV_LORA), F32),
                pltpu.VMEM((KV_LORA, r), F32),
            ],
        ),
        out_shape=jax.ShapeDtypeStruct((nb_s, r, KV_LORA), F32),
        compiler_params=_cparams("parallel", "arbitrary"),
        name="mla_sample",
    )(page_table.reshape(-1), q_cat, *([cache_t] * B_PAGES_PER_STEP), kv_rows)


def _mla_out_sample_body(o_ref, x_ref, wuv_ref, wo_ref, out_ref, u_scr, *, l_s):
    nb = o_ref.shape[0]
    for hd in range(B_HEADS):
        oh = o_ref[:, hd * l_s:(hd + 1) * l_s, :].reshape(nb * l_s, KV_LORA).astype(BF16)
        u_scr[:, hd * V_HEAD:(hd + 1) * V_HEAD] = _dot(oh, wuv_ref[hd])
    out_ref[...] = x_ref[...] + _dot(u_scr[...].astype(BF16), wo_ref[...])


def _mla_out_sample(x, o_lat, w, row0, l_s):
    nb_s = o_lat.shape[0]
    nb = ROW_TM // l_s
    blk0 = row0 // ROW_TM
    return pl.pallas_call(
        functools.partial(_mla_out_sample_body, l_s=l_s),
        grid=(nb_s // nb,),
        in_specs=[
            pl.BlockSpec((nb, B_HEADS * l_s, KV_LORA), lambda i: (i, 0, 0)),
            pl.BlockSpec((ROW_TM, D_MODEL), lambda i: (i + blk0, 0)),
            _resident((B_HEADS, KV_LORA, V_HEAD), lambda i: (0, 0, 0)),
            _resident((B_HEADS * V_HEAD, D_MODEL), lambda i: (0, 0)),
        ],
        out_specs=pl.BlockSpec((ROW_TM, D_MODEL), lambda i: (i + blk0, 0)),
        out_shape=jax.ShapeDtypeStruct(x.shape, F32),
        scratch_shapes=[pltpu.VMEM((ROW_TM, B_HEADS * V_HEAD), F32)],
        input_output_aliases={1: 0},
        compiler_params=_cparams("parallel"),
        name="mla_out_sample",
    )(o_lat, x, w["w_uv"], w["w_o"])


C_IN_COLS = D_INNER + CONV_DIM + LANES


def _mamba_in_body(x_ref, g_ref, w_ref, z_ref, xbc_ref, dt_ref):
    h = _rms(x_ref[...], g_ref[...]).astype(BF16)
    z_ref[...] = _dot(h, w_ref[:, :D_INNER])
    xbc_ref[...] = _dot(h, w_ref[:, D_INNER:D_INNER + CONV_DIM])
    dt_ref[...] = _dot(h, w_ref[:, D_INNER + CONV_DIM:])


def _mamba_in(x, norm, layer, w_in):
    rows = x.shape[0]
    return pl.pallas_call(
        _mamba_in_body,
        grid=(rows // ROW_TM,),
        in_specs=[
            pl.BlockSpec((ROW_TM, D_MODEL), lambda i: (i, 0)),
            _resident((None, 1, D_MODEL), lambda i: (layer, 0, 0)),
            _resident((D_MODEL, C_IN_COLS), lambda i: (0, 0)),
        ],
        out_specs=[
            pl.BlockSpec((ROW_TM, D_INNER), lambda i: (i, 0)),
            pl.BlockSpec((ROW_TM, CONV_DIM), lambda i: (i, 0)),
            pl.BlockSpec((ROW_TM, LANES), lambda i: (i, 0)),
        ],
        out_shape=[
            jax.ShapeDtypeStruct((rows, D_INNER), F32),
            jax.ShapeDtypeStruct((rows, CONV_DIM), F32),
            jax.ShapeDtypeStruct((rows, LANES), F32),
        ],
        compiler_params=_cparams("parallel"),
        name="mamba_in",
    )(x, norm, w_in)


def _causal_conv(ext_scr, u, cw_ref, cb_ref):
    t = u.shape[0]
    ext_scr[SUBLANES:SUBLANES + t, :] = u
    acc = cb_ref[...] + cw_ref[D_CONV - 1:D_CONV, :] * u
    for k in range(D_CONV - 1):
        lo = SUBLANES - (D_CONV - 1) + k
        acc = acc + cw_ref[k:k + 1, :] * ext_scr[lo:lo + t, :]
    return acc


def _ssd_prompt_body(xbc_ref, dt_ref, cw_ref, cb_ref, dtb_ref, alog_ref, dskip_ref, y_ref, st_ref,
                     ext_scr, ht_scr):
    c = pl.program_id(1)
    t = SSD_CHUNK

    @pl.when(c == 0)
    def _():
        ext_scr[0:SUBLANES, :] = jnp.zeros((SUBLANES, CONV_DIM), F32)
        ht_scr[...] = jnp.zeros(ht_scr.shape, F32)

    u = xbc_ref[...]
    act = _silu(_causal_conv(ext_scr, u, cw_ref, cb_ref))
    ext_scr[0:SUBLANES, :] = u[t - SUBLANES:, :]

    dt = _softplus(dt_ref[...] + dtb_ref[...])
    a = -jnp.exp(alog_ref[...])
    row = lax.broadcasted_iota(jnp.int32, (t, t), 0)
    col = lax.broadcasted_iota(jnp.int32, (t, t), 1)
    causal = col <= row
    acs = _dot_f32(causal.astype(F32), dt * a)
    acs_t = acs.T
    dt_t = dt.T
    to_end = jnp.exp(acs[t - 1:t, :] - acs) * dt
    gw = C_HPG * C_HEAD_DIM
    heads_per_vreg = LANES // C_HEAD_DIM
    assert heads_per_vreg == 2 and t == LANES
    first = lax.broadcasted_iota(jnp.int32, (t, LANES), 1) < C_HEAD_DIM
    for g in range(C_GROUPS):
        bg = act[:, D_INNER + g * D_STATE:D_INNER + (g + 1) * D_STATE]
        cg = act[:, D_INNER + C_BC + g * D_STATE:D_INNER + C_BC + (g + 1) * D_STATE].astype(BF16)
        cb = _dot_nt(cg, bg.astype(BF16))
        bg_t = bg.T.astype(BF16)
        y_off = _dot(cg, ht_scr[:, g * gw:(g + 1) * gw].astype(BF16))
        for jp in range(C_HPG // heads_per_vreg):
            h0 = g * C_HPG + heads_per_vreg * jp
            ps = slice(h0 * C_HEAD_DIM, h0 * C_HEAD_DIM + LANES)
            ms, a_cols, te_cols = [], [], []
            for h in (h0, h0 + 1):
                a_col = jnp.broadcast_to(acs[:, h:h + 1], (t, t))
                a_row = jnp.broadcast_to(acs_t[h:h + 1, :], (t, t))
                dt_row = jnp.broadcast_to(dt_t[h:h + 1, :], (t, t))
                ms.append(cb * jnp.exp(jnp.where(causal, a_col - a_row, -jnp.inf)) * dt_row)
                a_cols.append(a_col)
                te_cols.append(jnp.broadcast_to(to_end[:, h:h + 1], (t, LANES)))
            x2 = act[:, ps]
            lhs = jnp.concatenate(ms, axis=1).astype(BF16)
            rhs = jnp.concatenate([jnp.where(first, x2, 0.0), jnp.where(first, 0.0, x2)], axis=0).astype(BF16)
            a_col2 = jnp.where(first, a_cols[0], a_cols[1])
            y_ref[:, ps] = (_dot(lhs, rhs)
                            + y_off[:, jp * LANES:(jp + 1) * LANES] * jnp.exp(a_col2)
                            + dskip_ref[:, ps] * x2)
            xw = (x2 * jnp.where(first, te_cols[0], te_cols[1])).astype(BF16)
            ht_scr[:, ps] = jnp.exp(a_col2[t - 1:t, :]) * ht_scr[:, ps] + _dot(bg_t, xw)

    @pl.when(c == pl.num_programs(1) - 1)
    def _():
        st_ref[...] = ht_scr[...].T


def _ssd_prompt(xbc, dt, w, nb_p, t_p):
    nc = t_p // SSD_CHUNK
    const2 = lambda b, c: (0, 0)
    return pl.pallas_call(
        _ssd_prompt_body,
        grid=(nb_p, nc),
        in_specs=[
            pl.BlockSpec((SSD_CHUNK, CONV_DIM), lambda b, c: (b * nc + c, 0)),
            pl.BlockSpec((SSD_CHUNK, LANES), lambda b, c: (b * nc + c, 0)),
            _resident((D_CONV, CONV_DIM), const2),
            _resident((1, CONV_DIM), const2),
            _resident((1, LANES), const2),
            _resident((1, LANES), const2),
            _resident((1, D_INNER), const2),
        ],
        out_specs=[
            pl.BlockSpec((SSD_CHUNK, D_INNER), lambda b, c: (b * nc + c, 0)),
            pl.BlockSpec((None, D_INNER, D_STATE), lambda b, c: (b, 0, 0)),
        ],
        out_shape=[
            jax.ShapeDtypeStruct((nb_p * t_p, D_INNER), F32),
            jax.ShapeDtypeStruct((nb_p, D_INNER, D_STATE), F32),
        ],
        scratch_shapes=[
            pltpu.VMEM((SUBLANES + SSD_CHUNK, CONV_DIM), F32),
            pltpu.VMEM((D_STATE, D_INNER), F32),
        ],
        compiler_params=_cparams("parallel", "arbitrary"),
        name="ssd_prompt",
    )(xbc, dt, w["conv_w"], w["conv_b"], w["dt_bias"], w["a_log"], w["d_skip"])


def _ssd_sample_body(xbc_ref, dtx_ref, cprev_ref, h0_ref, cw_ref, cb_ref, dtb_ref, alog_ref, dskip_ref,
                     y_ref, st_ref, ext_scr, *, l_s):
    u = xbc_ref[...]
    ext_scr[SUBLANES - (D_CONV - 1):SUBLANES, :] = cprev_ref[...]
    act = _silu(_causal_conv(ext_scr, u, cw_ref, cb_ref))
    xs = act[:, :D_INNER]

    dt = _softplus(dtx_ref[...] + dtb_ref[...])
    a = -jnp.exp(alog_ref[...])
    row = lax.broadcasted_iota(jnp.int32, (l_s, l_s), 0)
    col = lax.broadcasted_iota(jnp.int32, (l_s, l_s), 1)
    acs = _dot_f32((col <= row).astype(F32), dt * a)
    last = acs[l_s - 1:l_s, :]
    xdt = xs * dt
    xw = xdt * jnp.exp(last - acs)
    decay_col = jnp.broadcast_to(jnp.exp(last), (LANES, D_INNER)).T
    gw = C_HPG * C_HEAD_DIM
    lrow = lax.broadcasted_iota(jnp.int32, (l_s, gw), 0)
    for g in range(C_GROUPS):
        gs = slice(g * gw, (g + 1) * gw)
        bg = act[:, D_INNER + g * D_STATE:D_INNER + (g + 1) * D_STATE]
        cg = act[:, D_INNER + C_BC + g * D_STATE:D_INNER + C_BC + (g + 1) * D_STATE]
        cb = _dot_nt(cg.astype(BF16), bg.astype(BF16))
        h0 = h0_ref[gs, :]
        y = _dot_nt(cg.astype(BF16), h0.astype(BF16)) * jnp.exp(acs[:, gs])
        for s in range(l_s):
            seg = acs[:, gs] - acs[s:s + 1, gs]
            w_s = jnp.exp(jnp.where(lrow >= s, seg, -jnp.inf)) * jnp.broadcast_to(cb[:, s:s + 1], (l_s, gw))
            y = y + w_s * xdt[s:s + 1, gs]
        y_ref[:, gs] = y + dskip_ref[:, gs] * xs[:, gs]
        st_ref[gs, :] = decay_col[gs, :] * h0 + _dot_tn(xw[:, gs].astype(BF16), bg.astype(BF16))


def _ssd_sample(xbc, dtx, conv_prev, h0, sl, w, row0, nb_s, l_s):
    blk0 = row0 // l_s
    const2 = lambda b: (0, 0)
    return pl.pallas_call(
        functools.partial(_ssd_sample_body, l_s=l_s),
        grid=(nb_s,),
        in_specs=[
            pl.BlockSpec((l_s, CONV_DIM), lambda b: (b + blk0, 0)),
            pl.BlockSpec((l_s, D_INNER), lambda b: (b, 0)),
            pl.BlockSpec((None, None, D_CONV - 1, CONV_DIM), lambda b: (sl, b, 0, 0)),
            pl.BlockSpec((None, None, D_INNER, D_STATE), lambda b: (sl, b, 0, 0)),
            _resident((D_CONV, CONV_DIM), const2),
            _resident((1, CONV_DIM), const2),
            _resident((1, D_INNER), const2),
            _resident((1, D_INNER), const2),
            _resident((1, D_INNER), const2),
        ],
        out_specs=[
            pl.BlockSpec((l_s, D_INNER), lambda b: (b, 0)),
            pl.BlockSpec((None, D_INNER, D_STATE), lambda b: (b, 0, 0)),
        ],
        out_shape=[
            jax.ShapeDtypeStruct((nb_s * l_s, D_INNER), F32),
            jax.ShapeDtypeStruct((nb_s, D_INNER, D_STATE), F32),
        ],
        scratch_shapes=[pltpu.VMEM((SUBLANES + l_s, CONV_DIM), F32)],
        compiler_params=_cparams("parallel"),
        name="ssd_sample",
    )(xbc, dtx, conv_prev, h0, w["conv_w"], w["conv_b"], w["dt_bias_x"], w["a_log_x"], w["d_skip"])


def _mamba_out_body(y_ref, z_ref, x_ref, ng_ref, wo_ref, o_ref):
    yz = y_ref[...] * _silu(z_ref[...])
    gw = D_INNER // C_GROUPS
    parts = []
    for g in range(C_GROUPS):
        part = yz[:, g * gw:(g + 1) * gw]
        parts.append(part * lax.rsqrt(jnp.mean(part * part, axis=-1, keepdims=True) + EPS))
    yn = jnp.concatenate(parts, axis=1) * ng_ref[...]
    o_ref[...] = x_ref[...] + _dot(yn.astype(BF16), wo_ref[...])


def _mamba_out(x, y, z, norm_g, w_o, row0):
    rows = y.shape[0]
    blk0 = row0 // ROW_TM
    return pl.pallas_call(
        _mamba_out_body,
        grid=(rows // ROW_TM,),
        in_specs=[
            pl.BlockSpec((ROW_TM, D_INNER), lambda i: (i, 0)),
            pl.BlockSpec((ROW_TM, D_INNER), lambda i: (i + blk0, 0)),
            pl.BlockSpec((ROW_TM, D_MODEL), lambda i: (i + blk0, 0)),
            _resident((1, D_INNER), lambda i: (0, 0)),
            _resident((D_INNER, D_MODEL), lambda i: (0, 0)),
        ],
        out_specs=pl.BlockSpec((ROW_TM, D_MODEL), lambda i: (i + blk0, 0)),
        out_shape=jax.ShapeDtypeStruct(x.shape, F32),
        input_output_aliases={2: 0},
        compiler_params=_cparams("parallel"),
        name="mamba_out",
    )(y, z, x, norm_g, w_o)


def _rope_tables(pos):
    half = QK_ROPE // 2
    inv = jnp.power(ROPE_THETA, -jnp.arange(half, dtype=F32) / half)
    ang = pos.astype(F32)[:, None] * inv
    reps = LANES // half
    return jnp.tile(jnp.cos(ang), (1, reps)), jnp.tile(jnp.sin(ang), (1, reps))


def _rot_cols(w):
    half = w.shape[-1] // 2
    return jnp.concatenate([-w[..., half:], w[..., :half]], axis=-1)


def _mla_weights(w_in, q_norm, w_uq, kv_norm, w_ukv, w_o):
    kpe = w_in[:, B_KPE:]
    pad = jnp.zeros((D_MODEL, LANES - QK_ROPE), w_in.dtype)
    w_in_aug = jnp.concatenate([w_in[:, :B_KPE], kpe, pad, _rot_cols(kpe), pad], axis=1)
    uq = w_uq.reshape(Q_LORA, B_HEADS, QK_NOPE + QK_ROPE)
    pe = uq[:, :, QK_NOPE:]
    ukv = w_ukv.reshape(KV_LORA, B_HEADS, QK_NOPE + V_HEAD)
    return {
        "w_in": w_in_aug.astype(BF16),
        "q_norm": q_norm.reshape(1, Q_LORA),
        "kv_norm": kv_norm.reshape(1, KV_LORA),
        "w_nope": uq[:, :, :QK_NOPE].reshape(Q_LORA, B_HEADS * QK_NOPE).astype(BF16),
        "w_pe": pe.reshape(Q_LORA, B_HEADS * QK_ROPE).astype(BF16),
        "w_pe_rot": _rot_cols(pe).reshape(Q_LORA, B_HEADS * QK_ROPE).astype(BF16),
        "w_uk": jnp.transpose(ukv[:, :, :QK_NOPE], (1, 2, 0)).astype(BF16),
        "w_uv": jnp.transpose(ukv[:, :, QK_NOPE:], (1, 0, 2)).astype(BF16),
        "w_o": w_o.astype(BF16),
    }


def _pad_lanes(v):
    return jnp.pad(v, (0, LANES - v.shape[0])).reshape(1, LANES)


def _expand_heads(v):
    return jnp.repeat(v, C_HEAD_DIM).reshape(1, D_INNER)


def _mamba_weights(w_in, conv_w, conv_b, dt_bias, a_log, d_skip, norm_g, w_o):
    w_dt = w_in[:, D_INNER + CONV_DIM:]
    pad = jnp.zeros((D_MODEL, LANES - C_HEADS), w_in.dtype)
    return {
        "w_in": jnp.concatenate([w_in, pad], axis=1).astype(BF16),
        "w_dt_x": jnp.repeat(w_dt, C_HEAD_DIM, axis=1).astype(BF16)[None],
        "conv_w": conv_w,
        "conv_b": conv_b.reshape(1, CONV_DIM),
        "dt_bias": _pad_lanes(dt_bias.astype(F32)),
        "a_log": _pad_lanes(a_log.astype(F32)),
        "dt_bias_x": _expand_heads(dt_bias.astype(F32)),
        "a_log_x": _expand_heads(a_log.astype(F32)),
        "d_skip": _expand_heads(d_skip),
        "norm_g": norm_g.reshape(1, D_INNER),
        "w_o": w_o.astype(BF16),
    }


def kernel(x_prompt, x_sample, state_swa_k, state_swa_v, cache_mla, state_ssm, state_conv, page_table, p_prompt, p_sample, norm_ffn1, ffn1_w_gu, ffn1_w_down, norm_mix, norm_ffn2, ffn2_w_gu, ffn2_w_down, norm_ple, ple_w_proj, ple_w_gate, rel_bias_table, a_w_qkv, a_w_o, a_sinks, b_w_in, b_q_norm, b_w_uq, b_kv_norm, b_w_ukv, b_w_o, c_w_in, c_conv_w, c_conv_b, c_dt_bias, c_a_log, c_d, c_norm, c_w_o, final_norm):
    nb_p, t_p, _ = x_prompt.shape
    nb_s, l_s, _ = x_sample.shape
    depth = norm_ffn1.shape[0]
    rows_p = nb_p * t_p
    rows_s = nb_s * l_s
    w_buf = state_swa_k.shape[2]
    assert rows_p % FFN_TM == 0 and (rows_p + rows_s) % FFN_TM == 0 and rows_s % ROW_TM == 0
    assert t_p % B_KVBLOCK == 0 and t_p % SSD_CHUNK == 0 and t_p % A_BLOCK == 0
    assert l_s == SUBLANES and w_buf == WINDOW and page_table.shape[1] % B_PAGES_PER_STEP == 0

    x = jnp.concatenate([x_prompt.reshape(rows_p, D_MODEL), x_sample.reshape(rows_s, D_MODEL)], axis=0)
    p_all = jnp.concatenate([p_prompt.reshape(depth, rows_p, D_PLE), p_sample.reshape(depth, rows_s, D_PLE)], axis=1)

    def norm3(g):
        return g.reshape(g.shape[0], 1, g.shape[1])

    norm_ffn1, norm_mix, norm_ffn2, norm_ple = map(norm3, (norm_ffn1, norm_mix, norm_ffn2, norm_ple))
    ffn1_w_gu, ffn1_w_down, ffn2_w_gu, ffn2_w_down, ple_w_proj, ple_w_gate, a_w_qkv, a_w_o = (
        w.astype(BF16) for w in (ffn1_w_gu, ffn1_w_down, ffn2_w_gu, ffn2_w_down, ple_w_proj, ple_w_gate, a_w_qkv, a_w_o))
    final_norm = final_norm.reshape(1, D_MODEL)

    i = jnp.arange(A_BLOCK)[:, None]
    j = jnp.arange(2 * A_BLOCK)[None, :]
    bias_p = _rel_bias(rel_bias_table, _t5_bucket(A_BLOCK + i - j))
    q_pos = PAST_LEN + jnp.arange(l_s)
    k_pos = jnp.concatenate([PAST_LEN - w_buf + jnp.arange(w_buf), q_pos])
    bias_s = _rel_bias(rel_bias_table, _t5_bucket(q_pos[:, None] - k_pos[None, :]))
    swa_k_buf = jnp.transpose(state_swa_k, (0, 1, 3, 4, 2))
    swa_v_buf = jnp.transpose(state_swa_v, (0, 1, 3, 4, 2))

    pos = jnp.concatenate([jnp.tile(jnp.arange(t_p), nb_p), jnp.tile(PAST_LEN + jnp.arange(l_s), nb_s)])
    cos, sin = _rope_tables(pos)

    ssm_flat = state_ssm.reshape(state_ssm.shape[0], nb_s, D_INNER, D_STATE)
    cache_t = jnp.swapaxes(cache_mla, 2, 3)

    def last_rows(a, n):
        return jnp.stack([a[(b + 1) * t_p - n:(b + 1) * t_p] for b in range(nb_p)])

    swa_k_p, swa_v_p, mla_p, ssm_p, conv_p = [], [], [], [], []
    swa_k_s, swa_v_s, mla_s, ssm_s, conv_s = [], [], [], [], []
    for layer in range(depth):
        kind, sl = layer % N_MIXERS, layer // N_MIXERS
        prev_ple = None if layer == 0 else (p_all, norm_ple, ple_w_proj, ple_w_gate, layer - 1)
        x = _ffn(x, norm_ffn1, ffn1_w_gu, ffn1_w_down, layer, ple=prev_ple)
        if kind == 0:
            qkv = _norm_proj(x, norm_mix, a_w_qkv, layer, sl, "swa_qkv")
            x = _swa_prompt(x, qkv, bias_p, a_sinks[sl], a_w_o, sl, nb_p, t_p)
            o_s = _swa_sample(qkv, swa_k_buf, swa_v_buf, bias_s, a_sinks[sl], sl, rows_p, nb_s, l_s)
            x = _res_proj(x, o_s, a_w_o, sl, rows_p, "swa_out_sample")
            kv_p = last_rows(qkv, w_buf)[:, :, A_Q:].reshape(nb_p, w_buf, 2, A_KV_HEADS, A_HEAD_DIM)
            kv_s = qkv[rows_p:, A_Q:].reshape(nb_s, l_s, 2, A_KV_HEADS, A_HEAD_DIM)
            swa_k_p.append(kv_p[:, :, 0])
            swa_v_p.append(kv_p[:, :, 1])
            swa_k_s.append(jnp.concatenate([state_swa_k[sl], kv_s[:, :, 0]], axis=1)[:, l_s:])
            swa_v_s.append(jnp.concatenate([state_swa_v[sl], kv_s[:, :, 1]], axis=1)[:, l_s:])
        elif kind == 1:
            w = _mla_weights(b_w_in[sl], b_q_norm[sl], b_w_uq[sl], b_kv_norm[sl], b_w_ukv[sl], b_w_o[sl])
            kv_rows_p, q_p, kv_bf16 = _mla_proj(x, norm_mix, layer, w, cos, sin, 0, rows_p, None)
            kv_rows_s, q_s = _mla_proj(x, norm_mix, layer, w, cos, sin, rows_p, rows_s, l_s)
            x = _mla_prompt(x, q_p, kv_bf16, w, nb_p, t_p)
            o_lat = _mla_sample(q_s, kv_rows_s, cache_t, page_table, sl, 0, nb_s, l_s)
            x = _mla_out_sample(x, o_lat, w, rows_p, l_s)
            mla_p.append(kv_rows_p.reshape(nb_p, t_p, KV_ROW))
            mla_s.append(kv_rows_s.reshape(nb_s, l_s, KV_ROW))
        else:
            w = _mamba_weights(c_w_in[sl], c_conv_w[sl], c_conv_b[sl], c_dt_bias[sl], c_a_log[sl], c_d[sl],
                               c_norm[sl], c_w_o[sl])
            z, xbc, dt = _mamba_in(x, norm_mix, layer, w["w_in"])
            dtx = _norm_proj(x[rows_p:], norm_mix, w["w_dt_x"], layer, 0, "mamba_dt_sample")
            y_p, st_p = _ssd_prompt(xbc, dt, w, nb_p, t_p)
            y_s, st_s = _ssd_sample(xbc, dtx, state_conv, ssm_flat, sl, w, rows_p, nb_s, l_s)
            x = _mamba_out(x, y_p, z, w["norm_g"], w["w_o"], 0)
            x = _mamba_out(x, y_s, z, w["norm_g"], w["w_o"], rows_p)
            ssm_p.append(st_p.reshape(nb_p, C_HEADS, C_HEAD_DIM, D_STATE))
            ssm_s.append(st_s.reshape(nb_s, C_HEADS, C_HEAD_DIM, D_STATE))
            conv_p.append(last_rows(xbc, D_CONV - 1))
            conv_s.append(jnp.concatenate([state_conv[sl], xbc[rows_p:].reshape(nb_s, l_s, CONV_DIM)],
                                          axis=1)[:, -(D_CONV - 1):])
        x = _ffn(x, norm_ffn2, ffn2_w_gu, ffn2_w_down, layer)
    ple_last = functools.partial(_ple, x, p_all, norm_ple, ple_w_proj, ple_w_gate, final_norm, depth - 1, True)
    y_p = ple_last(0, rows_p)
    y_s = ple_last(rows_p, rows_s)

    return (y_p.reshape(nb_p, t_p, D_MODEL), y_s.reshape(nb_s, l_s, D_MODEL),
            jnp.stack(swa_k_p), jnp.stack(swa_v_p), jnp.stack(mla_p), jnp.stack(ssm_p), jnp.stack(conv_p),
            jnp.stack(swa_k_s), jnp.stack(swa_v_s), jnp.stack(mla_s), jnp.stack(ssm_s), jnp.stack(conv_s))
```

```python
import functools
import math

import jax
import jax.numpy as jnp
from jax import lax
from jax.experimental import pallas as pl
from jax.experimental.pallas import tpu as pltpu

F32 = jnp.float32
BF16 = jnp.bfloat16

D_MODEL = 1024
D_FF = 2816
D_PLE = 256
EPS = 1e-6
NEG_INF = -1e30
PAST_LEN = 8192
PAGE_SIZE = 128
N_MIXERS = 3

A_HEADS = 16
A_KV_HEADS = 4
A_HEAD_DIM = 64
A_GROUP = A_HEADS // A_KV_HEADS
A_Q = A_HEADS * A_HEAD_DIM
A_KV = A_KV_HEADS * A_HEAD_DIM
WINDOW = 128
A_BLOCK = 128
N_BUCKETS = 32
BUCKET_MAX_DIST = 128

B_HEADS = 16
Q_LORA = 768
KV_LORA = 256
QK_NOPE = 64
QK_ROPE = 32
V_HEAD = 64
KV_ROW = KV_LORA + QK_ROPE
ROPE_THETA = 10000.0
MLA_SCALE = (QK_NOPE + QK_ROPE) ** -0.5
B_QBLOCK = 128
B_KVBLOCK = 512
B_PAGES_PER_STEP = 32

D_INNER = 2 * D_MODEL
C_HEAD_DIM = 64
C_HEADS = D_INNER // C_HEAD_DIM
C_GROUPS = 4
C_HPG = C_HEADS // C_GROUPS
D_STATE = 128
D_CONV = 4
C_BC = C_GROUPS * D_STATE
CONV_DIM = D_INNER + 2 * C_BC
SSD_CHUNK = 128

LANES = 128
SUBLANES = 8
VMEM_LIMIT = 56 * 1024 * 1024


def _cparams(*sem):
    return pltpu.CompilerParams(dimension_semantics=sem, vmem_limit_bytes=VMEM_LIMIT)


def _resident(shape, index_map):
    return pl.BlockSpec(shape, index_map, pipeline_mode=pl.Buffered(1))


def _dot(a, b):
    return jnp.dot(a, b, preferred_element_type=F32)


def _dot_nt(a, b):
    return lax.dot_general(a, b, (((1,), (1,)), ((), ())), preferred_element_type=F32)


def _dot_tn(a, b):
    return lax.dot_general(a, b, (((0,), (0,)), ((), ())), preferred_element_type=F32)


def _dot_f32(a, b):
    return jnp.dot(a, b, preferred_element_type=F32, precision=lax.Precision.HIGHEST)


def _rms(x, g):
    r = lax.rsqrt(jnp.mean(x * x, axis=-1, keepdims=True) + EPS)
    return (x * r) * g


def _silu(x):
    return x * jax.nn.sigmoid(x)


def _softplus(x):
    return jnp.maximum(x, 0.0) + jnp.log1p(jnp.exp(-jnp.abs(x)))


def _lanes(a, n):
    if n <= LANES:
        return a[:, :n]
    return jnp.concatenate([a] * (n // LANES), axis=1)


FFN_TM = 512
FFN_TF = 256


def _ple_update(x, p_ref, g_ref, wp_ref, wg_ref):
    gate = _dot(_rms(x, g_ref[...]).astype(BF16), wg_ref[...])
    proj = _dot(p_ref[...].astype(BF16), wp_ref[...])
    return x + proj * jax.nn.sigmoid(gate)


def _ffn_body(x_ref, g_ref, wgu_ref, wd_ref, *refs, ple, proj):
    x = x_ref[...]
    if ple:
        x = _ple_update(x, *refs[:4])
        refs = refs[4:]
    o_ref = refs[-2] if proj else refs[-1]
    h = _rms(x, g_ref[...]).astype(BF16)
    acc = jnp.zeros(x.shape, F32)
    for j in range(D_FF // FFN_TF):
        lo = j * FFN_TF
        gate = _dot(h, wgu_ref[:, lo:lo + FFN_TF])
        up = _dot(h, wgu_ref[:, D_FF + lo:D_FF + lo + FFN_TF])
        act = (_silu(gate) * up).astype(BF16)
        acc = acc + _dot(act, wd_ref[lo:lo + FFN_TF, :])
    y = x + 0.5 * acc
    o_ref[...] = y
    if proj:
        gm_ref, wm_ref, m_ref = refs[0], refs[1], refs[-1]
        m_ref[...] = _dot(_rms(y, gm_ref[...]).astype(BF16), wm_ref[...])


def _ffn(x, norm, w_gu, w_down, layer, ple=None, proj=None):
    rows = x.shape[0]
    in_specs = [
        pl.BlockSpec((FFN_TM, D_MODEL), lambda i: (i, 0)),
        _resident((None, 1, D_MODEL), lambda i: (layer, 0, 0)),
        _resident((None, D_MODEL, 2 * D_FF), lambda i: (layer, 0, 0)),
        _resident((None, D_FF, D_MODEL), lambda i: (layer, 0, 0)),
    ]
    args = [x, norm, w_gu, w_down]
    if ple is not None:
        pl_layer = ple[4]
        in_specs += [
            pl.BlockSpec((None, FFN_TM, D_PLE), lambda i: (pl_layer, i, 0)),
            _resident((None, 1, D_MODEL), lambda i: (pl_layer, 0, 0)),
            _resident((None, D_PLE, D_MODEL), lambda i: (pl_layer, 0, 0)),
            _resident((None, D_MODEL, D_MODEL), lambda i: (pl_layer, 0, 0)),
        ]
        args += list(ple[:4])
    out_specs = pl.BlockSpec((FFN_TM, D_MODEL), lambda i: (i, 0))
    out_shape = jax.ShapeDtypeStruct(x.shape, F32)
    if proj is not None:
        m_norm, m_layer, m_w, m_wl = proj
        n = m_w.shape[2]
        in_specs += [
            _resident((None, 1, D_MODEL), lambda i: (m_layer, 0, 0)),
            _resident((None, D_MODEL, n), lambda i: (m_wl, 0, 0)),
        ]
        args += [m_norm, m_w]
        out_specs = [out_specs, pl.BlockSpec((FFN_TM, n), lambda i: (i, 0))]
        out_shape = [out_shape, jax.ShapeDtypeStruct((rows, n), F32)]
    return pl.pallas_call(
        functools.partial(_ffn_body, ple=ple is not None, proj=proj is not None),
        grid=(rows // FFN_TM,),
        in_specs=in_specs,
        out_specs=out_specs,
        out_shape=out_shape,
        compiler_params=_cparams("parallel"),
        name="ffn" + ("_ple" if ple is not None else "") + ("_proj" if proj is not None else ""),
    )(*args)


ROW_TM = 256


def _ple_body(x_ref, p_ref, g_ref, wp_ref, wg_ref, gf_ref, o_ref, *, final):
    y = _ple_update(x_ref[...], p_ref, g_ref, wp_ref, wg_ref)
    if final:
        y = _rms(y, gf_ref[...])
    o_ref[...] = y


def _ple(x, p_all, norm, w_proj, w_gate, final_norm, layer, final, row0=0, rows=None):
    rows = x.shape[0] if rows is None else rows
    blk0 = row0 // ROW_TM
    return pl.pallas_call(
        functools.partial(_ple_body, final=final),
        grid=(rows // ROW_TM,),
        in_specs=[
            pl.BlockSpec((ROW_TM, D_MODEL), lambda i: (i + blk0, 0)),
            pl.BlockSpec((None, ROW_TM, D_PLE), lambda i: (layer, i + blk0, 0)),
            _resident((None, 1, D_MODEL), lambda i: (layer, 0, 0)),
            _resident((None, D_PLE, D_MODEL), lambda i: (layer, 0, 0)),
            _resident((None, D_MODEL, D_MODEL), lambda i: (layer, 0, 0)),
            _resident((1, D_MODEL), lambda i: (0, 0)),
        ],
        out_specs=pl.BlockSpec((ROW_TM, D_MODEL), lambda i: (i, 0)),
        out_shape=jax.ShapeDtypeStruct((rows, D_MODEL), F32),
        compiler_params=_cparams("parallel"),
        name="ple",
    )(x, p_all, norm, w_proj, w_gate, final_norm)


def _norm_proj_body(x_ref, g_ref, w_ref, o_ref):
    o_ref[...] = _dot(_rms(x_ref[...], g_ref[...]).astype(BF16), w_ref[...])


def _norm_proj(x, norm, w, layer, wl, name):
    rows = x.shape[0]
    k, n = w.shape[1:]
    return pl.pallas_call(
        _norm_proj_body,
        grid=(rows // ROW_TM,),
        in_specs=[
            pl.BlockSpec((ROW_TM, k), lambda i: (i, 0)),
            _resident((None, 1, k), lambda i: (layer, 0, 0)),
            _resident((None, k, n), lambda i: (wl, 0, 0)),
        ],
        out_specs=pl.BlockSpec((ROW_TM, n), lambda i: (i, 0)),
        out_shape=jax.ShapeDtypeStruct((rows, n), F32),
        compiler_params=_cparams("parallel"),
        name=name,
    )(x, norm, w)


def _res_proj_body(x_ref, a_ref, w_ref, o_ref):
    o_ref[...] = x_ref[...] + _dot(a_ref[...].astype(BF16), w_ref[...])


def _res_proj(x, a, w, wl, row0, name):
    rows = a.shape[0]
    k, n = w.shape[1:]
    blk0 = row0 // ROW_TM
    return pl.pallas_call(
        _res_proj_body,
        grid=(rows // ROW_TM,),
        in_specs=[
            pl.BlockSpec((ROW_TM, n), lambda i: (i + blk0, 0)),
            pl.BlockSpec((ROW_TM, k), lambda i: (i, 0)),
            _resident((None, k, n), lambda i: (wl, 0, 0)),
        ],
        out_specs=pl.BlockSpec((ROW_TM, n), lambda i: (i + blk0, 0)),
        out_shape=jax.ShapeDtypeStruct(x.shape, F32),
        input_output_aliases={0: 0},
        compiler_params=_cparams("parallel"),
        name=name,
    )(x, a, w)


def _t5_bucket(dist):
    max_exact = N_BUCKETS // 2
    d = jnp.maximum(dist, 0)
    large = max_exact + (jnp.log(jnp.maximum(d, 1).astype(F32) / max_exact)
                         / math.log(BUCKET_MAX_DIST / max_exact) * (N_BUCKETS - max_exact)).astype(jnp.int32)
    large = jnp.minimum(large, N_BUCKETS - 1)
    return jnp.where(d < max_exact, d, large)


def _bias_body(tab_ref, bm_ref, valid_ref, o_ref):
    h = pl.program_id(0)
    bm = bm_ref[...]
    acc = jnp.zeros(bm.shape, F32)
    for b in range(N_BUCKETS):
        acc = jnp.where(bm == b, tab_ref[b, h], acc)
    o_ref[...] = jnp.where(valid_ref[...] > 0, acc, NEG_INF)


def _rel_bias(table, bucket_map, valid):
    r, c = bucket_map.shape
    return pl.pallas_call(
        _bias_body,
        grid=(A_HEADS,),
        in_specs=[pl.BlockSpec(memory_space=pltpu.SMEM), pl.BlockSpec((r, c), lambda h: (0, 0)),
                  pl.BlockSpec((r, c), lambda h: (0, 0))],
        out_specs=pl.BlockSpec((None, r, c), lambda h: (h, 0, 0)),
        out_shape=jax.ShapeDtypeStruct((A_HEADS, r, c), F32),
        compiler_params=_cparams("parallel"),
        name="rel_bias",
    )(table, bucket_map, valid.astype(jnp.int32))


def _group_sinks(sinks, t):
    return jnp.broadcast_to(sinks.reshape(A_KV_HEADS, A_GROUP, 1, 1),
                            (A_KV_HEADS, A_GROUP, t, LANES)).reshape(A_KV_HEADS, A_GROUP * t, LANES)


def _swa_prompt_body(sink_ref, q_ref, kp_ref, ko_ref, vp_ref, vo_ref, bias_ref, x_ref, wo_ref, o_ref, o_scr):
    scale = A_HEAD_DIM ** -0.5
    for g in range(A_KV_HEADS):
        gs = slice(g * A_HEAD_DIM, (g + 1) * A_HEAD_DIM)
        kg = jnp.concatenate([kp_ref[:, gs], ko_ref[:, gs]], axis=0).astype(BF16)
        vg = jnp.concatenate([vp_ref[:, gs], vo_ref[:, gs]], axis=0).astype(BF16)
        for h in range(g * A_GROUP, (g + 1) * A_GROUP):
            hs = slice(h * A_HEAD_DIM, (h + 1) * A_HEAD_DIM)
            s = _dot_nt(q_ref[:, hs].astype(BF16), kg) * scale + bias_ref[h]
            sink = sink_ref[h]
            m = jnp.maximum(jnp.max(s, axis=-1, keepdims=True), sink)
            e = jnp.exp(s - m)
            den = jnp.sum(e, axis=-1, keepdims=True) + jnp.exp(sink - m)
            o_scr[:, hs] = _dot((e / den).astype(BF16), vg)
    o_ref[...] = x_ref[...] + _dot(o_scr[...].astype(BF16), wo_ref[...])


def _swa_prompt(x, qkv, bias, sinks, w_o, wl, nb_p, t_p):
    nblk = t_p // A_BLOCK
    kcol = A_Q // A_KV
    vcol = kcol + 1

    def own(b, n):
        return b * nblk + n

    def prev(b, n):
        return b * nblk + jnp.maximum(n - 1, 0)

    return pl.pallas_call(
        _swa_prompt_body,
        grid=(nb_p, nblk),
        in_specs=[
            pl.BlockSpec(memory_space=pltpu.SMEM),
            pl.BlockSpec((A_BLOCK, A_Q), lambda b, n: (own(b, n), 0)),
            pl.BlockSpec((A_BLOCK, A_KV), lambda b, n: (prev(b, n), kcol)),
            pl.BlockSpec((A_BLOCK, A_KV), lambda b, n: (own(b, n), kcol)),
            pl.BlockSpec((A_BLOCK, A_KV), lambda b, n: (prev(b, n), vcol)),
            pl.BlockSpec((A_BLOCK, A_KV), lambda b, n: (own(b, n), vcol)),
            pl.BlockSpec((None, A_HEADS, A_BLOCK, 2 * A_BLOCK), lambda b, n: (jnp.minimum(n, 1), 0, 0, 0)),
            pl.BlockSpec((A_BLOCK, D_MODEL), lambda b, n: (own(b, n), 0)),
            _resident((None, A_Q, D_MODEL), lambda b, n: (wl, 0, 0)),
        ],
        out_specs=pl.BlockSpec((A_BLOCK, D_MODEL), lambda b, n: (own(b, n), 0)),
        out_shape=jax.ShapeDtypeStruct(x.shape, F32),
        scratch_shapes=[pltpu.VMEM((A_BLOCK, A_Q), F32)],
        input_output_aliases={7: 0},
        compiler_params=_cparams("parallel", "arbitrary"),
        name="swa_prompt",
    )(sinks, qkv, qkv, qkv, qkv, qkv, bias, x, w_o)


SWA_S_NB = 8


def _swa_sample_body(qkv_ref, kt_ref, vt_ref, bb_ref, bn_ref, sink_ref, o_ref, *, l_s, w_buf):
    scale = A_HEAD_DIM ** -0.5

    def one(b, carry):
        rows = pl.ds(pl.multiple_of(b * l_s, l_s), l_s)
        for g in range(A_KV_HEADS):
            heads = range(g * A_GROUP, (g + 1) * A_GROUP)
            qg = jnp.concatenate([qkv_ref[rows, h * A_HEAD_DIM:(h + 1) * A_HEAD_DIM] for h in heads], axis=0)
            qg = qg.astype(BF16)
            kn = qkv_ref[rows, A_Q + g * A_HEAD_DIM:A_Q + (g + 1) * A_HEAD_DIM].astype(BF16)
            vn = qkv_ref[rows, A_Q + A_KV + g * A_HEAD_DIM:A_Q + A_KV + (g + 1) * A_HEAD_DIM].astype(BF16)
            sb = _dot(qg, kt_ref[b, g].astype(BF16)) * scale + bb_ref[g]
            sn = _dot_nt(qg, kn) * scale + bn_ref[g]
            sink = sink_ref[g][:, :1]
            m = jnp.maximum(jnp.maximum(jnp.max(sb, axis=-1, keepdims=True), jnp.max(sn, axis=-1, keepdims=True)), sink)
            eb = jnp.exp(sb - m)
            en = jnp.exp(sn - m)
            den = jnp.sum(eb, axis=-1, keepdims=True) + jnp.sum(en, axis=-1, keepdims=True) + jnp.exp(sink - m)
            og = _dot_nt((eb / den).astype(BF16), vt_ref[b, g].astype(BF16)) + _dot((en / den).astype(BF16), vn)
            for i, h in enumerate(heads):
                o_ref[rows, h * A_HEAD_DIM:(h + 1) * A_HEAD_DIM] = og[i * l_s:(i + 1) * l_s]
        return carry

    lax.fori_loop(0, SWA_S_NB, one, 0, unroll=True)


def _swa_sample(qkv, kt_buf, vt_buf, bias, sinks, sl, row0, nb_s, l_s):
    w_buf = kt_buf.shape[-1]
    r = A_GROUP * l_s
    tm = SWA_S_NB * l_s
    blk0 = row0 // tm
    bias_g = bias.reshape(A_KV_HEADS, r, w_buf + l_s)
    sink_g = _group_sinks(sinks, l_s)
    buf_spec = pl.BlockSpec((None, SWA_S_NB, A_KV_HEADS, A_HEAD_DIM, w_buf), lambda i: (sl, i, 0, 0, 0))
    const3 = lambda i: (0, 0, 0)
    return pl.pallas_call(
        functools.partial(_swa_sample_body, l_s=l_s, w_buf=w_buf),
        grid=(nb_s // SWA_S_NB,),
        in_specs=[
            pl.BlockSpec((tm, A_Q + 2 * A_KV), lambda i: (i + blk0, 0)),
            buf_spec,
            buf_spec,
            _resident((A_KV_HEADS, r, w_buf), const3),
            _resident((A_KV_HEADS, r, l_s), const3),
            _resident((A_KV_HEADS, r, LANES), const3),
        ],
        out_specs=pl.BlockSpec((tm, A_Q), lambda i: (i, 0)),
        out_shape=jax.ShapeDtypeStruct((nb_s * l_s, A_Q), F32),
        compiler_params=_cparams("parallel"),
        name="swa_sample",
    )(qkv, kt_buf, vt_buf, bias_g[:, :, :w_buf], bias_g[:, :, w_buf:], sink_g)


B_IN_COLS = Q_LORA + KV_LORA + 2 * LANES
B_KPE = Q_LORA + KV_LORA
B_KPE_ROT = B_KPE + LANES


def _mla_proj_body(x_ref, g_ref, win_ref, qn_ref, kvn_ref, wnope_ref, wpe_ref, wper_ref, wuk_ref,
                   cos_ref, sin_ref, kv_ref, q_ref, *maybe_kvb_ref, sample):
    tm = x_ref.shape[0]
    h = _rms(x_ref[...], g_ref[...]).astype(BF16)
    c = _dot(h, win_ref[...])
    cos = cos_ref[...]
    sin = sin_ref[...]
    kv_lat = _rms(c[:, Q_LORA:B_KPE], kvn_ref[...])
    kv_rope = (c[:, B_KPE:B_KPE + QK_ROPE] * cos[:, :QK_ROPE]
               + c[:, B_KPE_ROT:B_KPE_ROT + QK_ROPE] * sin[:, :QK_ROPE])
    kv_ref[:, :KV_LORA] = kv_lat
    kv_ref[:, KV_LORA:] = kv_rope
    for kvb_ref in maybe_kvb_ref:
        kvb_ref[:, :KV_LORA] = kv_lat.astype(BF16)
        kvb_ref[:, KV_LORA:] = kv_rope.astype(BF16)
    qn = _rms(c[:, :Q_LORA], qn_ref[...]).astype(BF16)
    qnope = _dot(qn, wnope_ref[...])
    n_rep = B_HEADS * QK_ROPE // LANES
    qpe = (_dot(qn, wpe_ref[...]) * jnp.concatenate([cos] * n_rep, axis=1)
           + _dot(qn, wper_ref[...]) * jnp.concatenate([sin] * n_rep, axis=1))
    for hd in range(B_HEADS):
        qlat = _dot(qnope[:, hd * QK_NOPE:(hd + 1) * QK_NOPE].astype(BF16), wuk_ref[hd])
        qrope = qpe[:, hd * QK_ROPE:(hd + 1) * QK_ROPE]
        if sample:
            l_s = q_ref.shape[1] // B_HEADS
            rs = slice(hd * l_s, (hd + 1) * l_s)
            q_ref[:, rs, :KV_LORA] = qlat.reshape(tm // l_s, l_s, KV_LORA)
            q_ref[:, rs, KV_LORA:] = qrope.reshape(tm // l_s, l_s, QK_ROPE)
        else:
            for s in range(tm // B_QBLOCK):
                ts = slice(s * B_QBLOCK, (s + 1) * B_QBLOCK)
                q_ref[s, hd, :, :KV_LORA] = qlat[ts].astype(BF16)
                q_ref[s, hd, :, KV_LORA:] = qrope[ts].astype(BF16)


def _mla_proj(x, norm, layer, w, cos, sin, row0, rows, l_s):
    sample = l_s is not None
    tm = ROW_TM
    blk0 = row0 // tm
    if sample:
        q_shape = jax.ShapeDtypeStruct((rows // l_s, B_HEADS * l_s, KV_ROW), F32)
        q_spec = pl.BlockSpec((tm // l_s, B_HEADS * l_s, KV_ROW), lambda i: (i, 0, 0))
    else:
        q_shape = jax.ShapeDtypeStruct((rows // B_QBLOCK, B_HEADS, B_QBLOCK, KV_ROW), BF16)
        q_spec = pl.BlockSpec((tm // B_QBLOCK, B_HEADS, B_QBLOCK, KV_ROW), lambda i: (i, 0, 0, 0))
    const2 = lambda i: (0, 0)
    kv_spec = pl.BlockSpec((tm, KV_ROW), lambda i: (i, 0))
    out_specs = [kv_spec, q_spec]
    out_shape = [jax.ShapeDtypeStruct((rows, KV_ROW), F32), q_shape]
    if not sample:
        out_specs.append(kv_spec)
        out_shape.append(jax.ShapeDtypeStruct((rows, KV_ROW), BF16))
    return pl.pallas_call(
        functools.partial(_mla_proj_body, sample=sample),
        grid=(rows // tm,),
        in_specs=[
            pl.BlockSpec((tm, D_MODEL), lambda i: (i + blk0, 0)),
            _resident((None, 1, D_MODEL), lambda i: (layer, 0, 0)),
            _resident((D_MODEL, B_IN_COLS), const2),
            _resident((1, Q_LORA), const2),
            _resident((1, KV_LORA), const2),
            _resident((Q_LORA, B_HEADS * QK_NOPE), const2),
            _resident((Q_LORA, B_HEADS * QK_ROPE), const2),
            _resident((Q_LORA, B_HEADS * QK_ROPE), const2),
            _resident((B_HEADS, QK_NOPE, KV_LORA), lambda i: (0, 0, 0)),
            pl.BlockSpec((tm, LANES), lambda i: (i + blk0, 0)),
            pl.BlockSpec((tm, LANES), lambda i: (i + blk0, 0)),
        ],
        out_specs=out_specs,
        out_shape=out_shape,
        compiler_params=_cparams("parallel"),
        name="mla_proj_sample" if sample else "mla_proj_prompt",
    )(x, norm, w["w_in"], w["q_norm"], w["kv_norm"], w["w_nope"], w["w_pe"], w["w_pe_rot"], w["w_uk"], cos, sin)


def _softmax_stats(s, m_scr, l_scr):
    c = MLA_SCALE * math.log2(math.e)
    m_prev = m_scr[...]
    m_new = jnp.maximum(m_prev, jnp.max(s, axis=-1, keepdims=True))
    alpha = jnp.exp2((m_prev - m_new) * c)
    p = jnp.exp2((s - _lanes(m_new, s.shape[1])) * c)
    l_scr[...] = alpha * l_scr[...] + jnp.sum(p, axis=-1, keepdims=True)
    m_scr[...] = m_new
    return alpha, p.astype(BF16)


def _softmax_accumulate(s, pv, m_scr, l_scr, acc_scr):
    alpha, p = _softmax_stats(s, m_scr, l_scr)
    acc_scr[...] = acc_scr[...] * _lanes(alpha, KV_LORA) + pv(p)


def _softmax_update(q, kvb, m_scr, l_scr, acc_scr, valid=None):
    s = _dot_nt(q, kvb)
    if valid is not None:
        s = jnp.where(valid, s, NEG_INF)
    _softmax_accumulate(s, lambda p: _dot(p, kvb[:, :KV_LORA]), m_scr, l_scr, acc_scr)


def _softmax_init(m_scr, l_scr, acc_scr):
    m_scr[...] = jnp.full(m_scr.shape, NEG_INF, F32)
    l_scr[...] = jnp.zeros(l_scr.shape, F32)
    acc_scr[...] = jnp.zeros(acc_scr.shape, F32)


def _mla_prompt_body(q_ref, kv_ref, x_ref, wuv_ref, wo_ref, o_ref, s0_scr, s1_scr, m_scr, l_scr, acc_scr, u_scr):
    qi = pl.program_id(1)
    r = B_HEADS * B_QBLOCK
    n_off = qi // (B_KVBLOCK // B_QBLOCK)
    q = q_ref[...].reshape(r, KV_ROW)

    def kv_block(k):
        return kv_ref[pl.ds(pl.multiple_of(k * B_KVBLOCK, B_KVBLOCK), B_KVBLOCK), :]

    def update(s, k):
        kvb = kv_block(k)
        _softmax_accumulate(s, lambda p: _dot(p, kvb[:, :KV_LORA]), m_scr, l_scr, acc_scr)

    def step(s_cur, s_next, k):
        s_next[...] = _dot_nt(q, kv_block(k + 1))
        update(s_cur[...], k)

    def last(s_cur):
        tok = lax.broadcasted_iota(jnp.int32, (r, B_KVBLOCK), 0) & (B_QBLOCK - 1)
        key = lax.broadcasted_iota(jnp.int32, (r, B_KVBLOCK), 1)
        valid = key - tok <= qi * B_QBLOCK - n_off * B_KVBLOCK
        update(jnp.where(valid, s_cur[...], NEG_INF), n_off)
        o = acc_scr[...] * _lanes(1.0 / l_scr[...], KV_LORA)
        for hd in range(B_HEADS):
            oh = o[hd * B_QBLOCK:(hd + 1) * B_QBLOCK].astype(BF16)
            u_scr[:, hd * V_HEAD:(hd + 1) * V_HEAD] = _dot(oh, wuv_ref[hd])
        o_ref[...] = x_ref[...] + _dot(u_scr[...].astype(BF16), wo_ref[...])

    _softmax_init(m_scr, l_scr, acc_scr)
    s0_scr[...] = _dot_nt(q, kv_block(0))

    def two_steps(j, carry):
        step(s0_scr, s1_scr, 2 * j)
        step(s1_scr, s0_scr, 2 * j + 1)
        return carry

    lax.fori_loop(0, n_off // 2, two_steps, 0)
    odd = n_off % 2 == 1

    @pl.when(odd)
    def _():
        step(s0_scr, s1_scr, n_off - 1)
        last(s1_scr)

    @pl.when(jnp.logical_not(odd))
    def _():
        last(s0_scr)


def _mla_prompt(x, q_cat, kv_bf16, w, nb_p, t_p):
    nq = t_p // B_QBLOCK
    r = B_HEADS * B_QBLOCK
    return pl.pallas_call(
        _mla_prompt_body,
        grid=(nb_p, nq),
        in_specs=[
            pl.BlockSpec((None, B_HEADS, B_QBLOCK, KV_ROW), lambda b, qi: (b * nq + qi, 0, 0, 0)),
            pl.BlockSpec((t_p, KV_ROW), lambda b, qi: (b, 0)),
            pl.BlockSpec((B_QBLOCK, D_MODEL), lambda b, qi: (b * nq + qi, 0)),
            _resident((B_HEADS, KV_LORA, V_HEAD), lambda b, qi: (0, 0, 0)),
            _resident((B_HEADS * V_HEAD, D_MODEL), lambda b, qi: (0, 0)),
        ],
        out_specs=pl.BlockSpec((B_QBLOCK, D_MODEL), lambda b, qi: (b * nq + qi, 0)),
        out_shape=jax.ShapeDtypeStruct(x.shape, F32),
        scratch_shapes=[
            pltpu.VMEM((r, B_KVBLOCK), F32),
            pltpu.VMEM((r, B_KVBLOCK), F32),
            pltpu.VMEM((r, LANES), F32),
            pltpu.VMEM((r, LANES), F32),
            pltpu.VMEM((r, KV_LORA), F32),
            pltpu.VMEM((B_QBLOCK, B_HEADS * V_HEAD), F32),
        ],
        input_output_aliases={2: 0},
        compiler_params=_cparams("parallel", "arbitrary"),
        name="mla_prompt",
    )(q_cat, kv_bf16, x, w["w_uv"], w["w_o"])


def _mla_sample_body(pt_ref, q_ref, *refs, l_s):
    page_refs = refs[:B_PAGES_PER_STEP]
    kvn_ref, o_ref, m_scr, l_scr, acc_scr, acct_scr = refs[B_PAGES_PER_STEP:]
    pc = pl.program_id(1)
    r = q_ref.shape[0]
    assert r == LANES

    @pl.when(pc == 0)
    def _():
        _softmax_init(m_scr, l_scr, acct_scr)

    q = q_ref[...].astype(BF16)
    kt = jnp.concatenate([pr[...].astype(BF16) for pr in page_refs], axis=1)
    alpha, p = _softmax_stats(_dot(q, kt), m_scr, l_scr)
    alpha_t = jnp.concatenate([alpha.T] * (KV_LORA // LANES), axis=0)
    acct_scr[...] = acct_scr[...] * alpha_t + _dot_nt(kt[:KV_LORA, :], p)

    @pl.when(pc == pl.num_programs(1) - 1)
    def _():
        acc_scr[...] = acct_scr[...].T
        tok = lax.broadcasted_iota(jnp.int32, (r, l_s), 0) % l_s
        key = lax.broadcasted_iota(jnp.int32, (r, l_s), 1)
        _softmax_update(q, kvn_ref[...].astype(BF16), m_scr, l_scr, acc_scr, valid=key <= tok)
        o_ref[...] = acc_scr[...] * _lanes(1.0 / l_scr[...], KV_LORA)


def _mla_sample(q_cat, kv_rows, cache_t, page_table, cl, row0, nb_s, l_s):
    n_pages = page_table.shape[1]
    steps = n_pages // B_PAGES_PER_STEP
    r = B_HEADS * l_s
    blk0 = row0 // l_s

    def page_spec(k):
        return pl.BlockSpec((None, None, KV_ROW, PAGE_SIZE),
                            lambda b, pc, pt: (cl, pt[b * n_pages + pc * B_PAGES_PER_STEP + k], 0, 0))

    return pl.pallas_call(
        functools.partial(_mla_sample_body, l_s=l_s),
        grid_spec=pltpu.PrefetchScalarGridSpec(
            num_scalar_prefetch=1,
            grid=(nb_s, steps),
            in_specs=[pl.BlockSpec((None, r, KV_ROW), lambda b, pc, pt: (b, 0, 0))]
            + [page_spec(k) for k in range(B_PAGES_PER_STEP)]
            + [pl.BlockSpec((l_s, KV_ROW), lambda b, pc, pt: (b + blk0, 0))],
            out_specs=pl.BlockSpec((None, r, KV_LORA), lambda b, pc, pt: (b, 0, 0)),
            scratch_shapes=[
                pltpu.VMEM((r, LANES), F32),
                pltpu.VMEM((r, LANES), F32),
                pltpu.VMEM((r, KV_LORA), F32),
                pltpu.VMEM((KV_LORA, r), F32),
            ],
        ),
        out_shape=jax.ShapeDtypeStruct((nb_s, r, KV_LORA), F32),
        compiler_params=_cparams("parallel", "arbitrary"),
        name="mla_sample",
    )(page_table.reshape(-1), q_cat, *([cache_t] * B_PAGES_PER_STEP), kv_rows)


def _mla_out_sample_body(o_ref, x_ref, wuv_ref, wo_ref, out_ref, u_scr, *, l_s):
    nb = o_ref.shape[0]
    for hd in range(B_HEADS):
        oh = o_ref[:, hd * l_s:(hd + 1) * l_s, :].reshape(nb * l_s, KV_LORA).astype(BF16)
        u_scr[:, hd * V_HEAD:(hd + 1) * V_HEAD] = _dot(oh, wuv_ref[hd])
    out_ref[...] = x_ref[...] + _dot(u_scr[...].astype(BF16), wo_ref[...])


def _mla_out_sample(x, o_lat, w, row0, l_s):
    nb_s = o_lat.shape[0]
    nb = ROW_TM // l_s
    blk0 = row0 // ROW_TM
    return pl.pallas_call(
        functools.partial(_mla_out_sample_body, l_s=l_s),
        grid=(nb_s // nb,),
        in_specs=[
            pl.BlockSpec((nb, B_HEADS * l_s, KV_LORA), lambda i: (i, 0, 0)),
            pl.BlockSpec((ROW_TM, D_MODEL), lambda i: (i + blk0, 0)),
            _resident((B_HEADS, KV_LORA, V_HEAD), lambda i: (0, 0, 0)),
            _resident((B_HEADS * V_HEAD, D_MODEL), lambda i: (0, 0)),
        ],
        out_specs=pl.BlockSpec((ROW_TM, D_MODEL), lambda i: (i + blk0, 0)),
        out_shape=jax.ShapeDtypeStruct(x.shape, F32),
        scratch_shapes=[pltpu.VMEM((ROW_TM, B_HEADS * V_HEAD), F32)],
        input_output_aliases={1: 0},
        compiler_params=_cparams("parallel"),
        name="mla_out_sample",
    )(o_lat, x, w["w_uv"], w["w_o"])


C_IN_COLS = D_INNER + CONV_DIM + LANES


def _mamba_in_body(x_ref, g_ref, w_ref, z_ref, xbc_ref, dt_ref):
    h = _rms(x_ref[...], g_ref[...]).astype(BF16)
    z_ref[...] = _dot(h, w_ref[:, :D_INNER])
    xbc_ref[...] = _dot(h, w_ref[:, D_INNER:D_INNER + CONV_DIM])
    dt_ref[...] = _dot(h, w_ref[:, D_INNER + CONV_DIM:])


def _mamba_in(x, norm, layer, w_in):
    rows = x.shape[0]
    return pl.pallas_call(
        _mamba_in_body,
        grid=(rows // ROW_TM,),
        in_specs=[
            pl.BlockSpec((ROW_TM, D_MODEL), lambda i: (i, 0)),
            _resident((None, 1, D_MODEL), lambda i: (layer, 0, 0)),
            _resident((D_MODEL, C_IN_COLS), lambda i: (0, 0)),
        ],
        out_specs=[
            pl.BlockSpec((ROW_TM, D_INNER), lambda i: (i, 0)),
            pl.BlockSpec((ROW_TM, CONV_DIM), lambda i: (i, 0)),
            pl.BlockSpec((ROW_TM, LANES), lambda i: (i, 0)),
        ],
        out_shape=[
            jax.ShapeDtypeStruct((rows, D_INNER), F32),
            jax.ShapeDtypeStruct((rows, CONV_DIM), F32),
            jax.ShapeDtypeStruct((rows, LANES), F32),
        ],
        compiler_params=_cparams("parallel"),
        name="mamba_in",
    )(x, norm, w_in)


def _causal_conv(ext_scr, u, cw_ref, cb_ref):
    t = u.shape[0]
    ext_scr[SUBLANES:SUBLANES + t, :] = u
    acc = cb_ref[...] + cw_ref[D_CONV - 1:D_CONV, :] * u
    for k in range(D_CONV - 1):
        lo = SUBLANES - (D_CONV - 1) + k
        acc = acc + cw_ref[k:k + 1, :] * ext_scr[lo:lo + t, :]
    return acc


def _gated_norm(y, z, norm_g):
    yz = y * _silu(z)
    gw = D_INNER // C_GROUPS
    parts = []
    for g in range(C_GROUPS):
        part = yz[:, g * gw:(g + 1) * gw]
        parts.append(part * lax.rsqrt(jnp.mean(part * part, axis=-1, keepdims=True) + EPS))
    return jnp.concatenate(parts, axis=1) * norm_g


def _ssd_prompt_body(xbc_ref, dt_ref, z_ref, cw_ref, cb_ref, dtb_ref, alog_ref, dskip_ref, ng_ref, yn_ref, st_ref,
                     ext_scr, ht_scr, y_ref):
    c = pl.program_id(1)
    t = SSD_CHUNK

    @pl.when(c == 0)
    def _():
        ext_scr[0:SUBLANES, :] = jnp.zeros((SUBLANES, CONV_DIM), F32)
        ht_scr[...] = jnp.zeros(ht_scr.shape, F32)

    u = xbc_ref[...]
    act = _silu(_causal_conv(ext_scr, u, cw_ref, cb_ref))
    ext_scr[0:SUBLANES, :] = u[t - SUBLANES:, :]

    dt = _softplus(dt_ref[...] + dtb_ref[...])
    a = -jnp.exp(alog_ref[...])
    row = lax.broadcasted_iota(jnp.int32, (t, t), 0)
    col = lax.broadcasted_iota(jnp.int32, (t, t), 1)
    causal = col <= row
    acs = _dot_f32(causal.astype(F32), dt * a)
    acs_t = acs.T
    dt_t = dt.T
    to_end = jnp.exp(acs[t - 1:t, :] - acs) * dt
    gw = C_HPG * C_HEAD_DIM
    heads_per_vreg = LANES // C_HEAD_DIM
    assert heads_per_vreg == 2 and t == LANES
    first = lax.broadcasted_iota(jnp.int32, (t, LANES), 1) < C_HEAD_DIM
    for g in range(C_GROUPS):
        bg = act[:, D_INNER + g * D_STATE:D_INNER + (g + 1) * D_STATE]
        cg = act[:, D_INNER + C_BC + g * D_STATE:D_INNER + C_BC + (g + 1) * D_STATE].astype(BF16)
        cb = _dot_nt(cg, bg.astype(BF16))
        bg_t = bg.T.astype(BF16)
        y_off = _dot(cg, ht_scr[:, g * gw:(g + 1) * gw].astype(BF16))
        for jp in range(C_HPG // heads_per_vreg):
            h0 = g * C_HPG + heads_per_vreg * jp
            ps = slice(h0 * C_HEAD_DIM, h0 * C_HEAD_DIM + LANES)
            ms, a_cols, te_cols = [], [], []
            for h in (h0, h0 + 1):
                a_col = jnp.broadcast_to(acs[:, h:h + 1], (t, t))
                a_row = jnp.broadcast_to(acs_t[h:h + 1, :], (t, t))
                dt_row = jnp.broadcast_to(dt_t[h:h + 1, :], (t, t))
                ms.append(cb * jnp.exp(jnp.where(causal, a_col - a_row, -jnp.inf)) * dt_row)
                a_cols.append(a_col)
                te_cols.append(jnp.broadcast_to(to_end[:, h:h + 1], (t, LANES)))
            x2 = act[:, ps]
            lhs = jnp.concatenate(ms, axis=1).astype(BF16)
            rhs = jnp.concatenate([jnp.where(first, x2, 0.0), jnp.where(first, 0.0, x2)], axis=0).astype(BF16)
            a_col2 = jnp.where(first, a_cols[0], a_cols[1])
            y_ref[:, ps] = (_dot(lhs, rhs)
                            + y_off[:, jp * LANES:(jp + 1) * LANES] * jnp.exp(a_col2)
                            + dskip_ref[:, ps] * x2)
            xw = (x2 * jnp.where(first, te_cols[0], te_cols[1])).astype(BF16)
            ht_scr[:, ps] = jnp.exp(a_col2[t - 1:t, :]) * ht_scr[:, ps] + _dot(bg_t, xw)

    yn_ref[...] = _gated_norm(y_ref[...], z_ref[...], ng_ref[...]).astype(BF16)

    @pl.when(c == pl.num_programs(1) - 1)
    def _():
        st_ref[...] = ht_scr[...].T


def _ssd_prompt(xbc, dt, z, w, nb_p, t_p):
    nc = t_p // SSD_CHUNK
    const2 = lambda b, c: (0, 0)
    rows = lambda b, c: (b * nc + c, 0)
    return pl.pallas_call(
        _ssd_prompt_body,
        grid=(nb_p, nc),
        in_specs=[
            pl.BlockSpec((SSD_CHUNK, CONV_DIM), rows),
            pl.BlockSpec((SSD_CHUNK, LANES), rows),
            pl.BlockSpec((SSD_CHUNK, D_INNER), rows),
            _resident((D_CONV, CONV_DIM), const2),
            _resident((1, CONV_DIM), const2),
            _resident((1, LANES), const2),
            _resident((1, LANES), const2),
            _resident((1, D_INNER), const2),
            _resident((1, D_INNER), const2),
        ],
        out_specs=[
            pl.BlockSpec((SSD_CHUNK, D_INNER), rows),
            pl.BlockSpec((None, D_INNER, D_STATE), lambda b, c: (b, 0, 0)),
        ],
        out_shape=[
            jax.ShapeDtypeStruct((nb_p * t_p, D_INNER), BF16),
            jax.ShapeDtypeStruct((nb_p, D_INNER, D_STATE), F32),
        ],
        scratch_shapes=[
            pltpu.VMEM((SUBLANES + SSD_CHUNK, CONV_DIM), F32),
            pltpu.VMEM((D_STATE, D_INNER), F32),
            pltpu.VMEM((SSD_CHUNK, D_INNER), F32),
        ],
        compiler_params=_cparams("parallel", "arbitrary"),
        name="ssd_prompt",
    )(xbc, dt, z, w["conv_w"], w["conv_b"], w["dt_bias"], w["a_log"], w["d_skip"], w["norm_g"])


def _ssd_sample_body(xbc_ref, dtx_ref, z_ref, cprev_ref, h0_ref, cw_ref, cb_ref, dtb_ref, alog_ref, dskip_ref, ng_ref,
                     y_ref, st_ref, ext_scr, *, l_s):
    u = xbc_ref[...]
    ext_scr[SUBLANES - (D_CONV - 1):SUBLANES, :] = cprev_ref[...]
    act = _silu(_causal_conv(ext_scr, u, cw_ref, cb_ref))
    xs = act[:, :D_INNER]

    dt = _softplus(dtx_ref[...] + dtb_ref[...])
    a = -jnp.exp(alog_ref[...])
    row = lax.broadcasted_iota(jnp.int32, (l_s, l_s), 0)
    col = lax.broadcasted_iota(jnp.int32, (l_s, l_s), 1)
    acs = _dot_f32((col <= row).astype(F32), dt * a)
    last = acs[l_s - 1:l_s, :]
    xdt = xs * dt
    xw = xdt * jnp.exp(last - acs)
    decay_col = jnp.broadcast_to(jnp.exp(last), (LANES, D_INNER)).T
    gw = C_HPG * C_HEAD_DIM
    lrow = lax.broadcasted_iota(jnp.int32, (l_s, gw), 0)
    for g in range(C_GROUPS):
        gs = slice(g * gw, (g + 1) * gw)
        bg = act[:, D_INNER + g * D_STATE:D_INNER + (g + 1) * D_STATE]
        cg = act[:, D_INNER + C_BC + g * D_STATE:D_INNER + C_BC + (g + 1) * D_STATE]
        cb = _dot_nt(cg.astype(BF16), bg.astype(BF16))
        h0 = h0_ref[gs, :]
        y = _dot_nt(cg.astype(BF16), h0.astype(BF16)) * jnp.exp(acs[:, gs])
        for s in range(l_s):
            seg = acs[:, gs] - acs[s:s + 1, gs]
            w_s = jnp.exp(jnp.where(lrow >= s, seg, -jnp.inf)) * jnp.broadcast_to(cb[:, s:s + 1], (l_s, gw))
            y = y + w_s * xdt[s:s + 1, gs]
        y_ref[:, gs] = y + dskip_ref[:, gs] * xs[:, gs]
        st_ref[gs, :] = decay_col[gs, :] * h0 + _dot_tn(xw[:, gs].astype(BF16), bg.astype(BF16))
    y_ref[...] = _gated_norm(y_ref[...], z_ref[...], ng_ref[...])


def _ssd_sample(xbc, dtx, z, conv_prev, h0, sl, w, row0, nb_s, l_s):
    blk0 = row0 // l_s
    const2 = lambda b: (0, 0)
    return pl.pallas_call(
        functools.partial(_ssd_sample_body, l_s=l_s),
        grid=(nb_s,),
        in_specs=[
            pl.BlockSpec((l_s, CONV_DIM), lambda b: (b + blk0, 0)),
            pl.BlockSpec((l_s, D_INNER), lambda b: (b, 0)),
            pl.BlockSpec((l_s, D_INNER), lambda b: (b + blk0, 0)),
            pl.BlockSpec((None, None, D_CONV - 1, CONV_DIM), lambda b: (sl, b, 0, 0)),
            pl.BlockSpec((None, None, D_INNER, D_STATE), lambda b: (sl, b, 0, 0)),
            _resident((D_CONV, CONV_DIM), const2),
            _resident((1, CONV_DIM), const2),
            _resident((1, D_INNER), const2),
            _resident((1, D_INNER), const2),
            _resident((1, D_INNER), const2),
            _resident((1, D_INNER), const2),
        ],
        out_specs=[
            pl.BlockSpec((l_s, D_INNER), lambda b: (b, 0)),
            pl.BlockSpec((None, D_INNER, D_STATE), lambda b: (b, 0, 0)),
        ],
        out_shape=[
            jax.ShapeDtypeStruct((nb_s * l_s, D_INNER), F32),
            jax.ShapeDtypeStruct((nb_s, D_INNER, D_STATE), F32),
        ],
        scratch_shapes=[pltpu.VMEM((SUBLANES + l_s, CONV_DIM), F32)],
        compiler_params=_cparams("parallel"),
        name="ssd_sample",
    )(xbc, dtx, z, conv_prev, h0, w["conv_w"], w["conv_b"], w["dt_bias_x"], w["a_log_x"], w["d_skip"], w["norm_g"])


def _rope_tables(pos):
    half = QK_ROPE // 2
    inv = jnp.power(ROPE_THETA, -jnp.arange(half, dtype=F32) / half)
    ang = pos.astype(F32)[:, None] * inv
    reps = LANES // half
    return jnp.tile(jnp.cos(ang), (1, reps)), jnp.tile(jnp.sin(ang), (1, reps))


def _rot_cols(w):
    half = w.shape[-1] // 2
    return jnp.concatenate([-w[..., half:], w[..., :half]], axis=-1)


def _mla_weights(w_in, q_norm, w_uq, kv_norm, w_ukv, w_o):
    kpe = w_in[:, B_KPE:]
    pad = jnp.zeros((D_MODEL, LANES - QK_ROPE), w_in.dtype)
    w_in_aug = jnp.concatenate([w_in[:, :B_KPE], kpe, pad, _rot_cols(kpe), pad], axis=1)
    uq = w_uq.reshape(Q_LORA, B_HEADS, QK_NOPE + QK_ROPE)
    pe = uq[:, :, QK_NOPE:]
    ukv = w_ukv.reshape(KV_LORA, B_HEADS, QK_NOPE + V_HEAD)
    return {
        "w_in": w_in_aug.astype(BF16),
        "q_norm": q_norm.reshape(1, Q_LORA),
        "kv_norm": kv_norm.reshape(1, KV_LORA),
        "w_nope": uq[:, :, :QK_NOPE].reshape(Q_LORA, B_HEADS * QK_NOPE).astype(BF16),
        "w_pe": pe.reshape(Q_LORA, B_HEADS * QK_ROPE).astype(BF16),
        "w_pe_rot": _rot_cols(pe).reshape(Q_LORA, B_HEADS * QK_ROPE).astype(BF16),
        "w_uk": jnp.transpose(ukv[:, :, :QK_NOPE], (1, 2, 0)).astype(BF16),
        "w_uv": jnp.transpose(ukv[:, :, QK_NOPE:], (1, 0, 2)).astype(BF16),
        "w_o": w_o.astype(BF16),
    }


def _pad_lanes(v):
    return jnp.pad(v, (0, LANES - v.shape[0])).reshape(1, LANES)


def _expand_heads(v):
    return jnp.repeat(v, C_HEAD_DIM).reshape(1, D_INNER)


def _mamba_weights(w_in, conv_w, conv_b, dt_bias, a_log, d_skip, norm_g, w_o):
    w_dt = w_in[:, D_INNER + CONV_DIM:]
    pad = jnp.zeros((D_MODEL, LANES - C_HEADS), w_in.dtype)
    return {
        "w_in": jnp.concatenate([w_in, pad], axis=1).astype(BF16),
        "w_dt_x": jnp.repeat(w_dt, C_HEAD_DIM, axis=1).astype(BF16)[None],
        "conv_w": conv_w,
        "conv_b": conv_b.reshape(1, CONV_DIM),
        "dt_bias": _pad_lanes(dt_bias.astype(F32)),
        "a_log": _pad_lanes(a_log.astype(F32)),
        "dt_bias_x": _expand_heads(dt_bias.astype(F32)),
        "a_log_x": _expand_heads(a_log.astype(F32)),
        "d_skip": _expand_heads(d_skip),
        "norm_g": norm_g.reshape(1, D_INNER),
        "w_o": w_o.astype(BF16),
    }


def kernel(x_prompt, x_sample, state_swa_k, state_swa_v, cache_mla, state_ssm, state_conv, page_table, p_prompt, p_sample, norm_ffn1, ffn1_w_gu, ffn1_w_down, norm_mix, norm_ffn2, ffn2_w_gu, ffn2_w_down, norm_ple, ple_w_proj, ple_w_gate, rel_bias_table, a_w_qkv, a_w_o, a_sinks, b_w_in, b_q_norm, b_w_uq, b_kv_norm, b_w_ukv, b_w_o, c_w_in, c_conv_w, c_conv_b, c_dt_bias, c_a_log, c_d, c_norm, c_w_o, final_norm):
    nb_p, t_p, _ = x_prompt.shape
    nb_s, l_s, _ = x_sample.shape
    depth = norm_ffn1.shape[0]
    rows_p = nb_p * t_p
    rows_s = nb_s * l_s
    w_buf = state_swa_k.shape[2]
    assert rows_p % FFN_TM == 0 and (rows_p + rows_s) % FFN_TM == 0 and rows_s % ROW_TM == 0
    assert t_p % B_KVBLOCK == 0 and t_p % SSD_CHUNK == 0 and t_p % A_BLOCK == 0
    assert l_s == SUBLANES and w_buf == WINDOW and page_table.shape[1] % B_PAGES_PER_STEP == 0

    x = jnp.concatenate([x_prompt.reshape(rows_p, D_MODEL), x_sample.reshape(rows_s, D_MODEL)], axis=0)
    p_all = jnp.concatenate([p_prompt.reshape(depth, rows_p, D_PLE), p_sample.reshape(depth, rows_s, D_PLE)], axis=1)

    def norm3(g):
        return g.reshape(g.shape[0], 1, g.shape[1])

    norm_ffn1, norm_mix, norm_ffn2, norm_ple = map(norm3, (norm_ffn1, norm_mix, norm_ffn2, norm_ple))
    ffn1_w_gu, ffn1_w_down, ffn2_w_gu, ffn2_w_down, ple_w_proj, ple_w_gate, a_w_qkv, a_w_o = (
        w.astype(BF16) for w in (ffn1_w_gu, ffn1_w_down, ffn2_w_gu, ffn2_w_down, ple_w_proj, ple_w_gate, a_w_qkv, a_w_o))
    final_norm = final_norm.reshape(1, D_MODEL)

    i = jnp.arange(A_BLOCK)[:, None]
    j = jnp.arange(2 * A_BLOCK)[None, :]
    dist = A_BLOCK + i - j
    in_window = (dist >= 0) & (dist < WINDOW)
    bias_p = jnp.stack([_rel_bias(rel_bias_table, _t5_bucket(dist), in_window & (j >= A_BLOCK)),
                        _rel_bias(rel_bias_table, _t5_bucket(dist), in_window)])
    q_pos = PAST_LEN + jnp.arange(l_s)
    k_pos = jnp.concatenate([PAST_LEN - w_buf + jnp.arange(w_buf), q_pos])
    dist = q_pos[:, None] - k_pos[None, :]
    bias_s = _rel_bias(rel_bias_table, _t5_bucket(dist), (dist >= 0) & (dist < WINDOW))
    swa_k_buf = jnp.transpose(state_swa_k, (0, 1, 3, 4, 2))
    swa_v_buf = jnp.transpose(state_swa_v, (0, 1, 3, 4, 2))

    pos = jnp.concatenate([jnp.tile(jnp.arange(t_p), nb_p), jnp.tile(PAST_LEN + jnp.arange(l_s), nb_s)])
    cos, sin = _rope_tables(pos)

    ssm_flat = state_ssm.reshape(state_ssm.shape[0], nb_s, D_INNER, D_STATE)
    cache_t = jnp.swapaxes(cache_mla, 2, 3)

    def last_rows(a, n):
        return jnp.stack([a[(b + 1) * t_p - n:(b + 1) * t_p] for b in range(nb_p)])

    swa_k_p, swa_v_p, mla_p, ssm_p, conv_p = [], [], [], [], []
    swa_k_s, swa_v_s, mla_s, ssm_s, conv_s = [], [], [], [], []
    for layer in range(depth):
        kind, sl = layer % N_MIXERS, layer // N_MIXERS
        prev_ple = None if layer == 0 else (p_all, norm_ple, ple_w_proj, ple_w_gate, layer - 1)
        if kind == 0:
            x, qkv = _ffn(x, norm_ffn1, ffn1_w_gu, ffn1_w_down, layer, ple=prev_ple,
                          proj=(norm_mix, layer, a_w_qkv, sl))
        else:
            x = _ffn(x, norm_ffn1, ffn1_w_gu, ffn1_w_down, layer, ple=prev_ple)
        if kind == 0:
            x = _swa_prompt(x, qkv, bias_p, a_sinks[sl], a_w_o, sl, nb_p, t_p)
            o_s = _swa_sample(qkv, swa_k_buf, swa_v_buf, bias_s, a_sinks[sl], sl, rows_p, nb_s, l_s)
            x = _res_proj(x, o_s, a_w_o, sl, rows_p, "swa_out_sample")
            kv_p = last_rows(qkv, w_buf)[:, :, A_Q:].reshape(nb_p, w_buf, 2, A_KV_HEADS, A_HEAD_DIM)
            kv_s = qkv[rows_p:, A_Q:].reshape(nb_s, l_s, 2, A_KV_HEADS, A_HEAD_DIM)
            swa_k_p.append(kv_p[:, :, 0])
            swa_v_p.append(kv_p[:, :, 1])
            swa_k_s.append(jnp.concatenate([state_swa_k[sl], kv_s[:, :, 0]], axis=1)[:, l_s:])
            swa_v_s.append(jnp.concatenate([state_swa_v[sl], kv_s[:, :, 1]], axis=1)[:, l_s:])
        elif kind == 1:
            w = _mla_weights(b_w_in[sl], b_q_norm[sl], b_w_uq[sl], b_kv_norm[sl], b_w_ukv[sl], b_w_o[sl])
            kv_rows_p, q_p, kv_bf16 = _mla_proj(x, norm_mix, layer, w, cos, sin, 0, rows_p, None)
            kv_rows_s, q_s = _mla_proj(x, norm_mix, layer, w, cos, sin, rows_p, rows_s, l_s)
            x = _mla_prompt(x, q_p, kv_bf16, w, nb_p, t_p)
            o_lat = _mla_sample(q_s, kv_rows_s, cache_t, page_table, sl, 0, nb_s, l_s)
            x = _mla_out_sample(x, o_lat, w, rows_p, l_s)
            mla_p.append(kv_rows_p.reshape(nb_p, t_p, KV_ROW))
            mla_s.append(kv_rows_s.reshape(nb_s, l_s, KV_ROW))
        else:
            w = _mamba_weights(c_w_in[sl], c_conv_w[sl], c_conv_b[sl], c_dt_bias[sl], c_a_log[sl], c_d[sl],
                               c_norm[sl], c_w_o[sl])
            z, xbc, dt = _mamba_in(x, norm_mix, layer, w["w_in"])
            dtx = _norm_proj(x[rows_p:], norm_mix, w["w_dt_x"], layer, 0, "mamba_dt_sample")
            y_p, st_p = _ssd_prompt(xbc, dt, z, w, nb_p, t_p)
            y_s, st_s = _ssd_sample(xbc, dtx, z, state_conv, ssm_flat, sl, w, rows_p, nb_s, l_s)
            x = _res_proj(x, y_p, w["w_o"][None], 0, 0, "mamba_out_prompt")
            x = _res_proj(x, y_s, w["w_o"][None], 0, rows_p, "mamba_out_sample")
            ssm_p.append(st_p.reshape(nb_p, C_HEADS, C_HEAD_DIM, D_STATE))
            ssm_s.append(st_s.reshape(nb_s, C_HEADS, C_HEAD_DIM, D_STATE))
            conv_p.append(last_rows(xbc, D_CONV - 1))
            conv_s.append(jnp.concatenate([state_conv[sl], xbc[rows_p:].reshape(nb_s, l_s, CONV_DIM)],
                                          axis=1)[:, -(D_CONV - 1):])
        x = _ffn(x, norm_ffn2, ffn2_w_gu, ffn2_w_down, layer)
    ple_last = functools.partial(_ple, x, p_all, norm_ple, ple_w_proj, ple_w_gate, final_norm, depth - 1, True)
    y_p = ple_last(0, rows_p)
    y_s = ple_last(rows_p, rows_s)

    return (y_p.reshape(nb_p, t_p, D_MODEL), y_s.reshape(nb_s, l_s, D_MODEL),
            jnp.stack(swa_k_p), jnp.stack(swa_v_p), jnp.stack(mla_p), jnp.stack(ssm_p), jnp.stack(conv_p),
            jnp.stack(swa_k_s), jnp.stack(swa_v_s), jnp.stack(mla_s), jnp.stack(ssm_s), jnp.stack(conv_s))
```

```python
import functools
import math

import jax
import jax.numpy as jnp
from jax import lax
from jax.experimental import pallas as pl
from jax.experimental.pallas import tpu as pltpu

F32 = jnp.float32
BF16 = jnp.bfloat16

D_MODEL = 1024
D_FF = 2816
D_PLE = 256
EPS = 1e-6
NEG_INF = -1e30
PAST_LEN = 8192
PAGE_SIZE = 128
N_MIXERS = 3

A_HEADS = 16
A_KV_HEADS = 4
A_HEAD_DIM = 64
A_GROUP = A_HEADS // A_KV_HEADS
A_Q = A_HEADS * A_HEAD_DIM
A_KV = A_KV_HEADS * A_HEAD_DIM
WINDOW = 128
A_BLOCK = 128
N_BUCKETS = 32
BUCKET_MAX_DIST = 128

B_HEADS = 16
Q_LORA = 768
KV_LORA = 256
QK_NOPE = 64
QK_ROPE = 32
V_HEAD = 64
KV_ROW = KV_LORA + QK_ROPE
ROPE_THETA = 10000.0
MLA_SCALE = (QK_NOPE + QK_ROPE) ** -0.5
B_QBLOCK = 128
B_KVBLOCK = 512
B_PAGES_PER_STEP = 32

D_INNER = 2 * D_MODEL
C_HEAD_DIM = 64
C_HEADS = D_INNER // C_HEAD_DIM
C_GROUPS = 4
C_HPG = C_HEADS // C_GROUPS
D_STATE = 128
D_CONV = 4
C_BC = C_GROUPS * D_STATE
CONV_DIM = D_INNER + 2 * C_BC
SSD_CHUNK = 128

LANES = 128
SUBLANES = 8
VMEM_LIMIT = 56 * 1024 * 1024


def _cparams(*sem):
    return pltpu.CompilerParams(dimension_semantics=sem, vmem_limit_bytes=VMEM_LIMIT)


def _resident(shape, index_map):
    return pl.BlockSpec(shape, index_map, pipeline_mode=pl.Buffered(1))


def _dot(a, b):
    return jnp.dot(a, b, preferred_element_type=F32)


def _dot_nt(a, b):
    return lax.dot_general(a, b, (((1,), (1,)), ((), ())), preferred_element_type=F32)


def _dot_tn(a, b):
    return lax.dot_general(a, b, (((0,), (0,)), ((), ())), preferred_element_type=F32)


def _dot_f32(a, b):
    return jnp.dot(a, b, preferred_element_type=F32, precision=lax.Precision.HIGHEST)


def _rms(x, g):
    r = lax.rsqrt(jnp.mean(x * x, axis=-1, keepdims=True) + EPS)
    return (x * r) * g


def _silu(x):
    return x * jax.nn.sigmoid(x)


def _softplus(x):
    return jnp.maximum(x, 0.0) + jnp.log1p(jnp.exp(-jnp.abs(x)))


def _lanes(a, n):
    if n <= LANES:
        return a[:, :n]
    return jnp.concatenate([a] * (n // LANES), axis=1)


FFN_TM = 512
FFN_TF = 256


def _ple_update(x, p_ref, g_ref, wp_ref, wg_ref):
    gate = _dot(_rms(x, g_ref[...]).astype(BF16), wg_ref[...])
    proj = _dot(p_ref[...].astype(BF16), wp_ref[...])
    return x + proj * jax.nn.sigmoid(gate)


def _ffn_body(x_ref, g_ref, wgu_ref, wd_ref, *refs, ple, proj):
    x = x_ref[...]
    if ple:
        x = _ple_update(x, *refs[:4])
        refs = refs[4:]
    o_ref = refs[-2] if proj else refs[-1]
    h = _rms(x, g_ref[...]).astype(BF16)
    acc = jnp.zeros(x.shape, F32)
    for j in range(D_FF // FFN_TF):
        lo = j * FFN_TF
        gate = _dot(h, wgu_ref[:, lo:lo + FFN_TF])
        up = _dot(h, wgu_ref[:, D_FF + lo:D_FF + lo + FFN_TF])
        act = (_silu(gate) * up).astype(BF16)
        acc = acc + _dot(act, wd_ref[lo:lo + FFN_TF, :])
    y = x + 0.5 * acc
    o_ref[...] = y
    if proj:
        gm_ref, wm_ref, m_ref = refs[0], refs[1], refs[-1]
        m_ref[...] = _dot(_rms(y, gm_ref[...]).astype(BF16), wm_ref[...])


def _ffn(x, norm, w_gu, w_down, layer, ple=None, proj=None):
    rows = x.shape[0]
    in_specs = [
        pl.BlockSpec((FFN_TM, D_MODEL), lambda i: (i, 0)),
        _resident((None, 1, D_MODEL), lambda i: (layer, 0, 0)),
        _resident((None, D_MODEL, 2 * D_FF), lambda i: (layer, 0, 0)),
        _resident((None, D_FF, D_MODEL), lambda i: (layer, 0, 0)),
    ]
    args = [x, norm, w_gu, w_down]
    if ple is not None:
        pl_layer = ple[4]
        in_specs += [
            pl.BlockSpec((None, FFN_TM, D_PLE), lambda i: (pl_layer, i, 0)),
            _resident((None, 1, D_MODEL), lambda i: (pl_layer, 0, 0)),
            _resident((None, D_PLE, D_MODEL), lambda i: (pl_layer, 0, 0)),
            _resident((None, D_MODEL, D_MODEL), lambda i: (pl_layer, 0, 0)),
        ]
        args += list(ple[:4])
    out_specs = pl.BlockSpec((FFN_TM, D_MODEL), lambda i: (i, 0))
    out_shape = jax.ShapeDtypeStruct(x.shape, F32)
    if proj is not None:
        m_norm, m_layer, m_w, m_wl = proj
        n = m_w.shape[2]
        in_specs += [
            _resident((None, 1, D_MODEL), lambda i: (m_layer, 0, 0)),
            _resident((None, D_MODEL, n), lambda i: (m_wl, 0, 0)),
        ]
        args += [m_norm, m_w]
        out_specs = [out_specs, pl.BlockSpec((FFN_TM, n), lambda i: (i, 0))]
        out_shape = [out_shape, jax.ShapeDtypeStruct((rows, n), F32)]
    return pl.pallas_call(
        functools.partial(_ffn_body, ple=ple is not None, proj=proj is not None),
        grid=(rows // FFN_TM,),
        in_specs=in_specs,
        out_specs=out_specs,
        out_shape=out_shape,
        compiler_params=_cparams("parallel"),
        name="ffn" + ("_ple" if ple is not None else "") + ("_proj" if proj is not None else ""),
    )(*args)


ROW_TM = 256


def _ple_body(x_ref, p_ref, g_ref, wp_ref, wg_ref, gf_ref, o_ref, *, final):
    y = _ple_update(x_ref[...], p_ref, g_ref, wp_ref, wg_ref)
    if final:
        y = _rms(y, gf_ref[...])
    o_ref[...] = y


def _ple(x, p_all, norm, w_proj, w_gate, final_norm, layer, final, row0=0, rows=None):
    rows = x.shape[0] if rows is None else rows
    blk0 = row0 // ROW_TM
    return pl.pallas_call(
        functools.partial(_ple_body, final=final),
        grid=(rows // ROW_TM,),
        in_specs=[
            pl.BlockSpec((ROW_TM, D_MODEL), lambda i: (i + blk0, 0)),
            pl.BlockSpec((None, ROW_TM, D_PLE), lambda i: (layer, i + blk0, 0)),
            _resident((None, 1, D_MODEL), lambda i: (layer, 0, 0)),
            _resident((None, D_PLE, D_MODEL), lambda i: (layer, 0, 0)),
            _resident((None, D_MODEL, D_MODEL), lambda i: (layer, 0, 0)),
            _resident((1, D_MODEL), lambda i: (0, 0)),
        ],
        out_specs=pl.BlockSpec((ROW_TM, D_MODEL), lambda i: (i, 0)),
        out_shape=jax.ShapeDtypeStruct((rows, D_MODEL), F32),
        compiler_params=_cparams("parallel"),
        name="ple",
    )(x, p_all, norm, w_proj, w_gate, final_norm)


def _norm_proj_body(x_ref, g_ref, w_ref, o_ref):
    o_ref[...] = _dot(_rms(x_ref[...], g_ref[...]).astype(BF16), w_ref[...])


def _norm_proj(x, norm, w, layer, wl, name):
    rows = x.shape[0]
    k, n = w.shape[1:]
    return pl.pallas_call(
        _norm_proj_body,
        grid=(rows // ROW_TM,),
        in_specs=[
            pl.BlockSpec((ROW_TM, k), lambda i: (i, 0)),
            _resident((None, 1, k), lambda i: (layer, 0, 0)),
            _resident((None, k, n), lambda i: (wl, 0, 0)),
        ],
        out_specs=pl.BlockSpec((ROW_TM, n), lambda i: (i, 0)),
        out_shape=jax.ShapeDtypeStruct((rows, n), F32),
        compiler_params=_cparams("parallel"),
        name=name,
    )(x, norm, w)


def _res_proj_body(x_ref, a_ref, w_ref, o_ref):
    o_ref[...] = x_ref[...] + _dot(a_ref[...].astype(BF16), w_ref[...])


def _res_proj(x, a, w, wl, row0, name):
    rows = a.shape[0]
    k, n = w.shape[1:]
    blk0 = row0 // ROW_TM
    return pl.pallas_call(
        _res_proj_body,
        grid=(rows // ROW_TM,),
        in_specs=[
            pl.BlockSpec((ROW_TM, n), lambda i: (i + blk0, 0)),
            pl.BlockSpec((ROW_TM, k), lambda i: (i, 0)),
            _resident((None, k, n), lambda i: (wl, 0, 0)),
        ],
        out_specs=pl.BlockSpec((ROW_TM, n), lambda i: (i + blk0, 0)),
        out_shape=jax.ShapeDtypeStruct(x.shape, F32),
        input_output_aliases={0: 0},
        compiler_params=_cparams("parallel"),
        name=name,
    )(x, a, w)


def _t5_bucket(dist):
    max_exact = N_BUCKETS // 2
    d = jnp.maximum(dist, 0)
    large = max_exact + (jnp.log(jnp.maximum(d, 1).astype(F32) / max_exact)
                         / math.log(BUCKET_MAX_DIST / max_exact) * (N_BUCKETS - max_exact)).astype(jnp.int32)
    large = jnp.minimum(large, N_BUCKETS - 1)
    return jnp.where(d < max_exact, d, large)


def _bias_body(tab_ref, bm_ref, valid_ref, o_ref):
    h = pl.program_id(0)
    bm = bm_ref[...]
    acc = jnp.zeros(bm.shape, F32)
    for b in range(N_BUCKETS):
        acc = jnp.where(bm == b, tab_ref[b, h], acc)
    o_ref[...] = jnp.where(valid_ref[...] > 0, acc, NEG_INF)


def _rel_bias(table, bucket_map, valid):
    r, c = bucket_map.shape
    return pl.pallas_call(
        _bias_body,
        grid=(A_HEADS,),
        in_specs=[pl.BlockSpec(memory_space=pltpu.SMEM), pl.BlockSpec((r, c), lambda h: (0, 0)),
                  pl.BlockSpec((r, c), lambda h: (0, 0))],
        out_specs=pl.BlockSpec((None, r, c), lambda h: (h, 0, 0)),
        out_shape=jax.ShapeDtypeStruct((A_HEADS, r, c), F32),
        compiler_params=_cparams("parallel"),
        name="rel_bias",
    )(table, bucket_map, valid.astype(jnp.int32))


def _group_sinks(sinks, t):
    return jnp.broadcast_to(sinks.reshape(A_KV_HEADS, A_GROUP, 1, 1),
                            (A_KV_HEADS, A_GROUP, t, LANES)).reshape(A_KV_HEADS, A_GROUP * t, LANES)


def _swa_prompt_body(sink_ref, q_ref, kp_ref, ko_ref, vp_ref, vo_ref, bias_ref, x_ref, wo_ref, o_ref, o_scr):
    scale = A_HEAD_DIM ** -0.5
    for g in range(A_KV_HEADS):
        gs = slice(g * A_HEAD_DIM, (g + 1) * A_HEAD_DIM)
        kg = jnp.concatenate([kp_ref[:, gs], ko_ref[:, gs]], axis=0).astype(BF16)
        vg = jnp.concatenate([vp_ref[:, gs], vo_ref[:, gs]], axis=0).astype(BF16)
        for h in range(g * A_GROUP, (g + 1) * A_GROUP):
            hs = slice(h * A_HEAD_DIM, (h + 1) * A_HEAD_DIM)
            s = _dot_nt(q_ref[:, hs].astype(BF16), kg) * scale + bias_ref[h]
            sink = sink_ref[h]
            m = jnp.maximum(jnp.max(s, axis=-1, keepdims=True), sink)
            e = jnp.exp(s - m)
            den = jnp.sum(e, axis=-1, keepdims=True) + jnp.exp(sink - m)
            o_scr[:, hs] = _dot((e / den).astype(BF16), vg)
    o_ref[...] = x_ref[...] + _dot(o_scr[...].astype(BF16), wo_ref[...])


def _swa_prompt(x, qkv, bias, sinks, w_o, wl, nb_p, t_p):
    nblk = t_p // A_BLOCK
    kcol = A_Q // A_KV
    vcol = kcol + 1

    def own(b, n):
        return b * nblk + n

    def prev(b, n):
        return b * nblk + jnp.maximum(n - 1, 0)

    return pl.pallas_call(
        _swa_prompt_body,
        grid=(nb_p, nblk),
        in_specs=[
            pl.BlockSpec(memory_space=pltpu.SMEM),
            pl.BlockSpec((A_BLOCK, A_Q), lambda b, n: (own(b, n), 0)),
            pl.BlockSpec((A_BLOCK, A_KV), lambda b, n: (prev(b, n), kcol)),
            pl.BlockSpec((A_BLOCK, A_KV), lambda b, n: (own(b, n), kcol)),
            pl.BlockSpec((A_BLOCK, A_KV), lambda b, n: (prev(b, n), vcol)),
            pl.BlockSpec((A_BLOCK, A_KV), lambda b, n: (own(b, n), vcol)),
            pl.BlockSpec((None, A_HEADS, A_BLOCK, 2 * A_BLOCK), lambda b, n: (jnp.minimum(n, 1), 0, 0, 0)),
            pl.BlockSpec((A_BLOCK, D_MODEL), lambda b, n: (own(b, n), 0)),
            _resident((None, A_Q, D_MODEL), lambda b, n: (wl, 0, 0)),
        ],
        out_specs=pl.BlockSpec((A_BLOCK, D_MODEL), lambda b, n: (own(b, n), 0)),
        out_shape=jax.ShapeDtypeStruct(x.shape, F32),
        scratch_shapes=[pltpu.VMEM((A_BLOCK, A_Q), F32)],
        input_output_aliases={7: 0},
        compiler_params=_cparams("parallel", "arbitrary"),
        name="swa_prompt",
    )(sinks, qkv, qkv, qkv, qkv, qkv, bias, x, w_o)


SWA_S_NB = 8


def _swa_sample_body(qkv_ref, kt_ref, vt_ref, bb_ref, bn_ref, sink_ref, o_ref, *, l_s, w_buf):
    scale = A_HEAD_DIM ** -0.5

    def one(b, carry):
        rows = pl.ds(pl.multiple_of(b * l_s, l_s), l_s)
        for g in range(A_KV_HEADS):
            heads = range(g * A_GROUP, (g + 1) * A_GROUP)
            qg = jnp.concatenate([qkv_ref[rows, h * A_HEAD_DIM:(h + 1) * A_HEAD_DIM] for h in heads], axis=0)
            qg = qg.astype(BF16)
            kn = qkv_ref[rows, A_Q + g * A_HEAD_DIM:A_Q + (g + 1) * A_HEAD_DIM].astype(BF16)
            vn = qkv_ref[rows, A_Q + A_KV + g * A_HEAD_DIM:A_Q + A_KV + (g + 1) * A_HEAD_DIM].astype(BF16)
            sb = _dot(qg, kt_ref[b, g].astype(BF16)) * scale + bb_ref[g]
            sn = _dot_nt(qg, kn) * scale + bn_ref[g]
            sink = sink_ref[g][:, :1]
            m = jnp.maximum(jnp.maximum(jnp.max(sb, axis=-1, keepdims=True), jnp.max(sn, axis=-1, keepdims=True)), sink)
            eb = jnp.exp(sb - m)
            en = jnp.exp(sn - m)
            den = jnp.sum(eb, axis=-1, keepdims=True) + jnp.sum(en, axis=-1, keepdims=True) + jnp.exp(sink - m)
            og = _dot_nt((eb / den).astype(BF16), vt_ref[b, g].astype(BF16)) + _dot((en / den).astype(BF16), vn)
            for i, h in enumerate(heads):
                o_ref[rows, h * A_HEAD_DIM:(h + 1) * A_HEAD_DIM] = og[i * l_s:(i + 1) * l_s]
        return carry

    lax.fori_loop(0, SWA_S_NB, one, 0, unroll=True)


def _swa_sample(qkv, kt_buf, vt_buf, bias, sinks, sl, row0, nb_s, l_s):
    w_buf = kt_buf.shape[-1]
    r = A_GROUP * l_s
    tm = SWA_S_NB * l_s
    blk0 = row0 // tm
    bias_g = bias.reshape(A_KV_HEADS, r, w_buf + l_s)
    sink_g = _group_sinks(sinks, l_s)
    buf_spec = pl.BlockSpec((None, SWA_S_NB, A_KV_HEADS, A_HEAD_DIM, w_buf), lambda i: (sl, i, 0, 0, 0))
    const3 = lambda i: (0, 0, 0)
    return pl.pallas_call(
        functools.partial(_swa_sample_body, l_s=l_s, w_buf=w_buf),
        grid=(nb_s // SWA_S_NB,),
        in_specs=[
            pl.BlockSpec((tm, A_Q + 2 * A_KV), lambda i: (i + blk0, 0)),
            buf_spec,
            buf_spec,
            _resident((A_KV_HEADS, r, w_buf), const3),
            _resident((A_KV_HEADS, r, l_s), const3),
            _resident((A_KV_HEADS, r, LANES), const3),
        ],
        out_specs=pl.BlockSpec((tm, A_Q), lambda i: (i, 0)),
        out_shape=jax.ShapeDtypeStruct((nb_s * l_s, A_Q), F32),
        compiler_params=_cparams("parallel"),
        name="swa_sample",
    )(qkv, kt_buf, vt_buf, bias_g[:, :, :w_buf], bias_g[:, :, w_buf:], sink_g)


B_IN_COLS = Q_LORA + KV_LORA + 2 * LANES
B_KPE = Q_LORA + KV_LORA
B_KPE_ROT = B_KPE + LANES


def _mla_proj_body(x_ref, g_ref, win_ref, qn_ref, kvn_ref, wnope_ref, wpe_ref, wper_ref, wuk_ref,
                   cos_ref, sin_ref, kv_ref, q_ref, *maybe_kvb_ref, sample):
    tm = x_ref.shape[0]
    h = _rms(x_ref[...], g_ref[...]).astype(BF16)
    c = _dot(h, win_ref[...])
    cos = cos_ref[...]
    sin = sin_ref[...]
    kv_lat = _rms(c[:, Q_LORA:B_KPE], kvn_ref[...])
    kv_rope = (c[:, B_KPE:B_KPE + QK_ROPE] * cos[:, :QK_ROPE]
               + c[:, B_KPE_ROT:B_KPE_ROT + QK_ROPE] * sin[:, :QK_ROPE])
    kv_ref[:, :KV_LORA] = kv_lat
    kv_ref[:, KV_LORA:] = kv_rope
    for kvb_ref in maybe_kvb_ref:
        kvb_ref[:, :KV_LORA] = kv_lat.astype(BF16)
        kvb_ref[:, KV_LORA:] = kv_rope.astype(BF16)
    qn = _rms(c[:, :Q_LORA], qn_ref[...]).astype(BF16)
    qnope = _dot(qn, wnope_ref[...])
    n_rep = B_HEADS * QK_ROPE // LANES
    qpe = (_dot(qn, wpe_ref[...]) * jnp.concatenate([cos] * n_rep, axis=1)
           + _dot(qn, wper_ref[...]) * jnp.concatenate([sin] * n_rep, axis=1))
    for hd in range(B_HEADS):
        qlat = _dot(qnope[:, hd * QK_NOPE:(hd + 1) * QK_NOPE].astype(BF16), wuk_ref[hd])
        qrope = qpe[:, hd * QK_ROPE:(hd + 1) * QK_ROPE]
        if sample:
            l_s = q_ref.shape[1] // B_HEADS
            rs = slice(hd * l_s, (hd + 1) * l_s)
            q_ref[:, rs, :KV_LORA] = qlat.reshape(tm // l_s, l_s, KV_LORA)
            q_ref[:, rs, KV_LORA:] = qrope.reshape(tm // l_s, l_s, QK_ROPE)
        else:
            for s in range(tm // B_QBLOCK):
                ts = slice(s * B_QBLOCK, (s + 1) * B_QBLOCK)
                q_ref[s, hd, :, :KV_LORA] = qlat[ts].astype(BF16)
                q_ref[s, hd, :, KV_LORA:] = qrope[ts].astype(BF16)


def _mla_proj(x, norm, layer, w, cos, sin, row0, rows, l_s):
    sample = l_s is not None
    tm = ROW_TM
    blk0 = row0 // tm
    if sample:
        q_shape = jax.ShapeDtypeStruct((rows // l_s, B_HEADS * l_s, KV_ROW), F32)
        q_spec = pl.BlockSpec((tm // l_s, B_HEADS * l_s, KV_ROW), lambda i: (i, 0, 0))
    else:
        q_shape = jax.ShapeDtypeStruct((rows // B_QBLOCK, B_HEADS, B_QBLOCK, KV_ROW), BF16)
        q_spec = pl.BlockSpec((tm // B_QBLOCK, B_HEADS, B_QBLOCK, KV_ROW), lambda i: (i, 0, 0, 0))
    const2 = lambda i: (0, 0)
    kv_spec = pl.BlockSpec((tm, KV_ROW), lambda i: (i, 0))
    out_specs = [kv_spec, q_spec]
    out_shape = [jax.ShapeDtypeStruct((rows, KV_ROW), F32), q_shape]
    if not sample:
        out_specs.append(kv_spec)
        out_shape.append(jax.ShapeDtypeStruct((rows, KV_ROW), BF16))
    return pl.pallas_call(
        functools.partial(_mla_proj_body, sample=sample),
        grid=(rows // tm,),
        in_specs=[
            pl.BlockSpec((tm, D_MODEL), lambda i: (i + blk0, 0)),
            _resident((None, 1, D_MODEL), lambda i: (layer, 0, 0)),
            _resident((D_MODEL, B_IN_COLS), const2),
            _resident((1, Q_LORA), const2),
            _resident((1, KV_LORA), const2),
            _resident((Q_LORA, B_HEADS * QK_NOPE), const2),
            _resident((Q_LORA, B_HEADS * QK_ROPE), const2),
            _resident((Q_LORA, B_HEADS * QK_ROPE), const2),
            _resident((B_HEADS, QK_NOPE, KV_LORA), lambda i: (0, 0, 0)),
            pl.BlockSpec((tm, LANES), lambda i: (i + blk0, 0)),
            pl.BlockSpec((tm, LANES), lambda i: (i + blk0, 0)),
        ],
        out_specs=out_specs,
        out_shape=out_shape,
        compiler_params=_cparams("parallel"),
        name="mla_proj_sample" if sample else "mla_proj_prompt",
    )(x, norm, w["w_in"], w["q_norm"], w["kv_norm"], w["w_nope"], w["w_pe"], w["w_pe_rot"], w["w_uk"], cos, sin)


def _softmax_stats(s, m_scr, l_scr):
    c = MLA_SCALE * math.log2(math.e)
    m_prev = m_scr[...]
    m_new = jnp.maximum(m_prev, jnp.max(s, axis=-1, keepdims=True))
    alpha = jnp.exp2((m_prev - m_new) * c)
    p = jnp.exp2((s - _lanes(m_new, s.shape[1])) * c)
    l_scr[...] = alpha * l_scr[...] + jnp.sum(p, axis=-1, keepdims=True)
    m_scr[...] = m_new
    return alpha, p.astype(BF16)


def _softmax_accumulate(s, pv, m_scr, l_scr, acc_scr):
    alpha, p = _softmax_stats(s, m_scr, l_scr)
    acc_scr[...] = acc_scr[...] * _lanes(alpha, KV_LORA) + pv(p)


def _softmax_update(q, kvb, m_scr, l_scr, acc_scr, valid=None):
    s = _dot_nt(q, kvb)
    if valid is not None:
        s = jnp.where(valid, s, NEG_INF)
    _softmax_accumulate(s, lambda p: _dot(p, kvb[:, :KV_LORA]), m_scr, l_scr, acc_scr)


def _softmax_init(m_scr, l_scr, acc_scr):
    m_scr[...] = jnp.full(m_scr.shape, NEG_INF, F32)
    l_scr[...] = jnp.zeros(l_scr.shape, F32)
    acc_scr[...] = jnp.zeros(acc_scr.shape, F32)


def _mla_prompt_body(q_ref, kv_ref, x_ref, wuv_ref, wo_ref, o_ref, s0_scr, s1_scr, m_scr, l_scr, acc_scr, u_scr):
    qi = pl.program_id(1)
    r = B_HEADS * B_QBLOCK
    n_off = qi // (B_KVBLOCK // B_QBLOCK)
    q = q_ref[...].reshape(r, KV_ROW)

    def kv_block(k):
        return kv_ref[pl.ds(pl.multiple_of(k * B_KVBLOCK, B_KVBLOCK), B_KVBLOCK), :]

    def update(s, k):
        kvb = kv_block(k)
        _softmax_accumulate(s, lambda p: _dot(p, kvb[:, :KV_LORA]), m_scr, l_scr, acc_scr)

    def step(s_cur, s_next, k):
        s_next[...] = _dot_nt(q, kv_block(k + 1))
        update(s_cur[...], k)

    def last(s_cur):
        tok = lax.broadcasted_iota(jnp.int32, (r, B_KVBLOCK), 0) & (B_QBLOCK - 1)
        key = lax.broadcasted_iota(jnp.int32, (r, B_KVBLOCK), 1)
        valid = key - tok <= qi * B_QBLOCK - n_off * B_KVBLOCK
        update(jnp.where(valid, s_cur[...], NEG_INF), n_off)
        o = acc_scr[...] * _lanes(1.0 / l_scr[...], KV_LORA)
        for hd in range(B_HEADS):
            oh = o[hd * B_QBLOCK:(hd + 1) * B_QBLOCK].astype(BF16)
            u_scr[:, hd * V_HEAD:(hd + 1) * V_HEAD] = _dot(oh, wuv_ref[hd])
        o_ref[...] = x_ref[...] + _dot(u_scr[...].astype(BF16), wo_ref[...])

    _softmax_init(m_scr, l_scr, acc_scr)
    s0_scr[...] = _dot_nt(q, kv_block(0))

    def two_steps(j, carry):
        step(s0_scr, s1_scr, 2 * j)
        step(s1_scr, s0_scr, 2 * j + 1)
        return carry

    lax.fori_loop(0, n_off // 2, two_steps, 0)
    odd = n_off % 2 == 1

    @pl.when(odd)
    def _():
        step(s0_scr, s1_scr, n_off - 1)
        last(s1_scr)

    @pl.when(jnp.logical_not(odd))
    def _():
        last(s0_scr)


def _mla_prompt(x, q_cat, kv_bf16, w, nb_p, t_p):
    nq = t_p // B_QBLOCK
    r = B_HEADS * B_QBLOCK
    return pl.pallas_call(
        _mla_prompt_body,
        grid=(nb_p, nq),
        in_specs=[
            pl.BlockSpec((None, B_HEADS, B_QBLOCK, KV_ROW), lambda b, qi: (b * nq + qi, 0, 0, 0)),
            pl.BlockSpec((t_p, KV_ROW), lambda b, qi: (b, 0)),
            pl.BlockSpec((B_QBLOCK, D_MODEL), lambda b, qi: (b * nq + qi, 0)),
            _resident((B_HEADS, KV_LORA, V_HEAD), lambda b, qi: (0, 0, 0)),
            _resident((B_HEADS * V_HEAD, D_MODEL), lambda b, qi: (0, 0)),
        ],
        out_specs=pl.BlockSpec((B_QBLOCK, D_MODEL), lambda b, qi: (b * nq + qi, 0)),
        out_shape=jax.ShapeDtypeStruct(x.shape, F32),
        scratch_shapes=[
            pltpu.VMEM((r, B_KVBLOCK), F32),
            pltpu.VMEM((r, B_KVBLOCK), F32),
            pltpu.VMEM((r, LANES), F32),
            pltpu.VMEM((r, LANES), F32),
            pltpu.VMEM((r, KV_LORA), F32),
            pltpu.VMEM((B_QBLOCK, B_HEADS * V_HEAD), F32),
        ],
        input_output_aliases={2: 0},
        compiler_params=_cparams("parallel", "arbitrary"),
        name="mla_prompt",
    )(q_cat, kv_bf16, x, w["w_uv"], w["w_o"])


def _mla_sample_body(pt_ref, q_ref, cache_ref, kvn_ref, o_ref, buf, sem, m_scr, l_scr, acc_scr, acct_scr,
                     *, l_s, n_pages, cl):
    b = pl.program_id(0)
    slot = b % 2
    r = q_ref.shape[0]
    assert r == LANES and n_pages % B_PAGES_PER_STEP == 0

    def page_copy(batch, slot_, k):
        page = pt_ref[batch * n_pages + k]
        return pltpu.make_async_copy(cache_ref.at[cl, page], buf.at[slot_, k], sem.at[slot_, k])

    def start_pages(batch, slot_):
        for k in range(n_pages):
            page_copy(batch, slot_, k).start(priority=k % 2)

    @pl.when(b == 0)
    def _():
        start_pages(0, 0)

    @pl.when(b + 1 < pl.num_programs(0))
    def _():
        start_pages(b + 1, 1 - slot)

    _softmax_init(m_scr, l_scr, acct_scr)
    q = q_ref[...].astype(BF16)
    for c in range(n_pages // B_PAGES_PER_STEP):
        ks = range(c * B_PAGES_PER_STEP, (c + 1) * B_PAGES_PER_STEP)
        for k in ks:
            page_copy(b, slot, k).wait()
        kt = jnp.concatenate([buf[slot, k].astype(BF16) for k in ks], axis=1)
        alpha, p = _softmax_stats(_dot(q, kt), m_scr, l_scr)
        alpha_t = jnp.concatenate([alpha.T] * (KV_LORA // LANES), axis=0)
        acct_scr[...] = acct_scr[...] * alpha_t + _dot_nt(kt[:KV_LORA, :], p)

    acc_scr[...] = acct_scr[...].T
    tok = lax.broadcasted_iota(jnp.int32, (r, l_s), 0) % l_s
    key = lax.broadcasted_iota(jnp.int32, (r, l_s), 1)
    _softmax_update(q, kvn_ref[...].astype(BF16), m_scr, l_scr, acc_scr, valid=key <= tok)
    o_ref[...] = acc_scr[...] * _lanes(1.0 / l_scr[...], KV_LORA)


def _mla_sample(q_cat, kv_rows, cache_t, page_table, cl, row0, nb_s, l_s):
    n_pages = page_table.shape[1]
    r = B_HEADS * l_s
    blk0 = row0 // l_s
    return pl.pallas_call(
        functools.partial(_mla_sample_body, l_s=l_s, n_pages=n_pages, cl=cl),
        grid_spec=pltpu.PrefetchScalarGridSpec(
            num_scalar_prefetch=1,
            grid=(nb_s,),
            in_specs=[
                pl.BlockSpec((None, r, KV_ROW), lambda b, pt: (b, 0, 0)),
                pl.BlockSpec(memory_space=pl.ANY),
                pl.BlockSpec((l_s, KV_ROW), lambda b, pt: (b + blk0, 0)),
            ],
            out_specs=pl.BlockSpec((None, r, KV_LORA), lambda b, pt: (b, 0, 0)),
            scratch_shapes=[
                pltpu.VMEM((2, n_pages, KV_ROW, PAGE_SIZE), F32),
                pltpu.SemaphoreType.DMA((2, n_pages)),
                pltpu.VMEM((r, LANES), F32),
                pltpu.VMEM((r, LANES), F32),
                pltpu.VMEM((r, KV_LORA), F32),
                pltpu.VMEM((KV_LORA, r), F32),
            ],
        ),
        out_shape=jax.ShapeDtypeStruct((nb_s, r, KV_LORA), F32),
        compiler_params=_cparams("arbitrary"),
        name="mla_sample",
    )(page_table.reshape(-1), q_cat, cache_t, kv_rows)


def _mla_out_sample_body(o_ref, x_ref, wuv_ref, wo_ref, out_ref, u_scr, *, l_s):
    nb = o_ref.shape[0]
    for hd in range(B_HEADS):
        oh = o_ref[:, hd * l_s:(hd + 1) * l_s, :].reshape(nb * l_s, KV_LORA).astype(BF16)
        u_scr[:, hd * V_HEAD:(hd + 1) * V_HEAD] = _dot(oh, wuv_ref[hd])
    out_ref[...] = x_ref[...] + _dot(u_scr[...].astype(BF16), wo_ref[...])


def _mla_out_sample(x, o_lat, w, row0, l_s):
    nb_s = o_lat.shape[0]
    nb = ROW_TM // l_s
    blk0 = row0 // ROW_TM
    return pl.pallas_call(
        functools.partial(_mla_out_sample_body, l_s=l_s),
        grid=(nb_s // nb,),
        in_specs=[
            pl.BlockSpec((nb, B_HEADS * l_s, KV_LORA), lambda i: (i, 0, 0)),
            pl.BlockSpec((ROW_TM, D_MODEL), lambda i: (i + blk0, 0)),
            _resident((B_HEADS, KV_LORA, V_HEAD), lambda i: (0, 0, 0)),
            _resident((B_HEADS * V_HEAD, D_MODEL), lambda i: (0, 0)),
        ],
        out_specs=pl.BlockSpec((ROW_TM, D_MODEL), lambda i: (i + blk0, 0)),
        out_shape=jax.ShapeDtypeStruct(x.shape, F32),
        scratch_shapes=[pltpu.VMEM((ROW_TM, B_HEADS * V_HEAD), F32)],
        input_output_aliases={1: 0},
        compiler_params=_cparams("parallel"),
        name="mla_out_sample",
    )(o_lat, x, w["w_uv"], w["w_o"])


C_IN_COLS = D_INNER + CONV_DIM + LANES


def _mamba_in_body(x_ref, g_ref, w_ref, z_ref, xbc_ref, dt_ref):
    h = _rms(x_ref[...], g_ref[...]).astype(BF16)
    z_ref[...] = _dot(h, w_ref[:, :D_INNER])
    xbc_ref[...] = _dot(h, w_ref[:, D_INNER:D_INNER + CONV_DIM])
    dt_ref[...] = _dot(h, w_ref[:, D_INNER + CONV_DIM:])


def _mamba_in(x, norm, layer, w_in):
    rows = x.shape[0]
    return pl.pallas_call(
        _mamba_in_body,
        grid=(rows // ROW_TM,),
        in_specs=[
            pl.BlockSpec((ROW_TM, D_MODEL), lambda i: (i, 0)),
            _resident((None, 1, D_MODEL), lambda i: (layer, 0, 0)),
            _resident((D_MODEL, C_IN_COLS), lambda i: (0, 0)),
        ],
        out_specs=[
            pl.BlockSpec((ROW_TM, D_INNER), lambda i: (i, 0)),
            pl.BlockSpec((ROW_TM, CONV_DIM), lambda i: (i, 0)),
            pl.BlockSpec((ROW_TM, LANES), lambda i: (i, 0)),
        ],
        out_shape=[
            jax.ShapeDtypeStruct((rows, D_INNER), F32),
            jax.ShapeDtypeStruct((rows, CONV_DIM), F32),
            jax.ShapeDtypeStruct((rows, LANES), F32),
        ],
        compiler_params=_cparams("parallel"),
        name="mamba_in",
    )(x, norm, w_in)


def _causal_conv(ext_scr, u, cw_ref, cb_ref):
    t = u.shape[0]
    ext_scr[SUBLANES:SUBLANES + t, :] = u
    acc = cb_ref[...] + cw_ref[D_CONV - 1:D_CONV, :] * u
    for k in range(D_CONV - 1):
        lo = SUBLANES - (D_CONV - 1) + k
        acc = acc + cw_ref[k:k + 1, :] * ext_scr[lo:lo + t, :]
    return acc


def _gated_norm(y, z, norm_g):
    yz = y * _silu(z)
    gw = D_INNER // C_GROUPS
    parts = []
    for g in range(C_GROUPS):
        part = yz[:, g * gw:(g + 1) * gw]
        parts.append(part * lax.rsqrt(jnp.mean(part * part, axis=-1, keepdims=True) + EPS))
    return jnp.concatenate(parts, axis=1) * norm_g


def _ssd_prompt_body(xbc_ref, dt_ref, z_ref, cw_ref, cb_ref, dtb_ref, alog_ref, dskip_ref, ng_ref, yn_ref, st_ref,
                     ext_scr, ht_scr, y_ref):
    c = pl.program_id(1)
    t = SSD_CHUNK

    @pl.when(c == 0)
    def _():
        ext_scr[0:SUBLANES, :] = jnp.zeros((SUBLANES, CONV_DIM), F32)
        ht_scr[...] = jnp.zeros(ht_scr.shape, F32)

    u = xbc_ref[...]
    act = _silu(_causal_conv(ext_scr, u, cw_ref, cb_ref))
    ext_scr[0:SUBLANES, :] = u[t - SUBLANES:, :]

    dt = _softplus(dt_ref[...] + dtb_ref[...])
    a = -jnp.exp(alog_ref[...])
    row = lax.broadcasted_iota(jnp.int32, (t, t), 0)
    col = lax.broadcasted_iota(jnp.int32, (t, t), 1)
    causal = col <= row
    acs = _dot_f32(causal.astype(F32), dt * a)
    acs_t = acs.T
    dt_t = dt.T
    to_end = jnp.exp(acs[t - 1:t, :] - acs) * dt
    gw = C_HPG * C_HEAD_DIM
    heads_per_vreg = LANES // C_HEAD_DIM
    assert heads_per_vreg == 2 and t == LANES
    first = lax.broadcasted_iota(jnp.int32, (t, LANES), 1) < C_HEAD_DIM
    for g in range(C_GROUPS):
        bg = act[:, D_INNER + g * D_STATE:D_INNER + (g + 1) * D_STATE]
        cg = act[:, D_INNER + C_BC + g * D_STATE:D_INNER + C_BC + (g + 1) * D_STATE].astype(BF16)
        cb = _dot_nt(cg, bg.astype(BF16))
        bg_t = bg.T.astype(BF16)
        y_off = _dot(cg, ht_scr[:, g * gw:(g + 1) * gw].astype(BF16))
        for jp in range(C_HPG // heads_per_vreg):
            h0 = g * C_HPG + heads_per_vreg * jp
            ps = slice(h0 * C_HEAD_DIM, h0 * C_HEAD_DIM + LANES)
            ms, a_cols, te_cols = [], [], []
            for h in (h0, h0 + 1):
                a_col = jnp.broadcast_to(acs[:, h:h + 1], (t, t))
                a_row = jnp.broadcast_to(acs_t[h:h + 1, :], (t, t))
                dt_row = jnp.broadcast_to(dt_t[h:h + 1, :], (t, t))
                ms.append(cb * jnp.exp(jnp.where(causal, a_col - a_row, -jnp.inf)) * dt_row)
                a_cols.append(a_col)
                te_cols.append(jnp.broadcast_to(to_end[:, h:h + 1], (t, LANES)))
            x2 = act[:, ps]
            lhs = jnp.concatenate(ms, axis=1).astype(BF16)
            rhs = jnp.concatenate([jnp.where(first, x2, 0.0), jnp.where(first, 0.0, x2)], axis=0).astype(BF16)
            a_col2 = jnp.where(first, a_cols[0], a_cols[1])
            y_ref[:, ps] = (_dot(lhs, rhs)
                            + y_off[:, jp * LANES:(jp + 1) * LANES] * jnp.exp(a_col2)
                            + dskip_ref[:, ps] * x2)
            xw = (x2 * jnp.where(first, te_cols[0], te_cols[1])).astype(BF16)
            ht_scr[:, ps] = jnp.exp(a_col2[t - 1:t, :]) * ht_scr[:, ps] + _dot(bg_t, xw)

    yn_ref[...] = _gated_norm(y_ref[...], z_ref[...], ng_ref[...]).astype(BF16)

    @pl.when(c == pl.num_programs(1) - 1)
    def _():
        st_ref[...] = ht_scr[...].T


def _ssd_prompt(xbc, dt, z, w, nb_p, t_p):
    nc = t_p // SSD_CHUNK
    const2 = lambda b, c: (0, 0)
    rows = lambda b, c: (b * nc + c, 0)
    return pl.pallas_call(
        _ssd_prompt_body,
        grid=(nb_p, nc),
        in_specs=[
            pl.BlockSpec((SSD_CHUNK, CONV_DIM), rows),
            pl.BlockSpec((SSD_CHUNK, LANES), rows),
            pl.BlockSpec((SSD_CHUNK, D_INNER), rows),
            _resident((D_CONV, CONV_DIM), const2),
            _resident((1, CONV_DIM), const2),
            _resident((1, LANES), const2),
            _resident((1, LANES), const2),
            _resident((1, D_INNER), const2),
            _resident((1, D_INNER), const2),
        ],
        out_specs=[
            pl.BlockSpec((SSD_CHUNK, D_INNER), rows),
            pl.BlockSpec((None, D_INNER, D_STATE), lambda b, c: (b, 0, 0)),
        ],
        out_shape=[
            jax.ShapeDtypeStruct((nb_p * t_p, D_INNER), BF16),
            jax.ShapeDtypeStruct((nb_p, D_INNER, D_STATE), F32),
        ],
        scratch_shapes=[
            pltpu.VMEM((SUBLANES + SSD_CHUNK, CONV_DIM), F32),
            pltpu.VMEM((D_STATE, D_INNER), F32),
            pltpu.VMEM((SSD_CHUNK, D_INNER), F32),
        ],
        compiler_params=_cparams("parallel", "arbitrary"),
        name="ssd_prompt",
    )(xbc, dt, z, w["conv_w"], w["conv_b"], w["dt_bias"], w["a_log"], w["d_skip"], w["norm_g"])


def _ssd_sample_body(xbc_ref, dtx_ref, z_ref, cprev_ref, h0_ref, cw_ref, cb_ref, dtb_ref, alog_ref, dskip_ref, ng_ref,
                     y_ref, st_ref, ext_scr, *, l_s):
    u = xbc_ref[...]
    ext_scr[SUBLANES - (D_CONV - 1):SUBLANES, :] = cprev_ref[...]
    act = _silu(_causal_conv(ext_scr, u, cw_ref, cb_ref))
    xs = act[:, :D_INNER]

    dt = _softplus(dtx_ref[...] + dtb_ref[...])
    a = -jnp.exp(alog_ref[...])
    row = lax.broadcasted_iota(jnp.int32, (l_s, l_s), 0)
    col = lax.broadcasted_iota(jnp.int32, (l_s, l_s), 1)
    acs = _dot_f32((col <= row).astype(F32), dt * a)
    last = acs[l_s - 1:l_s, :]
    xdt = xs * dt
    xw = xdt * jnp.exp(last - acs)
    decay_col = jnp.broadcast_to(jnp.exp(last), (LANES, D_INNER)).T
    gw = C_HPG * C_HEAD_DIM
    lrow = lax.broadcasted_iota(jnp.int32, (l_s, gw), 0)
    for g in range(C_GROUPS):
        gs = slice(g * gw, (g + 1) * gw)
        bg = act[:, D_INNER + g * D_STATE:D_INNER + (g + 1) * D_STATE]
        cg = act[:, D_INNER + C_BC + g * D_STATE:D_INNER + C_BC + (g + 1) * D_STATE]
        cb = _dot_nt(cg.astype(BF16), bg.astype(BF16))
        h0 = h0_ref[gs, :]
        y = _dot_nt(cg.astype(BF16), h0.astype(BF16)) * jnp.exp(acs[:, gs])
        for s in range(l_s):
            seg = acs[:, gs] - acs[s:s + 1, gs]
            w_s = jnp.exp(jnp.where(lrow >= s, seg, -jnp.inf)) * jnp.broadcast_to(cb[:, s:s + 1], (l_s, gw))
            y = y + w_s * xdt[s:s + 1, gs]
        y_ref[:, gs] = y + dskip_ref[:, gs] * xs[:, gs]
        st_ref[gs, :] = decay_col[gs, :] * h0 + _dot_tn(xw[:, gs].astype(BF16), bg.astype(BF16))
    y_ref[...] = _gated_norm(y_ref[...], z_ref[...], ng_ref[...])


def _ssd_sample(xbc, dtx, z, conv_prev, h0, sl, w, row0, nb_s, l_s):
    blk0 = row0 // l_s
    const2 = lambda b: (0, 0)
    return pl.pallas_call(
        functools.partial(_ssd_sample_body, l_s=l_s),
        grid=(nb_s,),
        in_specs=[
            pl.BlockSpec((l_s, CONV_DIM), lambda b: (b + blk0, 0)),
            pl.BlockSpec((l_s, D_INNER), lambda b: (b, 0)),
            pl.BlockSpec((l_s, D_INNER), lambda b: (b + blk0, 0)),
            pl.BlockSpec((None, None, D_CONV - 1, CONV_DIM), lambda b: (sl, b, 0, 0)),
            pl.BlockSpec((None, None, D_INNER, D_STATE), lambda b: (sl, b, 0, 0)),
            _resident((D_CONV, CONV_DIM), const2),
            _resident((1, CONV_DIM), const2),
            _resident((1, D_INNER), const2),
            _resident((1, D_INNER), const2),
            _resident((1, D_INNER), const2),
            _resident((1, D_INNER), const2),
        ],
        out_specs=[
            pl.BlockSpec((l_s, D_INNER), lambda b: (b, 0)),
            pl.BlockSpec((None, D_INNER, D_STATE), lambda b: (b, 0, 0)),
        ],
        out_shape=[
            jax.ShapeDtypeStruct((nb_s * l_s, D_INNER), F32),
            jax.ShapeDtypeStruct((nb_s, D_INNER, D_STATE), F32),
        ],
        scratch_shapes=[pltpu.VMEM((SUBLANES + l_s, CONV_DIM), F32)],
        compiler_params=_cparams("parallel"),
        name="ssd_sample",
    )(xbc, dtx, z, conv_prev, h0, w["conv_w"], w["conv_b"], w["dt_bias_x"], w["a_log_x"], w["d_skip"], w["norm_g"])


def _rope_tables(pos):
    half = QK_ROPE // 2
    inv = jnp.power(ROPE_THETA, -jnp.arange(half, dtype=F32) / half)
    ang = pos.astype(F32)[:, None] * inv
    reps = LANES // half
    return jnp.tile(jnp.cos(ang), (1, reps)), jnp.tile(jnp.sin(ang), (1, reps))


def _rot_cols(w):
    half = w.shape[-1] // 2
    return jnp.concatenate([-w[..., half:], w[..., :half]], axis=-1)


def _mla_weights(w_in, q_norm, w_uq, kv_norm, w_ukv, w_o):
    kpe = w_in[:, B_KPE:]
    pad = jnp.zeros((D_MODEL, LANES - QK_ROPE), w_in.dtype)
    w_in_aug = jnp.concatenate([w_in[:, :B_KPE], kpe, pad, _rot_cols(kpe), pad], axis=1)
    uq = w_uq.reshape(Q_LORA, B_HEADS, QK_NOPE + QK_ROPE)
    pe = uq[:, :, QK_NOPE:]
    ukv = w_ukv.reshape(KV_LORA, B_HEADS, QK_NOPE + V_HEAD)
    return {
        "w_in": w_in_aug.astype(BF16),
        "q_norm": q_norm.reshape(1, Q_LORA),
        "kv_norm": kv_norm.reshape(1, KV_LORA),
        "w_nope": uq[:, :, :QK_NOPE].reshape(Q_LORA, B_HEADS * QK_NOPE).astype(BF16),
        "w_pe": pe.reshape(Q_LORA, B_HEADS * QK_ROPE).astype(BF16),
        "w_pe_rot": _rot_cols(pe).reshape(Q_LORA, B_HEADS * QK_ROPE).astype(BF16),
        "w_uk": jnp.transpose(ukv[:, :, :QK_NOPE], (1, 2, 0)).astype(BF16),
        "w_uv": jnp.transpose(ukv[:, :, QK_NOPE:], (1, 0, 2)).astype(BF16),
        "w_o": w_o.astype(BF16),
    }


def _pad_lanes(v):
    return jnp.pad(v, (0, LANES - v.shape[0])).reshape(1, LANES)


def _expand_heads(v):
    return jnp.repeat(v, C_HEAD_DIM).reshape(1, D_INNER)


def _mamba_weights(w_in, conv_w, conv_b, dt_bias, a_log, d_skip, norm_g, w_o):
    w_dt = w_in[:, D_INNER + CONV_DIM:]
    pad = jnp.zeros((D_MODEL, LANES - C_HEADS), w_in.dtype)
    return {
        "w_in": jnp.concatenate([w_in, pad], axis=1).astype(BF16),
        "w_dt_x": jnp.repeat(w_dt, C_HEAD_DIM, axis=1).astype(BF16)[None],
        "conv_w": conv_w,
        "conv_b": conv_b.reshape(1, CONV_DIM),
        "dt_bias": _pad_lanes(dt_bias.astype(F32)),
        "a_log": _pad_lanes(a_log.astype(F32)),
        "dt_bias_x": _expand_heads(dt_bias.astype(F32)),
        "a_log_x": _expand_heads(a_log.astype(F32)),
        "d_skip": _expand_heads(d_skip),
        "norm_g": norm_g.reshape(1, D_INNER),
        "w_o": w_o.astype(BF16),
    }


def kernel(x_prompt, x_sample, state_swa_k, state_swa_v, cache_mla, state_ssm, state_conv, page_table, p_prompt, p_sample, norm_ffn1, ffn1_w_gu, ffn1_w_down, norm_mix, norm_ffn2, ffn2_w_gu, ffn2_w_down, norm_ple, ple_w_proj, ple_w_gate, rel_bias_table, a_w_qkv, a_w_o, a_sinks, b_w_in, b_q_norm, b_w_uq, b_kv_norm, b_w_ukv, b_w_o, c_w_in, c_conv_w, c_conv_b, c_dt_bias, c_a_log, c_d, c_norm, c_w_o, final_norm):
    nb_p, t_p, _ = x_prompt.shape
    nb_s, l_s, _ = x_sample.shape
    depth = norm_ffn1.shape[0]
    rows_p = nb_p * t_p
    rows_s = nb_s * l_s
    w_buf = state_swa_k.shape[2]
    assert rows_p % FFN_TM == 0 and (rows_p + rows_s) % FFN_TM == 0 and rows_s % ROW_TM == 0
    assert t_p % B_KVBLOCK == 0 and t_p % SSD_CHUNK == 0 and t_p % A_BLOCK == 0
    assert l_s == SUBLANES and w_buf == WINDOW and page_table.shape[1] % B_PAGES_PER_STEP == 0

    x = jnp.concatenate([x_prompt.reshape(rows_p, D_MODEL), x_sample.reshape(rows_s, D_MODEL)], axis=0)
    p_all = jnp.concatenate([p_prompt.reshape(depth, rows_p, D_PLE), p_sample.reshape(depth, rows_s, D_PLE)], axis=1)

    def norm3(g):
        return g.reshape(g.shape[0], 1, g.shape[1])

    norm_ffn1, norm_mix, norm_ffn2, norm_ple = map(norm3, (norm_ffn1, norm_mix, norm_ffn2, norm_ple))
    ffn1_w_gu, ffn1_w_down, ffn2_w_gu, ffn2_w_down, ple_w_proj, ple_w_gate, a_w_qkv, a_w_o = (
        w.astype(BF16) for w in (ffn1_w_gu, ffn1_w_down, ffn2_w_gu, ffn2_w_down, ple_w_proj, ple_w_gate, a_w_qkv, a_w_o))
    final_norm = final_norm.reshape(1, D_MODEL)

    i = jnp.arange(A_BLOCK)[:, None]
    j = jnp.arange(2 * A_BLOCK)[None, :]
    dist = A_BLOCK + i - j
    in_window = (dist >= 0) & (dist < WINDOW)
    bias_p = jnp.stack([_rel_bias(rel_bias_table, _t5_bucket(dist), in_window & (j >= A_BLOCK)),
                        _rel_bias(rel_bias_table, _t5_bucket(dist), in_window)])
    q_pos = PAST_LEN + jnp.arange(l_s)
    k_pos = jnp.concatenate([PAST_LEN - w_buf + jnp.arange(w_buf), q_pos])
    dist = q_pos[:, None] - k_pos[None, :]
    bias_s = _rel_bias(rel_bias_table, _t5_bucket(dist), (dist >= 0) & (dist < WINDOW))
    swa_k_buf = jnp.transpose(state_swa_k, (0, 1, 3, 4, 2))
    swa_v_buf = jnp.transpose(state_swa_v, (0, 1, 3, 4, 2))

    pos = jnp.concatenate([jnp.tile(jnp.arange(t_p), nb_p), jnp.tile(PAST_LEN + jnp.arange(l_s), nb_s)])
    cos, sin = _rope_tables(pos)

    ssm_flat = state_ssm.reshape(state_ssm.shape[0], nb_s, D_INNER, D_STATE)
    cache_t = jnp.swapaxes(cache_mla, 2, 3)

    def last_rows(a, n):
        return jnp.stack([a[(b + 1) * t_p - n:(b + 1) * t_p] for b in range(nb_p)])

    swa_k_p, swa_v_p, mla_p, ssm_p, conv_p = [], [], [], [], []
    swa_k_s, swa_v_s, mla_s, ssm_s, conv_s = [], [], [], [], []
    for layer in range(depth):
        kind, sl = layer % N_MIXERS, layer // N_MIXERS
        prev_ple = None if layer == 0 else (p_all, norm_ple, ple_w_proj, ple_w_gate, layer - 1)
        if kind == 0:
            x, qkv = _ffn(x, norm_ffn1, ffn1_w_gu, ffn1_w_down, layer, ple=prev_ple,
                          proj=(norm_mix, layer, a_w_qkv, sl))
        else:
            x = _ffn(x, norm_ffn1, ffn1_w_gu, ffn1_w_down, layer, ple=prev_ple)
        if kind == 0:
            x = _swa_prompt(x, qkv, bias_p, a_sinks[sl], a_w_o, sl, nb_p, t_p)
            o_s = _swa_sample(qkv, swa_k_buf, swa_v_buf, bias_s, a_sinks[sl], sl, rows_p, nb_s, l_s)
            x = _res_proj(x, o_s, a_w_o, sl, rows_p, "swa_out_sample")
            kv_p = last_rows(qkv, w_buf)[:, :, A_Q:].reshape(nb_p, w_buf, 2, A_KV_HEADS, A_HEAD_DIM)
            kv_s = qkv[rows_p:, A_Q:].reshape(nb_s, l_s, 2, A_KV_HEADS, A_HEAD_DIM)
            swa_k_p.append(kv_p[:, :, 0])
            swa_v_p.append(kv_p[:, :, 1])
            swa_k_s.append(jnp.concatenate([state_swa_k[sl], kv_s[:, :, 0]], axis=1)[:, l_s:])
            swa_v_s.append(jnp.concatenate([state_swa_v[sl], kv_s[:, :, 1]], axis=1)[:, l_s:])
        elif kind == 1:
            w = _mla_weights(b_w_in[sl], b_q_norm[sl], b_w_uq[sl], b_kv_norm[sl], b_w_ukv[sl], b_w_o[sl])
            kv_rows_p, q_p, kv_bf16 = _mla_proj(x, norm_mix, layer, w, cos, sin, 0, rows_p, None)
            kv_rows_s, q_s = _mla_proj(x, norm_mix, layer, w, cos, sin, rows_p, rows_s, l_s)
            x = _mla_prompt(x, q_p, kv_bf16, w, nb_p, t_p)
            o_lat = _mla_sample(q_s, kv_rows_s, cache_t, page_table, sl, 0, nb_s, l_s)
            x = _mla_out_sample(x, o_lat, w, rows_p, l_s)
            mla_p.append(kv_rows_p.reshape(nb_p, t_p, KV_ROW))
            mla_s.append(kv_rows_s.reshape(nb_s, l_s, KV_ROW))
        else:
            w = _mamba_weights(c_w_in[sl], c_conv_w[sl], c_conv_b[sl], c_dt_bias[sl], c_a_log[sl], c_d[sl],
                               c_norm[sl], c_w_o[sl])
            z, xbc, dt = _mamba_in(x, norm_mix, layer, w["w_in"])
            dtx = _norm_proj(x[rows_p:], norm_mix, w["w_dt_x"], layer, 0, "mamba_dt_sample")
            y_p, st_p = _ssd_prompt(xbc, dt, z, w, nb_p, t_p)
            y_s, st_s = _ssd_sample(xbc, dtx, z, state_conv, ssm_flat, sl, w, rows_p, nb_s, l_s)
            x = _res_proj(x, y_p, w["w_o"][None], 0, 0, "mamba_out_prompt")
            x = _res_proj(x, y_s, w["w_o"][None], 0, rows_p, "mamba_out_sample")
            ssm_p.append(st_p.reshape(nb_p, C_HEADS, C_HEAD_DIM, D_STATE))
            ssm_s.append(st_s.reshape(nb_s, C_HEADS, C_HEAD_DIM, D_STATE))
            conv_p.append(last_rows(xbc, D_CONV - 1))
            conv_s.append(jnp.concatenate([state_conv[sl], xbc[rows_p:].reshape(nb_s, l_s, CONV_DIM)],
                                          axis=1)[:, -(D_CONV - 1):])
        x = _ffn(x, norm_ffn2, ffn2_w_gu, ffn2_w_down, layer)
    ple_last = functools.partial(_ple, x, p_all, norm_ple, ple_w_proj, ple_w_gate, final_norm, depth - 1, True)
    y_p = ple_last(0, rows_p)
    y_s = ple_last(rows_p, rows_s)

    return (y_p.reshape(nb_p, t_p, D_MODEL), y_s.reshape(nb_s, l_s, D_MODEL),
            jnp.stack(swa_k_p), jnp.stack(swa_v_p), jnp.stack(mla_p), jnp.stack(ssm_p), jnp.stack(conv_p),
            jnp.stack(swa_k_s), jnp.stack(swa_v_s), jnp.stack(mla_s), jnp.stack(ssm_s), jnp.stack(conv_s))
```

```python
import functools
import math

import jax
import jax.numpy as jnp
from jax import lax
from jax.experimental import pallas as pl
from jax.experimental.pallas import tpu as pltpu

F32 = jnp.float32
BF16 = jnp.bfloat16

D_MODEL = 1024
D_FF = 2816
D_PLE = 256
EPS = 1e-6
NEG_INF = -1e30
PAST_LEN = 8192
PAGE_SIZE = 128
N_MIXERS = 3

A_HEADS = 16
A_KV_HEADS = 4
A_HEAD_DIM = 64
A_GROUP = A_HEADS // A_KV_HEADS
A_Q = A_HEADS * A_HEAD_DIM
A_KV = A_KV_HEADS * A_HEAD_DIM
WINDOW = 128
A_BLOCK = 128
N_BUCKETS = 32
BUCKET_MAX_DIST = 128

B_HEADS = 16
Q_LORA = 768
KV_LORA = 256
QK_NOPE = 64
QK_ROPE = 32
V_HEAD = 64
KV_ROW = KV_LORA + QK_ROPE
ROPE_THETA = 10000.0
MLA_SCALE = (QK_NOPE + QK_ROPE) ** -0.5
B_QBLOCK = 128
B_KVBLOCK = 512
B_PAGES_PER_STEP = 64

D_INNER = 2 * D_MODEL
C_HEAD_DIM = 64
C_HEADS = D_INNER // C_HEAD_DIM
C_GROUPS = 4
C_HPG = C_HEADS // C_GROUPS
D_STATE = 128
D_CONV = 4
C_BC = C_GROUPS * D_STATE
CONV_DIM = D_INNER + 2 * C_BC
SSD_CHUNK = 128

LANES = 128
SUBLANES = 8
VMEM_LIMIT = 56 * 1024 * 1024


def _cparams(*sem):
    return pltpu.CompilerParams(dimension_semantics=sem, vmem_limit_bytes=VMEM_LIMIT)


def _resident(shape, index_map):
    return pl.BlockSpec(shape, index_map, pipeline_mode=pl.Buffered(1))


def _dot(a, b):
    return jnp.dot(a, b, preferred_element_type=F32)


def _dot_nt(a, b):
    return lax.dot_general(a, b, (((1,), (1,)), ((), ())), preferred_element_type=F32)


def _dot_tn(a, b):
    return lax.dot_general(a, b, (((0,), (0,)), ((), ())), preferred_element_type=F32)


def _dot_f32(a, b):
    return jnp.dot(a, b, preferred_element_type=F32, precision=lax.Precision.HIGHEST)


def _rms(x, g):
    r = lax.rsqrt(jnp.mean(x * x, axis=-1, keepdims=True) + EPS)
    return (x * r) * g


def _silu(x):
    return x * jax.nn.sigmoid(x)


def _softplus(x):
    return jnp.maximum(x, 0.0) + jnp.log1p(jnp.exp(-jnp.abs(x)))


def _lanes(a, n):
    if n <= LANES:
        return a[:, :n]
    return jnp.concatenate([a] * (n // LANES), axis=1)


FFN_TM = 512
FFN_TF = 256


def _ple_update(x, p, g_ref, wp_ref, wg_ref):
    gate = _dot(_rms(x, g_ref[...]).astype(BF16), wg_ref[...])
    proj = _dot(p.astype(BF16), wp_ref[...])
    return x + proj * jax.nn.sigmoid(gate)


def _tile_of_pair(n_first, first_ref, second_ref):
    step = jnp.zeros(first_ref.shape, jnp.int32) + pl.program_id(0)
    return jnp.where(step < n_first, first_ref[...], second_ref[...])


def _pair_specs(n_first, block, lead=()):
    squeezed = (None,) * len(lead)
    return [pl.BlockSpec(squeezed + block, lambda i: lead + (jnp.minimum(i, n_first - 1), 0)),
            pl.BlockSpec(squeezed + block, lambda i: lead + (jnp.maximum(i - n_first, 0), 0))]


def _ffn_body(*refs, ple, proj, x_pair, n_first):
    refs = list(refs)
    x = _tile_of_pair(n_first, refs.pop(0), refs.pop(0)) if x_pair else refs.pop(0)[...]
    g_ref, wgu_ref, wd_ref = refs[:3]
    refs = refs[3:]
    if ple:
        x = _ple_update(x, _tile_of_pair(n_first, refs[0], refs[1]), *refs[2:5])
        refs = refs[5:]
    o_ref = refs[-2] if proj else refs[-1]
    h = _rms(x, g_ref[...]).astype(BF16)
    acc = jnp.zeros(x.shape, F32)
    for j in range(D_FF // FFN_TF):
        lo = j * FFN_TF
        gate = _dot(h, wgu_ref[:, lo:lo + FFN_TF])
        up = _dot(h, wgu_ref[:, D_FF + lo:D_FF + lo + FFN_TF])
        act = (_silu(gate) * up).astype(BF16)
        acc = acc + _dot(act, wd_ref[lo:lo + FFN_TF, :])
    y = x + 0.5 * acc
    o_ref[...] = y
    if proj:
        gm_ref, wm_ref, m_ref = refs[0], refs[1], refs[-1]
        m_ref[...] = _dot(_rms(y, gm_ref[...]).astype(BF16), wm_ref[...])


def _ffn(x, norm, w_gu, w_down, layer, ple=None, proj=None, n_first=None):
    x_pair = isinstance(x, tuple)
    rows = x[0].shape[0] + x[1].shape[0] if x_pair else x.shape[0]
    n_first_tiles = None if n_first is None else n_first // FFN_TM
    if x_pair:
        in_specs = _pair_specs(n_first_tiles, (FFN_TM, D_MODEL))
        args = list(x)
    else:
        in_specs = [pl.BlockSpec((FFN_TM, D_MODEL), lambda i: (i, 0))]
        args = [x]
    in_specs += [
        _resident((None, 1, D_MODEL), lambda i: (layer, 0, 0)),
        _resident((None, D_MODEL, 2 * D_FF), lambda i: (layer, 0, 0)),
        _resident((None, D_FF, D_MODEL), lambda i: (layer, 0, 0)),
    ]
    args += [norm, w_gu, w_down]
    if ple is not None:
        pl_layer = ple[4]
        in_specs += _pair_specs(n_first_tiles, (FFN_TM, D_PLE), lead=(pl_layer,)) + [
            _resident((None, 1, D_MODEL), lambda i: (pl_layer, 0, 0)),
            _resident((None, D_PLE, D_MODEL), lambda i: (pl_layer, 0, 0)),
            _resident((None, D_MODEL, D_MODEL), lambda i: (pl_layer, 0, 0)),
        ]
        args += list(ple[0]) + list(ple[1:4])
    out_specs = pl.BlockSpec((FFN_TM, D_MODEL), lambda i: (i, 0))
    out_shape = jax.ShapeDtypeStruct((rows, D_MODEL), F32)
    if proj is not None:
        m_norm, m_layer, m_w, m_wl = proj
        n = m_w.shape[2]
        in_specs += [
            _resident((None, 1, D_MODEL), lambda i: (m_layer, 0, 0)),
            _resident((None, D_MODEL, n), lambda i: (m_wl, 0, 0)),
        ]
        args += [m_norm, m_w]
        out_specs = [out_specs, pl.BlockSpec((FFN_TM, n), lambda i: (i, 0))]
        out_shape = [out_shape, jax.ShapeDtypeStruct((rows, n), F32)]
    return pl.pallas_call(
        functools.partial(_ffn_body, ple=ple is not None, proj=proj is not None, x_pair=x_pair,
                          n_first=n_first_tiles),
        grid=(rows // FFN_TM,),
        in_specs=in_specs,
        out_specs=out_specs,
        out_shape=out_shape,
        compiler_params=_cparams("parallel"),
        name="ffn" + ("_ple" if ple is not None else "") + ("_proj" if proj is not None else ""),
    )(*args)


ROW_TM = 256


def _ple_body(x_ref, p_ref, g_ref, wp_ref, wg_ref, gf_ref, o_ref, *, final):
    y = _ple_update(x_ref[...], p_ref[...], g_ref, wp_ref, wg_ref)
    if final:
        y = _rms(y, gf_ref[...])
    o_ref[...] = y


def _ple(x, p, norm, w_proj, w_gate, final_norm, layer, final, row0):
    rows = p.shape[1]
    blk0 = row0 // ROW_TM
    return pl.pallas_call(
        functools.partial(_ple_body, final=final),
        grid=(rows // ROW_TM,),
        in_specs=[
            pl.BlockSpec((ROW_TM, D_MODEL), lambda i: (i + blk0, 0)),
            pl.BlockSpec((None, ROW_TM, D_PLE), lambda i: (layer, i, 0)),
            _resident((None, 1, D_MODEL), lambda i: (layer, 0, 0)),
            _resident((None, D_PLE, D_MODEL), lambda i: (layer, 0, 0)),
            _resident((None, D_MODEL, D_MODEL), lambda i: (layer, 0, 0)),
            _resident((1, D_MODEL), lambda i: (0, 0)),
        ],
        out_specs=pl.BlockSpec((ROW_TM, D_MODEL), lambda i: (i, 0)),
        out_shape=jax.ShapeDtypeStruct((rows, D_MODEL), F32),
        compiler_params=_cparams("parallel"),
        name="ple",
    )(x, p, norm, w_proj, w_gate, final_norm)


def _norm_proj_body(x_ref, g_ref, w_ref, o_ref):
    o_ref[...] = _dot(_rms(x_ref[...], g_ref[...]).astype(BF16), w_ref[...])


def _norm_proj(x, norm, w, layer, wl, name):
    rows = x.shape[0]
    k, n = w.shape[1:]
    return pl.pallas_call(
        _norm_proj_body,
        grid=(rows // ROW_TM,),
        in_specs=[
            pl.BlockSpec((ROW_TM, k), lambda i: (i, 0)),
            _resident((None, 1, k), lambda i: (layer, 0, 0)),
            _resident((None, k, n), lambda i: (wl, 0, 0)),
        ],
        out_specs=pl.BlockSpec((ROW_TM, n), lambda i: (i, 0)),
        out_shape=jax.ShapeDtypeStruct((rows, n), F32),
        compiler_params=_cparams("parallel"),
        name=name,
    )(x, norm, w)


def _res_proj_body(x_ref, a_ref, w_ref, o_ref):
    o_ref[...] = x_ref[...] + _dot(a_ref[...].astype(BF16), w_ref[...])


def _res_proj(x, a, w, wl, row0, name):
    rows = a.shape[0]
    k, n = w.shape[1:]
    blk0 = row0 // ROW_TM
    return pl.pallas_call(
        _res_proj_body,
        grid=(rows // ROW_TM,),
        in_specs=[
            pl.BlockSpec((ROW_TM, n), lambda i: (i + blk0, 0)),
            pl.BlockSpec((ROW_TM, k), lambda i: (i, 0)),
            _resident((None, k, n), lambda i: (wl, 0, 0)),
        ],
        out_specs=pl.BlockSpec((ROW_TM, n), lambda i: (i + blk0, 0)),
        out_shape=jax.ShapeDtypeStruct(x.shape, F32),
        input_output_aliases={0: 0},
        compiler_params=_cparams("parallel"),
        name=name,
    )(x, a, w)


def _t5_bucket(dist):
    max_exact = N_BUCKETS // 2
    d = jnp.maximum(dist, 0)
    large = max_exact + (jnp.log(jnp.maximum(d, 1).astype(F32) / max_exact)
                         / math.log(BUCKET_MAX_DIST / max_exact) * (N_BUCKETS - max_exact)).astype(jnp.int32)
    large = jnp.minimum(large, N_BUCKETS - 1)
    return jnp.where(d < max_exact, d, large)


def _bias_body(tab_ref, bm_ref, valid_ref, o_ref):
    h = pl.program_id(0)
    bm = bm_ref[...]
    acc = jnp.zeros(bm.shape, F32)
    for b in range(N_BUCKETS):
        acc = jnp.where(bm == b, tab_ref[b, h], acc)
    o_ref[...] = jnp.where(valid_ref[...] > 0, acc, NEG_INF)


def _rel_bias(table, bucket_map, valid):
    r, c = bucket_map.shape
    return pl.pallas_call(
        _bias_body,
        grid=(A_HEADS,),
        in_specs=[pl.BlockSpec(memory_space=pltpu.SMEM), pl.BlockSpec((r, c), lambda h: (0, 0)),
                  pl.BlockSpec((r, c), lambda h: (0, 0))],
        out_specs=pl.BlockSpec((None, r, c), lambda h: (h, 0, 0)),
        out_shape=jax.ShapeDtypeStruct((A_HEADS, r, c), F32),
        compiler_params=_cparams("parallel"),
        name="rel_bias",
    )(table, bucket_map, valid.astype(jnp.int32))


def _group_sinks(sinks, t):
    return jnp.broadcast_to(sinks.reshape(A_KV_HEADS, A_GROUP, 1, 1),
                            (A_KV_HEADS, A_GROUP, t, LANES)).reshape(A_KV_HEADS, A_GROUP * t, LANES)


def _swa_prompt_body(sink_ref, q_ref, kp_ref, ko_ref, vp_ref, vo_ref, bias_ref, x_ref, wo_ref, o_ref, o_scr):
    scale = A_HEAD_DIM ** -0.5
    for g in range(A_KV_HEADS):
        gs = slice(g * A_HEAD_DIM, (g + 1) * A_HEAD_DIM)
        kg = jnp.concatenate([kp_ref[:, gs], ko_ref[:, gs]], axis=0).astype(BF16)
        vg = jnp.concatenate([vp_ref[:, gs], vo_ref[:, gs]], axis=0).astype(BF16)
        for h in range(g * A_GROUP, (g + 1) * A_GROUP):
            hs = slice(h * A_HEAD_DIM, (h + 1) * A_HEAD_DIM)
            s = _dot_nt(q_ref[:, hs].astype(BF16), kg) * scale + bias_ref[h]
            sink = sink_ref[h]
            m = jnp.maximum(jnp.max(s, axis=-1, keepdims=True), sink)
            e = jnp.exp(s - m)
            den = jnp.sum(e, axis=-1, keepdims=True) + jnp.exp(sink - m)
            o_scr[:, hs] = _dot((e / den).astype(BF16), vg)
    o_ref[...] = x_ref[...] + _dot(o_scr[...].astype(BF16), wo_ref[...])


def _swa_prompt(x, qkv, bias, sinks, w_o, wl, nb_p, t_p):
    nblk = t_p // A_BLOCK
    kcol = A_Q // A_KV
    vcol = kcol + 1

    def own(b, n):
        return b * nblk + n

    def prev(b, n):
        return b * nblk + jnp.maximum(n - 1, 0)

    return pl.pallas_call(
        _swa_prompt_body,
        grid=(nb_p, nblk),
        in_specs=[
            pl.BlockSpec(memory_space=pltpu.SMEM),
            pl.BlockSpec((A_BLOCK, A_Q), lambda b, n: (own(b, n), 0)),
            pl.BlockSpec((A_BLOCK, A_KV), lambda b, n: (prev(b, n), kcol)),
            pl.BlockSpec((A_BLOCK, A_KV), lambda b, n: (own(b, n), kcol)),
            pl.BlockSpec((A_BLOCK, A_KV), lambda b, n: (prev(b, n), vcol)),
            pl.BlockSpec((A_BLOCK, A_KV), lambda b, n: (own(b, n), vcol)),
            pl.BlockSpec((None, A_HEADS, A_BLOCK, 2 * A_BLOCK), lambda b, n: (jnp.minimum(n, 1), 0, 0, 0)),
            pl.BlockSpec((A_BLOCK, D_MODEL), lambda b, n: (own(b, n), 0)),
            _resident((None, A_Q, D_MODEL), lambda b, n: (wl, 0, 0)),
        ],
        out_specs=pl.BlockSpec((A_BLOCK, D_MODEL), lambda b, n: (own(b, n), 0)),
        out_shape=jax.ShapeDtypeStruct(x.shape, F32),
        scratch_shapes=[pltpu.VMEM((A_BLOCK, A_Q), F32)],
        input_output_aliases={7: 0},
        compiler_params=_cparams("parallel", "arbitrary"),
        name="swa_prompt",
    )(sinks, qkv, qkv, qkv, qkv, qkv, bias, x, w_o)


SWA_S_NB = 8


def _swa_sample_body(qkv_ref, kt_ref, vt_ref, bb_ref, bn_ref, sink_ref, o_ref, *, l_s):
    scale = A_HEAD_DIM ** -0.5

    def one(b, carry):
        rows = pl.ds(pl.multiple_of(b * l_s, l_s), l_s)
        for g in range(A_KV_HEADS):
            heads = range(g * A_GROUP, (g + 1) * A_GROUP)
            qg = jnp.concatenate([qkv_ref[rows, h * A_HEAD_DIM:(h + 1) * A_HEAD_DIM] for h in heads], axis=0)
            qg = qg.astype(BF16)
            kn = qkv_ref[rows, A_Q + g * A_HEAD_DIM:A_Q + (g + 1) * A_HEAD_DIM].astype(BF16)
            vn = qkv_ref[rows, A_Q + A_KV + g * A_HEAD_DIM:A_Q + A_KV + (g + 1) * A_HEAD_DIM].astype(BF16)
            sb = _dot(qg, kt_ref[b, g].astype(BF16)) * scale + bb_ref[g]
            sn = _dot_nt(qg, kn) * scale + bn_ref[g]
            sink = sink_ref[g][:, :1]
            m = jnp.maximum(jnp.maximum(jnp.max(sb, axis=-1, keepdims=True), jnp.max(sn, axis=-1, keepdims=True)), sink)
            eb = jnp.exp(sb - m)
            en = jnp.exp(sn - m)
            den = jnp.sum(eb, axis=-1, keepdims=True) + jnp.sum(en, axis=-1, keepdims=True) + jnp.exp(sink - m)
            og = _dot_nt((eb / den).astype(BF16), vt_ref[b, g].astype(BF16)) + _dot((en / den).astype(BF16), vn)
            for i, h in enumerate(heads):
                o_ref[rows, h * A_HEAD_DIM:(h + 1) * A_HEAD_DIM] = og[i * l_s:(i + 1) * l_s]
        return carry

    lax.fori_loop(0, SWA_S_NB, one, 0, unroll=True)


def _swa_sample(qkv, kt_buf, vt_buf, bias, sinks, sl, row0, nb_s, l_s):
    w_buf = kt_buf.shape[-1]
    r = A_GROUP * l_s
    tm = SWA_S_NB * l_s
    blk0 = row0 // tm
    bias_g = bias.reshape(A_KV_HEADS, r, w_buf + l_s)
    sink_g = _group_sinks(sinks, l_s)
    buf_spec = pl.BlockSpec((None, SWA_S_NB, A_KV_HEADS, A_HEAD_DIM, w_buf), lambda i: (sl, i, 0, 0, 0))
    const3 = lambda i: (0, 0, 0)
    return pl.pallas_call(
        functools.partial(_swa_sample_body, l_s=l_s),
        grid=(nb_s // SWA_S_NB,),
        in_specs=[
            pl.BlockSpec((tm, A_Q + 2 * A_KV), lambda i: (i + blk0, 0)),
            buf_spec,
            buf_spec,
            _resident((A_KV_HEADS, r, w_buf), const3),
            _resident((A_KV_HEADS, r, l_s), const3),
            _resident((A_KV_HEADS, r, LANES), const3),
        ],
        out_specs=pl.BlockSpec((tm, A_Q), lambda i: (i, 0)),
        out_shape=jax.ShapeDtypeStruct((nb_s * l_s, A_Q), F32),
        compiler_params=_cparams("parallel"),
        name="swa_sample",
    )(qkv, kt_buf, vt_buf, bias_g[:, :, :w_buf], bias_g[:, :, w_buf:], sink_g)


B_IN_COLS = Q_LORA + KV_LORA + 2 * LANES
B_KPE = Q_LORA + KV_LORA
B_KPE_ROT = B_KPE + LANES


def _mla_proj_body(x_ref, g_ref, win_ref, qn_ref, kvn_ref, wnope_ref, wpe_ref, wper_ref, wuk_ref,
                   cos_ref, sin_ref, kv_ref, q_ref, *maybe_kvb_ref, sample):
    tm = x_ref.shape[0]
    h = _rms(x_ref[...], g_ref[...]).astype(BF16)
    c = _dot(h, win_ref[...])
    cos = cos_ref[...]
    sin = sin_ref[...]
    kv_lat = _rms(c[:, Q_LORA:B_KPE], kvn_ref[...])
    kv_rope = (c[:, B_KPE:B_KPE + QK_ROPE] * cos[:, :QK_ROPE]
               + c[:, B_KPE_ROT:B_KPE_ROT + QK_ROPE] * sin[:, :QK_ROPE])
    kv_ref[:, :KV_LORA] = kv_lat
    kv_ref[:, KV_LORA:] = kv_rope
    for kvb_ref in maybe_kvb_ref:
        kvb_ref[:, :KV_LORA] = kv_lat.astype(BF16)
        kvb_ref[:, KV_LORA:] = kv_rope.astype(BF16)
    qn = _rms(c[:, :Q_LORA], qn_ref[...]).astype(BF16)
    qnope = _dot(qn, wnope_ref[...])
    n_rep = B_HEADS * QK_ROPE // LANES
    qpe = (_dot(qn, wpe_ref[...]) * jnp.concatenate([cos] * n_rep, axis=1)
           + _dot(qn, wper_ref[...]) * jnp.concatenate([sin] * n_rep, axis=1))
    for hd in range(B_HEADS):
        qlat = _dot(qnope[:, hd * QK_NOPE:(hd + 1) * QK_NOPE].astype(BF16), wuk_ref[hd])
        qrope = qpe[:, hd * QK_ROPE:(hd + 1) * QK_ROPE]
        if sample:
            l_s = q_ref.shape[1] // B_HEADS
            rs = slice(hd * l_s, (hd + 1) * l_s)
            q_ref[:, rs, :KV_LORA] = qlat.reshape(tm // l_s, l_s, KV_LORA)
            q_ref[:, rs, KV_LORA:] = qrope.reshape(tm // l_s, l_s, QK_ROPE)
        else:
            for s in range(tm // B_QBLOCK):
                ts = slice(s * B_QBLOCK, (s + 1) * B_QBLOCK)
                q_ref[s, hd, :, :KV_LORA] = qlat[ts].astype(BF16)
                q_ref[s, hd, :, KV_LORA:] = qrope[ts].astype(BF16)


def _mla_proj(x, norm, layer, w, cos, sin, row0, rows, l_s):
    sample = l_s is not None
    tm = ROW_TM
    blk0 = row0 // tm
    if sample:
        q_shape = jax.ShapeDtypeStruct((rows // l_s, B_HEADS * l_s, KV_ROW), F32)
        q_spec = pl.BlockSpec((tm // l_s, B_HEADS * l_s, KV_ROW), lambda i: (i, 0, 0))
    else:
        q_shape = jax.ShapeDtypeStruct((rows // B_QBLOCK, B_HEADS, B_QBLOCK, KV_ROW), BF16)
        q_spec = pl.BlockSpec((tm // B_QBLOCK, B_HEADS, B_QBLOCK, KV_ROW), lambda i: (i, 0, 0, 0))
    const2 = lambda i: (0, 0)
    kv_spec = pl.BlockSpec((tm, KV_ROW), lambda i: (i, 0))
    out_specs = [kv_spec, q_spec]
    out_shape = [jax.ShapeDtypeStruct((rows, KV_ROW), F32), q_shape]
    if not sample:
        out_specs.append(kv_spec)
        out_shape.append(jax.ShapeDtypeStruct((rows, KV_ROW), BF16))
    return pl.pallas_call(
        functools.partial(_mla_proj_body, sample=sample),
        grid=(rows // tm,),
        in_specs=[
            pl.BlockSpec((tm, D_MODEL), lambda i: (i + blk0, 0)),
            _resident((None, 1, D_MODEL), lambda i: (layer, 0, 0)),
            _resident((D_MODEL, B_IN_COLS), const2),
            _resident((1, Q_LORA), const2),
            _resident((1, KV_LORA), const2),
            _resident((Q_LORA, B_HEADS * QK_NOPE), const2),
            _resident((Q_LORA, B_HEADS * QK_ROPE), const2),
            _resident((Q_LORA, B_HEADS * QK_ROPE), const2),
            _resident((B_HEADS, QK_NOPE, KV_LORA), lambda i: (0, 0, 0)),
            pl.BlockSpec((tm, LANES), lambda i: (i + blk0, 0)),
            pl.BlockSpec((tm, LANES), lambda i: (i + blk0, 0)),
        ],
        out_specs=out_specs,
        out_shape=out_shape,
        compiler_params=_cparams("parallel"),
        name="mla_proj_sample" if sample else "mla_proj_prompt",
    )(x, norm, w["w_in"], w["q_norm"], w["kv_norm"], w["w_nope"], w["w_pe"], w["w_pe_rot"], w["w_uk"], cos, sin)


def _softmax_stats(s, m_scr, l_scr):
    c = MLA_SCALE * math.log2(math.e)
    m_prev = m_scr[...]
    m_new = jnp.maximum(m_prev, jnp.max(s, axis=-1, keepdims=True))
    alpha = jnp.exp2((m_prev - m_new) * c)
    p = jnp.exp2((s - _lanes(m_new, s.shape[1])) * c)
    l_scr[...] = alpha * l_scr[...] + jnp.sum(p, axis=-1, keepdims=True)
    m_scr[...] = m_new
    return alpha, p.astype(BF16)


def _softmax_accumulate(s, pv, m_scr, l_scr, acc_scr):
    alpha, p = _softmax_stats(s, m_scr, l_scr)
    acc_scr[...] = acc_scr[...] * _lanes(alpha, KV_LORA) + pv(p)


def _softmax_update(q, kvb, m_scr, l_scr, acc_scr, valid=None):
    s = _dot_nt(q, kvb)
    if valid is not None:
        s = jnp.where(valid, s, NEG_INF)
    _softmax_accumulate(s, lambda p: _dot(p, kvb[:, :KV_LORA]), m_scr, l_scr, acc_scr)


def _softmax_init(m_scr, l_scr, acc_scr):
    m_scr[...] = jnp.full(m_scr.shape, NEG_INF, F32)
    l_scr[...] = jnp.zeros(l_scr.shape, F32)
    acc_scr[...] = jnp.zeros(acc_scr.shape, F32)


def _mla_prompt_body(q_ref, kv_ref, x_ref, wuv_ref, wo_ref, o_ref, s0_scr, s1_scr, m_scr, l_scr, acc_scr, u_scr):
    qi = pl.program_id(1)
    r = B_HEADS * B_QBLOCK
    n_off = qi // (B_KVBLOCK // B_QBLOCK)
    q = q_ref[...].reshape(r, KV_ROW)

    def kv_block(k):
        return kv_ref[pl.ds(pl.multiple_of(k * B_KVBLOCK, B_KVBLOCK), B_KVBLOCK), :]

    def update(s, k):
        kvb = kv_block(k)
        _softmax_accumulate(s, lambda p: _dot(p, kvb[:, :KV_LORA]), m_scr, l_scr, acc_scr)

    def step(s_cur, s_next, k):
        s_next[...] = _dot_nt(q, kv_block(k + 1))
        update(s_cur[...], k)

    def last(s_cur):
        tok = lax.broadcasted_iota(jnp.int32, (r, B_KVBLOCK), 0) & (B_QBLOCK - 1)
        key = lax.broadcasted_iota(jnp.int32, (r, B_KVBLOCK), 1)
        valid = key - tok <= qi * B_QBLOCK - n_off * B_KVBLOCK
        update(jnp.where(valid, s_cur[...], NEG_INF), n_off)
        o = acc_scr[...] * _lanes(1.0 / l_scr[...], KV_LORA)
        for hd in range(B_HEADS):
            oh = o[hd * B_QBLOCK:(hd + 1) * B_QBLOCK].astype(BF16)
            u_scr[:, hd * V_HEAD:(hd + 1) * V_HEAD] = _dot(oh, wuv_ref[hd])
        o_ref[...] = x_ref[...] + _dot(u_scr[...].astype(BF16), wo_ref[...])

    _softmax_init(m_scr, l_scr, acc_scr)
    s0_scr[...] = _dot_nt(q, kv_block(0))

    def two_steps(j, carry):
        step(s0_scr, s1_scr, 2 * j)
        step(s1_scr, s0_scr, 2 * j + 1)
        return carry

    lax.fori_loop(0, n_off // 2, two_steps, 0)
    odd = n_off % 2 == 1

    @pl.when(odd)
    def _():
        step(s0_scr, s1_scr, n_off - 1)
        last(s1_scr)

    @pl.when(jnp.logical_not(odd))
    def _():
        last(s0_scr)


def _mla_prompt(x, q_cat, kv_bf16, w, nb_p, t_p):
    nq = t_p // B_QBLOCK
    r = B_HEADS * B_QBLOCK
    return pl.pallas_call(
        _mla_prompt_body,
        grid=(nb_p, nq),
        in_specs=[
            pl.BlockSpec((None, B_HEADS, B_QBLOCK, KV_ROW), lambda b, qi: (b * nq + qi, 0, 0, 0)),
            pl.BlockSpec((t_p, KV_ROW), lambda b, qi: (b, 0)),
            pl.BlockSpec((B_QBLOCK, D_MODEL), lambda b, qi: (b * nq + qi, 0)),
            _resident((B_HEADS, KV_LORA, V_HEAD), lambda b, qi: (0, 0, 0)),
            _resident((B_HEADS * V_HEAD, D_MODEL), lambda b, qi: (0, 0)),
        ],
        out_specs=pl.BlockSpec((B_QBLOCK, D_MODEL), lambda b, qi: (b * nq + qi, 0)),
        out_shape=jax.ShapeDtypeStruct(x.shape, F32),
        scratch_shapes=[
            pltpu.VMEM((r, B_KVBLOCK), F32),
            pltpu.VMEM((r, B_KVBLOCK), F32),
            pltpu.VMEM((r, LANES), F32),
            pltpu.VMEM((r, LANES), F32),
            pltpu.VMEM((r, KV_LORA), F32),
            pltpu.VMEM((B_QBLOCK, B_HEADS * V_HEAD), F32),
        ],
        input_output_aliases={2: 0},
        compiler_params=_cparams("parallel", "arbitrary"),
        name="mla_prompt",
    )(q_cat, kv_bf16, x, w["w_uv"], w["w_o"])


def _mla_sample_body(pt_ref, q_ref, cache_ref, kvn_ref, o_ref, buf, sem, m_scr, l_scr, acc_scr, acct_scr,
                     *, l_s, n_pages, cl):
    b = pl.program_id(0)
    slot = b % 2
    r = q_ref.shape[0]
    chunk = math.gcd(n_pages, B_PAGES_PER_STEP)
    assert r == LANES

    def page_copy(batch, slot_, k):
        page = pt_ref[batch * n_pages + k]
        return pltpu.make_async_copy(cache_ref.at[cl, page], buf.at[slot_, k], sem.at[slot_, k])

    def start_pages(batch, slot_):
        for k in range(n_pages):
            page_copy(batch, slot_, k).start(priority=k % 2)

    @pl.when(b == 0)
    def _():
        start_pages(0, 0)

    @pl.when(b + 1 < pl.num_programs(0))
    def _():
        start_pages(b + 1, 1 - slot)

    _softmax_init(m_scr, l_scr, acct_scr)
    q = q_ref[...].astype(BF16)
    for c in range(n_pages // chunk):
        ks = range(c * chunk, (c + 1) * chunk)
        for k in ks:
            page_copy(b, slot, k).wait()
        kt = jnp.concatenate([buf[slot, k].astype(BF16) for k in ks], axis=1)
        alpha, p = _softmax_stats(_dot(q, kt), m_scr, l_scr)
        alpha_t = jnp.concatenate([alpha.T] * (KV_LORA // LANES), axis=0)
        acct_scr[...] = acct_scr[...] * alpha_t + _dot_nt(kt[:KV_LORA, :], p)

    acc_scr[...] = acct_scr[...].T
    tok = lax.broadcasted_iota(jnp.int32, (r, l_s), 0) % l_s
    key = lax.broadcasted_iota(jnp.int32, (r, l_s), 1)
    _softmax_update(q, kvn_ref[...].astype(BF16), m_scr, l_scr, acc_scr, valid=key <= tok)
    o_ref[...] = acc_scr[...] * _lanes(1.0 / l_scr[...], KV_LORA)


def _mla_sample(q_cat, kv_rows, cache_t, page_table, cl, row0, nb_s, l_s):
    n_pages = page_table.shape[1]
    r = B_HEADS * l_s
    blk0 = row0 // l_s
    return pl.pallas_call(
        functools.partial(_mla_sample_body, l_s=l_s, n_pages=n_pages, cl=cl),
        grid_spec=pltpu.PrefetchScalarGridSpec(
            num_scalar_prefetch=1,
            grid=(nb_s,),
            in_specs=[
                pl.BlockSpec((None, r, KV_ROW), lambda b, pt: (b, 0, 0)),
                pl.BlockSpec(memory_space=pl.ANY),
                pl.BlockSpec((l_s, KV_ROW), lambda b, pt: (b + blk0, 0)),
            ],
            out_specs=pl.BlockSpec((None, r, KV_LORA), lambda b, pt: (b, 0, 0)),
            scratch_shapes=[
                pltpu.VMEM((2, n_pages, KV_ROW, PAGE_SIZE), F32),
                pltpu.SemaphoreType.DMA((2, n_pages)),
                pltpu.VMEM((r, LANES), F32),
                pltpu.VMEM((r, LANES), F32),
                pltpu.VMEM((r, KV_LORA), F32),
                pltpu.VMEM((KV_LORA, r), F32),
            ],
        ),
        out_shape=jax.ShapeDtypeStruct((nb_s, r, KV_LORA), F32),
        compiler_params=_cparams("arbitrary"),
        name="mla_sample",
    )(page_table.reshape(-1), q_cat, cache_t, kv_rows)


def _mla_out_sample_body(o_ref, x_ref, wuv_ref, wo_ref, out_ref, u_scr, *, l_s):
    nb = o_ref.shape[0]
    for hd in range(B_HEADS):
        oh = o_ref[:, hd * l_s:(hd + 1) * l_s, :].reshape(nb * l_s, KV_LORA).astype(BF16)
        u_scr[:, hd * V_HEAD:(hd + 1) * V_HEAD] = _dot(oh, wuv_ref[hd])
    out_ref[...] = x_ref[...] + _dot(u_scr[...].astype(BF16), wo_ref[...])


def _mla_out_sample(x, o_lat, w, row0, l_s):
    nb_s = o_lat.shape[0]
    nb = ROW_TM // l_s
    blk0 = row0 // ROW_TM
    return pl.pallas_call(
        functools.partial(_mla_out_sample_body, l_s=l_s),
        grid=(nb_s // nb,),
        in_specs=[
            pl.BlockSpec((nb, B_HEADS * l_s, KV_LORA), lambda i: (i, 0, 0)),
            pl.BlockSpec((ROW_TM, D_MODEL), lambda i: (i + blk0, 0)),
            _resident((B_HEADS, KV_LORA, V_HEAD), lambda i: (0, 0, 0)),
            _resident((B_HEADS * V_HEAD, D_MODEL), lambda i: (0, 0)),
        ],
        out_specs=pl.BlockSpec((ROW_TM, D_MODEL), lambda i: (i + blk0, 0)),
        out_shape=jax.ShapeDtypeStruct(x.shape, F32),
        scratch_shapes=[pltpu.VMEM((ROW_TM, B_HEADS * V_HEAD), F32)],
        input_output_aliases={1: 0},
        compiler_params=_cparams("parallel"),
        name="mla_out_sample",
    )(o_lat, x, w["w_uv"], w["w_o"])


C_IN_COLS = D_INNER + CONV_DIM + LANES


def _mamba_in_body(x_ref, g_ref, w_ref, z_ref, xbc_ref, dt_ref):
    h = _rms(x_ref[...], g_ref[...]).astype(BF16)
    z_ref[...] = _dot(h, w_ref[:, :D_INNER])
    xbc_ref[...] = _dot(h, w_ref[:, D_INNER:D_INNER + CONV_DIM])
    dt_ref[...] = _dot(h, w_ref[:, D_INNER + CONV_DIM:])


def _mamba_in(x, norm, layer, w_in):
    rows = x.shape[0]
    return pl.pallas_call(
        _mamba_in_body,
        grid=(rows // ROW_TM,),
        in_specs=[
            pl.BlockSpec((ROW_TM, D_MODEL), lambda i: (i, 0)),
            _resident((None, 1, D_MODEL), lambda i: (layer, 0, 0)),
            _resident((D_MODEL, C_IN_COLS), lambda i: (0, 0)),
        ],
        out_specs=[
            pl.BlockSpec((ROW_TM, D_INNER), lambda i: (i, 0)),
            pl.BlockSpec((ROW_TM, CONV_DIM), lambda i: (i, 0)),
            pl.BlockSpec((ROW_TM, LANES), lambda i: (i, 0)),
        ],
        out_shape=[
            jax.ShapeDtypeStruct((rows, D_INNER), F32),
            jax.ShapeDtypeStruct((rows, CONV_DIM), F32),
            jax.ShapeDtypeStruct((rows, LANES), F32),
        ],
        compiler_params=_cparams("parallel"),
        name="mamba_in",
    )(x, norm, w_in)


def _causal_conv(ext_scr, u, cw_ref, cb_ref):
    t = u.shape[0]
    ext_scr[SUBLANES:SUBLANES + t, :] = u
    acc = cb_ref[...] + cw_ref[D_CONV - 1:D_CONV, :] * u
    for k in range(D_CONV - 1):
        lo = SUBLANES - (D_CONV - 1) + k
        acc = acc + cw_ref[k:k + 1, :] * ext_scr[lo:lo + t, :]
    return acc


def _gated_norm(y, z, norm_g):
    yz = y * _silu(z)
    gw = D_INNER // C_GROUPS
    parts = []
    for g in range(C_GROUPS):
        part = yz[:, g * gw:(g + 1) * gw]
        parts.append(part * lax.rsqrt(jnp.mean(part * part, axis=-1, keepdims=True) + EPS))
    return jnp.concatenate(parts, axis=1) * norm_g


def _ssd_prompt_body(xbc_ref, dt_ref, z_ref, cw_ref, cb_ref, dtb_ref, alog_ref, dskip_ref, ng_ref, yn_ref, st_ref,
                     ext_scr, ht_scr, y_ref):
    c = pl.program_id(1)
    t = SSD_CHUNK

    @pl.when(c == 0)
    def _():
        ext_scr[0:SUBLANES, :] = jnp.zeros((SUBLANES, CONV_DIM), F32)
        ht_scr[...] = jnp.zeros(ht_scr.shape, F32)

    u = xbc_ref[...]
    act = _silu(_causal_conv(ext_scr, u, cw_ref, cb_ref))
    ext_scr[0:SUBLANES, :] = u[t - SUBLANES:, :]

    dt = _softplus(dt_ref[...] + dtb_ref[...])
    a = -jnp.exp(alog_ref[...])
    row = lax.broadcasted_iota(jnp.int32, (t, t), 0)
    col = lax.broadcasted_iota(jnp.int32, (t, t), 1)
    causal = col <= row
    acs = _dot_f32(causal.astype(F32), dt * a)
    acs_t = acs.T
    dt_t = dt.T
    to_end = jnp.exp(acs[t - 1:t, :] - acs) * dt
    gw = C_HPG * C_HEAD_DIM
    heads_per_vreg = LANES // C_HEAD_DIM
    assert heads_per_vreg == 2 and t == LANES
    first = lax.broadcasted_iota(jnp.int32, (t, LANES), 1) < C_HEAD_DIM
    for g in range(C_GROUPS):
        bg = act[:, D_INNER + g * D_STATE:D_INNER + (g + 1) * D_STATE]
        cg = act[:, D_INNER + C_BC + g * D_STATE:D_INNER + C_BC + (g + 1) * D_STATE].astype(BF16)
        cb = _dot_nt(cg, bg.astype(BF16))
        bg_t = bg.T.astype(BF16)
        y_off = _dot(cg, ht_scr[:, g * gw:(g + 1) * gw].astype(BF16))
        for jp in range(C_HPG // heads_per_vreg):
            h0 = g * C_HPG + heads_per_vreg * jp
            ps = slice(h0 * C_HEAD_DIM, h0 * C_HEAD_DIM + LANES)
            ms, a_cols, te_cols = [], [], []
            for h in (h0, h0 + 1):
                a_col = jnp.broadcast_to(acs[:, h:h + 1], (t, t))
                a_row = jnp.broadcast_to(acs_t[h:h + 1, :], (t, t))
                dt_row = jnp.broadcast_to(dt_t[h:h + 1, :], (t, t))
                ms.append(cb * jnp.exp(jnp.where(causal, a_col - a_row, -jnp.inf)) * dt_row)
                a_cols.append(a_col)
                te_cols.append(jnp.broadcast_to(to_end[:, h:h + 1], (t, LANES)))
            x2 = act[:, ps]
            lhs = jnp.concatenate(ms, axis=1).astype(BF16)
            rhs = jnp.concatenate([jnp.where(first, x2, 0.0), jnp.where(first, 0.0, x2)], axis=0).astype(BF16)
            a_col2 = jnp.where(first, a_cols[0], a_cols[1])
            y_ref[:, ps] = (_dot(lhs, rhs)
                            + y_off[:, jp * LANES:(jp + 1) * LANES] * jnp.exp(a_col2)
                            + dskip_ref[:, ps] * x2)
            xw = (x2 * jnp.where(first, te_cols[0], te_cols[1])).astype(BF16)
            ht_scr[:, ps] = jnp.exp(a_col2[t - 1:t, :]) * ht_scr[:, ps] + _dot(bg_t, xw)

    yn_ref[...] = _gated_norm(y_ref[...], z_ref[...], ng_ref[...]).astype(BF16)

    @pl.when(c == pl.num_programs(1) - 1)
    def _():
        st_ref[...] = ht_scr[...].T


def _ssd_prompt(xbc, dt, z, w, nb_p, t_p):
    nc = t_p // SSD_CHUNK
    const2 = lambda b, c: (0, 0)
    rows = lambda b, c: (b * nc + c, 0)
    return pl.pallas_call(
        _ssd_prompt_body,
        grid=(nb_p, nc),
        in_specs=[
            pl.BlockSpec((SSD_CHUNK, CONV_DIM), rows),
            pl.BlockSpec((SSD_CHUNK, LANES), rows),
            pl.BlockSpec((SSD_CHUNK, D_INNER), rows),
            _resident((D_CONV, CONV_DIM), const2),
            _resident((1, CONV_DIM), const2),
            _resident((1, LANES), const2),
            _resident((1, LANES), const2),
            _resident((1, D_INNER), const2),
            _resident((1, D_INNER), const2),
        ],
        out_specs=[
            pl.BlockSpec((SSD_CHUNK, D_INNER), rows),
            pl.BlockSpec((None, D_INNER, D_STATE), lambda b, c: (b, 0, 0)),
        ],
        out_shape=[
            jax.ShapeDtypeStruct((nb_p * t_p, D_INNER), BF16),
            jax.ShapeDtypeStruct((nb_p, D_INNER, D_STATE), F32),
        ],
        scratch_shapes=[
            pltpu.VMEM((SUBLANES + SSD_CHUNK, CONV_DIM), F32),
            pltpu.VMEM((D_STATE, D_INNER), F32),
            pltpu.VMEM((SSD_CHUNK, D_INNER), F32),
        ],
        compiler_params=_cparams("parallel", "arbitrary"),
        name="ssd_prompt",
    )(xbc, dt, z, w["conv_w"], w["conv_b"], w["dt_bias"], w["a_log"], w["d_skip"], w["norm_g"])


def _ssd_sample_body(xbc_ref, dtx_ref, z_ref, cprev_ref, h0_ref, cw_ref, cb_ref, dtb_ref, alog_ref, dskip_ref, ng_ref,
                     y_ref, st_ref, ext_scr, *, l_s):
    u = xbc_ref[...]
    ext_scr[SUBLANES - (D_CONV - 1):SUBLANES, :] = cprev_ref[...]
    act = _silu(_causal_conv(ext_scr, u, cw_ref, cb_ref))
    xs = act[:, :D_INNER]

    dt = _softplus(dtx_ref[...] + dtb_ref[...])
    a = -jnp.exp(alog_ref[...])
    row = lax.broadcasted_iota(jnp.int32, (l_s, l_s), 0)
    col = lax.broadcasted_iota(jnp.int32, (l_s, l_s), 1)
    acs = _dot_f32((col <= row).astype(F32), dt * a)
    last = acs[l_s - 1:l_s, :]
    xdt = xs * dt
    xw = xdt * jnp.exp(last - acs)
    decay_col = jnp.broadcast_to(jnp.exp(last), (LANES, D_INNER)).T
    gw = C_HPG * C_HEAD_DIM
    lrow = lax.broadcasted_iota(jnp.int32, (l_s, gw), 0)
    for g in range(C_GROUPS):
        gs = slice(g * gw, (g + 1) * gw)
        bg = act[:, D_INNER + g * D_STATE:D_INNER + (g + 1) * D_STATE]
        cg = act[:, D_INNER + C_BC + g * D_STATE:D_INNER + C_BC + (g + 1) * D_STATE]
        cb = _dot_nt(cg.astype(BF16), bg.astype(BF16))
        h0 = h0_ref[gs, :]
        y = _dot_nt(cg.astype(BF16), h0.astype(BF16)) * jnp.exp(acs[:, gs])
        for s in range(l_s):
            seg = acs[:, gs] - acs[s:s + 1, gs]
            w_s = jnp.exp(jnp.where(lrow >= s, seg, -jnp.inf)) * jnp.broadcast_to(cb[:, s:s + 1], (l_s, gw))
            y = y + w_s * xdt[s:s + 1, gs]
        y_ref[:, gs] = y + dskip_ref[:, gs] * xs[:, gs]
        st_ref[gs, :] = decay_col[gs, :] * h0 + _dot_tn(xw[:, gs].astype(BF16), bg.astype(BF16))
    y_ref[...] = _gated_norm(y_ref[...], z_ref[...], ng_ref[...])


def _ssd_sample(xbc, dtx, z, conv_prev, h0, sl, w, row0, nb_s, l_s):
    blk0 = row0 // l_s
    const2 = lambda b: (0, 0)
    return pl.pallas_call(
        functools.partial(_ssd_sample_body, l_s=l_s),
        grid=(nb_s,),
        in_specs=[
            pl.BlockSpec((l_s, CONV_DIM), lambda b: (b + blk0, 0)),
            pl.BlockSpec((l_s, D_INNER), lambda b: (b, 0)),
            pl.BlockSpec((l_s, D_INNER), lambda b: (b + blk0, 0)),
            pl.BlockSpec((None, None, D_CONV - 1, CONV_DIM), lambda b: (sl, b, 0, 0)),
            pl.BlockSpec((None, None, D_INNER, D_STATE), lambda b: (sl, b, 0, 0)),
            _resident((D_CONV, CONV_DIM), const2),
            _resident((1, CONV_DIM), const2),
            _resident((1, D_INNER), const2),
            _resident((1, D_INNER), const2),
            _resident((1, D_INNER), const2),
            _resident((1, D_INNER), const2),
        ],
        out_specs=[
            pl.BlockSpec((l_s, D_INNER), lambda b: (b, 0)),
            pl.BlockSpec((None, D_INNER, D_STATE), lambda b: (b, 0, 0)),
        ],
        out_shape=[
            jax.ShapeDtypeStruct((nb_s * l_s, D_INNER), F32),
            jax.ShapeDtypeStruct((nb_s, D_INNER, D_STATE), F32),
        ],
        scratch_shapes=[pltpu.VMEM((SUBLANES + l_s, CONV_DIM), F32)],
        compiler_params=_cparams("parallel"),
        name="ssd_sample",
    )(xbc, dtx, z, conv_prev, h0, w["conv_w"], w["conv_b"], w["dt_bias_x"], w["a_log_x"], w["d_skip"], w["norm_g"])


def _rope_tables(pos):
    half = QK_ROPE // 2
    inv = jnp.power(ROPE_THETA, -jnp.arange(half, dtype=F32) / half)
    ang = pos.astype(F32)[:, None] * inv
    reps = LANES // half
    return jnp.tile(jnp.cos(ang), (1, reps)), jnp.tile(jnp.sin(ang), (1, reps))


def _rot_cols(w):
    half = w.shape[-1] // 2
    return jnp.concatenate([-w[..., half:], w[..., :half]], axis=-1)


def _mla_weights(w_in, q_norm, w_uq, kv_norm, w_ukv, w_o):
    kpe = w_in[:, B_KPE:]
    pad = jnp.zeros((D_MODEL, LANES - QK_ROPE), w_in.dtype)
    w_in_aug = jnp.concatenate([w_in[:, :B_KPE], kpe, pad, _rot_cols(kpe), pad], axis=1)
    uq = w_uq.reshape(Q_LORA, B_HEADS, QK_NOPE + QK_ROPE)
    pe = uq[:, :, QK_NOPE:]
    ukv = w_ukv.reshape(KV_LORA, B_HEADS, QK_NOPE + V_HEAD)
    return {
        "w_in": w_in_aug.astype(BF16),
        "q_norm": q_norm.reshape(1, Q_LORA),
        "kv_norm": kv_norm.reshape(1, KV_LORA),
        "w_nope": uq[:, :, :QK_NOPE].reshape(Q_LORA, B_HEADS * QK_NOPE).astype(BF16),
        "w_pe": pe.reshape(Q_LORA, B_HEADS * QK_ROPE).astype(BF16),
        "w_pe_rot": _rot_cols(pe).reshape(Q_LORA, B_HEADS * QK_ROPE).astype(BF16),
        "w_uk": jnp.transpose(ukv[:, :, :QK_NOPE], (1, 2, 0)).astype(BF16),
        "w_uv": jnp.transpose(ukv[:, :, QK_NOPE:], (1, 0, 2)).astype(BF16),
        "w_o": w_o.astype(BF16),
    }


def _pad_lanes(v):
    return jnp.pad(v, (0, LANES - v.shape[0])).reshape(1, LANES)


def _expand_heads(v):
    return jnp.repeat(v, C_HEAD_DIM).reshape(1, D_INNER)


def _mamba_weights(w_in, conv_w, conv_b, dt_bias, a_log, d_skip, norm_g, w_o):
    w_dt = w_in[:, D_INNER + CONV_DIM:]
    pad = jnp.zeros((D_MODEL, LANES - C_HEADS), w_in.dtype)
    return {
        "w_in": jnp.concatenate([w_in, pad], axis=1).astype(BF16),
        "w_dt_x": jnp.repeat(w_dt, C_HEAD_DIM, axis=1).astype(BF16)[None],
        "conv_w": conv_w,
        "conv_b": conv_b.reshape(1, CONV_DIM),
        "dt_bias": _pad_lanes(dt_bias.astype(F32)),
        "a_log": _pad_lanes(a_log.astype(F32)),
        "dt_bias_x": _expand_heads(dt_bias.astype(F32)),
        "a_log_x": _expand_heads(a_log.astype(F32)),
        "d_skip": _expand_heads(d_skip),
        "norm_g": norm_g.reshape(1, D_INNER),
        "w_o": w_o.astype(BF16),
    }


def kernel(x_prompt, x_sample, state_swa_k, state_swa_v, cache_mla, state_ssm, state_conv, page_table, p_prompt, p_sample, norm_ffn1, ffn1_w_gu, ffn1_w_down, norm_mix, norm_ffn2, ffn2_w_gu, ffn2_w_down, norm_ple, ple_w_proj, ple_w_gate, rel_bias_table, a_w_qkv, a_w_o, a_sinks, b_w_in, b_q_norm, b_w_uq, b_kv_norm, b_w_ukv, b_w_o, c_w_in, c_conv_w, c_conv_b, c_dt_bias, c_a_log, c_d, c_norm, c_w_o, final_norm):
    nb_p, t_p, _ = x_prompt.shape
    nb_s, l_s, _ = x_sample.shape
    depth = norm_ffn1.shape[0]
    rows_p = nb_p * t_p
    rows_s = nb_s * l_s
    w_buf = state_swa_k.shape[2]
    assert rows_p % FFN_TM == 0 and (rows_p + rows_s) % FFN_TM == 0 and rows_s % ROW_TM == 0
    assert t_p % B_KVBLOCK == 0 and t_p % SSD_CHUNK == 0 and t_p % A_BLOCK == 0
    assert l_s == SUBLANES and w_buf == WINDOW

    x = (x_prompt.reshape(rows_p, D_MODEL), x_sample.reshape(rows_s, D_MODEL))
    p_parts = (p_prompt.reshape(depth, rows_p, D_PLE), p_sample.reshape(depth, rows_s, D_PLE))

    def norm3(g):
        return g.reshape(g.shape[0], 1, g.shape[1])

    norm_ffn1, norm_mix, norm_ffn2, norm_ple = map(norm3, (norm_ffn1, norm_mix, norm_ffn2, norm_ple))
    ffn1_w_gu, ffn1_w_down, ffn2_w_gu, ffn2_w_down, ple_w_proj, ple_w_gate, a_w_qkv, a_w_o = (
        w.astype(BF16) for w in (ffn1_w_gu, ffn1_w_down, ffn2_w_gu, ffn2_w_down, ple_w_proj, ple_w_gate, a_w_qkv, a_w_o))
    final_norm = final_norm.reshape(1, D_MODEL)

    i = jnp.arange(A_BLOCK)[:, None]
    j = jnp.arange(2 * A_BLOCK)[None, :]
    dist = A_BLOCK + i - j
    in_window = (dist >= 0) & (dist < WINDOW)
    bias_p = jnp.stack([_rel_bias(rel_bias_table, _t5_bucket(dist), in_window & (j >= A_BLOCK)),
                        _rel_bias(rel_bias_table, _t5_bucket(dist), in_window)])
    q_pos = PAST_LEN + jnp.arange(l_s)
    k_pos = jnp.concatenate([PAST_LEN - w_buf + jnp.arange(w_buf), q_pos])
    dist = q_pos[:, None] - k_pos[None, :]
    bias_s = _rel_bias(rel_bias_table, _t5_bucket(dist), (dist >= 0) & (dist < WINDOW))
    swa_k_buf = jnp.transpose(state_swa_k, (0, 1, 3, 4, 2))
    swa_v_buf = jnp.transpose(state_swa_v, (0, 1, 3, 4, 2))

    pos = jnp.concatenate([jnp.tile(jnp.arange(t_p), nb_p), jnp.tile(PAST_LEN + jnp.arange(l_s), nb_s)])
    cos, sin = _rope_tables(pos)

    ssm_flat = state_ssm.reshape(state_ssm.shape[0], nb_s, D_INNER, D_STATE)
    cache_t = jnp.swapaxes(cache_mla, 2, 3)

    def last_rows(a, n):
        return jnp.stack([a[(b + 1) * t_p - n:(b + 1) * t_p] for b in range(nb_p)])

    swa_k_p, swa_v_p, mla_p, ssm_p, conv_p = [], [], [], [], []
    swa_k_s, swa_v_s, mla_s, ssm_s, conv_s = [], [], [], [], []
    for layer in range(depth):
        kind, sl = layer % N_MIXERS, layer // N_MIXERS
        prev_ple = None if layer == 0 else (p_parts, norm_ple, ple_w_proj, ple_w_gate, layer - 1)
        if kind == 0:
            x, qkv = _ffn(x, norm_ffn1, ffn1_w_gu, ffn1_w_down, layer, ple=prev_ple,
                          proj=(norm_mix, layer, a_w_qkv, sl), n_first=rows_p)
        else:
            x = _ffn(x, norm_ffn1, ffn1_w_gu, ffn1_w_down, layer, ple=prev_ple, n_first=rows_p)
        if kind == 0:
            x = _swa_prompt(x, qkv, bias_p, a_sinks[sl], a_w_o, sl, nb_p, t_p)
            o_s = _swa_sample(qkv, swa_k_buf, swa_v_buf, bias_s, a_sinks[sl], sl, rows_p, nb_s, l_s)
            x = _res_proj(x, o_s, a_w_o, sl, rows_p, "swa_out_sample")
            kv_p = last_rows(qkv, w_buf)[:, :, A_Q:].reshape(nb_p, w_buf, 2, A_KV_HEADS, A_HEAD_DIM)
            kv_s = qkv[rows_p:, A_Q:].reshape(nb_s, l_s, 2, A_KV_HEADS, A_HEAD_DIM)
            swa_k_p.append(kv_p[:, :, 0])
            swa_v_p.append(kv_p[:, :, 1])
            swa_k_s.append(jnp.concatenate([state_swa_k[sl], kv_s[:, :, 0]], axis=1)[:, l_s:])
            swa_v_s.append(jnp.concatenate([state_swa_v[sl], kv_s[:, :, 1]], axis=1)[:, l_s:])
        elif kind == 1:
            w = _mla_weights(b_w_in[sl], b_q_norm[sl], b_w_uq[sl], b_kv_norm[sl], b_w_ukv[sl], b_w_o[sl])
            kv_rows_p, q_p, kv_bf16 = _mla_proj(x, norm_mix, layer, w, cos, sin, 0, rows_p, None)
            kv_rows_s, q_s = _mla_proj(x, norm_mix, layer, w, cos, sin, rows_p, rows_s, l_s)
            x = _mla_prompt(x, q_p, kv_bf16, w, nb_p, t_p)
            o_lat = _mla_sample(q_s, kv_rows_s, cache_t, page_table, sl, 0, nb_s, l_s)
            x = _mla_out_sample(x, o_lat, w, rows_p, l_s)
            mla_p.append(kv_rows_p.reshape(nb_p, t_p, KV_ROW))
            mla_s.append(kv_rows_s.reshape(nb_s, l_s, KV_ROW))
        else:
            w = _mamba_weights(c_w_in[sl], c_conv_w[sl], c_conv_b[sl], c_dt_bias[sl], c_a_log[sl], c_d[sl],
                               c_norm[sl], c_w_o[sl])
            z, xbc, dt = _mamba_in(x, norm_mix, layer, w["w_in"])
            dtx = _norm_proj(x[rows_p:], norm_mix, w["w_dt_x"], layer, 0, "mamba_dt_sample")
            y_p, st_p = _ssd_prompt(xbc, dt, z, w, nb_p, t_p)
            y_s, st_s = _ssd_sample(xbc, dtx, z, state_conv, ssm_flat, sl, w, rows_p, nb_s, l_s)
            x = _res_proj(x, y_p, w["w_o"][None], 0, 0, "mamba_out_prompt")
            x = _res_proj(x, y_s, w["w_o"][None], 0, rows_p, "mamba_out_sample")
            ssm_p.append(st_p.reshape(nb_p, C_HEADS, C_HEAD_DIM, D_STATE))
            ssm_s.append(st_s.reshape(nb_s, C_HEADS, C_HEAD_DIM, D_STATE))
            conv_p.append(last_rows(xbc, D_CONV - 1))
            conv_s.append(jnp.concatenate([state_conv[sl], xbc[rows_p:].reshape(nb_s, l_s, CONV_DIM)],
                                          axis=1)[:, -(D_CONV - 1):])
        x = _ffn(x, norm_ffn2, ffn2_w_gu, ffn2_w_down, layer)
    y_p = _ple(x, p_parts[0], norm_ple, ple_w_proj, ple_w_gate, final_norm, depth - 1, True, 0)
    y_s = _ple(x, p_parts[1], norm_ple, ple_w_proj, ple_w_gate, final_norm, depth - 1, True, rows_p)

    return (y_p.reshape(nb_p, t_p, D_MODEL), y_s.reshape(nb_s, l_s, D_MODEL),
            jnp.stack(swa_k_p), jnp.stack(swa_v_p), jnp.stack(mla_p), jnp.stack(ssm_p), jnp.stack(conv_p),
            jnp.stack(swa_k_s), jnp.stack(swa_v_s), jnp.stack(mla_s), jnp.stack(ssm_s), jnp.stack(conv_s))
```

```python
import functools
import math

import jax
import jax.numpy as jnp
from jax import lax
from jax.experimental import pallas as pl
from jax.experimental.pallas import tpu as pltpu

F32 = jnp.float32
BF16 = jnp.bfloat16

D_MODEL = 1024
D_FF = 2816
D_PLE = 256
EPS = 1e-6
NEG_INF = -1e30
PAST_LEN = 8192
PAGE_SIZE = 128
N_MIXERS = 3

A_HEADS = 16
A_KV_HEADS = 4
A_HEAD_DIM = 64
A_GROUP = A_HEADS // A_KV_HEADS
A_Q = A_HEADS * A_HEAD_DIM
A_KV = A_KV_HEADS * A_HEAD_DIM
WINDOW = 128
A_BLOCK = 128
N_BUCKETS = 32
BUCKET_MAX_DIST = 128

B_HEADS = 16
Q_LORA = 768
KV_LORA = 256
QK_NOPE = 64
QK_ROPE = 32
V_HEAD = 64
KV_ROW = KV_LORA + QK_ROPE
ROPE_THETA = 10000.0
MLA_SCALE = (QK_NOPE + QK_ROPE) ** -0.5
B_QBLOCK = 128
B_KVBLOCK = 512
B_PAGES_PER_STEP = 64

D_INNER = 2 * D_MODEL
C_HEAD_DIM = 64
C_HEADS = D_INNER // C_HEAD_DIM
C_GROUPS = 4
C_HPG = C_HEADS // C_GROUPS
D_STATE = 128
D_CONV = 4
C_BC = C_GROUPS * D_STATE
CONV_DIM = D_INNER + 2 * C_BC
SSD_CHUNK = 128

LANES = 128
SUBLANES = 8
VMEM_LIMIT = 56 * 1024 * 1024


def _cparams(*sem):
    return pltpu.CompilerParams(dimension_semantics=sem, vmem_limit_bytes=VMEM_LIMIT)


def _resident(shape, index_map):
    return pl.BlockSpec(shape, index_map, pipeline_mode=pl.Buffered(1))


def _dot(a, b):
    return jnp.dot(a, b, preferred_element_type=F32)


def _dot_nt(a, b):
    return lax.dot_general(a, b, (((1,), (1,)), ((), ())), preferred_element_type=F32)


def _dot_tn(a, b):
    return lax.dot_general(a, b, (((0,), (0,)), ((), ())), preferred_element_type=F32)


def _dot_f32(a, b):
    return jnp.dot(a, b, preferred_element_type=F32, precision=lax.Precision.HIGHEST)


def _rms(x, g):
    r = lax.rsqrt(jnp.mean(x * x, axis=-1, keepdims=True) + EPS)
    return (x * r) * g


def _silu(x):
    return x * jax.nn.sigmoid(x)


def _softplus(x):
    return jnp.maximum(x, 0.0) + jnp.log1p(jnp.exp(-jnp.abs(x)))


def _lanes(a, n):
    if n <= LANES:
        return a[:, :n]
    return jnp.concatenate([a] * (n // LANES), axis=1)


FFN_TM = 512
FFN_TF = 256


def _ple_update(x, p, g_ref, wp_ref, wg_ref):
    gate = _dot(_rms(x, g_ref[...]).astype(BF16), wg_ref[...])
    proj = _dot(p.astype(BF16), wp_ref[...])
    return x + proj * jax.nn.sigmoid(gate)


def _tile_of_pair(n_first, first_ref, second_ref):
    step = jnp.zeros(first_ref.shape, jnp.int32) + pl.program_id(0)
    return jnp.where(step < n_first, first_ref[...], second_ref[...])


def _pair_specs(n_first, block, lead=()):
    squeezed = (None,) * len(lead)
    return [pl.BlockSpec(squeezed + block, lambda i: lead + (jnp.minimum(i, n_first - 1), 0)),
            pl.BlockSpec(squeezed + block, lambda i: lead + (jnp.maximum(i - n_first, 0), 0))]


def _ffn_body(*refs, ple, proj, x_pair, n_first):
    refs = list(refs)
    x = _tile_of_pair(n_first, refs.pop(0), refs.pop(0)) if x_pair else refs.pop(0)[...]
    g_ref, wgu_ref, wd_ref = refs[:3]
    refs = refs[3:]
    if ple:
        x = _ple_update(x, _tile_of_pair(n_first, refs[0], refs[1]), *refs[2:5])
        refs = refs[5:]
    o_ref = refs[-2] if proj else refs[-1]
    h = _rms(x, g_ref[...]).astype(BF16)
    acc = jnp.zeros(x.shape, F32)
    for j in range(D_FF // FFN_TF):
        lo = j * FFN_TF
        gate = _dot(h, wgu_ref[:, lo:lo + FFN_TF])
        up = _dot(h, wgu_ref[:, D_FF + lo:D_FF + lo + FFN_TF])
        act = (_silu(gate) * up).astype(BF16)
        acc = acc + _dot(act, wd_ref[lo:lo + FFN_TF, :])
    y = x + 0.5 * acc
    o_ref[...] = y
    if proj:
        gm_ref, wm_ref, m_ref = refs[0], refs[1], refs[-1]
        m_ref[...] = _dot(_rms(y, gm_ref[...]).astype(BF16), wm_ref[...])


def _ffn(x, norm, w_gu, w_down, layer, ple=None, proj=None, n_first=None):
    x_pair = isinstance(x, tuple)
    rows = x[0].shape[0] + x[1].shape[0] if x_pair else x.shape[0]
    n_first_tiles = None if n_first is None else n_first // FFN_TM
    if x_pair:
        in_specs = _pair_specs(n_first_tiles, (FFN_TM, D_MODEL))
        args = list(x)
    else:
        in_specs = [pl.BlockSpec((FFN_TM, D_MODEL), lambda i: (i, 0))]
        args = [x]
    in_specs += [
        _resident((None, 1, D_MODEL), lambda i: (layer, 0, 0)),
        _resident((None, D_MODEL, 2 * D_FF), lambda i: (layer, 0, 0)),
        _resident((None, D_FF, D_MODEL), lambda i: (layer, 0, 0)),
    ]
    args += [norm, w_gu, w_down]
    if ple is not None:
        pl_layer = ple[4]
        in_specs += _pair_specs(n_first_tiles, (FFN_TM, D_PLE), lead=(pl_layer,)) + [
            _resident((None, 1, D_MODEL), lambda i: (pl_layer, 0, 0)),
            _resident((None, D_PLE, D_MODEL), lambda i: (pl_layer, 0, 0)),
            _resident((None, D_MODEL, D_MODEL), lambda i: (pl_layer, 0, 0)),
        ]
        args += list(ple[0]) + list(ple[1:4])
    out_specs = pl.BlockSpec((FFN_TM, D_MODEL), lambda i: (i, 0))
    out_shape = jax.ShapeDtypeStruct((rows, D_MODEL), F32)
    if proj is not None:
        m_norm, m_layer, m_w, m_wl = proj
        n = m_w.shape[2]
        in_specs += [
            _resident((None, 1, D_MODEL), lambda i: (m_layer, 0, 0)),
            _resident((None, D_MODEL, n), lambda i: (m_wl, 0, 0)),
        ]
        args += [m_norm, m_w]
        out_specs = [out_specs, pl.BlockSpec((FFN_TM, n), lambda i: (i, 0))]
        out_shape = [out_shape, jax.ShapeDtypeStruct((rows, n), F32)]
    return pl.pallas_call(
        functools.partial(_ffn_body, ple=ple is not None, proj=proj is not None, x_pair=x_pair,
                          n_first=n_first_tiles),
        grid=(rows // FFN_TM,),
        in_specs=in_specs,
        out_specs=out_specs,
        out_shape=out_shape,
        compiler_params=_cparams("parallel"),
        name="ffn" + ("_ple" if ple is not None else "") + ("_proj" if proj is not None else ""),
    )(*args)


ROW_TM = 256


def _ple_body(x_ref, p_ref, g_ref, wp_ref, wg_ref, gf_ref, o_ref, *, final):
    y = _ple_update(x_ref[...], p_ref[...], g_ref, wp_ref, wg_ref)
    if final:
        y = _rms(y, gf_ref[...])
    o_ref[...] = y


def _ple(x, p, norm, w_proj, w_gate, final_norm, layer, final, row0):
    rows = p.shape[1]
    blk0 = row0 // ROW_TM
    return pl.pallas_call(
        functools.partial(_ple_body, final=final),
        grid=(rows // ROW_TM,),
        in_specs=[
            pl.BlockSpec((ROW_TM, D_MODEL), lambda i: (i + blk0, 0)),
            pl.BlockSpec((None, ROW_TM, D_PLE), lambda i: (layer, i, 0)),
            _resident((None, 1, D_MODEL), lambda i: (layer, 0, 0)),
            _resident((None, D_PLE, D_MODEL), lambda i: (layer, 0, 0)),
            _resident((None, D_MODEL, D_MODEL), lambda i: (layer, 0, 0)),
            _resident((1, D_MODEL), lambda i: (0, 0)),
        ],
        out_specs=pl.BlockSpec((ROW_TM, D_MODEL), lambda i: (i, 0)),
        out_shape=jax.ShapeDtypeStruct((rows, D_MODEL), F32),
        compiler_params=_cparams("parallel"),
        name="ple",
    )(x, p, norm, w_proj, w_gate, final_norm)


def _norm_proj_body(x_ref, g_ref, w_ref, o_ref):
    o_ref[...] = _dot(_rms(x_ref[...], g_ref[...]).astype(BF16), w_ref[...])


def _norm_proj(x, norm, w, layer, wl, name):
    rows = x.shape[0]
    k, n = w.shape[1:]
    return pl.pallas_call(
        _norm_proj_body,
        grid=(rows // ROW_TM,),
        in_specs=[
            pl.BlockSpec((ROW_TM, k), lambda i: (i, 0)),
            _resident((None, 1, k), lambda i: (layer, 0, 0)),
            _resident((None, k, n), lambda i: (wl, 0, 0)),
        ],
        out_specs=pl.BlockSpec((ROW_TM, n), lambda i: (i, 0)),
        out_shape=jax.ShapeDtypeStruct((rows, n), F32),
        compiler_params=_cparams("parallel"),
        name=name,
    )(x, norm, w)


def _res_proj_body(x_ref, a_ref, w_ref, o_ref):
    o_ref[...] = x_ref[...] + _dot(a_ref[...].astype(BF16), w_ref[...])


def _res_proj(x, a, w, wl, row0, name):
    rows = a.shape[0]
    k, n = w.shape[1:]
    blk0 = row0 // ROW_TM
    return pl.pallas_call(
        _res_proj_body,
        grid=(rows // ROW_TM,),
        in_specs=[
            pl.BlockSpec((ROW_TM, n), lambda i: (i + blk0, 0)),
            pl.BlockSpec((ROW_TM, k), lambda i: (i, 0)),
            _resident((None, k, n), lambda i: (wl, 0, 0)),
        ],
        out_specs=pl.BlockSpec((ROW_TM, n), lambda i: (i + blk0, 0)),
        out_shape=jax.ShapeDtypeStruct(x.shape, F32),
        input_output_aliases={0: 0},
        compiler_params=_cparams("parallel"),
        name=name,
    )(x, a, w)


def _t5_bucket(dist):
    max_exact = N_BUCKETS // 2
    d = jnp.maximum(dist, 0)
    large = max_exact + (jnp.log(jnp.maximum(d, 1).astype(F32) / max_exact)
                         / math.log(BUCKET_MAX_DIST / max_exact) * (N_BUCKETS - max_exact)).astype(jnp.int32)
    large = jnp.minimum(large, N_BUCKETS - 1)
    return jnp.where(d < max_exact, d, large)


def _bias_body(tab_ref, bm_ref, valid_ref, o_ref):
    h = pl.program_id(0)
    bm = bm_ref[...]
    acc = jnp.zeros(bm.shape, F32)
    for b in range(N_BUCKETS):
        acc = jnp.where(bm == b, tab_ref[b, h], acc)
    o_ref[...] = jnp.where(valid_ref[...] > 0, acc, NEG_INF)


def _rel_bias(table, bucket_map, valid):
    r, c = bucket_map.shape
    return pl.pallas_call(
        _bias_body,
        grid=(A_HEADS,),
        in_specs=[pl.BlockSpec(memory_space=pltpu.SMEM), pl.BlockSpec((r, c), lambda h: (0, 0)),
                  pl.BlockSpec((r, c), lambda h: (0, 0))],
        out_specs=pl.BlockSpec((None, r, c), lambda h: (h, 0, 0)),
        out_shape=jax.ShapeDtypeStruct((A_HEADS, r, c), F32),
        compiler_params=_cparams("parallel"),
        name="rel_bias",
    )(table, bucket_map, valid.astype(jnp.int32))


def _group_sinks(sinks, t):
    return jnp.broadcast_to(sinks.reshape(A_KV_HEADS, A_GROUP, 1, 1),
                            (A_KV_HEADS, A_GROUP, t, LANES)).reshape(A_KV_HEADS, A_GROUP * t, LANES)


SWA_QROWS = A_BLOCK


def _swa_prompt_body(sink_ref, q_ref, kp_ref, ko_ref, vp_ref, vo_ref, bias_ref, x_ref, wo_ref, o_ref, o_scr):
    scale = A_HEAD_DIM ** -0.5
    for g in range(A_KV_HEADS):
        gs = slice(g * A_HEAD_DIM, (g + 1) * A_HEAD_DIM)
        kg = jnp.concatenate([kp_ref[:, gs], ko_ref[:, gs]], axis=0).astype(BF16)
        vg = jnp.concatenate([vp_ref[:, gs], vo_ref[:, gs]], axis=0).astype(BF16)
        for h in range(g * A_GROUP, (g + 1) * A_GROUP):
            hs = slice(h * A_HEAD_DIM, (h + 1) * A_HEAD_DIM)
            s = _dot_nt(q_ref[:, hs].astype(BF16), kg) * scale + bias_ref[h]
            sink = sink_ref[h]
            m = jnp.maximum(jnp.max(s, axis=-1, keepdims=True), sink)
            e = jnp.exp(s - m)
            den = jnp.sum(e, axis=-1, keepdims=True) + jnp.exp(sink - m)
            o_scr[:, hs] = _dot((e / den).astype(BF16), vg)
    o_ref[...] = x_ref[...] + _dot(o_scr[...].astype(BF16), wo_ref[...])


def _swa_prompt(x, qkv, bias, sinks, w_o, wl, nb_p, t_p):
    nstep = t_p // SWA_QROWS
    blocks_per_step = SWA_QROWS // A_BLOCK
    kcol = A_Q // A_KV
    vcol = kcol + 1

    def own(b, n):
        return b * nstep + n

    def prev(b, n):
        return b * nstep * blocks_per_step + jnp.maximum(n * blocks_per_step - 1, 0)

    return pl.pallas_call(
        _swa_prompt_body,
        grid=(nb_p, nstep),
        in_specs=[
            pl.BlockSpec(memory_space=pltpu.SMEM),
            pl.BlockSpec((SWA_QROWS, A_Q), lambda b, n: (own(b, n), 0)),
            pl.BlockSpec((A_BLOCK, A_KV), lambda b, n: (prev(b, n), kcol)),
            pl.BlockSpec((SWA_QROWS, A_KV), lambda b, n: (own(b, n), kcol)),
            pl.BlockSpec((A_BLOCK, A_KV), lambda b, n: (prev(b, n), vcol)),
            pl.BlockSpec((SWA_QROWS, A_KV), lambda b, n: (own(b, n), vcol)),
            pl.BlockSpec((None, A_HEADS, SWA_QROWS, A_BLOCK + SWA_QROWS),
                         lambda b, n: (jnp.minimum(n, 1), 0, 0, 0)),
            pl.BlockSpec((SWA_QROWS, D_MODEL), lambda b, n: (own(b, n), 0)),
            _resident((None, A_Q, D_MODEL), lambda b, n: (wl, 0, 0)),
        ],
        out_specs=pl.BlockSpec((SWA_QROWS, D_MODEL), lambda b, n: (own(b, n), 0)),
        out_shape=jax.ShapeDtypeStruct(x.shape, F32),
        scratch_shapes=[pltpu.VMEM((SWA_QROWS, A_Q), F32)],
        input_output_aliases={7: 0},
        compiler_params=_cparams("parallel", "arbitrary"),
        name="swa_prompt",
    )(sinks, qkv, qkv, qkv, qkv, qkv, bias, x, w_o)


SWA_S_NB = 8


def _slide_window(buf_t, new, l_s):
    d, w = buf_t.shape
    assert w == LANES and d <= LANES
    tile = jnp.concatenate([jnp.zeros((w - l_s, LANES), F32),
                            jnp.concatenate([new, jnp.zeros((l_s, LANES - d), F32)], axis=1)], axis=0)
    new_t = tile.T[:d]
    lane = lax.broadcasted_iota(jnp.int32, (d, w), 1)
    return jnp.where(lane < w - l_s, pltpu.roll(buf_t, w - l_s, axis=1), new_t)


def _swa_sample_body(qkv_ref, kt_ref, vt_ref, bb_ref, bn_ref, sink_ref, o_ref, kto_ref, vto_ref, *, l_s):
    scale = A_HEAD_DIM ** -0.5

    def one(b, carry):
        rows = pl.ds(pl.multiple_of(b * l_s, l_s), l_s)
        for g in range(A_KV_HEADS):
            heads = range(g * A_GROUP, (g + 1) * A_GROUP)
            qg = jnp.concatenate([qkv_ref[rows, h * A_HEAD_DIM:(h + 1) * A_HEAD_DIM] for h in heads], axis=0)
            qg = qg.astype(BF16)
            kn32 = qkv_ref[rows, A_Q + g * A_HEAD_DIM:A_Q + (g + 1) * A_HEAD_DIM]
            vn32 = qkv_ref[rows, A_Q + A_KV + g * A_HEAD_DIM:A_Q + A_KV + (g + 1) * A_HEAD_DIM]
            kto_ref[b, g] = _slide_window(kt_ref[b, g], kn32, l_s)
            vto_ref[b, g] = _slide_window(vt_ref[b, g], vn32, l_s)
            kn = kn32.astype(BF16)
            vn = vn32.astype(BF16)
            sb = _dot(qg, kt_ref[b, g].astype(BF16)) * scale + bb_ref[g]
            sn = _dot_nt(qg, kn) * scale + bn_ref[g]
            sink = sink_ref[g][:, :1]
            m = jnp.maximum(jnp.maximum(jnp.max(sb, axis=-1, keepdims=True), jnp.max(sn, axis=-1, keepdims=True)), sink)
            eb = jnp.exp(sb - m)
            en = jnp.exp(sn - m)
            den = jnp.sum(eb, axis=-1, keepdims=True) + jnp.sum(en, axis=-1, keepdims=True) + jnp.exp(sink - m)
            og = _dot_nt((eb / den).astype(BF16), vt_ref[b, g].astype(BF16)) + _dot((en / den).astype(BF16), vn)
            for i, h in enumerate(heads):
                o_ref[rows, h * A_HEAD_DIM:(h + 1) * A_HEAD_DIM] = og[i * l_s:(i + 1) * l_s]
        return carry

    lax.fori_loop(0, SWA_S_NB, one, 0, unroll=True)


def _swa_sample(qkv, kt_buf, vt_buf, bias, sinks, sl, row0, nb_s, l_s):
    w_buf = kt_buf.shape[-1]
    r = A_GROUP * l_s
    tm = SWA_S_NB * l_s
    blk0 = row0 // tm
    bias_g = bias.reshape(A_KV_HEADS, r, w_buf + l_s)
    sink_g = _group_sinks(sinks, l_s)
    buf_spec = pl.BlockSpec((None, SWA_S_NB, A_KV_HEADS, A_HEAD_DIM, w_buf), lambda i: (sl, i, 0, 0, 0))
    new_spec = pl.BlockSpec((SWA_S_NB, A_KV_HEADS, A_HEAD_DIM, w_buf), lambda i: (i, 0, 0, 0))
    new_shape = jax.ShapeDtypeStruct((nb_s, A_KV_HEADS, A_HEAD_DIM, w_buf), F32)
    const3 = lambda i: (0, 0, 0)
    return pl.pallas_call(
        functools.partial(_swa_sample_body, l_s=l_s),
        grid=(nb_s // SWA_S_NB,),
        in_specs=[
            pl.BlockSpec((tm, A_Q + 2 * A_KV), lambda i: (i + blk0, 0)),
            buf_spec,
            buf_spec,
            _resident((A_KV_HEADS, r, w_buf), const3),
            _resident((A_KV_HEADS, r, l_s), const3),
            _resident((A_KV_HEADS, r, LANES), const3),
        ],
        out_specs=[pl.BlockSpec((tm, A_Q), lambda i: (i, 0)), new_spec, new_spec],
        out_shape=[jax.ShapeDtypeStruct((nb_s * l_s, A_Q), F32), new_shape, new_shape],
        compiler_params=_cparams("parallel"),
        name="swa_sample",
    )(qkv, kt_buf, vt_buf, bias_g[:, :, :w_buf], bias_g[:, :, w_buf:], sink_g)


B_IN_COLS = Q_LORA + KV_LORA + 2 * LANES
B_KPE = Q_LORA + KV_LORA
B_KPE_ROT = B_KPE + LANES


def _mla_proj_body(x_ref, g_ref, win_ref, qn_ref, kvn_ref, wnope_ref, wpe_ref, wper_ref, wuk_ref,
                   cos_ref, sin_ref, kv_ref, q_ref, *maybe_kvb_ref, sample):
    tm = x_ref.shape[0]
    h = _rms(x_ref[...], g_ref[...]).astype(BF16)
    c = _dot(h, win_ref[...])
    cos = cos_ref[...]
    sin = sin_ref[...]
    kv_lat = _rms(c[:, Q_LORA:B_KPE], kvn_ref[...])
    kv_rope = (c[:, B_KPE:B_KPE + QK_ROPE] * cos[:, :QK_ROPE]
               + c[:, B_KPE_ROT:B_KPE_ROT + QK_ROPE] * sin[:, :QK_ROPE])
    kv_ref[:, :KV_LORA] = kv_lat
    kv_ref[:, KV_LORA:] = kv_rope
    for kvb_ref in maybe_kvb_ref:
        kvb_ref[:, :KV_LORA] = kv_lat.astype(BF16)
        kvb_ref[:, KV_LORA:] = kv_rope.astype(BF16)
    qn = _rms(c[:, :Q_LORA], qn_ref[...]).astype(BF16)
    qnope = _dot(qn, wnope_ref[...])
    n_rep = B_HEADS * QK_ROPE // LANES
    qpe = (_dot(qn, wpe_ref[...]) * jnp.concatenate([cos] * n_rep, axis=1)
           + _dot(qn, wper_ref[...]) * jnp.concatenate([sin] * n_rep, axis=1))
    for hd in range(B_HEADS):
        qlat = _dot(qnope[:, hd * QK_NOPE:(hd + 1) * QK_NOPE].astype(BF16), wuk_ref[hd])
        qrope = qpe[:, hd * QK_ROPE:(hd + 1) * QK_ROPE]
        if sample:
            l_s = q_ref.shape[1] // B_HEADS
            rs = slice(hd * l_s, (hd + 1) * l_s)
            q_ref[:, rs, :KV_LORA] = qlat.reshape(tm // l_s, l_s, KV_LORA)
            q_ref[:, rs, KV_LORA:] = qrope.reshape(tm // l_s, l_s, QK_ROPE)
        else:
            for s in range(tm // B_QBLOCK):
                ts = slice(s * B_QBLOCK, (s + 1) * B_QBLOCK)
                q_ref[s, hd, :, :KV_LORA] = qlat[ts].astype(BF16)
                q_ref[s, hd, :, KV_LORA:] = qrope[ts].astype(BF16)


def _mla_proj(x, norm, layer, w, cos, sin, row0, rows, l_s):
    sample = l_s is not None
    tm = ROW_TM
    blk0 = row0 // tm
    if sample:
        q_shape = jax.ShapeDtypeStruct((rows // l_s, B_HEADS * l_s, KV_ROW), F32)
        q_spec = pl.BlockSpec((tm // l_s, B_HEADS * l_s, KV_ROW), lambda i: (i, 0, 0))
    else:
        q_shape = jax.ShapeDtypeStruct((rows // B_QBLOCK, B_HEADS, B_QBLOCK, KV_ROW), BF16)
        q_spec = pl.BlockSpec((tm // B_QBLOCK, B_HEADS, B_QBLOCK, KV_ROW), lambda i: (i, 0, 0, 0))
    const2 = lambda i: (0, 0)
    kv_spec = pl.BlockSpec((tm, KV_ROW), lambda i: (i, 0))
    out_specs = [kv_spec, q_spec]
    out_shape = [jax.ShapeDtypeStruct((rows, KV_ROW), F32), q_shape]
    if not sample:
        out_specs.append(kv_spec)
        out_shape.append(jax.ShapeDtypeStruct((rows, KV_ROW), BF16))
    return pl.pallas_call(
        functools.partial(_mla_proj_body, sample=sample),
        grid=(rows // tm,),
        in_specs=[
            pl.BlockSpec((tm, D_MODEL), lambda i: (i + blk0, 0)),
            _resident((None, 1, D_MODEL), lambda i: (layer, 0, 0)),
            _resident((D_MODEL, B_IN_COLS), const2),
            _resident((1, Q_LORA), const2),
            _resident((1, KV_LORA), const2),
            _resident((Q_LORA, B_HEADS * QK_NOPE), const2),
            _resident((Q_LORA, B_HEADS * QK_ROPE), const2),
            _resident((Q_LORA, B_HEADS * QK_ROPE), const2),
            _resident((B_HEADS, QK_NOPE, KV_LORA), lambda i: (0, 0, 0)),
            pl.BlockSpec((tm, LANES), lambda i: (i + blk0, 0)),
            pl.BlockSpec((tm, LANES), lambda i: (i + blk0, 0)),
        ],
        out_specs=out_specs,
        out_shape=out_shape,
        compiler_params=_cparams("parallel"),
        name="mla_proj_sample" if sample else "mla_proj_prompt",
    )(x, norm, w["w_in"], w["q_norm"], w["kv_norm"], w["w_nope"], w["w_pe"], w["w_pe_rot"], w["w_uk"], cos, sin)


def _softmax_stats(s, m_scr, l_scr):
    c = MLA_SCALE * math.log2(math.e)
    m_prev = m_scr[...]
    m_new = jnp.maximum(m_prev, jnp.max(s, axis=-1, keepdims=True))
    alpha = jnp.exp2((m_prev - m_new) * c)
    p = jnp.exp2((s - _lanes(m_new, s.shape[1])) * c)
    l_scr[...] = alpha * l_scr[...] + jnp.sum(p, axis=-1, keepdims=True)
    m_scr[...] = m_new
    return alpha, p.astype(BF16)


def _softmax_accumulate(s, pv, m_scr, l_scr, acc_scr):
    alpha, p = _softmax_stats(s, m_scr, l_scr)
    acc_scr[...] = acc_scr[...] * _lanes(alpha, KV_LORA) + pv(p)


def _softmax_update(q, kvb, m_scr, l_scr, acc_scr, valid=None):
    s = _dot_nt(q, kvb)
    if valid is not None:
        s = jnp.where(valid, s, NEG_INF)
    _softmax_accumulate(s, lambda p: _dot(p, kvb[:, :KV_LORA]), m_scr, l_scr, acc_scr)


def _softmax_init(m_scr, l_scr, acc_scr):
    m_scr[...] = jnp.full(m_scr.shape, NEG_INF, F32)
    l_scr[...] = jnp.zeros(l_scr.shape, F32)
    acc_scr[...] = jnp.zeros(acc_scr.shape, F32)


def _mla_prompt_body(q_ref, kv_ref, x_ref, wuv_ref, wo_ref, o_ref, s0_scr, s1_scr, m_scr, l_scr, acc_scr, u_scr):
    qi = pl.program_id(1)
    r = B_HEADS * B_QBLOCK
    n_off = qi // (B_KVBLOCK // B_QBLOCK)
    q = q_ref[...].reshape(r, KV_ROW)

    def kv_block(k):
        return kv_ref[pl.ds(pl.multiple_of(k * B_KVBLOCK, B_KVBLOCK), B_KVBLOCK), :]

    def update(s, k):
        kvb = kv_block(k)
        _softmax_accumulate(s, lambda p: _dot(p, kvb[:, :KV_LORA]), m_scr, l_scr, acc_scr)

    def step(s_cur, s_next, k):
        s_next[...] = _dot_nt(q, kv_block(k + 1))
        update(s_cur[...], k)

    def last(s_cur):
        tok = lax.broadcasted_iota(jnp.int32, (r, B_KVBLOCK), 0) & (B_QBLOCK - 1)
        key = lax.broadcasted_iota(jnp.int32, (r, B_KVBLOCK), 1)
        valid = key - tok <= qi * B_QBLOCK - n_off * B_KVBLOCK
        update(jnp.where(valid, s_cur[...], NEG_INF), n_off)
        o = acc_scr[...] * _lanes(1.0 / l_scr[...], KV_LORA)
        for hd in range(B_HEADS):
            oh = o[hd * B_QBLOCK:(hd + 1) * B_QBLOCK].astype(BF16)
            u_scr[:, hd * V_HEAD:(hd + 1) * V_HEAD] = _dot(oh, wuv_ref[hd])
        o_ref[...] = x_ref[...] + _dot(u_scr[...].astype(BF16), wo_ref[...])

    _softmax_init(m_scr, l_scr, acc_scr)
    s0_scr[...] = _dot_nt(q, kv_block(0))

    def two_steps(j, carry):
        step(s0_scr, s1_scr, 2 * j)
        step(s1_scr, s0_scr, 2 * j + 1)
        return carry

    lax.fori_loop(0, n_off // 2, two_steps, 0)
    odd = n_off % 2 == 1

    @pl.when(odd)
    def _():
        step(s0_scr, s1_scr, n_off - 1)
        last(s1_scr)

    @pl.when(jnp.logical_not(odd))
    def _():
        last(s0_scr)


def _mla_prompt(x, q_cat, kv_bf16, w, nb_p, t_p):
    nq = t_p // B_QBLOCK
    r = B_HEADS * B_QBLOCK
    return pl.pallas_call(
        _mla_prompt_body,
        grid=(nb_p, nq),
        in_specs=[
            pl.BlockSpec((None, B_HEADS, B_QBLOCK, KV_ROW), lambda b, qi: (b * nq + qi, 0, 0, 0)),
            pl.BlockSpec((t_p, KV_ROW), lambda b, qi: (b, 0)),
            pl.BlockSpec((B_QBLOCK, D_MODEL), lambda b, qi: (b * nq + qi, 0)),
            _resident((B_HEADS, KV_LORA, V_HEAD), lambda b, qi: (0, 0, 0)),
            _resident((B_HEADS * V_HEAD, D_MODEL), lambda b, qi: (0, 0)),
        ],
        out_specs=pl.BlockSpec((B_QBLOCK, D_MODEL), lambda b, qi: (b * nq + qi, 0)),
        out_shape=jax.ShapeDtypeStruct(x.shape, F32),
        scratch_shapes=[
            pltpu.VMEM((r, B_KVBLOCK), F32),
            pltpu.VMEM((r, B_KVBLOCK), F32),
            pltpu.VMEM((r, LANES), F32),
            pltpu.VMEM((r, LANES), F32),
            pltpu.VMEM((r, KV_LORA), F32),
            pltpu.VMEM((B_QBLOCK, B_HEADS * V_HEAD), F32),
        ],
        input_output_aliases={2: 0},
        compiler_params=_cparams("parallel", "arbitrary"),
        name="mla_prompt",
    )(q_cat, kv_bf16, x, w["w_uv"], w["w_o"])


def _mla_sample_body(pt_ref, q_ref, cache_ref, kvn_ref, o_ref, buf, sem, m_scr, l_scr, acc_scr, acct_scr,
                     *, l_s, n_pages, cl):
    b = pl.program_id(0)
    slot = b % 2
    r = q_ref.shape[0]
    chunk = math.gcd(n_pages, B_PAGES_PER_STEP)
    assert r == LANES

    def page_copy(batch, slot_, k):
        page = pt_ref[batch * n_pages + k]
        return pltpu.make_async_copy(cache_ref.at[cl, page], buf.at[slot_, k], sem.at[slot_, k])

    def start_pages(batch, slot_):
        for k in range(n_pages):
            page_copy(batch, slot_, k).start(priority=k % 2)

    @pl.when(b == 0)
    def _():
        start_pages(0, 0)

    @pl.when(b + 1 < pl.num_programs(0))
    def _():
        start_pages(b + 1, 1 - slot)

    _softmax_init(m_scr, l_scr, acct_scr)
    q = q_ref[...].astype(BF16)
    for c in range(n_pages // chunk):
        ks = range(c * chunk, (c + 1) * chunk)
        for k in ks:
            page_copy(b, slot, k).wait()
        kt = jnp.concatenate([buf[slot, k].astype(BF16) for k in ks], axis=1)
        alpha, p = _softmax_stats(_dot(q, kt), m_scr, l_scr)
        alpha_t = jnp.concatenate([alpha.T] * (KV_LORA // LANES), axis=0)
        acct_scr[...] = acct_scr[...] * alpha_t + _dot_nt(kt[:KV_LORA, :], p)

    acc_scr[...] = acct_scr[...].T
    tok = lax.broadcasted_iota(jnp.int32, (r, l_s), 0) % l_s
    key = lax.broadcasted_iota(jnp.int32, (r, l_s), 1)
    _softmax_update(q, kvn_ref[...].astype(BF16), m_scr, l_scr, acc_scr, valid=key <= tok)
    o_ref[...] = acc_scr[...] * _lanes(1.0 / l_scr[...], KV_LORA)


def _mla_sample(q_cat, kv_rows, cache_t, page_table, cl, row0, nb_s, l_s):
    n_pages = page_table.shape[1]
    r = B_HEADS * l_s
    blk0 = row0 // l_s
    return pl.pallas_call(
        functools.partial(_mla_sample_body, l_s=l_s, n_pages=n_pages, cl=cl),
        grid_spec=pltpu.PrefetchScalarGridSpec(
            num_scalar_prefetch=1,
            grid=(nb_s,),
            in_specs=[
                pl.BlockSpec((None, r, KV_ROW), lambda b, pt: (b, 0, 0)),
                pl.BlockSpec(memory_space=pl.ANY),
                pl.BlockSpec((l_s, KV_ROW), lambda b, pt: (b + blk0, 0)),
            ],
            out_specs=pl.BlockSpec((None, r, KV_LORA), lambda b, pt: (b, 0, 0)),
            scratch_shapes=[
                pltpu.VMEM((2, n_pages, KV_ROW, PAGE_SIZE), F32),
                pltpu.SemaphoreType.DMA((2, n_pages)),
                pltpu.VMEM((r, LANES), F32),
                pltpu.VMEM((r, LANES), F32),
                pltpu.VMEM((r, KV_LORA), F32),
                pltpu.VMEM((KV_LORA, r), F32),
            ],
        ),
        out_shape=jax.ShapeDtypeStruct((nb_s, r, KV_LORA), F32),
        compiler_params=_cparams("arbitrary"),
        name="mla_sample",
    )(page_table.reshape(-1), q_cat, cache_t, kv_rows)


def _mla_out_sample_body(o_ref, x_ref, wuv_ref, wo_ref, out_ref, u_scr, *, l_s):
    nb = o_ref.shape[0]
    for hd in range(B_HEADS):
        oh = o_ref[:, hd * l_s:(hd + 1) * l_s, :].reshape(nb * l_s, KV_LORA).astype(BF16)
        u_scr[:, hd * V_HEAD:(hd + 1) * V_HEAD] = _dot(oh, wuv_ref[hd])
    out_ref[...] = x_ref[...] + _dot(u_scr[...].astype(BF16), wo_ref[...])


def _mla_out_sample(x, o_lat, w, row0, l_s):
    nb_s = o_lat.shape[0]
    nb = ROW_TM // l_s
    blk0 = row0 // ROW_TM
    return pl.pallas_call(
        functools.partial(_mla_out_sample_body, l_s=l_s),
        grid=(nb_s // nb,),
        in_specs=[
            pl.BlockSpec((nb, B_HEADS * l_s, KV_LORA), lambda i: (i, 0, 0)),
            pl.BlockSpec((ROW_TM, D_MODEL), lambda i: (i + blk0, 0)),
            _resident((B_HEADS, KV_LORA, V_HEAD), lambda i: (0, 0, 0)),
            _resident((B_HEADS * V_HEAD, D_MODEL), lambda i: (0, 0)),
        ],
        out_specs=pl.BlockSpec((ROW_TM, D_MODEL), lambda i: (i + blk0, 0)),
        out_shape=jax.ShapeDtypeStruct(x.shape, F32),
        scratch_shapes=[pltpu.VMEM((ROW_TM, B_HEADS * V_HEAD), F32)],
        input_output_aliases={1: 0},
        compiler_params=_cparams("parallel"),
        name="mla_out_sample",
    )(o_lat, x, w["w_uv"], w["w_o"])


C_IN_COLS = D_INNER + CONV_DIM + LANES


def _mamba_in_body(x_ref, g_ref, w_ref, z_ref, xbc_ref, dt_ref):
    h = _rms(x_ref[...], g_ref[...]).astype(BF16)
    z_ref[...] = _dot(h, w_ref[:, :D_INNER])
    xbc_ref[...] = _dot(h, w_ref[:, D_INNER:D_INNER + CONV_DIM])
    dt_ref[...] = _dot(h, w_ref[:, D_INNER + CONV_DIM:])


def _mamba_in(x, norm, layer, w_in):
    rows = x.shape[0]
    return pl.pallas_call(
        _mamba_in_body,
        grid=(rows // ROW_TM,),
        in_specs=[
            pl.BlockSpec((ROW_TM, D_MODEL), lambda i: (i, 0)),
            _resident((None, 1, D_MODEL), lambda i: (layer, 0, 0)),
            _resident((D_MODEL, C_IN_COLS), lambda i: (0, 0)),
        ],
        out_specs=[
            pl.BlockSpec((ROW_TM, D_INNER), lambda i: (i, 0)),
            pl.BlockSpec((ROW_TM, CONV_DIM), lambda i: (i, 0)),
            pl.BlockSpec((ROW_TM, LANES), lambda i: (i, 0)),
        ],
        out_shape=[
            jax.ShapeDtypeStruct((rows, D_INNER), F32),
            jax.ShapeDtypeStruct((rows, CONV_DIM), F32),
            jax.ShapeDtypeStruct((rows, LANES), F32),
        ],
        compiler_params=_cparams("parallel"),
        name="mamba_in",
    )(x, norm, w_in)


def _causal_conv(ext_scr, u, cw_ref, cb_ref):
    t = u.shape[0]
    ext_scr[SUBLANES:SUBLANES + t, :] = u
    acc = cb_ref[...] + cw_ref[D_CONV - 1:D_CONV, :] * u
    for k in range(D_CONV - 1):
        lo = SUBLANES - (D_CONV - 1) + k
        acc = acc + cw_ref[k:k + 1, :] * ext_scr[lo:lo + t, :]
    return acc


def _gated_norm(y, z, norm_g):
    yz = y * _silu(z)
    gw = D_INNER // C_GROUPS
    parts = []
    for g in range(C_GROUPS):
        part = yz[:, g * gw:(g + 1) * gw]
        parts.append(part * lax.rsqrt(jnp.mean(part * part, axis=-1, keepdims=True) + EPS))
    return jnp.concatenate(parts, axis=1) * norm_g


def _ssd_prompt_body(xbc_ref, dt_ref, z_ref, cw_ref, cb_ref, dtb_ref, alog_ref, dskip_ref, ng_ref, yn_ref, st_ref,
                     ext_scr, ht_scr, y_ref):
    c = pl.program_id(1)
    t = SSD_CHUNK

    @pl.when(c == 0)
    def _():
        ext_scr[0:SUBLANES, :] = jnp.zeros((SUBLANES, CONV_DIM), F32)
        ht_scr[...] = jnp.zeros(ht_scr.shape, F32)

    u = xbc_ref[...]
    act = _silu(_causal_conv(ext_scr, u, cw_ref, cb_ref))
    ext_scr[0:SUBLANES, :] = u[t - SUBLANES:, :]

    dt = _softplus(dt_ref[...] + dtb_ref[...])
    a = -jnp.exp(alog_ref[...])
    row = lax.broadcasted_iota(jnp.int32, (t, t), 0)
    col = lax.broadcasted_iota(jnp.int32, (t, t), 1)
    causal = col <= row
    acs = _dot_f32(causal.astype(F32), dt * a)
    acs_t = acs.T
    dt_t = dt.T
    to_end = jnp.exp(acs[t - 1:t, :] - acs) * dt
    gw = C_HPG * C_HEAD_DIM
    heads_per_vreg = LANES // C_HEAD_DIM
    assert heads_per_vreg == 2 and t == LANES
    first = lax.broadcasted_iota(jnp.int32, (t, LANES), 1) < C_HEAD_DIM
    for g in range(C_GROUPS):
        bg = act[:, D_INNER + g * D_STATE:D_INNER + (g + 1) * D_STATE]
        cg = act[:, D_INNER + C_BC + g * D_STATE:D_INNER + C_BC + (g + 1) * D_STATE].astype(BF16)
        cb = _dot_nt(cg, bg.astype(BF16))
        bg_t = bg.T.astype(BF16)
        y_off = _dot(cg, ht_scr[:, g * gw:(g + 1) * gw].astype(BF16))
        for jp in range(C_HPG // heads_per_vreg):
            h0 = g * C_HPG + heads_per_vreg * jp
            ps = slice(h0 * C_HEAD_DIM, h0 * C_HEAD_DIM + LANES)
            ms, a_cols, te_cols = [], [], []
            for h in (h0, h0 + 1):
                a_col = jnp.broadcast_to(acs[:, h:h + 1], (t, t))
                a_row = jnp.broadcast_to(acs_t[h:h + 1, :], (t, t))
                dt_row = jnp.broadcast_to(dt_t[h:h + 1, :], (t, t))
                ms.append(cb * jnp.exp(jnp.where(causal, a_col - a_row, -jnp.inf)) * dt_row)
                a_cols.append(a_col)
                te_cols.append(jnp.broadcast_to(to_end[:, h:h + 1], (t, LANES)))
            x2 = act[:, ps]
            lhs = jnp.concatenate(ms, axis=1).astype(BF16)
            rhs = jnp.concatenate([jnp.where(first, x2, 0.0), jnp.where(first, 0.0, x2)], axis=0).astype(BF16)
            a_col2 = jnp.where(first, a_cols[0], a_cols[1])
            y_ref[:, ps] = (_dot(lhs, rhs)
                            + y_off[:, jp * LANES:(jp + 1) * LANES] * jnp.exp(a_col2)
                            + dskip_ref[:, ps] * x2)
            xw = (x2 * jnp.where(first, te_cols[0], te_cols[1])).astype(BF16)
            ht_scr[:, ps] = jnp.exp(a_col2[t - 1:t, :]) * ht_scr[:, ps] + _dot(bg_t, xw)

    yn_ref[...] = _gated_norm(y_ref[...], z_ref[...], ng_ref[...]).astype(BF16)

    @pl.when(c == pl.num_programs(1) - 1)
    def _():
        st_ref[...] = ht_scr[...].T


def _ssd_prompt(xbc, dt, z, w, nb_p, t_p):
    nc = t_p // SSD_CHUNK
    const2 = lambda b, c: (0, 0)
    rows = lambda b, c: (b * nc + c, 0)
    return pl.pallas_call(
        _ssd_prompt_body,
        grid=(nb_p, nc),
        in_specs=[
            pl.BlockSpec((SSD_CHUNK, CONV_DIM), rows),
            pl.BlockSpec((SSD_CHUNK, LANES), rows),
            pl.BlockSpec((SSD_CHUNK, D_INNER), rows),
            _resident((D_CONV, CONV_DIM), const2),
            _resident((1, CONV_DIM), const2),
            _resident((1, LANES), const2),
            _resident((1, LANES), const2),
            _resident((1, D_INNER), const2),
            _resident((1, D_INNER), const2),
        ],
        out_specs=[
            pl.BlockSpec((SSD_CHUNK, D_INNER), rows),
            pl.BlockSpec((None, D_INNER, D_STATE), lambda b, c: (b, 0, 0)),
        ],
        out_shape=[
            jax.ShapeDtypeStruct((nb_p * t_p, D_INNER), BF16),
            jax.ShapeDtypeStruct((nb_p, D_INNER, D_STATE), F32),
        ],
        scratch_shapes=[
            pltpu.VMEM((SUBLANES + SSD_CHUNK, CONV_DIM), F32),
            pltpu.VMEM((D_STATE, D_INNER), F32),
            pltpu.VMEM((SSD_CHUNK, D_INNER), F32),
        ],
        compiler_params=_cparams("parallel", "arbitrary"),
        name="ssd_prompt",
    )(xbc, dt, z, w["conv_w"], w["conv_b"], w["dt_bias"], w["a_log"], w["d_skip"], w["norm_g"])


def _ssd_sample_body(xbc_ref, dtx_ref, z_ref, cprev_ref, h0_ref, cw_ref, cb_ref, dtb_ref, alog_ref, dskip_ref, ng_ref,
                     y_ref, st_ref, ext_scr, *, l_s):
    u = xbc_ref[...]
    ext_scr[SUBLANES - (D_CONV - 1):SUBLANES, :] = cprev_ref[...]
    act = _silu(_causal_conv(ext_scr, u, cw_ref, cb_ref))
    xs = act[:, :D_INNER]

    dt = _softplus(dtx_ref[...] + dtb_ref[...])
    a = -jnp.exp(alog_ref[...])
    row = lax.broadcasted_iota(jnp.int32, (l_s, l_s), 0)
    col = lax.broadcasted_iota(jnp.int32, (l_s, l_s), 1)
    acs = _dot_f32((col <= row).astype(F32), dt * a)
    last = acs[l_s - 1:l_s, :]
    xdt = xs * dt
    xw = xdt * jnp.exp(last - acs)
    decay_col = jnp.broadcast_to(jnp.exp(last), (LANES, D_INNER)).T
    gw = C_HPG * C_HEAD_DIM
    lrow = lax.broadcasted_iota(jnp.int32, (l_s, gw), 0)
    for g in range(C_GROUPS):
        gs = slice(g * gw, (g + 1) * gw)
        bg = act[:, D_INNER + g * D_STATE:D_INNER + (g + 1) * D_STATE]
        cg = act[:, D_INNER + C_BC + g * D_STATE:D_INNER + C_BC + (g + 1) * D_STATE]
        cb = _dot_nt(cg.astype(BF16), bg.astype(BF16))
        h0 = h0_ref[gs, :]
        y = _dot_nt(cg.astype(BF16), h0.astype(BF16)) * jnp.exp(acs[:, gs])
        for s in range(l_s):
            seg = acs[:, gs] - acs[s:s + 1, gs]
            w_s = jnp.exp(jnp.where(lrow >= s, seg, -jnp.inf)) * jnp.broadcast_to(cb[:, s:s + 1], (l_s, gw))
            y = y + w_s * xdt[s:s + 1, gs]
        y_ref[:, gs] = y + dskip_ref[:, gs] * xs[:, gs]
        st_ref[gs, :] = decay_col[gs, :] * h0 + _dot_tn(xw[:, gs].astype(BF16), bg.astype(BF16))
    y_ref[...] = _gated_norm(y_ref[...], z_ref[...], ng_ref[...])


def _ssd_sample(xbc, dtx, z, conv_prev, h0, sl, w, row0, nb_s, l_s):
    blk0 = row0 // l_s
    const2 = lambda b: (0, 0)
    return pl.pallas_call(
        functools.partial(_ssd_sample_body, l_s=l_s),
        grid=(nb_s,),
        in_specs=[
            pl.BlockSpec((l_s, CONV_DIM), lambda b: (b + blk0, 0)),
            pl.BlockSpec((l_s, D_INNER), lambda b: (b, 0)),
            pl.BlockSpec((l_s, D_INNER), lambda b: (b + blk0, 0)),
            pl.BlockSpec((None, None, D_CONV - 1, CONV_DIM), lambda b: (sl, b, 0, 0)),
            pl.BlockSpec((None, None, D_INNER, D_STATE), lambda b: (sl, b, 0, 0)),
            _resident((D_CONV, CONV_DIM), const2),
            _resident((1, CONV_DIM), const2),
            _resident((1, D_INNER), const2),
            _resident((1, D_INNER), const2),
            _resident((1, D_INNER), const2),
            _resident((1, D_INNER), const2),
        ],
        out_specs=[
            pl.BlockSpec((l_s, D_INNER), lambda b: (b, 0)),
            pl.BlockSpec((None, D_INNER, D_STATE), lambda b: (b, 0, 0)),
        ],
        out_shape=[
            jax.ShapeDtypeStruct((nb_s * l_s, D_INNER), F32),
            jax.ShapeDtypeStruct((nb_s, D_INNER, D_STATE), F32),
        ],
        scratch_shapes=[pltpu.VMEM((SUBLANES + l_s, CONV_DIM), F32)],
        compiler_params=_cparams("parallel"),
        name="ssd_sample",
    )(xbc, dtx, z, conv_prev, h0, w["conv_w"], w["conv_b"], w["dt_bias_x"], w["a_log_x"], w["d_skip"], w["norm_g"])


def _rope_tables(pos):
    half = QK_ROPE // 2
    inv = jnp.power(ROPE_THETA, -jnp.arange(half, dtype=F32) / half)
    ang = pos.astype(F32)[:, None] * inv
    reps = LANES // half
    return jnp.tile(jnp.cos(ang), (1, reps)), jnp.tile(jnp.sin(ang), (1, reps))


def _rot_cols(w):
    half = w.shape[-1] // 2
    return jnp.concatenate([-w[..., half:], w[..., :half]], axis=-1)


def _mla_weights(w_in, q_norm, w_uq, kv_norm, w_ukv, w_o):
    kpe = w_in[:, B_KPE:]
    pad = jnp.zeros((D_MODEL, LANES - QK_ROPE), w_in.dtype)
    w_in_aug = jnp.concatenate([w_in[:, :B_KPE], kpe, pad, _rot_cols(kpe), pad], axis=1)
    uq = w_uq.reshape(Q_LORA, B_HEADS, QK_NOPE + QK_ROPE)
    pe = uq[:, :, QK_NOPE:]
    ukv = w_ukv.reshape(KV_LORA, B_HEADS, QK_NOPE + V_HEAD)
    return {
        "w_in": w_in_aug.astype(BF16),
        "q_norm": q_norm.reshape(1, Q_LORA),
        "kv_norm": kv_norm.reshape(1, KV_LORA),
        "w_nope": uq[:, :, :QK_NOPE].reshape(Q_LORA, B_HEADS * QK_NOPE).astype(BF16),
        "w_pe": pe.reshape(Q_LORA, B_HEADS * QK_ROPE).astype(BF16),
        "w_pe_rot": _rot_cols(pe).reshape(Q_LORA, B_HEADS * QK_ROPE).astype(BF16),
        "w_uk": jnp.transpose(ukv[:, :, :QK_NOPE], (1, 2, 0)).astype(BF16),
        "w_uv": jnp.transpose(ukv[:, :, QK_NOPE:], (1, 0, 2)).astype(BF16),
        "w_o": w_o.astype(BF16),
    }


def _pad_lanes(v):
    return jnp.pad(v, (0, LANES - v.shape[0])).reshape(1, LANES)


def _expand_heads(v):
    return jnp.repeat(v, C_HEAD_DIM).reshape(1, D_INNER)


def _mamba_weights(w_in, conv_w, conv_b, dt_bias, a_log, d_skip, norm_g, w_o):
    w_dt = w_in[:, D_INNER + CONV_DIM:]
    pad = jnp.zeros((D_MODEL, LANES - C_HEADS), w_in.dtype)
    return {
        "w_in": jnp.concatenate([w_in, pad], axis=1).astype(BF16),
        "w_dt_x": jnp.repeat(w_dt, C_HEAD_DIM, axis=1).astype(BF16)[None],
        "conv_w": conv_w,
        "conv_b": conv_b.reshape(1, CONV_DIM),
        "dt_bias": _pad_lanes(dt_bias.astype(F32)),
        "a_log": _pad_lanes(a_log.astype(F32)),
        "dt_bias_x": _expand_heads(dt_bias.astype(F32)),
        "a_log_x": _expand_heads(a_log.astype(F32)),
        "d_skip": _expand_heads(d_skip),
        "norm_g": norm_g.reshape(1, D_INNER),
        "w_o": w_o.astype(BF16),
    }


def kernel(x_prompt, x_sample, state_swa_k, state_swa_v, cache_mla, state_ssm, state_conv, page_table, p_prompt, p_sample, norm_ffn1, ffn1_w_gu, ffn1_w_down, norm_mix, norm_ffn2, ffn2_w_gu, ffn2_w_down, norm_ple, ple_w_proj, ple_w_gate, rel_bias_table, a_w_qkv, a_w_o, a_sinks, b_w_in, b_q_norm, b_w_uq, b_kv_norm, b_w_ukv, b_w_o, c_w_in, c_conv_w, c_conv_b, c_dt_bias, c_a_log, c_d, c_norm, c_w_o, final_norm):
    nb_p, t_p, _ = x_prompt.shape
    nb_s, l_s, _ = x_sample.shape
    depth = norm_ffn1.shape[0]
    rows_p = nb_p * t_p
    rows_s = nb_s * l_s
    w_buf = state_swa_k.shape[2]
    assert rows_p % FFN_TM == 0 and (rows_p + rows_s) % FFN_TM == 0 and rows_s % ROW_TM == 0
    assert t_p % B_KVBLOCK == 0 and t_p % SSD_CHUNK == 0 and t_p % SWA_QROWS == 0
    assert l_s == SUBLANES and w_buf == WINDOW

    x = (x_prompt.reshape(rows_p, D_MODEL), x_sample.reshape(rows_s, D_MODEL))
    p_parts = (p_prompt.reshape(depth, rows_p, D_PLE), p_sample.reshape(depth, rows_s, D_PLE))

    def norm3(g):
        return g.reshape(g.shape[0], 1, g.shape[1])

    norm_ffn1, norm_mix, norm_ffn2, norm_ple = map(norm3, (norm_ffn1, norm_mix, norm_ffn2, norm_ple))
    ffn1_w_gu, ffn1_w_down, ffn2_w_gu, ffn2_w_down, ple_w_proj, ple_w_gate, a_w_qkv, a_w_o = (
        w.astype(BF16) for w in (ffn1_w_gu, ffn1_w_down, ffn2_w_gu, ffn2_w_down, ple_w_proj, ple_w_gate, a_w_qkv, a_w_o))
    final_norm = final_norm.reshape(1, D_MODEL)

    i = jnp.arange(SWA_QROWS)[:, None]
    j = jnp.arange(A_BLOCK + SWA_QROWS)[None, :]
    dist = A_BLOCK + i - j
    in_window = (dist >= 0) & (dist < WINDOW)
    bias_p = jnp.stack([_rel_bias(rel_bias_table, _t5_bucket(dist), in_window & (j >= A_BLOCK)),
                        _rel_bias(rel_bias_table, _t5_bucket(dist), in_window)])
    q_pos = PAST_LEN + jnp.arange(l_s)
    k_pos = jnp.concatenate([PAST_LEN - w_buf + jnp.arange(w_buf), q_pos])
    dist = q_pos[:, None] - k_pos[None, :]
    bias_s = _rel_bias(rel_bias_table, _t5_bucket(dist), (dist >= 0) & (dist < WINDOW))
    swa_k_buf = jnp.transpose(state_swa_k, (0, 1, 3, 4, 2))
    swa_v_buf = jnp.transpose(state_swa_v, (0, 1, 3, 4, 2))

    pos = jnp.concatenate([jnp.tile(jnp.arange(t_p), nb_p), jnp.tile(PAST_LEN + jnp.arange(l_s), nb_s)])
    cos, sin = _rope_tables(pos)

    ssm_flat = state_ssm.reshape(state_ssm.shape[0], nb_s, D_INNER, D_STATE)
    cache_t = jnp.swapaxes(cache_mla, 2, 3)

    def last_rows(a, n):
        return jnp.stack([a[(b + 1) * t_p - n:(b + 1) * t_p] for b in range(nb_p)])

    swa_k_p, swa_v_p, mla_p, ssm_p, conv_p = [], [], [], [], []
    swa_k_s, swa_v_s, mla_s, ssm_s, conv_s = [], [], [], [], []
    for layer in range(depth):
        kind, sl = layer % N_MIXERS, layer // N_MIXERS
        prev_ple = None if layer == 0 else (p_parts, norm_ple, ple_w_proj, ple_w_gate, layer - 1)
        if kind == 0:
            x, qkv = _ffn(x, norm_ffn1, ffn1_w_gu, ffn1_w_down, layer, ple=prev_ple,
                          proj=(norm_mix, layer, a_w_qkv, sl), n_first=rows_p)
        else:
            x = _ffn(x, norm_ffn1, ffn1_w_gu, ffn1_w_down, layer, ple=prev_ple, n_first=rows_p)
        if kind == 0:
            x = _swa_prompt(x, qkv, bias_p, a_sinks[sl], a_w_o, sl, nb_p, t_p)
            o_s, kt_new, vt_new = _swa_sample(qkv, swa_k_buf, swa_v_buf, bias_s, a_sinks[sl], sl, rows_p, nb_s, l_s)
            x = _res_proj(x, o_s, a_w_o, sl, rows_p, "swa_out_sample")
            kv_p = last_rows(qkv, w_buf)[:, :, A_Q:].reshape(nb_p, w_buf, 2, A_KV_HEADS, A_HEAD_DIM)
            swa_k_p.append(kv_p[:, :, 0])
            swa_v_p.append(kv_p[:, :, 1])
            swa_k_s.append(jnp.transpose(kt_new, (0, 3, 1, 2)))
            swa_v_s.append(jnp.transpose(vt_new, (0, 3, 1, 2)))
        elif kind == 1:
            w = _mla_weights(b_w_in[sl], b_q_norm[sl], b_w_uq[sl], b_kv_norm[sl], b_w_ukv[sl], b_w_o[sl])
            kv_rows_p, q_p, kv_bf16 = _mla_proj(x, norm_mix, layer, w, cos, sin, 0, rows_p, None)
            kv_rows_s, q_s = _mla_proj(x, norm_mix, layer, w, cos, sin, rows_p, rows_s, l_s)
            x = _mla_prompt(x, q_p, kv_bf16, w, nb_p, t_p)
            o_lat = _mla_sample(q_s, kv_rows_s, cache_t, page_table, sl, 0, nb_s, l_s)
            x = _mla_out_sample(x, o_lat, w, rows_p, l_s)
            mla_p.append(kv_rows_p.reshape(nb_p, t_p, KV_ROW))
            mla_s.append(kv_rows_s.reshape(nb_s, l_s, KV_ROW))
        else:
            w = _mamba_weights(c_w_in[sl], c_conv_w[sl], c_conv_b[sl], c_dt_bias[sl], c_a_log[sl], c_d[sl],
                               c_norm[sl], c_w_o[sl])
            z, xbc, dt = _mamba_in(x, norm_mix, layer, w["w_in"])
            dtx = _norm_proj(x[rows_p:], norm_mix, w["w_dt_x"], layer, 0, "mamba_dt_sample")
            y_p, st_p = _ssd_prompt(xbc, dt, z, w, nb_p, t_p)
            y_s, st_s = _ssd_sample(xbc, dtx, z, state_conv, ssm_flat, sl, w, rows_p, nb_s, l_s)
            x = _res_proj(x, y_p, w["w_o"][None], 0, 0, "mamba_out_prompt")
            x = _res_proj(x, y_s, w["w_o"][None], 0, rows_p, "mamba_out_sample")
            ssm_p.append(st_p.reshape(nb_p, C_HEADS, C_HEAD_DIM, D_STATE))
            ssm_s.append(st_s.reshape(nb_s, C_HEADS, C_HEAD_DIM, D_STATE))
            conv_p.append(last_rows(xbc, D_CONV - 1))
            conv_s.append(jnp.concatenate([state_conv[sl], xbc[rows_p:].reshape(nb_s, l_s, CONV_DIM)],
                                          axis=1)[:, -(D_CONV - 1):])
        x = _ffn(x, norm_ffn2, ffn2_w_gu, ffn2_w_down, layer)
    y_p = _ple(x, p_parts[0], norm_ple, ple_w_proj, ple_w_gate, final_norm, depth - 1, True, 0)
    y_s = _ple(x, p_parts[1], norm_ple, ple_w_proj, ple_w_gate, final_norm, depth - 1, True, rows_p)

    return (y_p.reshape(nb_p, t_p, D_MODEL), y_s.reshape(nb_s, l_s, D_MODEL),
            jnp.stack(swa_k_p), jnp.stack(swa_v_p), jnp.stack(mla_p), jnp.stack(ssm_p), jnp.stack(conv_p),
            jnp.stack(swa_k_s), jnp.stack(swa_v_s), jnp.stack(mla_s), jnp.stack(ssm_s), jnp.stack(conv_s))
```

```python
import functools
import math

import jax
import jax.numpy as jnp
from jax import lax
from jax.experimental import pallas as pl
from jax.experimental.pallas import tpu as pltpu

F32 = jnp.float32
BF16 = jnp.bfloat16

D_MODEL = 1024
D_FF = 2816
D_PLE = 256
EPS = 1e-6
NEG_INF = -1e30
PAST_LEN = 8192
PAGE_SIZE = 128
N_MIXERS = 3

A_HEADS = 16
A_KV_HEADS = 4
A_HEAD_DIM = 64
A_GROUP = A_HEADS // A_KV_HEADS
A_Q = A_HEADS * A_HEAD_DIM
A_KV = A_KV_HEADS * A_HEAD_DIM
WINDOW = 128
A_BLOCK = 128
N_BUCKETS = 32
BUCKET_MAX_DIST = 128

B_HEADS = 16
Q_LORA = 768
KV_LORA = 256
QK_NOPE = 64
QK_ROPE = 32
V_HEAD = 64
KV_ROW = KV_LORA + QK_ROPE
ROPE_THETA = 10000.0
MLA_SCALE = (QK_NOPE + QK_ROPE) ** -0.5
B_QBLOCK = 128
B_KVBLOCK = 512
B_PAGES_PER_STEP = 64

D_INNER = 2 * D_MODEL
C_HEAD_DIM = 64
C_HEADS = D_INNER // C_HEAD_DIM
C_GROUPS = 4
C_HPG = C_HEADS // C_GROUPS
D_STATE = 128
D_CONV = 4
C_BC = C_GROUPS * D_STATE
CONV_DIM = D_INNER + 2 * C_BC
SSD_CHUNK = 128

LANES = 128
SUBLANES = 8
VMEM_LIMIT = 56 * 1024 * 1024


def _cparams(*sem):
    return pltpu.CompilerParams(dimension_semantics=sem, vmem_limit_bytes=VMEM_LIMIT)


def _resident(shape, index_map):
    return pl.BlockSpec(shape, index_map, pipeline_mode=pl.Buffered(1))


def _dot(a, b):
    return jnp.dot(a, b, preferred_element_type=F32)


def _dot_nt(a, b):
    return lax.dot_general(a, b, (((1,), (1,)), ((), ())), preferred_element_type=F32)


def _dot_tn(a, b):
    return lax.dot_general(a, b, (((0,), (0,)), ((), ())), preferred_element_type=F32)


def _dot_f32(a, b):
    return jnp.dot(a, b, preferred_element_type=F32, precision=lax.Precision.HIGHEST)


def _rms(x, g):
    r = lax.rsqrt(jnp.mean(x * x, axis=-1, keepdims=True) + EPS)
    return (x * r) * g


def _silu(x):
    return x * jax.nn.sigmoid(x)


def _softplus(x):
    return jnp.maximum(x, 0.0) + jnp.log1p(jnp.exp(-jnp.abs(x)))


def _lanes(a, n):
    if n <= LANES:
        return a[:, :n]
    return jnp.concatenate([a] * (n // LANES), axis=1)


FFN_TM = 512
FFN_TF = 256


def _ple_update(x, p, g_ref, wp_ref, wg_ref):
    gate = _dot(_rms(x, g_ref[...]).astype(BF16), wg_ref[...])
    proj = _dot(p.astype(BF16), wp_ref[...])
    return x + proj * jax.nn.sigmoid(gate)


def _tile_of_pair(n_first, first_ref, second_ref):
    step = jnp.zeros(first_ref.shape, jnp.int32) + pl.program_id(0)
    return jnp.where(step < n_first, first_ref[...], second_ref[...])


def _pair_specs(n_first, block, lead=()):
    squeezed = (None,) * len(lead)
    return [pl.BlockSpec(squeezed + block, lambda i: lead + (jnp.minimum(i, n_first - 1), 0)),
            pl.BlockSpec(squeezed + block, lambda i: lead + (jnp.maximum(i - n_first, 0), 0))]


def _ffn_body(*refs, ple, proj, x_pair, n_first):
    refs = list(refs)
    x = _tile_of_pair(n_first, refs.pop(0), refs.pop(0)) if x_pair else refs.pop(0)[...]
    g_ref, wgu_ref, wd_ref = refs[:3]
    refs = refs[3:]
    if ple:
        x = _ple_update(x, _tile_of_pair(n_first, refs[0], refs[1]), *refs[2:5])
        refs = refs[5:]
    o_ref = refs[-2] if proj else refs[-1]
    h = _rms(x, g_ref[...]).astype(BF16)
    acc = jnp.zeros(x.shape, F32)
    for j in range(D_FF // FFN_TF):
        lo = j * FFN_TF
        gate = _dot(h, wgu_ref[:, lo:lo + FFN_TF])
        up = _dot(h, wgu_ref[:, D_FF + lo:D_FF + lo + FFN_TF])
        act = (_silu(gate) * up).astype(BF16)
        acc = acc + _dot(act, wd_ref[lo:lo + FFN_TF, :])
    y = x + 0.5 * acc
    o_ref[...] = y
    if proj:
        gm_ref, wm_ref, m_ref = refs[0], refs[1], refs[-1]
        m_ref[...] = _dot(_rms(y, gm_ref[...]).astype(BF16), wm_ref[...])


def _ffn(x, norm, w_gu, w_down, layer, ple=None, proj=None, n_first=None):
    x_pair = isinstance(x, tuple)
    rows = x[0].shape[0] + x[1].shape[0] if x_pair else x.shape[0]
    n_first_tiles = None if n_first is None else n_first // FFN_TM
    if x_pair:
        in_specs = _pair_specs(n_first_tiles, (FFN_TM, D_MODEL))
        args = list(x)
    else:
        in_specs = [pl.BlockSpec((FFN_TM, D_MODEL), lambda i: (i, 0))]
        args = [x]
    in_specs += [
        _resident((None, 1, D_MODEL), lambda i: (layer, 0, 0)),
        _resident((None, D_MODEL, 2 * D_FF), lambda i: (layer, 0, 0)),
        _resident((None, D_FF, D_MODEL), lambda i: (layer, 0, 0)),
    ]
    args += [norm, w_gu, w_down]
    if ple is not None:
        pl_layer = ple[4]
        in_specs += _pair_specs(n_first_tiles, (FFN_TM, D_PLE), lead=(pl_layer,)) + [
            _resident((None, 1, D_MODEL), lambda i: (pl_layer, 0, 0)),
            _resident((None, D_PLE, D_MODEL), lambda i: (pl_layer, 0, 0)),
            _resident((None, D_MODEL, D_MODEL), lambda i: (pl_layer, 0, 0)),
        ]
        args += list(ple[0]) + list(ple[1:4])
    out_specs = pl.BlockSpec((FFN_TM, D_MODEL), lambda i: (i, 0))
    out_shape = jax.ShapeDtypeStruct((rows, D_MODEL), F32)
    if proj is not None:
        m_norm, m_layer, m_w, m_wl = proj
        n = m_w.shape[2]
        in_specs += [
            _resident((None, 1, D_MODEL), lambda i: (m_layer, 0, 0)),
            _resident((None, D_MODEL, n), lambda i: (m_wl, 0, 0)),
        ]
        args += [m_norm, m_w]
        out_specs = [out_specs, pl.BlockSpec((FFN_TM, n), lambda i: (i, 0))]
        out_shape = [out_shape, jax.ShapeDtypeStruct((rows, n), F32)]
    return pl.pallas_call(
        functools.partial(_ffn_body, ple=ple is not None, proj=proj is not None, x_pair=x_pair,
                          n_first=n_first_tiles),
        grid=(rows // FFN_TM,),
        in_specs=in_specs,
        out_specs=out_specs,
        out_shape=out_shape,
        compiler_params=_cparams("parallel"),
        name="ffn" + ("_ple" if ple is not None else "") + ("_proj" if proj is not None else ""),
    )(*args)


ROW_TM = 512


def _ple_body(x_ref, p_ref, g_ref, wp_ref, wg_ref, gf_ref, o_ref, *, final):
    y = _ple_update(x_ref[...], p_ref[...], g_ref, wp_ref, wg_ref)
    if final:
        y = _rms(y, gf_ref[...])
    o_ref[...] = y


def _ple(x, p, norm, w_proj, w_gate, final_norm, layer, final, row0):
    rows = p.shape[1]
    blk0 = row0 // ROW_TM
    return pl.pallas_call(
        functools.partial(_ple_body, final=final),
        grid=(rows // ROW_TM,),
        in_specs=[
            pl.BlockSpec((ROW_TM, D_MODEL), lambda i: (i + blk0, 0)),
            pl.BlockSpec((None, ROW_TM, D_PLE), lambda i: (layer, i, 0)),
            _resident((None, 1, D_MODEL), lambda i: (layer, 0, 0)),
            _resident((None, D_PLE, D_MODEL), lambda i: (layer, 0, 0)),
            _resident((None, D_MODEL, D_MODEL), lambda i: (layer, 0, 0)),
            _resident((1, D_MODEL), lambda i: (0, 0)),
        ],
        out_specs=pl.BlockSpec((ROW_TM, D_MODEL), lambda i: (i, 0)),
        out_shape=jax.ShapeDtypeStruct((rows, D_MODEL), F32),
        compiler_params=_cparams("parallel"),
        name="ple",
    )(x, p, norm, w_proj, w_gate, final_norm)


def _norm_proj_body(x_ref, g_ref, w_ref, o_ref):
    o_ref[...] = _dot(_rms(x_ref[...], g_ref[...]).astype(BF16), w_ref[...])


def _norm_proj(x, norm, w, layer, wl, name):
    rows = x.shape[0]
    k, n = w.shape[1:]
    return pl.pallas_call(
        _norm_proj_body,
        grid=(rows // ROW_TM,),
        in_specs=[
            pl.BlockSpec((ROW_TM, k), lambda i: (i, 0)),
            _resident((None, 1, k), lambda i: (layer, 0, 0)),
            _resident((None, k, n), lambda i: (wl, 0, 0)),
        ],
        out_specs=pl.BlockSpec((ROW_TM, n), lambda i: (i, 0)),
        out_shape=jax.ShapeDtypeStruct((rows, n), F32),
        compiler_params=_cparams("parallel"),
        name=name,
    )(x, norm, w)


def _res_proj_body(x_ref, a_ref, w_ref, o_ref):
    o_ref[...] = x_ref[...] + _dot(a_ref[...].astype(BF16), w_ref[...])


def _res_proj(x, a, w, wl, row0, name):
    rows = a.shape[0]
    k, n = w.shape[1:]
    blk0 = row0 // ROW_TM
    return pl.pallas_call(
        _res_proj_body,
        grid=(rows // ROW_TM,),
        in_specs=[
            pl.BlockSpec((ROW_TM, n), lambda i: (i + blk0, 0)),
            pl.BlockSpec((ROW_TM, k), lambda i: (i, 0)),
            _resident((None, k, n), lambda i: (wl, 0, 0)),
        ],
        out_specs=pl.BlockSpec((ROW_TM, n), lambda i: (i + blk0, 0)),
        out_shape=jax.ShapeDtypeStruct(x.shape, F32),
        input_output_aliases={0: 0},
        compiler_params=_cparams("parallel"),
        name=name,
    )(x, a, w)


def _t5_bucket(dist):
    max_exact = N_BUCKETS // 2
    d = jnp.maximum(dist, 0)
    large = max_exact + (jnp.log(jnp.maximum(d, 1).astype(F32) / max_exact)
                         / math.log(BUCKET_MAX_DIST / max_exact) * (N_BUCKETS - max_exact)).astype(jnp.int32)
    large = jnp.minimum(large, N_BUCKETS - 1)
    return jnp.where(d < max_exact, d, large)


def _bias_body(tab_ref, bm_ref, valid_ref, o_ref):
    h = pl.program_id(0)
    bm = bm_ref[...]
    acc = jnp.zeros(bm.shape, F32)
    for b in range(N_BUCKETS):
        acc = jnp.where(bm == b, tab_ref[b, h], acc)
    o_ref[...] = jnp.where(valid_ref[...] > 0, acc, NEG_INF)


def _rel_bias(table, bucket_map, valid):
    r, c = bucket_map.shape
    return pl.pallas_call(
        _bias_body,
        grid=(A_HEADS,),
        in_specs=[pl.BlockSpec(memory_space=pltpu.SMEM), pl.BlockSpec((r, c), lambda h: (0, 0)),
                  pl.BlockSpec((r, c), lambda h: (0, 0))],
        out_specs=pl.BlockSpec((None, r, c), lambda h: (h, 0, 0)),
        out_shape=jax.ShapeDtypeStruct((A_HEADS, r, c), F32),
        compiler_params=_cparams("parallel"),
        name="rel_bias",
    )(table, bucket_map, valid.astype(jnp.int32))


def _group_sinks(sinks, t):
    return jnp.broadcast_to(sinks.reshape(A_KV_HEADS, A_GROUP, 1, 1),
                            (A_KV_HEADS, A_GROUP, t, LANES)).reshape(A_KV_HEADS, A_GROUP * t, LANES)


SWA_QROWS = A_BLOCK


def _swa_prompt_body(sink_ref, q_ref, kp_ref, ko_ref, vp_ref, vo_ref, bias_ref, x_ref, wo_ref, o_ref, o_scr):
    scale = A_HEAD_DIM ** -0.5
    for g in range(A_KV_HEADS):
        gs = slice(g * A_HEAD_DIM, (g + 1) * A_HEAD_DIM)
        kg = jnp.concatenate([kp_ref[:, gs], ko_ref[:, gs]], axis=0).astype(BF16)
        vg = jnp.concatenate([vp_ref[:, gs], vo_ref[:, gs]], axis=0).astype(BF16)
        for h in range(g * A_GROUP, (g + 1) * A_GROUP):
            hs = slice(h * A_HEAD_DIM, (h + 1) * A_HEAD_DIM)
            s = _dot_nt(q_ref[:, hs].astype(BF16), kg) * scale + bias_ref[h]
            sink = sink_ref[h]
            m = jnp.maximum(jnp.max(s, axis=-1, keepdims=True), sink)
            e = jnp.exp(s - m)
            den = jnp.sum(e, axis=-1, keepdims=True) + jnp.exp(sink - m)
            o_scr[:, hs] = _dot((e / den).astype(BF16), vg)
    o_ref[...] = x_ref[...] + _dot(o_scr[...].astype(BF16), wo_ref[...])


def _swa_prompt(x, qkv, bias, sinks, w_o, wl, nb_p, t_p):
    nstep = t_p // SWA_QROWS
    blocks_per_step = SWA_QROWS // A_BLOCK
    kcol = A_Q // A_KV
    vcol = kcol + 1

    def own(b, n):
        return b * nstep + n

    def prev(b, n):
        return b * nstep * blocks_per_step + jnp.maximum(n * blocks_per_step - 1, 0)

    return pl.pallas_call(
        _swa_prompt_body,
        grid=(nb_p, nstep),
        in_specs=[
            pl.BlockSpec(memory_space=pltpu.SMEM),
            pl.BlockSpec((SWA_QROWS, A_Q), lambda b, n: (own(b, n), 0)),
            pl.BlockSpec((A_BLOCK, A_KV), lambda b, n: (prev(b, n), kcol)),
            pl.BlockSpec((SWA_QROWS, A_KV), lambda b, n: (own(b, n), kcol)),
            pl.BlockSpec((A_BLOCK, A_KV), lambda b, n: (prev(b, n), vcol)),
            pl.BlockSpec((SWA_QROWS, A_KV), lambda b, n: (own(b, n), vcol)),
            pl.BlockSpec((None, A_HEADS, SWA_QROWS, A_BLOCK + SWA_QROWS),
                         lambda b, n: (jnp.minimum(n, 1), 0, 0, 0)),
            pl.BlockSpec((SWA_QROWS, D_MODEL), lambda b, n: (own(b, n), 0)),
            _resident((None, A_Q, D_MODEL), lambda b, n: (wl, 0, 0)),
        ],
        out_specs=pl.BlockSpec((SWA_QROWS, D_MODEL), lambda b, n: (own(b, n), 0)),
        out_shape=jax.ShapeDtypeStruct(x.shape, F32),
        scratch_shapes=[pltpu.VMEM((SWA_QROWS, A_Q), F32)],
        input_output_aliases={7: 0},
        compiler_params=_cparams("parallel", "arbitrary"),
        name="swa_prompt",
    )(sinks, qkv, qkv, qkv, qkv, qkv, bias, x, w_o)


SWA_S_NB = 8


def _slide_window(buf_t, new, l_s):
    d, w = buf_t.shape
    assert w == LANES and d <= LANES
    tile = jnp.concatenate([jnp.zeros((w - l_s, LANES), F32),
                            jnp.concatenate([new, jnp.zeros((l_s, LANES - d), F32)], axis=1)], axis=0)
    new_t = tile.T[:d]
    lane = lax.broadcasted_iota(jnp.int32, (d, w), 1)
    return jnp.where(lane < w - l_s, pltpu.roll(buf_t, w - l_s, axis=1), new_t)


def _swa_sample_body(qkv_ref, kt_ref, vt_ref, bb_ref, bn_ref, sink_ref, o_ref, kto_ref, vto_ref, *, l_s):
    scale = A_HEAD_DIM ** -0.5

    def one(b, carry):
        rows = pl.ds(pl.multiple_of(b * l_s, l_s), l_s)
        for g in range(A_KV_HEADS):
            heads = range(g * A_GROUP, (g + 1) * A_GROUP)
            qg = jnp.concatenate([qkv_ref[rows, h * A_HEAD_DIM:(h + 1) * A_HEAD_DIM] for h in heads], axis=0)
            qg = qg.astype(BF16)
            kn32 = qkv_ref[rows, A_Q + g * A_HEAD_DIM:A_Q + (g + 1) * A_HEAD_DIM]
            vn32 = qkv_ref[rows, A_Q + A_KV + g * A_HEAD_DIM:A_Q + A_KV + (g + 1) * A_HEAD_DIM]
            kto_ref[b, g] = _slide_window(kt_ref[b, g], kn32, l_s)
            vto_ref[b, g] = _slide_window(vt_ref[b, g], vn32, l_s)
            kn = kn32.astype(BF16)
            vn = vn32.astype(BF16)
            sb = _dot(qg, kt_ref[b, g].astype(BF16)) * scale + bb_ref[g]
            sn = _dot_nt(qg, kn) * scale + bn_ref[g]
            sink = sink_ref[g][:, :1]
            m = jnp.maximum(jnp.maximum(jnp.max(sb, axis=-1, keepdims=True), jnp.max(sn, axis=-1, keepdims=True)), sink)
            eb = jnp.exp(sb - m)
            en = jnp.exp(sn - m)
            den = jnp.sum(eb, axis=-1, keepdims=True) + jnp.sum(en, axis=-1, keepdims=True) + jnp.exp(sink - m)
            og = _dot_nt((eb / den).astype(BF16), vt_ref[b, g].astype(BF16)) + _dot((en / den).astype(BF16), vn)
            for i, h in enumerate(heads):
                o_ref[rows, h * A_HEAD_DIM:(h + 1) * A_HEAD_DIM] = og[i * l_s:(i + 1) * l_s]
        return carry

    lax.fori_loop(0, SWA_S_NB, one, 0, unroll=True)


def _swa_sample(qkv, kt_buf, vt_buf, bias, sinks, sl, row0, nb_s, l_s):
    w_buf = kt_buf.shape[-1]
    r = A_GROUP * l_s
    tm = SWA_S_NB * l_s
    blk0 = row0 // tm
    bias_g = bias.reshape(A_KV_HEADS, r, w_buf + l_s)
    sink_g = _group_sinks(sinks, l_s)
    buf_spec = pl.BlockSpec((None, SWA_S_NB, A_KV_HEADS, A_HEAD_DIM, w_buf), lambda i: (sl, i, 0, 0, 0))
    new_spec = pl.BlockSpec((SWA_S_NB, A_KV_HEADS, A_HEAD_DIM, w_buf), lambda i: (i, 0, 0, 0))
    new_shape = jax.ShapeDtypeStruct((nb_s, A_KV_HEADS, A_HEAD_DIM, w_buf), F32)
    const3 = lambda i: (0, 0, 0)
    return pl.pallas_call(
        functools.partial(_swa_sample_body, l_s=l_s),
        grid=(nb_s // SWA_S_NB,),
        in_specs=[
            pl.BlockSpec((tm, A_Q + 2 * A_KV), lambda i: (i + blk0, 0)),
            buf_spec,
            buf_spec,
            _resident((A_KV_HEADS, r, w_buf), const3),
            _resident((A_KV_HEADS, r, l_s), const3),
            _resident((A_KV_HEADS, r, LANES), const3),
        ],
        out_specs=[pl.BlockSpec((tm, A_Q), lambda i: (i, 0)), new_spec, new_spec],
        out_shape=[jax.ShapeDtypeStruct((nb_s * l_s, A_Q), F32), new_shape, new_shape],
        compiler_params=_cparams("parallel"),
        name="swa_sample",
    )(qkv, kt_buf, vt_buf, bias_g[:, :, :w_buf], bias_g[:, :, w_buf:], sink_g)


B_IN_COLS = Q_LORA + KV_LORA + 2 * LANES
B_KPE = Q_LORA + KV_LORA
B_KPE_ROT = B_KPE + LANES


def _mla_proj_body(x_ref, g_ref, win_ref, qn_ref, kvn_ref, wnope_ref, wpe_ref, wper_ref, wuk_ref,
                   cos_ref, sin_ref, kv_ref, q_ref, *maybe_kvb_ref, sample):
    tm = x_ref.shape[0]
    h = _rms(x_ref[...], g_ref[...]).astype(BF16)
    c = _dot(h, win_ref[...])
    cos = cos_ref[...]
    sin = sin_ref[...]
    kv_lat = _rms(c[:, Q_LORA:B_KPE], kvn_ref[...])
    kv_rope = (c[:, B_KPE:B_KPE + QK_ROPE] * cos[:, :QK_ROPE]
               + c[:, B_KPE_ROT:B_KPE_ROT + QK_ROPE] * sin[:, :QK_ROPE])
    kv_ref[:, :KV_LORA] = kv_lat
    kv_ref[:, KV_LORA:] = kv_rope
    for kvb_ref in maybe_kvb_ref:
        kvb_ref[:, :KV_LORA] = kv_lat.astype(BF16)
        kvb_ref[:, KV_LORA:] = kv_rope.astype(BF16)
    qn = _rms(c[:, :Q_LORA], qn_ref[...]).astype(BF16)
    qnope = _dot(qn, wnope_ref[...])
    n_rep = B_HEADS * QK_ROPE // LANES
    qpe = (_dot(qn, wpe_ref[...]) * jnp.concatenate([cos] * n_rep, axis=1)
           + _dot(qn, wper_ref[...]) * jnp.concatenate([sin] * n_rep, axis=1))
    for hd in range(B_HEADS):
        qlat = _dot(qnope[:, hd * QK_NOPE:(hd + 1) * QK_NOPE].astype(BF16), wuk_ref[hd])
        qrope = qpe[:, hd * QK_ROPE:(hd + 1) * QK_ROPE]
        if sample:
            l_s = q_ref.shape[1] // B_HEADS
            rs = slice(hd * l_s, (hd + 1) * l_s)
            q_ref[:, rs, :KV_LORA] = qlat.reshape(tm // l_s, l_s, KV_LORA)
            q_ref[:, rs, KV_LORA:] = qrope.reshape(tm // l_s, l_s, QK_ROPE)
        else:
            for s in range(tm // B_QBLOCK):
                ts = slice(s * B_QBLOCK, (s + 1) * B_QBLOCK)
                q_ref[s, hd, :, :KV_LORA] = qlat[ts].astype(BF16)
                q_ref[s, hd, :, KV_LORA:] = qrope[ts].astype(BF16)


def _mla_proj(x, norm, layer, w, cos, sin, row0, rows, l_s):
    sample = l_s is not None
    tm = ROW_TM
    blk0 = row0 // tm
    if sample:
        q_shape = jax.ShapeDtypeStruct((rows // l_s, B_HEADS * l_s, KV_ROW), F32)
        q_spec = pl.BlockSpec((tm // l_s, B_HEADS * l_s, KV_ROW), lambda i: (i, 0, 0))
    else:
        q_shape = jax.ShapeDtypeStruct((rows // B_QBLOCK, B_HEADS, B_QBLOCK, KV_ROW), BF16)
        q_spec = pl.BlockSpec((tm // B_QBLOCK, B_HEADS, B_QBLOCK, KV_ROW), lambda i: (i, 0, 0, 0))
    const2 = lambda i: (0, 0)
    kv_spec = pl.BlockSpec((tm, KV_ROW), lambda i: (i, 0))
    out_specs = [kv_spec, q_spec]
    out_shape = [jax.ShapeDtypeStruct((rows, KV_ROW), F32), q_shape]
    if not sample:
        out_specs.append(kv_spec)
        out_shape.append(jax.ShapeDtypeStruct((rows, KV_ROW), BF16))
    return pl.pallas_call(
        functools.partial(_mla_proj_body, sample=sample),
        grid=(rows // tm,),
        in_specs=[
            pl.BlockSpec((tm, D_MODEL), lambda i: (i + blk0, 0)),
            _resident((None, 1, D_MODEL), lambda i: (layer, 0, 0)),
            _resident((D_MODEL, B_IN_COLS), const2),
            _resident((1, Q_LORA), const2),
            _resident((1, KV_LORA), const2),
            _resident((Q_LORA, B_HEADS * QK_NOPE), const2),
            _resident((Q_LORA, B_HEADS * QK_ROPE), const2),
            _resident((Q_LORA, B_HEADS * QK_ROPE), const2),
            _resident((B_HEADS, QK_NOPE, KV_LORA), lambda i: (0, 0, 0)),
            pl.BlockSpec((tm, LANES), lambda i: (i + blk0, 0)),
            pl.BlockSpec((tm, LANES), lambda i: (i + blk0, 0)),
        ],
        out_specs=out_specs,
        out_shape=out_shape,
        compiler_params=_cparams("parallel"),
        name="mla_proj_sample" if sample else "mla_proj_prompt",
    )(x, norm, w["w_in"], w["q_norm"], w["kv_norm"], w["w_nope"], w["w_pe"], w["w_pe_rot"], w["w_uk"], cos, sin)


def _softmax_stats(s, m_scr, l_scr):
    c = MLA_SCALE * math.log2(math.e)
    m_prev = m_scr[...]
    m_new = jnp.maximum(m_prev, jnp.max(s, axis=-1, keepdims=True))
    alpha = jnp.exp2((m_prev - m_new) * c)
    p = jnp.exp2((s - _lanes(m_new, s.shape[1])) * c)
    l_scr[...] = alpha * l_scr[...] + jnp.sum(p, axis=-1, keepdims=True)
    m_scr[...] = m_new
    return alpha, p.astype(BF16)


def _softmax_accumulate(s, pv, m_scr, l_scr, acc_scr):
    alpha, p = _softmax_stats(s, m_scr, l_scr)
    acc_scr[...] = acc_scr[...] * _lanes(alpha, KV_LORA) + pv(p)


def _softmax_update(q, kvb, m_scr, l_scr, acc_scr, valid=None):
    s = _dot_nt(q, kvb)
    if valid is not None:
        s = jnp.where(valid, s, NEG_INF)
    _softmax_accumulate(s, lambda p: _dot(p, kvb[:, :KV_LORA]), m_scr, l_scr, acc_scr)


def _softmax_init(m_scr, l_scr, acc_scr):
    m_scr[...] = jnp.full(m_scr.shape, NEG_INF, F32)
    l_scr[...] = jnp.zeros(l_scr.shape, F32)
    acc_scr[...] = jnp.zeros(acc_scr.shape, F32)


def _mla_prompt_body(q_ref, kv_ref, x_ref, wuv_ref, wo_ref, o_ref, s0_scr, s1_scr, m_scr, l_scr, acc_scr, u_scr):
    qi = pl.program_id(1)
    r = B_HEADS * B_QBLOCK
    n_off = qi // (B_KVBLOCK // B_QBLOCK)
    q = q_ref[...].reshape(r, KV_ROW)

    def kv_block(k):
        return kv_ref[pl.ds(pl.multiple_of(k * B_KVBLOCK, B_KVBLOCK), B_KVBLOCK), :]

    def update(s, k):
        kvb = kv_block(k)
        _softmax_accumulate(s, lambda p: _dot(p, kvb[:, :KV_LORA]), m_scr, l_scr, acc_scr)

    def step(s_cur, s_next, k):
        s_next[...] = _dot_nt(q, kv_block(k + 1))
        update(s_cur[...], k)

    def last(s_cur):
        tok = lax.broadcasted_iota(jnp.int32, (r, B_KVBLOCK), 0) & (B_QBLOCK - 1)
        key = lax.broadcasted_iota(jnp.int32, (r, B_KVBLOCK), 1)
        valid = key - tok <= qi * B_QBLOCK - n_off * B_KVBLOCK
        update(jnp.where(valid, s_cur[...], NEG_INF), n_off)
        o = acc_scr[...] * _lanes(1.0 / l_scr[...], KV_LORA)
        for hd in range(B_HEADS):
            oh = o[hd * B_QBLOCK:(hd + 1) * B_QBLOCK].astype(BF16)
            u_scr[:, hd * V_HEAD:(hd + 1) * V_HEAD] = _dot(oh, wuv_ref[hd])
        o_ref[...] = x_ref[...] + _dot(u_scr[...].astype(BF16), wo_ref[...])

    _softmax_init(m_scr, l_scr, acc_scr)
    s0_scr[...] = _dot_nt(q, kv_block(0))

    def two_steps(j, carry):
        step(s0_scr, s1_scr, 2 * j)
        step(s1_scr, s0_scr, 2 * j + 1)
        return carry

    lax.fori_loop(0, n_off // 2, two_steps, 0)
    odd = n_off % 2 == 1

    @pl.when(odd)
    def _():
        step(s0_scr, s1_scr, n_off - 1)
        last(s1_scr)

    @pl.when(jnp.logical_not(odd))
    def _():
        last(s0_scr)


def _mla_prompt(x, q_cat, kv_bf16, w, nb_p, t_p):
    nq = t_p // B_QBLOCK
    r = B_HEADS * B_QBLOCK
    return pl.pallas_call(
        _mla_prompt_body,
        grid=(nb_p, nq),
        in_specs=[
            pl.BlockSpec((None, B_HEADS, B_QBLOCK, KV_ROW), lambda b, qi: (b * nq + qi, 0, 0, 0)),
            pl.BlockSpec((t_p, KV_ROW), lambda b, qi: (b, 0)),
            pl.BlockSpec((B_QBLOCK, D_MODEL), lambda b, qi: (b * nq + qi, 0)),
            _resident((B_HEADS, KV_LORA, V_HEAD), lambda b, qi: (0, 0, 0)),
            _resident((B_HEADS * V_HEAD, D_MODEL), lambda b, qi: (0, 0)),
        ],
        out_specs=pl.BlockSpec((B_QBLOCK, D_MODEL), lambda b, qi: (b * nq + qi, 0)),
        out_shape=jax.ShapeDtypeStruct(x.shape, F32),
        scratch_shapes=[
            pltpu.VMEM((r, B_KVBLOCK), F32),
            pltpu.VMEM((r, B_KVBLOCK), F32),
            pltpu.VMEM((r, LANES), F32),
            pltpu.VMEM((r, LANES), F32),
            pltpu.VMEM((r, KV_LORA), F32),
            pltpu.VMEM((B_QBLOCK, B_HEADS * V_HEAD), F32),
        ],
        input_output_aliases={2: 0},
        compiler_params=_cparams("parallel", "arbitrary"),
        name="mla_prompt",
    )(q_cat, kv_bf16, x, w["w_uv"], w["w_o"])


def _mla_sample_body(pt_ref, q_ref, cache_ref, kvn_ref, o_ref, buf, sem, m_scr, l_scr, acc_scr, acct_scr,
                     *, l_s, n_pages, cl):
    b = pl.program_id(0)
    slot = b % 2
    r = q_ref.shape[0]
    chunk = math.gcd(n_pages, B_PAGES_PER_STEP)
    assert r == LANES

    def page_copy(batch, slot_, k):
        page = pt_ref[batch * n_pages + k]
        return pltpu.make_async_copy(cache_ref.at[cl, page], buf.at[slot_, k], sem.at[slot_, k])

    def start_pages(batch, slot_):
        for k in range(n_pages):
            page_copy(batch, slot_, k).start(priority=k % 2)

    @pl.when(b == 0)
    def _():
        start_pages(0, 0)

    @pl.when(b + 1 < pl.num_programs(0))
    def _():
        start_pages(b + 1, 1 - slot)

    _softmax_init(m_scr, l_scr, acct_scr)
    q = q_ref[...].astype(BF16)
    for c in range(n_pages // chunk):
        ks = range(c * chunk, (c + 1) * chunk)
        for k in ks:
            page_copy(b, slot, k).wait()
        kt = jnp.concatenate([buf[slot, k].astype(BF16) for k in ks], axis=1)
        alpha, p = _softmax_stats(_dot(q, kt), m_scr, l_scr)
        alpha_t = jnp.concatenate([alpha.T] * (KV_LORA // LANES), axis=0)
        acct_scr[...] = acct_scr[...] * alpha_t + _dot_nt(kt[:KV_LORA, :], p)

    acc_scr[...] = acct_scr[...].T
    tok = lax.broadcasted_iota(jnp.int32, (r, l_s), 0) % l_s
    key = lax.broadcasted_iota(jnp.int32, (r, l_s), 1)
    _softmax_update(q, kvn_ref[...].astype(BF16), m_scr, l_scr, acc_scr, valid=key <= tok)
    o_ref[...] = acc_scr[...] * _lanes(1.0 / l_scr[...], KV_LORA)


def _mla_sample(q_cat, kv_rows, cache_t, page_table, cl, row0, nb_s, l_s):
    n_pages = page_table.shape[1]
    r = B_HEADS * l_s
    blk0 = row0 // l_s
    return pl.pallas_call(
        functools.partial(_mla_sample_body, l_s=l_s, n_pages=n_pages, cl=cl),
        grid_spec=pltpu.PrefetchScalarGridSpec(
            num_scalar_prefetch=1,
            grid=(nb_s,),
            in_specs=[
                pl.BlockSpec((None, r, KV_ROW), lambda b, pt: (b, 0, 0)),
                pl.BlockSpec(memory_space=pl.ANY),
                pl.BlockSpec((l_s, KV_ROW), lambda b, pt: (b + blk0, 0)),
            ],
            out_specs=pl.BlockSpec((None, r, KV_LORA), lambda b, pt: (b, 0, 0)),
            scratch_shapes=[
                pltpu.VMEM((2, n_pages, KV_ROW, PAGE_SIZE), F32),
                pltpu.SemaphoreType.DMA((2, n_pages)),
                pltpu.VMEM((r, LANES), F32),
                pltpu.VMEM((r, LANES), F32),
                pltpu.VMEM((r, KV_LORA), F32),
                pltpu.VMEM((KV_LORA, r), F32),
            ],
        ),
        out_shape=jax.ShapeDtypeStruct((nb_s, r, KV_LORA), F32),
        compiler_params=_cparams("arbitrary"),
        name="mla_sample",
    )(page_table.reshape(-1), q_cat, cache_t, kv_rows)


def _mla_out_sample_body(o_ref, x_ref, wuv_ref, wo_ref, out_ref, u_scr, *, l_s):
    nb = o_ref.shape[0]
    for hd in range(B_HEADS):
        oh = o_ref[:, hd * l_s:(hd + 1) * l_s, :].reshape(nb * l_s, KV_LORA).astype(BF16)
        u_scr[:, hd * V_HEAD:(hd + 1) * V_HEAD] = _dot(oh, wuv_ref[hd])
    out_ref[...] = x_ref[...] + _dot(u_scr[...].astype(BF16), wo_ref[...])


def _mla_out_sample(x, o_lat, w, row0, l_s):
    nb_s = o_lat.shape[0]
    nb = ROW_TM // l_s
    blk0 = row0 // ROW_TM
    return pl.pallas_call(
        functools.partial(_mla_out_sample_body, l_s=l_s),
        grid=(nb_s // nb,),
        in_specs=[
            pl.BlockSpec((nb, B_HEADS * l_s, KV_LORA), lambda i: (i, 0, 0)),
            pl.BlockSpec((ROW_TM, D_MODEL), lambda i: (i + blk0, 0)),
            _resident((B_HEADS, KV_LORA, V_HEAD), lambda i: (0, 0, 0)),
            _resident((B_HEADS * V_HEAD, D_MODEL), lambda i: (0, 0)),
        ],
        out_specs=pl.BlockSpec((ROW_TM, D_MODEL), lambda i: (i + blk0, 0)),
        out_shape=jax.ShapeDtypeStruct(x.shape, F32),
        scratch_shapes=[pltpu.VMEM((ROW_TM, B_HEADS * V_HEAD), F32)],
        input_output_aliases={1: 0},
        compiler_params=_cparams("parallel"),
        name="mla_out_sample",
    )(o_lat, x, w["w_uv"], w["w_o"])


C_IN_COLS = D_INNER + CONV_DIM + LANES


def _mamba_in_body(x_ref, g_ref, w_ref, z_ref, xbc_ref, dt_ref):
    h = _rms(x_ref[...], g_ref[...]).astype(BF16)
    z_ref[...] = _dot(h, w_ref[:, :D_INNER])
    xbc_ref[...] = _dot(h, w_ref[:, D_INNER:D_INNER + CONV_DIM])
    dt_ref[...] = _dot(h, w_ref[:, D_INNER + CONV_DIM:])


def _mamba_in(x, norm, layer, w_in):
    rows = x.shape[0]
    return pl.pallas_call(
        _mamba_in_body,
        grid=(rows // ROW_TM,),
        in_specs=[
            pl.BlockSpec((ROW_TM, D_MODEL), lambda i: (i, 0)),
            _resident((None, 1, D_MODEL), lambda i: (layer, 0, 0)),
            _resident((D_MODEL, C_IN_COLS), lambda i: (0, 0)),
        ],
        out_specs=[
            pl.BlockSpec((ROW_TM, D_INNER), lambda i: (i, 0)),
            pl.BlockSpec((ROW_TM, CONV_DIM), lambda i: (i, 0)),
            pl.BlockSpec((ROW_TM, LANES), lambda i: (i, 0)),
        ],
        out_shape=[
            jax.ShapeDtypeStruct((rows, D_INNER), F32),
            jax.ShapeDtypeStruct((rows, CONV_DIM), F32),
            jax.ShapeDtypeStruct((rows, LANES), F32),
        ],
        compiler_params=_cparams("parallel"),
        name="mamba_in",
    )(x, norm, w_in)


def _causal_conv(ext_scr, u, cw_ref, cb_ref):
    t = u.shape[0]
    ext_scr[SUBLANES:SUBLANES + t, :] = u
    acc = cb_ref[...] + cw_ref[D_CONV - 1:D_CONV, :] * u
    for k in range(D_CONV - 1):
        lo = SUBLANES - (D_CONV - 1) + k
        acc = acc + cw_ref[k:k + 1, :] * ext_scr[lo:lo + t, :]
    return acc


def _gated_norm(y, z, norm_g):
    yz = y * _silu(z)
    gw = D_INNER // C_GROUPS
    parts = []
    for g in range(C_GROUPS):
        part = yz[:, g * gw:(g + 1) * gw]
        parts.append(part * lax.rsqrt(jnp.mean(part * part, axis=-1, keepdims=True) + EPS))
    return jnp.concatenate(parts, axis=1) * norm_g


def _ssd_prompt_body(xbc_ref, dt_ref, z_ref, cw_ref, cb_ref, dtb_ref, alog_ref, dskip_ref, ng_ref, yn_ref, st_ref,
                     ext_scr, ht_scr, y_ref):
    c = pl.program_id(1)
    t = SSD_CHUNK

    @pl.when(c == 0)
    def _():
        ext_scr[0:SUBLANES, :] = jnp.zeros((SUBLANES, CONV_DIM), F32)
        ht_scr[...] = jnp.zeros(ht_scr.shape, F32)

    u = xbc_ref[...]
    act = _silu(_causal_conv(ext_scr, u, cw_ref, cb_ref))
    ext_scr[0:SUBLANES, :] = u[t - SUBLANES:, :]

    dt = _softplus(dt_ref[...] + dtb_ref[...])
    a = -jnp.exp(alog_ref[...])
    row = lax.broadcasted_iota(jnp.int32, (t, t), 0)
    col = lax.broadcasted_iota(jnp.int32, (t, t), 1)
    causal = col <= row
    acs = _dot_f32(causal.astype(F32), dt * a)
    acs_t = acs.T
    dt_t = dt.T
    to_end = jnp.exp(acs[t - 1:t, :] - acs) * dt
    gw = C_HPG * C_HEAD_DIM
    heads_per_vreg = LANES // C_HEAD_DIM
    assert heads_per_vreg == 2 and t == LANES
    first = lax.broadcasted_iota(jnp.int32, (t, LANES), 1) < C_HEAD_DIM
    for g in range(C_GROUPS):
        bg = act[:, D_INNER + g * D_STATE:D_INNER + (g + 1) * D_STATE]
        cg = act[:, D_INNER + C_BC + g * D_STATE:D_INNER + C_BC + (g + 1) * D_STATE].astype(BF16)
        cb = _dot_nt(cg, bg.astype(BF16))
        bg_t = bg.T.astype(BF16)
        y_off = _dot(cg, ht_scr[:, g * gw:(g + 1) * gw].astype(BF16))
        for jp in range(C_HPG // heads_per_vreg):
            h0 = g * C_HPG + heads_per_vreg * jp
            ps = slice(h0 * C_HEAD_DIM, h0 * C_HEAD_DIM + LANES)
            ms, a_cols, te_cols = [], [], []
            for h in (h0, h0 + 1):
                a_col = jnp.broadcast_to(acs[:, h:h + 1], (t, t))
                a_row = jnp.broadcast_to(acs_t[h:h + 1, :], (t, t))
                dt_row = jnp.broadcast_to(dt_t[h:h + 1, :], (t, t))
                ms.append(cb * jnp.exp(jnp.where(causal, a_col - a_row, -jnp.inf)) * dt_row)
                a_cols.append(a_col)
                te_cols.append(jnp.broadcast_to(to_end[:, h:h + 1], (t, LANES)))
            x2 = act[:, ps]
            lhs = jnp.concatenate(ms, axis=1).astype(BF16)
            rhs = jnp.concatenate([jnp.where(first, x2, 0.0), jnp.where(first, 0.0, x2)], axis=0).astype(BF16)
            a_col2 = jnp.where(first, a_cols[0], a_cols[1])
            y_ref[:, ps] = (_dot(lhs, rhs)
                            + y_off[:, jp * LANES:(jp + 1) * LANES] * jnp.exp(a_col2)
                            + dskip_ref[:, ps] * x2)
            xw = (x2 * jnp.where(first, te_cols[0], te_cols[1])).astype(BF16)
            ht_scr[:, ps] = jnp.exp(a_col2[t - 1:t, :]) * ht_scr[:, ps] + _dot(bg_t, xw)

    yn_ref[...] = _gated_norm(y_ref[...], z_ref[...], ng_ref[...]).astype(BF16)

    @pl.when(c == pl.num_programs(1) - 1)
    def _():
        st_ref[...] = ht_scr[...].T


def _ssd_prompt(xbc, dt, z, w, nb_p, t_p):
    nc = t_p // SSD_CHUNK
    const2 = lambda b, c: (0, 0)
    rows = lambda b, c: (b * nc + c, 0)
    return pl.pallas_call(
        _ssd_prompt_body,
        grid=(nb_p, nc),
        in_specs=[
            pl.BlockSpec((SSD_CHUNK, CONV_DIM), rows),
            pl.BlockSpec((SSD_CHUNK, LANES), rows),
            pl.BlockSpec((SSD_CHUNK, D_INNER), rows),
            _resident((D_CONV, CONV_DIM), const2),
            _resident((1, CONV_DIM), const2),
            _resident((1, LANES), const2),
            _resident((1, LANES), const2),
            _resident((1, D_INNER), const2),
            _resident((1, D_INNER), const2),
        ],
        out_specs=[
            pl.BlockSpec((SSD_CHUNK, D_INNER), rows),
            pl.BlockSpec((None, D_INNER, D_STATE), lambda b, c: (b, 0, 0)),
        ],
        out_shape=[
            jax.ShapeDtypeStruct((nb_p * t_p, D_INNER), BF16),
            jax.ShapeDtypeStruct((nb_p, D_INNER, D_STATE), F32),
        ],
        scratch_shapes=[
            pltpu.VMEM((SUBLANES + SSD_CHUNK, CONV_DIM), F32),
            pltpu.VMEM((D_STATE, D_INNER), F32),
            pltpu.VMEM((SSD_CHUNK, D_INNER), F32),
        ],
        compiler_params=_cparams("parallel", "arbitrary"),
        name="ssd_prompt",
    )(xbc, dt, z, w["conv_w"], w["conv_b"], w["dt_bias"], w["a_log"], w["d_skip"], w["norm_g"])


def _ssd_sample_body(xbc_ref, dtx_ref, z_ref, cprev_ref, h0_ref, cw_ref, cb_ref, dtb_ref, alog_ref, dskip_ref, ng_ref,
                     y_ref, st_ref, ext_scr, *, l_s):
    u = xbc_ref[...]
    ext_scr[SUBLANES - (D_CONV - 1):SUBLANES, :] = cprev_ref[...]
    act = _silu(_causal_conv(ext_scr, u, cw_ref, cb_ref))
    xs = act[:, :D_INNER]

    dt = _softplus(dtx_ref[...] + dtb_ref[...])
    a = -jnp.exp(alog_ref[...])
    row = lax.broadcasted_iota(jnp.int32, (l_s, l_s), 0)
    col = lax.broadcasted_iota(jnp.int32, (l_s, l_s), 1)
    acs = _dot_f32((col <= row).astype(F32), dt * a)
    last = acs[l_s - 1:l_s, :]
    xdt = xs * dt
    xw = xdt * jnp.exp(last - acs)
    decay_col = jnp.broadcast_to(jnp.exp(last), (LANES, D_INNER)).T
    gw = C_HPG * C_HEAD_DIM
    lrow = lax.broadcasted_iota(jnp.int32, (l_s, gw), 0)
    for g in range(C_GROUPS):
        gs = slice(g * gw, (g + 1) * gw)
        bg = act[:, D_INNER + g * D_STATE:D_INNER + (g + 1) * D_STATE]
        cg = act[:, D_INNER + C_BC + g * D_STATE:D_INNER + C_BC + (g + 1) * D_STATE]
        cb = _dot_nt(cg.astype(BF16), bg.astype(BF16))
        h0 = h0_ref[gs, :]
        y = _dot_nt(cg.astype(BF16), h0.astype(BF16)) * jnp.exp(acs[:, gs])
        for s in range(l_s):
            seg = acs[:, gs] - acs[s:s + 1, gs]
            w_s = jnp.exp(jnp.where(lrow >= s, seg, -jnp.inf)) * jnp.broadcast_to(cb[:, s:s + 1], (l_s, gw))
            y = y + w_s * xdt[s:s + 1, gs]
        y_ref[:, gs] = y + dskip_ref[:, gs] * xs[:, gs]
        st_ref[gs, :] = decay_col[gs, :] * h0 + _dot_tn(xw[:, gs].astype(BF16), bg.astype(BF16))
    y_ref[...] = _gated_norm(y_ref[...], z_ref[...], ng_ref[...])


def _ssd_sample(xbc, dtx, z, conv_prev, h0, sl, w, row0, nb_s, l_s):
    blk0 = row0 // l_s
    const2 = lambda b: (0, 0)
    return pl.pallas_call(
        functools.partial(_ssd_sample_body, l_s=l_s),
        grid=(nb_s,),
        in_specs=[
            pl.BlockSpec((l_s, CONV_DIM), lambda b: (b + blk0, 0)),
            pl.BlockSpec((l_s, D_INNER), lambda b: (b, 0)),
            pl.BlockSpec((l_s, D_INNER), lambda b: (b + blk0, 0)),
            pl.BlockSpec((None, None, D_CONV - 1, CONV_DIM), lambda b: (sl, b, 0, 0)),
            pl.BlockSpec((None, None, D_INNER, D_STATE), lambda b: (sl, b, 0, 0)),
            _resident((D_CONV, CONV_DIM), const2),
            _resident((1, CONV_DIM), const2),
            _resident((1, D_INNER), const2),
            _resident((1, D_INNER), const2),
            _resident((1, D_INNER), const2),
            _resident((1, D_INNER), const2),
        ],
        out_specs=[
            pl.BlockSpec((l_s, D_INNER), lambda b: (b, 0)),
            pl.BlockSpec((None, D_INNER, D_STATE), lambda b: (b, 0, 0)),
        ],
        out_shape=[
            jax.ShapeDtypeStruct((nb_s * l_s, D_INNER), F32),
            jax.ShapeDtypeStruct((nb_s, D_INNER, D_STATE), F32),
        ],
        scratch_shapes=[pltpu.VMEM((SUBLANES + l_s, CONV_DIM), F32)],
        compiler_params=_cparams("parallel"),
        name="ssd_sample",
    )(xbc, dtx, z, conv_prev, h0, w["conv_w"], w["conv_b"], w["dt_bias_x"], w["a_log_x"], w["d_skip"], w["norm_g"])


def _rope_tables(pos):
    half = QK_ROPE // 2
    inv = jnp.power(ROPE_THETA, -jnp.arange(half, dtype=F32) / half)
    ang = pos.astype(F32)[:, None] * inv
    reps = LANES // half
    return jnp.tile(jnp.cos(ang), (1, reps)), jnp.tile(jnp.sin(ang), (1, reps))


def _rot_cols(w):
    half = w.shape[-1] // 2
    return jnp.concatenate([-w[..., half:], w[..., :half]], axis=-1)


def _mla_weights(w_in, q_norm, w_uq, kv_norm, w_ukv, w_o):
    kpe = w_in[:, B_KPE:]
    pad = jnp.zeros((D_MODEL, LANES - QK_ROPE), w_in.dtype)
    w_in_aug = jnp.concatenate([w_in[:, :B_KPE], kpe, pad, _rot_cols(kpe), pad], axis=1)
    uq = w_uq.reshape(Q_LORA, B_HEADS, QK_NOPE + QK_ROPE)
    pe = uq[:, :, QK_NOPE:]
    ukv = w_ukv.reshape(KV_LORA, B_HEADS, QK_NOPE + V_HEAD)
    return {
        "w_in": w_in_aug.astype(BF16),
        "q_norm": q_norm.reshape(1, Q_LORA),
        "kv_norm": kv_norm.reshape(1, KV_LORA),
        "w_nope": uq[:, :, :QK_NOPE].reshape(Q_LORA, B_HEADS * QK_NOPE).astype(BF16),
        "w_pe": pe.reshape(Q_LORA, B_HEADS * QK_ROPE).astype(BF16),
        "w_pe_rot": _rot_cols(pe).reshape(Q_LORA, B_HEADS * QK_ROPE).astype(BF16),
        "w_uk": jnp.transpose(ukv[:, :, :QK_NOPE], (1, 2, 0)).astype(BF16),
        "w_uv": jnp.transpose(ukv[:, :, QK_NOPE:], (1, 0, 2)).astype(BF16),
        "w_o": w_o.astype(BF16),
    }


def _pad_lanes(v):
    return jnp.pad(v, (0, LANES - v.shape[0])).reshape(1, LANES)


def _expand_heads(v):
    return jnp.repeat(v, C_HEAD_DIM).reshape(1, D_INNER)


def _mamba_weights(w_in, conv_w, conv_b, dt_bias, a_log, d_skip, norm_g, w_o):
    w_dt = w_in[:, D_INNER + CONV_DIM:]
    pad = jnp.zeros((D_MODEL, LANES - C_HEADS), w_in.dtype)
    return {
        "w_in": jnp.concatenate([w_in, pad], axis=1).astype(BF16),
        "w_dt_x": jnp.repeat(w_dt, C_HEAD_DIM, axis=1).astype(BF16)[None],
        "conv_w": conv_w,
        "conv_b": conv_b.reshape(1, CONV_DIM),
        "dt_bias": _pad_lanes(dt_bias.astype(F32)),
        "a_log": _pad_lanes(a_log.astype(F32)),
        "dt_bias_x": _expand_heads(dt_bias.astype(F32)),
        "a_log_x": _expand_heads(a_log.astype(F32)),
        "d_skip": _expand_heads(d_skip),
        "norm_g": norm_g.reshape(1, D_INNER),
        "w_o": w_o.astype(BF16),
    }


def kernel(x_prompt, x_sample, state_swa_k, state_swa_v, cache_mla, state_ssm, state_conv, page_table, p_prompt, p_sample, norm_ffn1, ffn1_w_gu, ffn1_w_down, norm_mix, norm_ffn2, ffn2_w_gu, ffn2_w_down, norm_ple, ple_w_proj, ple_w_gate, rel_bias_table, a_w_qkv, a_w_o, a_sinks, b_w_in, b_q_norm, b_w_uq, b_kv_norm, b_w_ukv, b_w_o, c_w_in, c_conv_w, c_conv_b, c_dt_bias, c_a_log, c_d, c_norm, c_w_o, final_norm):
    nb_p, t_p, _ = x_prompt.shape
    nb_s, l_s, _ = x_sample.shape
    depth = norm_ffn1.shape[0]
    rows_p = nb_p * t_p
    rows_s = nb_s * l_s
    w_buf = state_swa_k.shape[2]
    assert rows_p % FFN_TM == 0 and (rows_p + rows_s) % FFN_TM == 0 and rows_s % ROW_TM == 0
    assert t_p % B_KVBLOCK == 0 and t_p % SSD_CHUNK == 0 and t_p % SWA_QROWS == 0
    assert l_s == SUBLANES and w_buf == WINDOW

    x = (x_prompt.reshape(rows_p, D_MODEL), x_sample.reshape(rows_s, D_MODEL))
    p_parts = (p_prompt.reshape(depth, rows_p, D_PLE), p_sample.reshape(depth, rows_s, D_PLE))

    def norm3(g):
        return g.reshape(g.shape[0], 1, g.shape[1])

    norm_ffn1, norm_mix, norm_ffn2, norm_ple = map(norm3, (norm_ffn1, norm_mix, norm_ffn2, norm_ple))
    ffn1_w_gu, ffn1_w_down, ffn2_w_gu, ffn2_w_down, ple_w_proj, ple_w_gate, a_w_qkv, a_w_o = (
        w.astype(BF16) for w in (ffn1_w_gu, ffn1_w_down, ffn2_w_gu, ffn2_w_down, ple_w_proj, ple_w_gate, a_w_qkv, a_w_o))
    final_norm = final_norm.reshape(1, D_MODEL)

    i = jnp.arange(SWA_QROWS)[:, None]
    j = jnp.arange(A_BLOCK + SWA_QROWS)[None, :]
    dist = A_BLOCK + i - j
    in_window = (dist >= 0) & (dist < WINDOW)
    bias_p = jnp.stack([_rel_bias(rel_bias_table, _t5_bucket(dist), in_window & (j >= A_BLOCK)),
                        _rel_bias(rel_bias_table, _t5_bucket(dist), in_window)])
    q_pos = PAST_LEN + jnp.arange(l_s)
    k_pos = jnp.concatenate([PAST_LEN - w_buf + jnp.arange(w_buf), q_pos])
    dist = q_pos[:, None] - k_pos[None, :]
    bias_s = _rel_bias(rel_bias_table, _t5_bucket(dist), (dist >= 0) & (dist < WINDOW))
    swa_k_buf = jnp.transpose(state_swa_k, (0, 1, 3, 4, 2))
    swa_v_buf = jnp.transpose(state_swa_v, (0, 1, 3, 4, 2))

    pos = jnp.concatenate([jnp.tile(jnp.arange(t_p), nb_p), jnp.tile(PAST_LEN + jnp.arange(l_s), nb_s)])
    cos, sin = _rope_tables(pos)

    ssm_flat = state_ssm.reshape(state_ssm.shape[0], nb_s, D_INNER, D_STATE)
    cache_t = jnp.swapaxes(cache_mla, 2, 3)

    def last_rows(a, n):
        return jnp.stack([a[(b + 1) * t_p - n:(b + 1) * t_p] for b in range(nb_p)])

    swa_k_p, swa_v_p, mla_p, ssm_p, conv_p = [], [], [], [], []
    swa_k_s, swa_v_s, mla_s, ssm_s, conv_s = [], [], [], [], []
    for layer in range(depth):
        kind, sl = layer % N_MIXERS, layer // N_MIXERS
        prev_ple = None if layer == 0 else (p_parts, norm_ple, ple_w_proj, ple_w_gate, layer - 1)
        if kind == 0:
            x, qkv = _ffn(x, norm_ffn1, ffn1_w_gu, ffn1_w_down, layer, ple=prev_ple,
                          proj=(norm_mix, layer, a_w_qkv, sl), n_first=rows_p)
        else:
            x = _ffn(x, norm_ffn1, ffn1_w_gu, ffn1_w_down, layer, ple=prev_ple, n_first=rows_p)
        if kind == 0:
            x = _swa_prompt(x, qkv, bias_p, a_sinks[sl], a_w_o, sl, nb_p, t_p)
            o_s, kt_new, vt_new = _swa_sample(qkv, swa_k_buf, swa_v_buf, bias_s, a_sinks[sl], sl, rows_p, nb_s, l_s)
            x = _res_proj(x, o_s, a_w_o, sl, rows_p, "swa_out_sample")
            kv_p = last_rows(qkv, w_buf)[:, :, A_Q:].reshape(nb_p, w_buf, 2, A_KV_HEADS, A_HEAD_DIM)
            swa_k_p.append(kv_p[:, :, 0])
            swa_v_p.append(kv_p[:, :, 1])
            swa_k_s.append(jnp.transpose(kt_new, (0, 3, 1, 2)))
            swa_v_s.append(jnp.transpose(vt_new, (0, 3, 1, 2)))
        elif kind == 1:
            w = _mla_weights(b_w_in[sl], b_q_norm[sl], b_w_uq[sl], b_kv_norm[sl], b_w_ukv[sl], b_w_o[sl])
            kv_rows_p, q_p, kv_bf16 = _mla_proj(x, norm_mix, layer, w, cos, sin, 0, rows_p, None)
            kv_rows_s, q_s = _mla_proj(x, norm_mix, layer, w, cos, sin, rows_p, rows_s, l_s)
            x = _mla_prompt(x, q_p, kv_bf16, w, nb_p, t_p)
            o_lat = _mla_sample(q_s, kv_rows_s, cache_t, page_table, sl, 0, nb_s, l_s)
            x = _mla_out_sample(x, o_lat, w, rows_p, l_s)
            mla_p.append(kv_rows_p.reshape(nb_p, t_p, KV_ROW))
            mla_s.append(kv_rows_s.reshape(nb_s, l_s, KV_ROW))
        else:
            w = _mamba_weights(c_w_in[sl], c_conv_w[sl], c_conv_b[sl], c_dt_bias[sl], c_a_log[sl], c_d[sl],
                               c_norm[sl], c_w_o[sl])
            z, xbc, dt = _mamba_in(x, norm_mix, layer, w["w_in"])
            dtx = _norm_proj(x[rows_p:], norm_mix, w["w_dt_x"], layer, 0, "mamba_dt_sample")
            y_p, st_p = _ssd_prompt(xbc, dt, z, w, nb_p, t_p)
            y_s, st_s = _ssd_sample(xbc, dtx, z, state_conv, ssm_flat, sl, w, rows_p, nb_s, l_s)
            x = _res_proj(x, y_p, w["w_o"][None], 0, 0, "mamba_out_prompt")
            x = _res_proj(x, y_s, w["w_o"][None], 0, rows_p, "mamba_out_sample")
            ssm_p.append(st_p.reshape(nb_p, C_HEADS, C_HEAD_DIM, D_STATE))
            ssm_s.append(st_s.reshape(nb_s, C_HEADS, C_HEAD_DIM, D_STATE))
            conv_p.append(last_rows(xbc, D_CONV - 1))
            conv_s.append(jnp.concatenate([state_conv[sl], xbc[rows_p:].reshape(nb_s, l_s, CONV_DIM)],
                                          axis=1)[:, -(D_CONV - 1):])
        x = _ffn(x, norm_ffn2, ffn2_w_gu, ffn2_w_down, layer)
    y_p = _ple(x, p_parts[0], norm_ple, ple_w_proj, ple_w_gate, final_norm, depth - 1, True, 0)
    y_s = _ple(x, p_parts[1], norm_ple, ple_w_proj, ple_w_gate, final_norm, depth - 1, True, rows_p)

    return (y_p.reshape(nb_p, t_p, D_MODEL), y_s.reshape(nb_s, l_s, D_MODEL),
            jnp.stack(swa_k_p), jnp.stack(swa_v_p), jnp.stack(mla_p), jnp.stack(ssm_p), jnp.stack(conv_p),
            jnp.stack(swa_k_s), jnp.stack(swa_v_s), jnp.stack(mla_s), jnp.stack(ssm_s), jnp.stack(conv_s))
```

```python
import functools
import math

import jax
import jax.numpy as jnp
from jax import lax
from jax.experimental import pallas as pl
from jax.experimental.pallas import tpu as pltpu

F32 = jnp.float32
BF16 = jnp.bfloat16

D_MODEL = 1024
D_FF = 2816
D_PLE = 256
EPS = 1e-6
NEG_INF = -1e30
PAST_LEN = 8192
PAGE_SIZE = 128
N_MIXERS = 3

A_HEADS = 16
A_KV_HEADS = 4
A_HEAD_DIM = 64
A_GROUP = A_HEADS // A_KV_HEADS
A_Q = A_HEADS * A_HEAD_DIM
A_KV = A_KV_HEADS * A_HEAD_DIM
WINDOW = 128
A_BLOCK = 128
N_BUCKETS = 32
BUCKET_MAX_DIST = 128

B_HEADS = 16
Q_LORA = 768
KV_LORA = 256
QK_NOPE = 64
QK_ROPE = 32
V_HEAD = 64
KV_ROW = KV_LORA + QK_ROPE
ROPE_THETA = 10000.0
MLA_SCALE = (QK_NOPE + QK_ROPE) ** -0.5
B_QBLOCK = 128
B_KVBLOCK = 512
B_PAGES_PER_STEP = 64

D_INNER = 2 * D_MODEL
C_HEAD_DIM = 64
C_HEADS = D_INNER // C_HEAD_DIM
C_GROUPS = 4
C_HPG = C_HEADS // C_GROUPS
D_STATE = 128
D_CONV = 4
C_BC = C_GROUPS * D_STATE
CONV_DIM = D_INNER + 2 * C_BC
SSD_CHUNK = 128

LANES = 128
SUBLANES = 8
VMEM_LIMIT = 56 * 1024 * 1024


def _cparams(*sem):
    return pltpu.CompilerParams(dimension_semantics=sem, vmem_limit_bytes=VMEM_LIMIT)


def _resident(shape, index_map):
    return pl.BlockSpec(shape, index_map, pipeline_mode=pl.Buffered(1))


def _dot(a, b):
    return jnp.dot(a, b, preferred_element_type=F32)


def _dot_nt(a, b):
    return lax.dot_general(a, b, (((1,), (1,)), ((), ())), preferred_element_type=F32)


def _dot_tn(a, b):
    return lax.dot_general(a, b, (((0,), (0,)), ((), ())), preferred_element_type=F32)


def _dot_f32(a, b):
    return jnp.dot(a, b, preferred_element_type=F32, precision=lax.Precision.HIGHEST)


def _rms(x, g):
    r = lax.rsqrt(jnp.mean(x * x, axis=-1, keepdims=True) + EPS)
    return (x * r) * g


def _silu(x):
    return x * jax.nn.sigmoid(x)


def _softplus(x):
    return jnp.maximum(x, 0.0) + jnp.log1p(jnp.exp(-jnp.abs(x)))


def _lanes(a, n):
    if n <= LANES:
        return a[:, :n]
    return jnp.concatenate([a] * (n // LANES), axis=1)


FFN_TM = 512
FFN_TF = 256


def _ple_update(x, p, g_ref, wp_ref, wg_ref):
    gate = _dot(_rms(x, g_ref[...]).astype(BF16), wg_ref[...])
    proj = _dot(p.astype(BF16), wp_ref[...])
    return x + proj * jax.nn.sigmoid(gate)


def _tile_of_pair(n_first, first_ref, second_ref):
    step = jnp.zeros(first_ref.shape, jnp.int32) + pl.program_id(0)
    return jnp.where(step < n_first, first_ref[...], second_ref[...])


def _pair_specs(n_first, block, lead=()):
    squeezed = (None,) * len(lead)
    return [pl.BlockSpec(squeezed + block, lambda i: lead + (jnp.minimum(i, n_first - 1), 0)),
            pl.BlockSpec(squeezed + block, lambda i: lead + (jnp.maximum(i - n_first, 0), 0))]


def _ffn_body(*refs, ple, proj, x_pair, n_first):
    refs = list(refs)
    x = _tile_of_pair(n_first, refs.pop(0), refs.pop(0)) if x_pair else refs.pop(0)[...]
    g_ref, wgu_ref, wd_ref = refs[:3]
    refs = refs[3:]
    if ple:
        x = _ple_update(x, _tile_of_pair(n_first, refs[0], refs[1]), *refs[2:5])
        refs = refs[5:]
    o_ref = refs[-2] if proj else refs[-1]
    h = _rms(x, g_ref[...]).astype(BF16)
    acc = jnp.zeros(x.shape, F32)
    for j in range(D_FF // FFN_TF):
        lo = j * FFN_TF
        gate = _dot(h, wgu_ref[:, lo:lo + FFN_TF].astype(BF16))
        up = _dot(h, wgu_ref[:, D_FF + lo:D_FF + lo + FFN_TF].astype(BF16))
        act = (_silu(gate) * up).astype(BF16)
        acc = acc + _dot(act, wd_ref[lo:lo + FFN_TF, :].astype(BF16))
    y = x + 0.5 * acc
    o_ref[...] = y
    if proj:
        gm_ref, wm_ref, m_ref = refs[0], refs[1], refs[-1]
        m_ref[...] = _dot(_rms(y, gm_ref[...]).astype(BF16), wm_ref[...])


def _ffn(x, norm, w_gu, w_down, layer, ple=None, proj=None, n_first=None):
    x_pair = isinstance(x, tuple)
    rows = x[0].shape[0] + x[1].shape[0] if x_pair else x.shape[0]
    n_first_tiles = None if n_first is None else n_first // FFN_TM
    if x_pair:
        in_specs = _pair_specs(n_first_tiles, (FFN_TM, D_MODEL))
        args = list(x)
    else:
        in_specs = [pl.BlockSpec((FFN_TM, D_MODEL), lambda i: (i, 0))]
        args = [x]
    in_specs += [
        _resident((None, 1, D_MODEL), lambda i: (layer, 0, 0)),
        _resident((None, D_MODEL, 2 * D_FF), lambda i: (layer, 0, 0)),
        _resident((None, D_FF, D_MODEL), lambda i: (layer, 0, 0)),
    ]
    args += [norm, w_gu, w_down]
    if ple is not None:
        pl_layer = ple[4]
        in_specs += _pair_specs(n_first_tiles, (FFN_TM, D_PLE), lead=(pl_layer,)) + [
            _resident((None, 1, D_MODEL), lambda i: (pl_layer, 0, 0)),
            _resident((None, D_PLE, D_MODEL), lambda i: (pl_layer, 0, 0)),
            _resident((None, D_MODEL, D_MODEL), lambda i: (pl_layer, 0, 0)),
        ]
        args += list(ple[0]) + list(ple[1:4])
    out_specs = pl.BlockSpec((FFN_TM, D_MODEL), lambda i: (i, 0))
    out_shape = jax.ShapeDtypeStruct((rows, D_MODEL), F32)
    if proj is not None:
        m_norm, m_layer, m_w, m_wl = proj
        n = m_w.shape[2]
        in_specs += [
            _resident((None, 1, D_MODEL), lambda i: (m_layer, 0, 0)),
            _resident((None, D_MODEL, n), lambda i: (m_wl, 0, 0)),
        ]
        args += [m_norm, m_w]
        out_specs = [out_specs, pl.BlockSpec((FFN_TM, n), lambda i: (i, 0))]
        out_shape = [out_shape, jax.ShapeDtypeStruct((rows, n), F32)]
    return pl.pallas_call(
        functools.partial(_ffn_body, ple=ple is not None, proj=proj is not None, x_pair=x_pair,
                          n_first=n_first_tiles),
        grid=(rows // FFN_TM,),
        in_specs=in_specs,
        out_specs=out_specs,
        out_shape=out_shape,
        compiler_params=_cparams("parallel"),
        name="ffn" + ("_ple" if ple is not None else "") + ("_proj" if proj is not None else ""),
    )(*args)


ROW_TM = 512


def _ple_body(x_ref, p_ref, g_ref, wp_ref, wg_ref, gf_ref, o_ref, *, final):
    y = _ple_update(x_ref[...], p_ref[...], g_ref, wp_ref, wg_ref)
    if final:
        y = _rms(y, gf_ref[...])
    o_ref[...] = y


def _ple(x, p, norm, w_proj, w_gate, final_norm, layer, final, row0):
    rows = p.shape[1]
    blk0 = row0 // ROW_TM
    return pl.pallas_call(
        functools.partial(_ple_body, final=final),
        grid=(rows // ROW_TM,),
        in_specs=[
            pl.BlockSpec((ROW_TM, D_MODEL), lambda i: (i + blk0, 0)),
            pl.BlockSpec((None, ROW_TM, D_PLE), lambda i: (layer, i, 0)),
            _resident((None, 1, D_MODEL), lambda i: (layer, 0, 0)),
            _resident((None, D_PLE, D_MODEL), lambda i: (layer, 0, 0)),
            _resident((None, D_MODEL, D_MODEL), lambda i: (layer, 0, 0)),
            _resident((1, D_MODEL), lambda i: (0, 0)),
        ],
        out_specs=pl.BlockSpec((ROW_TM, D_MODEL), lambda i: (i, 0)),
        out_shape=jax.ShapeDtypeStruct((rows, D_MODEL), F32),
        compiler_params=_cparams("parallel"),
        name="ple",
    )(x, p, norm, w_proj, w_gate, final_norm)


def _norm_proj_body(x_ref, g_ref, w_ref, o_ref):
    o_ref[...] = _dot(_rms(x_ref[...], g_ref[...]).astype(BF16), w_ref[...])


def _norm_proj(x, norm, w, layer, wl, name):
    rows = x.shape[0]
    k, n = w.shape[1:]
    return pl.pallas_call(
        _norm_proj_body,
        grid=(rows // ROW_TM,),
        in_specs=[
            pl.BlockSpec((ROW_TM, k), lambda i: (i, 0)),
            _resident((None, 1, k), lambda i: (layer, 0, 0)),
            _resident((None, k, n), lambda i: (wl, 0, 0)),
        ],
        out_specs=pl.BlockSpec((ROW_TM, n), lambda i: (i, 0)),
        out_shape=jax.ShapeDtypeStruct((rows, n), F32),
        compiler_params=_cparams("parallel"),
        name=name,
    )(x, norm, w)


def _res_proj_body(x_ref, a_ref, w_ref, o_ref):
    o_ref[...] = x_ref[...] + _dot(a_ref[...].astype(BF16), w_ref[...])


def _res_proj(x, a, w, wl, row0, name):
    rows = a.shape[0]
    k, n = w.shape[1:]
    blk0 = row0 // ROW_TM
    return pl.pallas_call(
        _res_proj_body,
        grid=(rows // ROW_TM,),
        in_specs=[
            pl.BlockSpec((ROW_TM, n), lambda i: (i + blk0, 0)),
            pl.BlockSpec((ROW_TM, k), lambda i: (i, 0)),
            _resident((None, k, n), lambda i: (wl, 0, 0)),
        ],
        out_specs=pl.BlockSpec((ROW_TM, n), lambda i: (i + blk0, 0)),
        out_shape=jax.ShapeDtypeStruct(x.shape, F32),
        input_output_aliases={0: 0},
        compiler_params=_cparams("parallel"),
        name=name,
    )(x, a, w)


def _t5_bucket(dist):
    max_exact = N_BUCKETS // 2
    d = jnp.maximum(dist, 0)
    large = max_exact + (jnp.log(jnp.maximum(d, 1).astype(F32) / max_exact)
                         / math.log(BUCKET_MAX_DIST / max_exact) * (N_BUCKETS - max_exact)).astype(jnp.int32)
    large = jnp.minimum(large, N_BUCKETS - 1)
    return jnp.where(d < max_exact, d, large)


def _bias_body(tab_ref, bm_ref, valid_ref, o_ref):
    h = pl.program_id(0)
    bm = bm_ref[...]
    acc = jnp.zeros(bm.shape, F32)
    for b in range(N_BUCKETS):
        acc = jnp.where(bm == b, tab_ref[b, h], acc)
    o_ref[...] = jnp.where(valid_ref[...] > 0, acc, NEG_INF)


def _rel_bias(table, bucket_map, valid):
    r, c = bucket_map.shape
    return pl.pallas_call(
        _bias_body,
        grid=(A_HEADS,),
        in_specs=[pl.BlockSpec(memory_space=pltpu.SMEM), pl.BlockSpec((r, c), lambda h: (0, 0)),
                  pl.BlockSpec((r, c), lambda h: (0, 0))],
        out_specs=pl.BlockSpec((None, r, c), lambda h: (h, 0, 0)),
        out_shape=jax.ShapeDtypeStruct((A_HEADS, r, c), F32),
        compiler_params=_cparams("parallel"),
        name="rel_bias",
    )(table, bucket_map, valid.astype(jnp.int32))


def _group_sinks(sinks, t):
    return jnp.broadcast_to(sinks.reshape(A_KV_HEADS, A_GROUP, 1, 1),
                            (A_KV_HEADS, A_GROUP, t, LANES)).reshape(A_KV_HEADS, A_GROUP * t, LANES)


SWA_QROWS = A_BLOCK


def _swa_prompt_body(sink_ref, q_ref, kp_ref, ko_ref, vp_ref, vo_ref, bias_ref, x_ref, wo_ref, o_ref, o_scr):
    scale = A_HEAD_DIM ** -0.5
    for g in range(A_KV_HEADS):
        gs = slice(g * A_HEAD_DIM, (g + 1) * A_HEAD_DIM)
        kg = jnp.concatenate([kp_ref[:, gs], ko_ref[:, gs]], axis=0).astype(BF16)
        vg = jnp.concatenate([vp_ref[:, gs], vo_ref[:, gs]], axis=0).astype(BF16)
        for h in range(g * A_GROUP, (g + 1) * A_GROUP):
            hs = slice(h * A_HEAD_DIM, (h + 1) * A_HEAD_DIM)
            s = _dot_nt(q_ref[:, hs].astype(BF16), kg) * scale + bias_ref[h]
            sink = sink_ref[h]
            m = jnp.maximum(jnp.max(s, axis=-1, keepdims=True), sink)
            e = jnp.exp(s - m)
            den = jnp.sum(e, axis=-1, keepdims=True) + jnp.exp(sink - m)
            o_scr[:, hs] = _dot((e / den).astype(BF16), vg)
    o_ref[...] = x_ref[...] + _dot(o_scr[...].astype(BF16), wo_ref[...])


def _swa_prompt(x, qkv, bias, sinks, w_o, wl, nb_p, t_p):
    nstep = t_p // SWA_QROWS
    blocks_per_step = SWA_QROWS // A_BLOCK
    kcol = A_Q // A_KV
    vcol = kcol + 1

    def own(b, n):
        return b * nstep + n

    def prev(b, n):
        return b * nstep * blocks_per_step + jnp.maximum(n * blocks_per_step - 1, 0)

    return pl.pallas_call(
        _swa_prompt_body,
        grid=(nb_p, nstep),
        in_specs=[
            pl.BlockSpec(memory_space=pltpu.SMEM),
            pl.BlockSpec((SWA_QROWS, A_Q), lambda b, n: (own(b, n), 0)),
            pl.BlockSpec((A_BLOCK, A_KV), lambda b, n: (prev(b, n), kcol)),
            pl.BlockSpec((SWA_QROWS, A_KV), lambda b, n: (own(b, n), kcol)),
            pl.BlockSpec((A_BLOCK, A_KV), lambda b, n: (prev(b, n), vcol)),
            pl.BlockSpec((SWA_QROWS, A_KV), lambda b, n: (own(b, n), vcol)),
            pl.BlockSpec((None, A_HEADS, SWA_QROWS, A_BLOCK + SWA_QROWS),
                         lambda b, n: (jnp.minimum(n, 1), 0, 0, 0)),
            pl.BlockSpec((SWA_QROWS, D_MODEL), lambda b, n: (own(b, n), 0)),
            _resident((None, A_Q, D_MODEL), lambda b, n: (wl, 0, 0)),
        ],
        out_specs=pl.BlockSpec((SWA_QROWS, D_MODEL), lambda b, n: (own(b, n), 0)),
        out_shape=jax.ShapeDtypeStruct(x.shape, F32),
        scratch_shapes=[pltpu.VMEM((SWA_QROWS, A_Q), F32)],
        input_output_aliases={7: 0},
        compiler_params=_cparams("parallel", "arbitrary"),
        name="swa_prompt",
    )(sinks, qkv, qkv, qkv, qkv, qkv, bias, x, w_o)


SWA_S_NB = 8


def _slide_window(buf_t, new, l_s):
    d, w = buf_t.shape
    assert w == LANES and d <= LANES
    tile = jnp.concatenate([jnp.zeros((w - l_s, LANES), F32),
                            jnp.concatenate([new, jnp.zeros((l_s, LANES - d), F32)], axis=1)], axis=0)
    new_t = tile.T[:d]
    lane = lax.broadcasted_iota(jnp.int32, (d, w), 1)
    return jnp.where(lane < w - l_s, pltpu.roll(buf_t, w - l_s, axis=1), new_t)


def _swa_sample_body(qkv_ref, kt_ref, vt_ref, bb_ref, bn_ref, sink_ref, o_ref, kto_ref, vto_ref, *, l_s):
    scale = A_HEAD_DIM ** -0.5

    def one(b, carry):
        rows = pl.ds(pl.multiple_of(b * l_s, l_s), l_s)
        for g in range(A_KV_HEADS):
            heads = range(g * A_GROUP, (g + 1) * A_GROUP)
            qg = jnp.concatenate([qkv_ref[rows, h * A_HEAD_DIM:(h + 1) * A_HEAD_DIM] for h in heads], axis=0)
            qg = qg.astype(BF16)
            kn32 = qkv_ref[rows, A_Q + g * A_HEAD_DIM:A_Q + (g + 1) * A_HEAD_DIM]
            vn32 = qkv_ref[rows, A_Q + A_KV + g * A_HEAD_DIM:A_Q + A_KV + (g + 1) * A_HEAD_DIM]
            kto_ref[b, g] = _slide_window(kt_ref[b, g], kn32, l_s)
            vto_ref[b, g] = _slide_window(vt_ref[b, g], vn32, l_s)
            kn = kn32.astype(BF16)
            vn = vn32.astype(BF16)
            sb = _dot(qg, kt_ref[b, g].astype(BF16)) * scale + bb_ref[g]
            sn = _dot_nt(qg, kn) * scale + bn_ref[g]
            sink = sink_ref[g][:, :1]
            m = jnp.maximum(jnp.maximum(jnp.max(sb, axis=-1, keepdims=True), jnp.max(sn, axis=-1, keepdims=True)), sink)
            eb = jnp.exp(sb - m)
            en = jnp.exp(sn - m)
            den = jnp.sum(eb, axis=-1, keepdims=True) + jnp.sum(en, axis=-1, keepdims=True) + jnp.exp(sink - m)
            og = _dot_nt((eb / den).astype(BF16), vt_ref[b, g].astype(BF16)) + _dot((en / den).astype(BF16), vn)
            for i, h in enumerate(heads):
                o_ref[rows, h * A_HEAD_DIM:(h + 1) * A_HEAD_DIM] = og[i * l_s:(i + 1) * l_s]
        return carry

    lax.fori_loop(0, SWA_S_NB, one, 0, unroll=True)


def _swa_sample(qkv, kt_buf, vt_buf, bias, sinks, sl, row0, nb_s, l_s):
    w_buf = kt_buf.shape[-1]
    r = A_GROUP * l_s
    tm = SWA_S_NB * l_s
    blk0 = row0 // tm
    bias_g = bias.reshape(A_KV_HEADS, r, w_buf + l_s)
    sink_g = _group_sinks(sinks, l_s)
    buf_spec = pl.BlockSpec((None, SWA_S_NB, A_KV_HEADS, A_HEAD_DIM, w_buf), lambda i: (sl, i, 0, 0, 0))
    new_spec = pl.BlockSpec((SWA_S_NB, A_KV_HEADS, A_HEAD_DIM, w_buf), lambda i: (i, 0, 0, 0))
    new_shape = jax.ShapeDtypeStruct((nb_s, A_KV_HEADS, A_HEAD_DIM, w_buf), F32)
    const3 = lambda i: (0, 0, 0)
    return pl.pallas_call(
        functools.partial(_swa_sample_body, l_s=l_s),
        grid=(nb_s // SWA_S_NB,),
        in_specs=[
            pl.BlockSpec((tm, A_Q + 2 * A_KV), lambda i: (i + blk0, 0)),
            buf_spec,
            buf_spec,
            _resident((A_KV_HEADS, r, w_buf), const3),
            _resident((A_KV_HEADS, r, l_s), const3),
            _resident((A_KV_HEADS, r, LANES), const3),
        ],
        out_specs=[pl.BlockSpec((tm, A_Q), lambda i: (i, 0)), new_spec, new_spec],
        out_shape=[jax.ShapeDtypeStruct((nb_s * l_s, A_Q), F32), new_shape, new_shape],
        compiler_params=_cparams("parallel"),
        name="swa_sample",
    )(qkv, kt_buf, vt_buf, bias_g[:, :, :w_buf], bias_g[:, :, w_buf:], sink_g)


B_IN_COLS = Q_LORA + KV_LORA + 2 * LANES
B_KPE = Q_LORA + KV_LORA
B_KPE_ROT = B_KPE + LANES


def _mla_proj_body(x_ref, g_ref, win_ref, qn_ref, kvn_ref, wnope_ref, wpe_ref, wper_ref, wuk_ref,
                   cos_ref, sin_ref, kv_ref, q_ref, *maybe_kvb_ref, sample):
    tm = x_ref.shape[0]
    h = _rms(x_ref[...], g_ref[...]).astype(BF16)
    c = _dot(h, win_ref[...])
    cos = cos_ref[...]
    sin = sin_ref[...]
    kv_lat = _rms(c[:, Q_LORA:B_KPE], kvn_ref[...])
    kv_rope = (c[:, B_KPE:B_KPE + QK_ROPE] * cos[:, :QK_ROPE]
               + c[:, B_KPE_ROT:B_KPE_ROT + QK_ROPE] * sin[:, :QK_ROPE])
    kv_ref[:, :KV_LORA] = kv_lat
    kv_ref[:, KV_LORA:] = kv_rope
    for kvb_ref in maybe_kvb_ref:
        kvb_ref[:, :KV_LORA] = kv_lat.astype(BF16)
        kvb_ref[:, KV_LORA:] = kv_rope.astype(BF16)
    qn = _rms(c[:, :Q_LORA], qn_ref[...]).astype(BF16)
    qnope = _dot(qn, wnope_ref[...])
    n_rep = B_HEADS * QK_ROPE // LANES
    qpe = (_dot(qn, wpe_ref[...]) * jnp.concatenate([cos] * n_rep, axis=1)
           + _dot(qn, wper_ref[...]) * jnp.concatenate([sin] * n_rep, axis=1))
    for hd in range(B_HEADS):
        qlat = _dot(qnope[:, hd * QK_NOPE:(hd + 1) * QK_NOPE].astype(BF16), wuk_ref[hd])
        qrope = qpe[:, hd * QK_ROPE:(hd + 1) * QK_ROPE]
        if sample:
            l_s = q_ref.shape[1] // B_HEADS
            rs = slice(hd * l_s, (hd + 1) * l_s)
            q_ref[:, rs, :KV_LORA] = qlat.reshape(tm // l_s, l_s, KV_LORA)
            q_ref[:, rs, KV_LORA:] = qrope.reshape(tm // l_s, l_s, QK_ROPE)
        else:
            for s in range(tm // B_QBLOCK):
                ts = slice(s * B_QBLOCK, (s + 1) * B_QBLOCK)
                q_ref[s, hd, :, :KV_LORA] = qlat[ts].astype(BF16)
                q_ref[s, hd, :, KV_LORA:] = qrope[ts].astype(BF16)


def _mla_proj(x, norm, layer, w, cos, sin, row0, rows, l_s):
    sample = l_s is not None
    tm = ROW_TM
    blk0 = row0 // tm
    if sample:
        q_shape = jax.ShapeDtypeStruct((rows // l_s, B_HEADS * l_s, KV_ROW), F32)
        q_spec = pl.BlockSpec((tm // l_s, B_HEADS * l_s, KV_ROW), lambda i: (i, 0, 0))
    else:
        q_shape = jax.ShapeDtypeStruct((rows // B_QBLOCK, B_HEADS, B_QBLOCK, KV_ROW), BF16)
        q_spec = pl.BlockSpec((tm // B_QBLOCK, B_HEADS, B_QBLOCK, KV_ROW), lambda i: (i, 0, 0, 0))
    const2 = lambda i: (0, 0)
    kv_spec = pl.BlockSpec((tm, KV_ROW), lambda i: (i, 0))
    out_specs = [kv_spec, q_spec]
    out_shape = [jax.ShapeDtypeStruct((rows, KV_ROW), F32), q_shape]
    if not sample:
        out_specs.append(kv_spec)
        out_shape.append(jax.ShapeDtypeStruct((rows, KV_ROW), BF16))
    return pl.pallas_call(
        functools.partial(_mla_proj_body, sample=sample),
        grid=(rows // tm,),
        in_specs=[
            pl.BlockSpec((tm, D_MODEL), lambda i: (i + blk0, 0)),
            _resident((None, 1, D_MODEL), lambda i: (layer, 0, 0)),
            _resident((D_MODEL, B_IN_COLS), const2),
            _resident((1, Q_LORA), const2),
            _resident((1, KV_LORA), const2),
            _resident((Q_LORA, B_HEADS * QK_NOPE), const2),
            _resident((Q_LORA, B_HEADS * QK_ROPE), const2),
            _resident((Q_LORA, B_HEADS * QK_ROPE), const2),
            _resident((B_HEADS, QK_NOPE, KV_LORA), lambda i: (0, 0, 0)),
            pl.BlockSpec((tm, LANES), lambda i: (i + blk0, 0)),
            pl.BlockSpec((tm, LANES), lambda i: (i + blk0, 0)),
        ],
        out_specs=out_specs,
        out_shape=out_shape,
        compiler_params=_cparams("parallel"),
        name="mla_proj_sample" if sample else "mla_proj_prompt",
    )(x, norm, w["w_in"], w["q_norm"], w["kv_norm"], w["w_nope"], w["w_pe"], w["w_pe_rot"], w["w_uk"], cos, sin)


def _softmax_stats(s, m_scr, l_scr):
    c = MLA_SCALE * math.log2(math.e)
    m_prev = m_scr[...]
    m_new = jnp.maximum(m_prev, jnp.max(s, axis=-1, keepdims=True))
    alpha = jnp.exp2((m_prev - m_new) * c)
    p = jnp.exp2((s - _lanes(m_new, s.shape[1])) * c)
    l_scr[...] = alpha * l_scr[...] + jnp.sum(p, axis=-1, keepdims=True)
    m_scr[...] = m_new
    return alpha, p.astype(BF16)


def _softmax_accumulate(s, pv, m_scr, l_scr, acc_scr):
    alpha, p = _softmax_stats(s, m_scr, l_scr)
    acc_scr[...] = acc_scr[...] * _lanes(alpha, KV_LORA) + pv(p)


def _softmax_update(q, kvb, m_scr, l_scr, acc_scr, valid=None):
    s = _dot_nt(q, kvb)
    if valid is not None:
        s = jnp.where(valid, s, NEG_INF)
    _softmax_accumulate(s, lambda p: _dot(p, kvb[:, :KV_LORA]), m_scr, l_scr, acc_scr)


def _softmax_init(m_scr, l_scr, acc_scr):
    m_scr[...] = jnp.full(m_scr.shape, NEG_INF, F32)
    l_scr[...] = jnp.zeros(l_scr.shape, F32)
    acc_scr[...] = jnp.zeros(acc_scr.shape, F32)


def _mla_prompt_body(q_ref, kv_ref, x_ref, wuv_ref, wo_ref, o_ref, s0_scr, s1_scr, m_scr, l_scr, acc_scr, u_scr):
    qi = pl.program_id(1)
    r = B_HEADS * B_QBLOCK
    n_off = qi // (B_KVBLOCK // B_QBLOCK)
    q = q_ref[...].reshape(r, KV_ROW)

    def kv_block(k):
        return kv_ref[pl.ds(pl.multiple_of(k * B_KVBLOCK, B_KVBLOCK), B_KVBLOCK), :]

    def update(s, k):
        kvb = kv_block(k)
        _softmax_accumulate(s, lambda p: _dot(p, kvb[:, :KV_LORA]), m_scr, l_scr, acc_scr)

    def step(s_cur, s_next, k):
        s_next[...] = _dot_nt(q, kv_block(k + 1))
        update(s_cur[...], k)

    def last(s_cur):
        tok = lax.broadcasted_iota(jnp.int32, (r, B_KVBLOCK), 0) & (B_QBLOCK - 1)
        key = lax.broadcasted_iota(jnp.int32, (r, B_KVBLOCK), 1)
        valid = key - tok <= qi * B_QBLOCK - n_off * B_KVBLOCK
        update(jnp.where(valid, s_cur[...], NEG_INF), n_off)
        o = acc_scr[...] * _lanes(1.0 / l_scr[...], KV_LORA)
        for hd in range(B_HEADS):
            oh = o[hd * B_QBLOCK:(hd + 1) * B_QBLOCK].astype(BF16)
            u_scr[:, hd * V_HEAD:(hd + 1) * V_HEAD] = _dot(oh, wuv_ref[hd])
        o_ref[...] = x_ref[...] + _dot(u_scr[...].astype(BF16), wo_ref[...])

    _softmax_init(m_scr, l_scr, acc_scr)
    s0_scr[...] = _dot_nt(q, kv_block(0))

    def two_steps(j, carry):
        step(s0_scr, s1_scr, 2 * j)
        step(s1_scr, s0_scr, 2 * j + 1)
        return carry

    lax.fori_loop(0, n_off // 2, two_steps, 0)
    odd = n_off % 2 == 1

    @pl.when(odd)
    def _():
        step(s0_scr, s1_scr, n_off - 1)
        last(s1_scr)

    @pl.when(jnp.logical_not(odd))
    def _():
        last(s0_scr)


def _mla_prompt(x, q_cat, kv_bf16, w, nb_p, t_p):
    nq = t_p // B_QBLOCK
    r = B_HEADS * B_QBLOCK
    return pl.pallas_call(
        _mla_prompt_body,
        grid=(nb_p, nq),
        in_specs=[
            pl.BlockSpec((None, B_HEADS, B_QBLOCK, KV_ROW), lambda b, qi: (b * nq + qi, 0, 0, 0)),
            pl.BlockSpec((t_p, KV_ROW), lambda b, qi: (b, 0)),
            pl.BlockSpec((B_QBLOCK, D_MODEL), lambda b, qi: (b * nq + qi, 0)),
            _resident((B_HEADS, KV_LORA, V_HEAD), lambda b, qi: (0, 0, 0)),
            _resident((B_HEADS * V_HEAD, D_MODEL), lambda b, qi: (0, 0)),
        ],
        out_specs=pl.BlockSpec((B_QBLOCK, D_MODEL), lambda b, qi: (b * nq + qi, 0)),
        out_shape=jax.ShapeDtypeStruct(x.shape, F32),
        scratch_shapes=[
            pltpu.VMEM((r, B_KVBLOCK), F32),
            pltpu.VMEM((r, B_KVBLOCK), F32),
            pltpu.VMEM((r, LANES), F32),
            pltpu.VMEM((r, LANES), F32),
            pltpu.VMEM((r, KV_LORA), F32),
            pltpu.VMEM((B_QBLOCK, B_HEADS * V_HEAD), F32),
        ],
        input_output_aliases={2: 0},
        compiler_params=_cparams("parallel", "arbitrary"),
        name="mla_prompt",
    )(q_cat, kv_bf16, x, w["w_uv"], w["w_o"])


def _mla_sample_body(pt_ref, q_ref, cache_ref, kvn_ref, o_ref, buf, sem, m_scr, l_scr, acc_scr, acct_scr,
                     *, l_s, n_pages, cl):
    b = pl.program_id(0)
    slot = b % 2
    r = q_ref.shape[0]
    chunk = math.gcd(n_pages, B_PAGES_PER_STEP)
    assert r == LANES

    def page_copy(batch, slot_, k):
        page = pt_ref[batch * n_pages + k]
        return pltpu.make_async_copy(cache_ref.at[cl, page], buf.at[slot_, k], sem.at[slot_, k])

    def start_pages(batch, slot_):
        for k in range(n_pages):
            page_copy(batch, slot_, k).start(priority=k % 2)

    @pl.when(b == 0)
    def _():
        start_pages(0, 0)

    @pl.when(b + 1 < pl.num_programs(0))
    def _():
        start_pages(b + 1, 1 - slot)

    _softmax_init(m_scr, l_scr, acct_scr)
    q = q_ref[...].astype(BF16)
    for c in range(n_pages // chunk):
        ks = range(c * chunk, (c + 1) * chunk)
        for k in ks:
            page_copy(b, slot, k).wait()
        kt = jnp.concatenate([buf[slot, k].astype(BF16) for k in ks], axis=1)
        alpha, p = _softmax_stats(_dot(q, kt), m_scr, l_scr)
        alpha_t = jnp.concatenate([alpha.T] * (KV_LORA // LANES), axis=0)
        acct_scr[...] = acct_scr[...] * alpha_t + _dot_nt(kt[:KV_LORA, :], p)

    acc_scr[...] = acct_scr[...].T
    tok = lax.broadcasted_iota(jnp.int32, (r, l_s), 0) % l_s
    key = lax.broadcasted_iota(jnp.int32, (r, l_s), 1)
    _softmax_update(q, kvn_ref[...].astype(BF16), m_scr, l_scr, acc_scr, valid=key <= tok)
    o_ref[...] = acc_scr[...] * _lanes(1.0 / l_scr[...], KV_LORA)


def _mla_sample(q_cat, kv_rows, cache_t, page_table, cl, row0, nb_s, l_s):
    n_pages = page_table.shape[1]
    r = B_HEADS * l_s
    blk0 = row0 // l_s
    return pl.pallas_call(
        functools.partial(_mla_sample_body, l_s=l_s, n_pages=n_pages, cl=cl),
        grid_spec=pltpu.PrefetchScalarGridSpec(
            num_scalar_prefetch=1,
            grid=(nb_s,),
            in_specs=[
                pl.BlockSpec((None, r, KV_ROW), lambda b, pt: (b, 0, 0)),
                pl.BlockSpec(memory_space=pl.ANY),
                pl.BlockSpec((l_s, KV_ROW), lambda b, pt: (b + blk0, 0)),
            ],
            out_specs=pl.BlockSpec((None, r, KV_LORA), lambda b, pt: (b, 0, 0)),
            scratch_shapes=[
                pltpu.VMEM((2, n_pages, KV_ROW, PAGE_SIZE), F32),
                pltpu.SemaphoreType.DMA((2, n_pages)),
                pltpu.VMEM((r, LANES), F32),
                pltpu.VMEM((r, LANES), F32),
                pltpu.VMEM((r, KV_LORA), F32),
                pltpu.VMEM((KV_LORA, r), F32),
            ],
        ),
        out_shape=jax.ShapeDtypeStruct((nb_s, r, KV_LORA), F32),
        compiler_params=_cparams("arbitrary"),
        name="mla_sample",
    )(page_table.reshape(-1), q_cat, cache_t, kv_rows)


def _mla_out_sample_body(o_ref, x_ref, wuv_ref, wo_ref, out_ref, u_scr, *, l_s):
    nb = o_ref.shape[0]
    for hd in range(B_HEADS):
        oh = o_ref[:, hd * l_s:(hd + 1) * l_s, :].reshape(nb * l_s, KV_LORA).astype(BF16)
        u_scr[:, hd * V_HEAD:(hd + 1) * V_HEAD] = _dot(oh, wuv_ref[hd])
    out_ref[...] = x_ref[...] + _dot(u_scr[...].astype(BF16), wo_ref[...])


def _mla_out_sample(x, o_lat, w, row0, l_s):
    nb_s = o_lat.shape[0]
    nb = ROW_TM // l_s
    blk0 = row0 // ROW_TM
    return pl.pallas_call(
        functools.partial(_mla_out_sample_body, l_s=l_s),
        grid=(nb_s // nb,),
        in_specs=[
            pl.BlockSpec((nb, B_HEADS * l_s, KV_LORA), lambda i: (i, 0, 0)),
            pl.BlockSpec((ROW_TM, D_MODEL), lambda i: (i + blk0, 0)),
            _resident((B_HEADS, KV_LORA, V_HEAD), lambda i: (0, 0, 0)),
            _resident((B_HEADS * V_HEAD, D_MODEL), lambda i: (0, 0)),
        ],
        out_specs=pl.BlockSpec((ROW_TM, D_MODEL), lambda i: (i + blk0, 0)),
        out_shape=jax.ShapeDtypeStruct(x.shape, F32),
        scratch_shapes=[pltpu.VMEM((ROW_TM, B_HEADS * V_HEAD), F32)],
        input_output_aliases={1: 0},
        compiler_params=_cparams("parallel"),
        name="mla_out_sample",
    )(o_lat, x, w["w_uv"], w["w_o"])


C_IN_COLS = D_INNER + CONV_DIM + LANES


def _mamba_in_body(x_ref, g_ref, w_ref, z_ref, xbc_ref, dt_ref):
    h = _rms(x_ref[...], g_ref[...]).astype(BF16)
    z_ref[...] = _dot(h, w_ref[:, :D_INNER])
    xbc_ref[...] = _dot(h, w_ref[:, D_INNER:D_INNER + CONV_DIM])
    dt_ref[...] = _dot(h, w_ref[:, D_INNER + CONV_DIM:])


def _mamba_in(x, norm, layer, w_in):
    rows = x.shape[0]
    return pl.pallas_call(
        _mamba_in_body,
        grid=(rows // ROW_TM,),
        in_specs=[
            pl.BlockSpec((ROW_TM, D_MODEL), lambda i: (i, 0)),
            _resident((None, 1, D_MODEL), lambda i: (layer, 0, 0)),
            _resident((D_MODEL, C_IN_COLS), lambda i: (0, 0)),
        ],
        out_specs=[
            pl.BlockSpec((ROW_TM, D_INNER), lambda i: (i, 0)),
            pl.BlockSpec((ROW_TM, CONV_DIM), lambda i: (i, 0)),
            pl.BlockSpec((ROW_TM, LANES), lambda i: (i, 0)),
        ],
        out_shape=[
            jax.ShapeDtypeStruct((rows, D_INNER), F32),
            jax.ShapeDtypeStruct((rows, CONV_DIM), F32),
            jax.ShapeDtypeStruct((rows, LANES), F32),
        ],
        compiler_params=_cparams("parallel"),
        name="mamba_in",
    )(x, norm, w_in)


def _causal_conv(ext_scr, u, cw_ref, cb_ref):
    t = u.shape[0]
    ext_scr[SUBLANES:SUBLANES + t, :] = u
    acc = cb_ref[...] + cw_ref[D_CONV - 1:D_CONV, :] * u
    for k in range(D_CONV - 1):
        lo = SUBLANES - (D_CONV - 1) + k
        acc = acc + cw_ref[k:k + 1, :] * ext_scr[lo:lo + t, :]
    return acc


def _gated_norm(y, z, norm_g):
    yz = y * _silu(z)
    gw = D_INNER // C_GROUPS
    parts = []
    for g in range(C_GROUPS):
        part = yz[:, g * gw:(g + 1) * gw]
        parts.append(part * lax.rsqrt(jnp.mean(part * part, axis=-1, keepdims=True) + EPS))
    return jnp.concatenate(parts, axis=1) * norm_g


def _ssd_prompt_body(xbc_ref, dt_ref, z_ref, cw_ref, cb_ref, dtb_ref, alog_ref, dskip_ref, ng_ref, yn_ref, st_ref,
                     ext_scr, ht_scr, y_ref):
    c = pl.program_id(1)
    t = SSD_CHUNK

    @pl.when(c == 0)
    def _():
        ext_scr[0:SUBLANES, :] = jnp.zeros((SUBLANES, CONV_DIM), F32)
        ht_scr[...] = jnp.zeros(ht_scr.shape, F32)

    u = xbc_ref[...]
    act = _silu(_causal_conv(ext_scr, u, cw_ref, cb_ref))
    ext_scr[0:SUBLANES, :] = u[t - SUBLANES:, :]

    dt = _softplus(dt_ref[...] + dtb_ref[...])
    a = -jnp.exp(alog_ref[...])
    row = lax.broadcasted_iota(jnp.int32, (t, t), 0)
    col = lax.broadcasted_iota(jnp.int32, (t, t), 1)
    causal = col <= row
    acs = _dot_f32(causal.astype(F32), dt * a)
    acs_t = acs.T
    dt_t = dt.T
    to_end = jnp.exp(acs[t - 1:t, :] - acs) * dt
    gw = C_HPG * C_HEAD_DIM
    heads_per_vreg = LANES // C_HEAD_DIM
    assert heads_per_vreg == 2 and t == LANES
    first = lax.broadcasted_iota(jnp.int32, (t, LANES), 1) < C_HEAD_DIM
    for g in range(C_GROUPS):
        bg = act[:, D_INNER + g * D_STATE:D_INNER + (g + 1) * D_STATE]
        cg = act[:, D_INNER + C_BC + g * D_STATE:D_INNER + C_BC + (g + 1) * D_STATE].astype(BF16)
        cb = _dot_nt(cg, bg.astype(BF16))
        bg_t = bg.T.astype(BF16)
        y_off = _dot(cg, ht_scr[:, g * gw:(g + 1) * gw].astype(BF16))
        for jp in range(C_HPG // heads_per_vreg):
            h0 = g * C_HPG + heads_per_vreg * jp
            ps = slice(h0 * C_HEAD_DIM, h0 * C_HEAD_DIM + LANES)
            ms, a_cols, te_cols = [], [], []
            for h in (h0, h0 + 1):
                a_col = jnp.broadcast_to(acs[:, h:h + 1], (t, t))
                a_row = jnp.broadcast_to(acs_t[h:h + 1, :], (t, t))
                dt_row = jnp.broadcast_to(dt_t[h:h + 1, :], (t, t))
                ms.append(cb * jnp.exp(jnp.where(causal, a_col - a_row, -jnp.inf)) * dt_row)
                a_cols.append(a_col)
                te_cols.append(jnp.broadcast_to(to_end[:, h:h + 1], (t, LANES)))
            x2 = act[:, ps]
            lhs = jnp.concatenate(ms, axis=1).astype(BF16)
            rhs = jnp.concatenate([jnp.where(first, x2, 0.0), jnp.where(first, 0.0, x2)], axis=0).astype(BF16)
            a_col2 = jnp.where(first, a_cols[0], a_cols[1])
            y_ref[:, ps] = (_dot(lhs, rhs)
                            + y_off[:, jp * LANES:(jp + 1) * LANES] * jnp.exp(a_col2)
                            + dskip_ref[:, ps] * x2)
            xw = (x2 * jnp.where(first, te_cols[0], te_cols[1])).astype(BF16)
            ht_scr[:, ps] = jnp.exp(a_col2[t - 1:t, :]) * ht_scr[:, ps] + _dot(bg_t, xw)

    yn_ref[...] = _gated_norm(y_ref[...], z_ref[...], ng_ref[...]).astype(BF16)

    @pl.when(c == pl.num_programs(1) - 1)
    def _():
        st_ref[...] = ht_scr[...].T


def _ssd_prompt(xbc, dt, z, w, nb_p, t_p):
    nc = t_p // SSD_CHUNK
    const2 = lambda b, c: (0, 0)
    rows = lambda b, c: (b * nc + c, 0)
    return pl.pallas_call(
        _ssd_prompt_body,
        grid=(nb_p, nc),
        in_specs=[
            pl.BlockSpec((SSD_CHUNK, CONV_DIM), rows),
            pl.BlockSpec((SSD_CHUNK, LANES), rows),
            pl.BlockSpec((SSD_CHUNK, D_INNER), rows),
            _resident((D_CONV, CONV_DIM), const2),
            _resident((1, CONV_DIM), const2),
            _resident((1, LANES), const2),
            _resident((1, LANES), const2),
            _resident((1, D_INNER), const2),
            _resident((1, D_INNER), const2),
        ],
        out_specs=[
            pl.BlockSpec((SSD_CHUNK, D_INNER), rows),
            pl.BlockSpec((None, D_INNER, D_STATE), lambda b, c: (b, 0, 0)),
        ],
        out_shape=[
            jax.ShapeDtypeStruct((nb_p * t_p, D_INNER), BF16),
            jax.ShapeDtypeStruct((nb_p, D_INNER, D_STATE), F32),
        ],
        scratch_shapes=[
            pltpu.VMEM((SUBLANES + SSD_CHUNK, CONV_DIM), F32),
            pltpu.VMEM((D_STATE, D_INNER), F32),
            pltpu.VMEM((SSD_CHUNK, D_INNER), F32),
        ],
        compiler_params=_cparams("parallel", "arbitrary"),
        name="ssd_prompt",
    )(xbc, dt, z, w["conv_w"], w["conv_b"], w["dt_bias"], w["a_log"], w["d_skip"], w["norm_g"])


def _ssd_sample_body(xbc_ref, dtx_ref, z_ref, cprev_ref, h0_ref, cw_ref, cb_ref, dtb_ref, alog_ref, dskip_ref, ng_ref,
                     y_ref, st_ref, ext_scr, *, l_s):
    u = xbc_ref[...]
    ext_scr[SUBLANES - (D_CONV - 1):SUBLANES, :] = cprev_ref[...]
    act = _silu(_causal_conv(ext_scr, u, cw_ref, cb_ref))
    xs = act[:, :D_INNER]

    dt = _softplus(dtx_ref[...] + dtb_ref[...])
    a = -jnp.exp(alog_ref[...])
    row = lax.broadcasted_iota(jnp.int32, (l_s, l_s), 0)
    col = lax.broadcasted_iota(jnp.int32, (l_s, l_s), 1)
    acs = _dot_f32((col <= row).astype(F32), dt * a)
    last = acs[l_s - 1:l_s, :]
    xdt = xs * dt
    xw = xdt * jnp.exp(last - acs)
    decay_col = jnp.broadcast_to(jnp.exp(last), (LANES, D_INNER)).T
    gw = C_HPG * C_HEAD_DIM
    lrow = lax.broadcasted_iota(jnp.int32, (l_s, gw), 0)
    for g in range(C_GROUPS):
        gs = slice(g * gw, (g + 1) * gw)
        bg = act[:, D_INNER + g * D_STATE:D_INNER + (g + 1) * D_STATE]
        cg = act[:, D_INNER + C_BC + g * D_STATE:D_INNER + C_BC + (g + 1) * D_STATE]
        cb = _dot_nt(cg.astype(BF16), bg.astype(BF16))
        h0 = h0_ref[gs, :]
        y = _dot_nt(cg.astype(BF16), h0.astype(BF16)) * jnp.exp(acs[:, gs])
        for s in range(l_s):
            seg = acs[:, gs] - acs[s:s + 1, gs]
            w_s = jnp.exp(jnp.where(lrow >= s, seg, -jnp.inf)) * jnp.broadcast_to(cb[:, s:s + 1], (l_s, gw))
            y = y + w_s * xdt[s:s + 1, gs]
        y_ref[:, gs] = y + dskip_ref[:, gs] * xs[:, gs]
        st_ref[gs, :] = decay_col[gs, :] * h0 + _dot_tn(xw[:, gs].astype(BF16), bg.astype(BF16))
    y_ref[...] = _gated_norm(y_ref[...], z_ref[...], ng_ref[...])


def _ssd_sample(xbc, dtx, z, conv_prev, h0, sl, w, row0, nb_s, l_s):
    blk0 = row0 // l_s
    const2 = lambda b: (0, 0)
    return pl.pallas_call(
        functools.partial(_ssd_sample_body, l_s=l_s),
        grid=(nb_s,),
        in_specs=[
            pl.BlockSpec((l_s, CONV_DIM), lambda b: (b + blk0, 0)),
            pl.BlockSpec((l_s, D_INNER), lambda b: (b, 0)),
            pl.BlockSpec((l_s, D_INNER), lambda b: (b + blk0, 0)),
            pl.BlockSpec((None, None, D_CONV - 1, CONV_DIM), lambda b: (sl, b, 0, 0)),
            pl.BlockSpec((None, None, D_INNER, D_STATE), lambda b: (sl, b, 0, 0)),
            _resident((D_CONV, CONV_DIM), const2),
            _resident((1, CONV_DIM), const2),
            _resident((1, D_INNER), const2),
            _resident((1, D_INNER), const2),
            _resident((1, D_INNER), const2),
            _resident((1, D_INNER), const2),
        ],
        out_specs=[
            pl.BlockSpec((l_s, D_INNER), lambda b: (b, 0)),
            pl.BlockSpec((None, D_INNER, D_STATE), lambda b: (b, 0, 0)),
        ],
        out_shape=[
            jax.ShapeDtypeStruct((nb_s * l_s, D_INNER), F32),
            jax.ShapeDtypeStruct((nb_s, D_INNER, D_STATE), F32),
        ],
        scratch_shapes=[pltpu.VMEM((SUBLANES + l_s, CONV_DIM), F32)],
        compiler_params=_cparams("parallel"),
        name="ssd_sample",
    )(xbc, dtx, z, conv_prev, h0, w["conv_w"], w["conv_b"], w["dt_bias_x"], w["a_log_x"], w["d_skip"], w["norm_g"])


def _rope_tables(pos):
    half = QK_ROPE // 2
    inv = jnp.power(ROPE_THETA, -jnp.arange(half, dtype=F32) / half)
    ang = pos.astype(F32)[:, None] * inv
    reps = LANES // half
    return jnp.tile(jnp.cos(ang), (1, reps)), jnp.tile(jnp.sin(ang), (1, reps))


def _rot_cols(w):
    half = w.shape[-1] // 2
    return jnp.concatenate([-w[..., half:], w[..., :half]], axis=-1)


def _mla_weights(w_in, q_norm, w_uq, kv_norm, w_ukv, w_o):
    kpe = w_in[:, B_KPE:]
    pad = jnp.zeros((D_MODEL, LANES - QK_ROPE), w_in.dtype)
    w_in_aug = jnp.concatenate([w_in[:, :B_KPE], kpe, pad, _rot_cols(kpe), pad], axis=1)
    uq = w_uq.reshape(Q_LORA, B_HEADS, QK_NOPE + QK_ROPE)
    pe = uq[:, :, QK_NOPE:]
    ukv = w_ukv.reshape(KV_LORA, B_HEADS, QK_NOPE + V_HEAD)
    return {
        "w_in": w_in_aug.astype(BF16),
        "q_norm": q_norm.reshape(1, Q_LORA),
        "kv_norm": kv_norm.reshape(1, KV_LORA),
        "w_nope": uq[:, :, :QK_NOPE].reshape(Q_LORA, B_HEADS * QK_NOPE).astype(BF16),
        "w_pe": pe.reshape(Q_LORA, B_HEADS * QK_ROPE).astype(BF16),
        "w_pe_rot": _rot_cols(pe).reshape(Q_LORA, B_HEADS * QK_ROPE).astype(BF16),
        "w_uk": jnp.transpose(ukv[:, :, :QK_NOPE], (1, 2, 0)).astype(BF16),
        "w_uv": jnp.transpose(ukv[:, :, QK_NOPE:], (1, 0, 2)).astype(BF16),
        "w_o": w_o.astype(BF16),
    }


def _pad_lanes(v):
    return jnp.pad(v, (0, LANES - v.shape[0])).reshape(1, LANES)


def _expand_heads(v):
    return jnp.repeat(v, C_HEAD_DIM).reshape(1, D_INNER)


def _mamba_weights(w_in, conv_w, conv_b, dt_bias, a_log, d_skip, norm_g, w_o):
    w_dt = w_in[:, D_INNER + CONV_DIM:]
    pad = jnp.zeros((D_MODEL, LANES - C_HEADS), w_in.dtype)
    return {
        "w_in": jnp.concatenate([w_in, pad], axis=1).astype(BF16),
        "w_dt_x": jnp.repeat(w_dt, C_HEAD_DIM, axis=1).astype(BF16)[None],
        "conv_w": conv_w,
        "conv_b": conv_b.reshape(1, CONV_DIM),
        "dt_bias": _pad_lanes(dt_bias.astype(F32)),
        "a_log": _pad_lanes(a_log.astype(F32)),
        "dt_bias_x": _expand_heads(dt_bias.astype(F32)),
        "a_log_x": _expand_heads(a_log.astype(F32)),
        "d_skip": _expand_heads(d_skip),
        "norm_g": norm_g.reshape(1, D_INNER),
        "w_o": w_o.astype(BF16),
    }


def kernel(x_prompt, x_sample, state_swa_k, state_swa_v, cache_mla, state_ssm, state_conv, page_table, p_prompt, p_sample, norm_ffn1, ffn1_w_gu, ffn1_w_down, norm_mix, norm_ffn2, ffn2_w_gu, ffn2_w_down, norm_ple, ple_w_proj, ple_w_gate, rel_bias_table, a_w_qkv, a_w_o, a_sinks, b_w_in, b_q_norm, b_w_uq, b_kv_norm, b_w_ukv, b_w_o, c_w_in, c_conv_w, c_conv_b, c_dt_bias, c_a_log, c_d, c_norm, c_w_o, final_norm):
    nb_p, t_p, _ = x_prompt.shape
    nb_s, l_s, _ = x_sample.shape
    depth = norm_ffn1.shape[0]
    rows_p = nb_p * t_p
    rows_s = nb_s * l_s
    w_buf = state_swa_k.shape[2]
    assert rows_p % FFN_TM == 0 and (rows_p + rows_s) % FFN_TM == 0 and rows_s % ROW_TM == 0
    assert t_p % B_KVBLOCK == 0 and t_p % SSD_CHUNK == 0 and t_p % SWA_QROWS == 0
    assert l_s == SUBLANES and w_buf == WINDOW

    x = (x_prompt.reshape(rows_p, D_MODEL), x_sample.reshape(rows_s, D_MODEL))
    p_parts = (p_prompt.reshape(depth, rows_p, D_PLE), p_sample.reshape(depth, rows_s, D_PLE))

    def norm3(g):
        return g.reshape(g.shape[0], 1, g.shape[1])

    norm_ffn1, norm_mix, norm_ffn2, norm_ple = map(norm3, (norm_ffn1, norm_mix, norm_ffn2, norm_ple))
    ffn1_w_gu, ffn1_w_down, ple_w_proj, ple_w_gate, a_w_qkv, a_w_o = (
        w.astype(BF16) for w in (ffn1_w_gu, ffn1_w_down, ple_w_proj, ple_w_gate, a_w_qkv, a_w_o))
    final_norm = final_norm.reshape(1, D_MODEL)

    i = jnp.arange(SWA_QROWS)[:, None]
    j = jnp.arange(A_BLOCK + SWA_QROWS)[None, :]
    dist = A_BLOCK + i - j
    in_window = (dist >= 0) & (dist < WINDOW)
    bias_p = jnp.stack([_rel_bias(rel_bias_table, _t5_bucket(dist), in_window & (j >= A_BLOCK)),
                        _rel_bias(rel_bias_table, _t5_bucket(dist), in_window)])
    q_pos = PAST_LEN + jnp.arange(l_s)
    k_pos = jnp.concatenate([PAST_LEN - w_buf + jnp.arange(w_buf), q_pos])
    dist = q_pos[:, None] - k_pos[None, :]
    bias_s = _rel_bias(rel_bias_table, _t5_bucket(dist), (dist >= 0) & (dist < WINDOW))
    swa_k_buf = jnp.transpose(state_swa_k, (0, 1, 3, 4, 2))
    swa_v_buf = jnp.transpose(state_swa_v, (0, 1, 3, 4, 2))

    pos = jnp.concatenate([jnp.tile(jnp.arange(t_p), nb_p), jnp.tile(PAST_LEN + jnp.arange(l_s), nb_s)])
    cos, sin = _rope_tables(pos)

    ssm_flat = state_ssm.reshape(state_ssm.shape[0], nb_s, D_INNER, D_STATE)
    cache_t = jnp.swapaxes(cache_mla, 2, 3)

    def last_rows(a, n):
        return jnp.stack([a[(b + 1) * t_p - n:(b + 1) * t_p] for b in range(nb_p)])

    swa_k_p, swa_v_p, mla_p, ssm_p, conv_p = [], [], [], [], []
    swa_k_s, swa_v_s, mla_s, ssm_s, conv_s = [], [], [], [], []
    for layer in range(depth):
        kind, sl = layer % N_MIXERS, layer // N_MIXERS
        prev_ple = None if layer == 0 else (p_parts, norm_ple, ple_w_proj, ple_w_gate, layer - 1)
        if kind == 0:
            x, qkv = _ffn(x, norm_ffn1, ffn1_w_gu, ffn1_w_down, layer, ple=prev_ple,
                          proj=(norm_mix, layer, a_w_qkv, sl), n_first=rows_p)
        else:
            x = _ffn(x, norm_ffn1, ffn1_w_gu, ffn1_w_down, layer, ple=prev_ple, n_first=rows_p)
        if kind == 0:
            x = _swa_prompt(x, qkv, bias_p, a_sinks[sl], a_w_o, sl, nb_p, t_p)
            o_s, kt_new, vt_new = _swa_sample(qkv, swa_k_buf, swa_v_buf, bias_s, a_sinks[sl], sl, rows_p, nb_s, l_s)
            x = _res_proj(x, o_s, a_w_o, sl, rows_p, "swa_out_sample")
            kv_p = last_rows(qkv, w_buf)[:, :, A_Q:].reshape(nb_p, w_buf, 2, A_KV_HEADS, A_HEAD_DIM)
            swa_k_p.append(kv_p[:, :, 0])
            swa_v_p.append(kv_p[:, :, 1])
            swa_k_s.append(jnp.transpose(kt_new, (0, 3, 1, 2)))
            swa_v_s.append(jnp.transpose(vt_new, (0, 3, 1, 2)))
        elif kind == 1:
            w = _mla_weights(b_w_in[sl], b_q_norm[sl], b_w_uq[sl], b_kv_norm[sl], b_w_ukv[sl], b_w_o[sl])
            kv_rows_p, q_p, kv_bf16 = _mla_proj(x, norm_mix, layer, w, cos, sin, 0, rows_p, None)
            kv_rows_s, q_s = _mla_proj(x, norm_mix, layer, w, cos, sin, rows_p, rows_s, l_s)
            x = _mla_prompt(x, q_p, kv_bf16, w, nb_p, t_p)
            o_lat = _mla_sample(q_s, kv_rows_s, cache_t, page_table, sl, 0, nb_s, l_s)
            x = _mla_out_sample(x, o_lat, w, rows_p, l_s)
            mla_p.append(kv_rows_p.reshape(nb_p, t_p, KV_ROW))
            mla_s.append(kv_rows_s.reshape(nb_s, l_s, KV_ROW))
        else:
            w = _mamba_weights(c_w_in[sl], c_conv_w[sl], c_conv_b[sl], c_dt_bias[sl], c_a_log[sl], c_d[sl],
                               c_norm[sl], c_w_o[sl])
            z, xbc, dt = _mamba_in(x, norm_mix, layer, w["w_in"])
            dtx = _norm_proj(x[rows_p:], norm_mix, w["w_dt_x"], layer, 0, "mamba_dt_sample")
            y_p, st_p = _ssd_prompt(xbc, dt, z, w, nb_p, t_p)
            y_s, st_s = _ssd_sample(xbc, dtx, z, state_conv, ssm_flat, sl, w, rows_p, nb_s, l_s)
            x = _res_proj(x, y_p, w["w_o"][None], 0, 0, "mamba_out_prompt")
            x = _res_proj(x, y_s, w["w_o"][None], 0, rows_p, "mamba_out_sample")
            ssm_p.append(st_p.reshape(nb_p, C_HEADS, C_HEAD_DIM, D_STATE))
            ssm_s.append(st_s.reshape(nb_s, C_HEADS, C_HEAD_DIM, D_STATE))
            conv_p.append(last_rows(xbc, D_CONV - 1))
            conv_s.append(jnp.concatenate([state_conv[sl], xbc[rows_p:].reshape(nb_s, l_s, CONV_DIM)],
                                          axis=1)[:, -(D_CONV - 1):])
        x = _ffn(x, norm_ffn2, ffn2_w_gu, ffn2_w_down, layer)
    y_p = _ple(x, p_parts[0], norm_ple, ple_w_proj, ple_w_gate, final_norm, depth - 1, True, 0)
    y_s = _ple(x, p_parts[1], norm_ple, ple_w_proj, ple_w_gate, final_norm, depth - 1, True, rows_p)

    return (y_p.reshape(nb_p, t_p, D_MODEL), y_s.reshape(nb_s, l_s, D_MODEL),
            jnp.stack(swa_k_p), jnp.stack(swa_v_p), jnp.stack(mla_p), jnp.stack(ssm_p), jnp.stack(conv_p),
            jnp.stack(swa_k_s), jnp.stack(swa_v_s), jnp.stack(mla_s), jnp.stack(ssm_s), jnp.stack(conv_s))
```
